```python
import math
import jax
import jax.numpy as jnp
from jax import lax
import numpy as np

D_MODEL = 1024
BATCH = 16
SEQ = 2048
DEPTH = 4

EPS = 1e-6
ROPE_THETA = 500000.0
Q_BLOCK = 128
D_FF = 2816
N_ADA = 9

D_RNN = 512
RNN_HEADS = 8
RNN_HEAD_DIM = D_RNN // RNN_HEADS
CONV_WIDTH = 4
LRU_C = 8.0

MLA_HEADS = 8
MLA_NOPE = 64
MLA_ROPE = 32
MLA_V = 64
MLA_QK = MLA_ROPE + MLA_NOPE
MLA_Q_LORA = 256
MLA_KV_LORA = 128

DSA_HEADS = 8
DSA_HEAD_DIM = 64
DSA_ROT = DSA_HEAD_DIM // 4
IDX_HEADS = 8
IDX_DIM = 32
IDX_ROT = IDX_DIM // 4
TOPK_MAX = 256

S5_GROUP = 16
S5_GROUPS = 32
D_S5 = S5_GROUP * S5_GROUPS
S5_STATE = 64

N_BRANCH = 4
BRANCH_W = 512
IN_SPLITS = (D_RNN, D_RNN, MLA_Q_LORA, MLA_KV_LORA, MLA_ROPE,
             DSA_HEADS * DSA_HEAD_DIM, DSA_HEAD_DIM, DSA_HEAD_DIM,
             IDX_HEADS * IDX_DIM, IDX_DIM, IDX_HEADS, D_S5, N_BRANCH * D_MODEL)
D_IN = 6984

kernel_name = 'hybrid_gated_rglru_mla_dsa_s5_block'


def rmsnorm(x, g):
    xf = x.astype(jnp.float32)
    y = xf * lax.rsqrt(jnp.mean(xf * xf, axis=-1, keepdims=True) + EPS)
    return (y * g.astype(jnp.float32)).astype(x.dtype)


def modulate(x, shift, scale):
    return x * (1.0 + scale) + shift


def swiglu(u, w1, w3, w2):
    return (jax.nn.silu(u @ w1) * (u @ w3)) @ w2


def rope_tables(positions, rot_dim):
    inv = ROPE_THETA ** (-jnp.arange(0, rot_dim, 2, dtype=jnp.float32) / rot_dim)
    ang = positions.astype(jnp.float32)[..., None] * inv
    return jnp.cos(ang), jnp.sin(ang)


def apply_rope(x, cs, rot_dim):
    cos, sin = cs
    if x.ndim == 4:
        cos, sin = cos[:, :, None, :], sin[:, :, None, :]
    cos, sin = cos.astype(x.dtype), sin.astype(x.dtype)
    half = rot_dim // 2
    x1, x2, rest = x[..., :half], x[..., half:rot_dim], x[..., rot_dim:]
    return jnp.concatenate([x1 * cos - x2 * sin, x2 * cos + x1 * sin, rest], axis=-1)


def to_blocks(a):
    b, t = a.shape[:2]
    return a.reshape((b, t // Q_BLOCK, Q_BLOCK) + a.shape[2:]).swapaxes(0, 1)


def from_blocks(o):
    nb, b, qb = o.shape[:3]
    return o.swapaxes(0, 1).reshape((b, nb * qb, -1))


def linear_combine(e1, e2):
    a1, b1 = e1
    a2, b2 = e2
    return a1 * a2, a2 * b1 + b2


def complex_combine(e1, e2):
    a1r, a1i, b1r, b1i = e1
    a2r, a2i, b2r, b2i = e2
    return (a2r * a1r - a2i * a1i, a2r * a1i + a2i * a1r,
            a2r * b1r - a2i * b1i + b2r, a2r * b1i + a2i * b1r + b2i)


def rglru_branch(x_rnn, gate_rnn, conv_w, conv_b, wa, ba, wx, bx, lam):
    b, t, _ = x_rnn.shape
    xc = lax.conv_general_dilated(x_rnn, conv_w[:, None, :], window_strides=(1,),
                                  padding=[(CONV_WIDTH - 1, 0)],
                                  dimension_numbers=('NWC', 'WIO', 'NWC'),
                                  feature_group_count=D_RNN) + conv_b
    xh = xc.reshape(b, t, RNN_HEADS, RNN_HEAD_DIM)
    r = jax.nn.sigmoid((jnp.einsum('bthi,hij->bthj', xh, wa).reshape(b, t, D_RNN) + ba).astype(jnp.float32))
    ig = jax.nn.sigmoid((jnp.einsum('bthi,hij->bthj', xh, wx).reshape(b, t, D_RNN) + bx).astype(jnp.float32))
    log_a = -LRU_C * r * jax.nn.softplus(-lam.astype(jnp.float32))
    a = jnp.exp(log_a)
    inp = jnp.sqrt(-jnp.expm1(2.0 * log_a)) * ig * xc.astype(jnp.float32)
    _, h = lax.associative_scan(linear_combine, (a, inp), axis=1)
    return h.astype(x_rnn.dtype) * jax.nn.gelu(gate_rnn)


def causal_dense_attention(q, k, v, scale):
    t = q.shape[1]
    kpos = jnp.arange(t)

    def block(args):
        qb, i = args
        qpos = i * Q_BLOCK + jnp.arange(Q_BLOCK)
        s = jnp.einsum('bqhd,bshd->bhqs', qb, k).astype(jnp.float32) * scale
        s = jnp.where(kpos[None, :] <= qpos[:, None], s, -jnp.inf)
        p = jax.nn.softmax(s, axis=-1).astype(v.dtype)
        return jnp.einsum('bhqs,bshd->bqhd', p, v)

    return from_blocks(lax.map(block, (to_blocks(q), jnp.arange(t // Q_BLOCK))))


def mla_branch(q_lat, kv_lat, k_pe, cs, q_norm_g, w_uq, kv_norm_g, w_ukv, qk_gain):
    b, t, _ = q_lat.shape
    q = (rmsnorm(q_lat, q_norm_g) @ w_uq).reshape(b, t, MLA_HEADS, MLA_QK)
    kv = (rmsnorm(kv_lat, kv_norm_g) @ w_ukv).reshape(b, t, MLA_HEADS, MLA_NOPE + MLA_V)
    k_nope, v = kv[..., :MLA_NOPE], kv[..., MLA_NOPE:]
    k_rope = jnp.broadcast_to(k_pe[:, :, None, :], (b, t, MLA_HEADS, MLA_ROPE))
    k = jnp.concatenate([k_rope, k_nope], axis=-1)
    q = apply_rope(rmsnorm(q, qk_gain[0]), cs, MLA_ROPE)
    k = apply_rope(rmsnorm(k, qk_gain[1]), cs, MLA_ROPE)
    return causal_dense_attention(q, k, v, MLA_QK ** -0.5)


def dsa_branch(q, k, v, q_idx, k_idx, w_idx, cs_main, cs_idx, qk_gain):
    b, t, _ = q.shape
    topk = min(TOPK_MAX, t // 4)
    q = apply_rope(rmsnorm(q.reshape(b, t, DSA_HEADS, DSA_HEAD_DIM), qk_gain[0]), cs_main, DSA_ROT)
    k = apply_rope(rmsnorm(k, qk_gain[1]), cs_main, DSA_ROT)
    q_idx = apply_rope(q_idx.reshape(b, t, IDX_HEADS, IDX_DIM), cs_idx, IDX_ROT)
    k_idx = apply_rope(k_idx, cs_idx, IDX_ROT)
    kpos = jnp.arange(t)
    gather = jax.vmap(lambda table, idx: table[idx])

    def block(args):
        qb, qib, wib, i = args
        qpos = i * Q_BLOCK + jnp.arange(Q_BLOCK)
        rel = jax.nn.relu(jnp.einsum('bqhd,bsd->bqhs', qib, k_idx).astype(jnp.float32))
        score = jnp.einsum('bqh,bqhs->bqs', wib.astype(jnp.float32), rel)
        score = jnp.where(kpos[None, :] <= qpos[:, None], score, -jnp.inf)
        _, sel = lax.top_k(score, topk)
        valid = sel <= qpos[None, :, None]
        k_sel = gather(k, sel)
        v_sel = gather(v, sel)
        s = jnp.einsum('bqhd,bqkd->bhqk', qb, k_sel).astype(jnp.float32) * DSA_HEAD_DIM ** -0.5
        s = jnp.where(valid[:, None], s, -jnp.inf)
        p = jax.nn.softmax(s, axis=-1).astype(v.dtype)
        return jnp.einsum('bhqk,bqkd->bqhd', p, v_sel)

    xs = (to_blocks(q), to_blocks(q_idx), to_blocks(w_idx), jnp.arange(t // Q_BLOCK))
    return from_blocks(lax.map(block, xs))


def s5_branch(u, lam_re, lam_im, log_dt, b_re, b_im, c_re, c_im, d, w_glu, b_glu):
    bsz, t, _ = u.shape
    f32 = jnp.float32
    lr, li = lam_re.astype(f32), lam_im.astype(f32)
    dt = jnp.exp(log_dt.astype(f32))[:, None]
    mag = jnp.exp(lr * dt)
    ab_re, ab_im = mag * jnp.cos(li * dt), mag * jnp.sin(li * dt)
    den = lr * lr + li * li
    nr, ni = ab_re - 1.0, ab_im
    f_re = (nr * lr + ni * li) / den
    f_im = (ni * lr - nr * li) / den
    br, bi = b_re.astype(f32), b_im.astype(f32)
    bb_re = f_re[..., None] * br - f_im[..., None] * bi
    bb_im = f_re[..., None] * bi + f_im[..., None] * br
    ug = u.reshape(bsz, t, S5_GROUPS, S5_GROUP).astype(f32)
    bu_re = jnp.einsum('gpj,btgj->btgp', bb_re, ug)
    bu_im = jnp.einsum('gpj,btgj->btgp', bb_im, ug)
    a_re = jnp.broadcast_to(ab_re[None, None], (1, t, S5_GROUPS, S5_STATE))
    a_im = jnp.broadcast_to(ab_im[None, None], (1, t, S5_GROUPS, S5_STATE))
    _, _, x_re, x_im = lax.associative_scan(complex_combine, (a_re, a_im, bu_re, bu_im), axis=1)
    y = (jnp.einsum('gjp,btgp->btgj', c_re.astype(f32), x_re)
         - jnp.einsum('gjp,btgp->btgj', c_im.astype(f32), x_im))
    y = y.reshape(bsz, t, D_S5) + d.astype(f32) * u.astype(f32)
    y = jax.nn.gelu(y).astype(u.dtype)
    return y * jax.nn.sigmoid(y @ w_glu + b_glu)


def setup_inputs(seed: int = 0) -> dict:
    key = jax.random.key(seed)
    ks = jax.random.split(key, 40)
    L = DEPTH

    def nrm(k, shape, scale):
        return jax.random.normal(k, shape, jnp.float32) * scale

    def gain(k, shape):
        return 1.0 + nrm(k, shape, 0.02)

    a_c = jax.random.uniform(ks[20], (L, D_RNN), jnp.float32, 0.9, 0.999)
    a0 = a_c ** (1.0 / LRU_C)
    s5_n = jnp.arange(S5_STATE, dtype=jnp.float32)
    return {
        'x': nrm(ks[0], (BATCH, SEQ, D_MODEL), 1.0),
        'c': nrm(ks[1], (BATCH, D_MODEL), 1.0),
        'positions': jnp.arange(SEQ, dtype=jnp.int32)[None, :] + jax.random.randint(ks[2], (BATCH, 1), 0, 1024, jnp.int32),
        'ada_w': nrm(ks[3], (L, D_MODEL, N_ADA * D_MODEL), 0.1 * D_MODEL ** -0.5),
        'ada_b': nrm(ks[4], (L, N_ADA * D_MODEL), 0.01),
        'norm_g': gain(ks[5], (L, 3, D_MODEL)),
        'ffn_w1': nrm(ks[6], (L, 2, D_MODEL, D_FF), D_MODEL ** -0.5),
        'ffn_w3': nrm(ks[7], (L, 2, D_MODEL, D_FF), D_MODEL ** -0.5),
        'ffn_w2': nrm(ks[8], (L, 2, D_FF, D_MODEL), D_FF ** -0.5),
        'w_in': nrm(ks[9], (L, D_MODEL, D_IN), D_MODEL ** -0.5),
        'conv_w': nrm(ks[10], (L, CONV_WIDTH, D_RNN), CONV_WIDTH ** -0.5),
        'conv_b': nrm(ks[11], (L, D_RNN), 0.01),
        'rg_wa': nrm(ks[12], (L, RNN_HEADS, RNN_HEAD_DIM, RNN_HEAD_DIM), RNN_HEAD_DIM ** -0.5),
        'rg_ba': nrm(ks[13], (L, D_RNN), 0.01),
        'rg_wx': nrm(ks[14], (L, RNN_HEADS, RNN_HEAD_DIM, RNN_HEAD_DIM), RNN_HEAD_DIM ** -0.5),
        'rg_bx': nrm(ks[15], (L, D_RNN), 0.01),
        'rg_lambda': jnp.log(a0) - jnp.log1p(-a0),
        'mla_q_norm': gain(ks[16], (L, MLA_Q_LORA)),
        'mla_w_uq': nrm(ks[17], (L, MLA_Q_LORA, MLA_HEADS * MLA_QK), MLA_Q_LORA ** -0.5),
        'mla_kv_norm': gain(ks[18], (L, MLA_KV_LORA)),
        'mla_w_ukv': nrm(ks[19], (L, MLA_KV_LORA, MLA_HEADS * (MLA_NOPE + MLA_V)), MLA_KV_LORA ** -0.5),
        'mla_qk_gain': gain(ks[21], (L, 2, MLA_QK)),
        'dsa_qk_gain': gain(ks[22], (L, 2, DSA_HEAD_DIM)),
        's5_lambda_re': -0.5 + nrm(ks[23], (L, S5_GROUPS, S5_STATE), 0.005),
        's5_lambda_im': math.pi * s5_n + nrm(ks[24], (L, S5_GROUPS, S5_STATE), 0.01),
        's5_log_dt': jax.random.uniform(ks[25], (L, S5_GROUPS), jnp.float32, math.log(0.001), math.log(0.1)),
        's5_b_re': nrm(ks[26], (L, S5_GROUPS, S5_STATE, S5_GROUP), (2.0 * S5_GROUP) ** -0.5),
        's5_b_im': nrm(ks[27], (L, S5_GROUPS, S5_STATE, S5_GROUP), (2.0 * S5_GROUP) ** -0.5),
        's5_c_re': nrm(ks[28], (L, S5_GROUPS, S5_GROUP, S5_STATE), (2.0 * S5_STATE) ** -0.5),
        's5_c_im': nrm(ks[29], (L, S5_GROUPS, S5_GROUP, S5_STATE), (2.0 * S5_STATE) ** -0.5),
        's5_d': nrm(ks[30], (L, D_S5), 1.0),
        's5_w_glu': nrm(ks[31], (L, D_S5, D_S5), D_S5 ** -0.5),
        's5_b_glu': nrm(ks[32], (L, D_S5), 0.01),
        'w_branch': nrm(ks[33], (L, N_BRANCH, BRANCH_W, D_MODEL), BRANCH_W ** -0.5),
        'w_out': nrm(ks[34], (L, D_MODEL, D_MODEL), D_MODEL ** -0.5),
    }


def reference(x, c, positions, ada_w, ada_b, norm_g, ffn_w1, ffn_w3, ffn_w2, w_in,
              conv_w, conv_b, rg_wa, rg_ba, rg_wx, rg_bx, rg_lambda,
              mla_q_norm, mla_w_uq, mla_kv_norm, mla_w_ukv, mla_qk_gain, dsa_qk_gain,
              s5_lambda_re, s5_lambda_im, s5_log_dt, s5_b_re, s5_b_im, s5_c_re, s5_c_im,
              s5_d, s5_w_glu, s5_b_glu, w_branch, w_out):
    b, t, _ = x.shape
    split_points = np.cumsum(np.array(IN_SPLITS))[:-1].tolist()
    cs_mla = rope_tables(positions, MLA_ROPE)
    cs_dsa = rope_tables(positions, DSA_ROT)
    cs_idx = rope_tables(positions, IDX_ROT)
    c_act = jax.nn.silu(c)
    for l in range(DEPTH):
        mod = (c_act @ ada_w[l] + ada_b[l])[:, None, :]
        sh1, sc1, g1, sh2, sc2, g2, sh3, sc3, g3 = jnp.split(mod, N_ADA, axis=-1)
        u = modulate(rmsnorm(x, norm_g[l, 0]), sh1, sc1)
        x = x + 0.5 * (1.0 + g1) * swiglu(u, ffn_w1[l, 0], ffn_w3[l, 0], ffn_w2[l, 0])
        u = modulate(rmsnorm(x, norm_g[l, 1]), sh2, sc2)
        z = u @ w_in[l]
        (x_rnn, gate_rnn, q_lat, kv_lat, k_pe, q_dsa, k_dsa, v_dsa,
         q_idx, k_idx, w_idx, u_s5, gate_logits) = jnp.split(z, split_points, axis=-1)
        y_a = rglru_branch(x_rnn, gate_rnn, conv_w[l], conv_b[l], rg_wa[l], rg_ba[l],
                           rg_wx[l], rg_bx[l], rg_lambda[l])
        y_b = mla_branch(q_lat, kv_lat, k_pe, cs_mla, mla_q_norm[l], mla_w_uq[l],
                         mla_kv_norm[l], mla_w_ukv[l], mla_qk_gain[l])
        y_c = dsa_branch(q_dsa, k_dsa, v_dsa, q_idx, k_idx, w_idx, cs_dsa, cs_idx, dsa_qk_gain[l])
        y_d = s5_branch(u_s5, s5_lambda_re[l], s5_lambda_im[l], s5_log_dt[l], s5_b_re[l],
                        s5_b_im[l], s5_c_re[l], s5_c_im[l], s5_d[l], s5_w_glu[l], s5_b_glu[l])
        gates = jax.nn.sigmoid(gate_logits.reshape(b, t, N_BRANCH, D_MODEL))
        ys = [y_a, y_b, y_c, y_d]
        merged = gates[:, :, 0] * (y_a @ w_branch[l, 0])
        for n in range(1, N_BRANCH):
            merged = merged + gates[:, :, n] * (ys[n] @ w_branch[l, n])
        x = x + (1.0 + g2) * (merged @ w_out[l])
        u = modulate(rmsnorm(x, norm_g[l, 2]), sh3, sc3)
        x = x + 0.5 * (1.0 + g3) * swiglu(u, ffn_w1[l, 1], ffn_w3[l, 1], ffn_w2[l, 1])
    return x
```

```python
import functools
import math

import jax
import jax.numpy as jnp
from jax import lax
from jax.experimental import pallas as pl
from jax.experimental.pallas import tpu as pltpu

F32 = jnp.float32
BF16 = jnp.bfloat16

D_MODEL = 1024
EPS = 1e-6
ROPE_THETA = 500000.0
D_FF = 2816
N_ADA = 9

D_RNN = 512
RNN_HEADS = 8
RNN_HEAD_DIM = D_RNN // RNN_HEADS
CONV_WIDTH = 4
LRU_C = 8.0

MLA_HEADS = 8
MLA_NOPE = 64
MLA_ROPE = 32
MLA_V = 64
MLA_QK = MLA_ROPE + MLA_NOPE
MLA_Q_LORA = 256
MLA_KV_LORA = 128

DSA_HEADS = 8
DSA_HEAD_DIM = 64
DSA_ROT = DSA_HEAD_DIM // 4
IDX_HEADS = 8
IDX_DIM = 32
IDX_ROT = IDX_DIM // 4
TOPK_MAX = 256

S5_GROUP = 16
S5_GROUPS = 32
D_S5 = S5_GROUP * S5_GROUPS
S5_STATE = 64
S5_CHUNK = 8
S5_GB = 8
S5_NGB = S5_GROUPS // S5_GB

N_BRANCH = 4
BRANCH_W = 512
IN_SPLITS = (D_RNN, D_RNN, MLA_Q_LORA, MLA_KV_LORA, MLA_ROPE,
             DSA_HEADS * DSA_HEAD_DIM, DSA_HEAD_DIM, DSA_HEAD_DIM,
             IDX_HEADS * IDX_DIM, IDX_DIM, IDX_HEADS, D_S5, N_BRANCH * D_MODEL)

LANE = 128
NEG_BIG = -1e30
VMEM_LIMIT = 56 * 1024 * 1024

_C_XR = 0
_C_GR = _C_XR + D_RNN
_C_QL = _C_GR + D_RNN
_C_KVL = _C_QL + MLA_Q_LORA
_C_KPE = _C_KVL + MLA_KV_LORA
_C_KPEP = _C_KPE + LANE
_C_QD = _C_KPEP + LANE
_C_QDP = _C_QD + DSA_HEADS * LANE
_C_KD = _C_QDP + DSA_HEADS * LANE
_C_KDP = _C_KD + LANE
_C_VD = _C_KDP + LANE
_C_QI = _C_VD + LANE
_C_QIP = _C_QI + IDX_HEADS * IDX_DIM
_C_KI = _C_QIP + IDX_HEADS * IDX_DIM
_C_KIP = _C_KI + IDX_HEADS * IDX_DIM
_C_WI = _C_KIP + IDX_HEADS * IDX_DIM
_C_US = _C_WI + LANE
_C_END = _C_US + D_S5


def _dot(a, b):
    return jnp.dot(a, b, preferred_element_type=F32)


def _dot_nt(a, b):
    return lax.dot_general(a, b, (((1,), (1,)), ((), ())), preferred_element_type=F32)


def _sigmoid(x):
    return jax.nn.sigmoid(x)


def _gelu_tanh(x):
    return 0.5 * x * (1.0 + jnp.tanh(0.7978845608028654 * (x + 0.044715 * (x * x * x))))


def _rms_mod(x, g, shift, scale):
    ms = jnp.mean(x * x, axis=-1, keepdims=True)
    y = x * lax.rsqrt(ms + EPS) * g
    return y * (1.0 + scale) + shift


def _resident(shape):
    return pl.BlockSpec(shape, lambda *_: (0,) * len(shape), pipeline_mode=pl.Buffered(1))


def _params(sem):
    return pltpu.CompilerParams(dimension_semantics=sem, vmem_limit_bytes=VMEM_LIMIT)


def _ada_kernel(c_ref, w_ref, b_ref, o_ref):
    c = c_ref[...]
    a = c * _sigmoid(c)
    w = w_ref[0]
    a_hi = a.astype(BF16)
    a_lo = (a - a_hi.astype(F32)).astype(BF16)
    w_hi = w.astype(BF16)
    w_lo = (w - w_hi.astype(F32)).astype(BF16)
    o_ref[0] = _dot(a_hi, w_hi) + _dot(a_lo, w_hi) + _dot(a_hi, w_lo) + b_ref[0]


def _ada_call(c, ada_w, ada_b):
    nl, d, n9 = ada_w.shape
    b = c.shape[0]
    tn = 1024
    out = pl.pallas_call(
        _ada_kernel,
        grid=(nl, n9 // tn),
        in_specs=[
            pl.BlockSpec((b, d), lambda l, j: (0, 0)),
            pl.BlockSpec((1, d, tn), lambda l, j: (l, 0, j)),
            pl.BlockSpec((1, 1, tn), lambda l, j: (l, 0, j)),
        ],
        out_specs=pl.BlockSpec((1, b, tn), lambda l, j: (l, 0, j)),
        out_shape=jax.ShapeDtypeStruct((nl, b, n9), F32),
        compiler_params=_params(("arbitrary", "arbitrary")),
        name="ada_mod",
    )(c, ada_w, ada_b.reshape(nl, 1, n9))
    return out.reshape(nl, b, N_ADA, d)


def _ffn_kernel(row0, nchunk, x_ref, mod_ref, g_ref, w1_ref, w3_ref, w2_ref, o_ref):
    x = x_ref[...]
    m = mod_ref[0]
    u = _rms_mod(x, g_ref[...], m[row0:row0 + 1], m[row0 + 1:row0 + 2]).astype(BF16)

    def body(c, acc):
        h1 = _dot(u, w1_ref[c])
        h3 = _dot(u, w3_ref[c])
        h = (h1 * _sigmoid(h1) * h3).astype(BF16)
        return acc + _dot(h, w2_ref[c])

    acc = lax.fori_loop(0, nchunk, body, jnp.zeros(x.shape, F32))
    o_ref[...] = x + 0.5 * (1.0 + m[row0 + 2:row0 + 3]) * acc


def _ffn_call(x, mod, g, w1c, w3c, w2c, row0, seq, tm):
    n, d = x.shape
    nchunk, _, fc = w1c.shape
    return pl.pallas_call(
        functools.partial(_ffn_kernel, row0, nchunk),
        grid=(n // tm,),
        in_specs=[
            pl.BlockSpec((tm, d), lambda i: (i, 0)),
            pl.BlockSpec((1, N_ADA, d), lambda i: (i * tm // seq, 0, 0)),
            _resident((1, d)),
            _resident((nchunk, d, fc)),
            _resident((nchunk, d, fc)),
            _resident((nchunk, fc, d)),
        ],
        out_specs=pl.BlockSpec((tm, d), lambda i: (i, 0)),
        out_shape=jax.ShapeDtypeStruct((n, d), F32),
        compiler_params=_params(("arbitrary",)),
        name="ffn",
    )(x, mod, g, w1c, w3c, w2c)


def _head_norm_rope(xs, xp, gc, gs, nheads, inv_dim, scale):
    outs = []
    for h in range(nheads):
        a = xs[:, h * LANE:(h + 1) * LANE]
        p = xp[:, h * LANE:(h + 1) * LANE]
        ss = jnp.sum(a * a, axis=-1, keepdims=True) * inv_dim
        s = lax.rsqrt(ss + EPS) * scale
        outs.append(s * (a * gc + p * gs))
    return outs


def _inproj_kernel(x_ref, mod_ref, g_ref, w_ref, wuq_ref, wukv_ref, vec_ref,
                   cm_ref, sm_ref, cd_ref, sd_ref, ci_ref, si_ref,
                   xr_ref, gr_ref, qm_ref, km_ref, vm_ref, qd_ref, kd_ref, vd_ref,
                   qi_ref, ki_ref, wi_ref, us_ref):
    x = x_ref[...]
    m = mod_ref[0]
    u = _rms_mod(x, g_ref[...], m[3:4], m[4:5]).astype(BF16)
    vec = vec_ref[...]

    def proj(lo, hi):
        return _dot(u, w_ref[:, lo:hi])

    xr_ref[...] = proj(_C_XR, _C_GR).astype(xr_ref.dtype)
    gr_ref[...] = proj(_C_GR, _C_QL).astype(gr_ref.dtype)

    ql = proj(_C_QL, _C_KVL)
    qn = (ql * lax.rsqrt(jnp.mean(ql * ql, axis=-1, keepdims=True) + EPS)
          * vec[0:1, :MLA_Q_LORA]).astype(BF16)
    q2 = _dot(qn, wuq_ref[...])
    kvl = proj(_C_KVL, _C_KPE)
    kvn = (kvl * lax.rsqrt(jnp.mean(kvl * kvl, axis=-1, keepdims=True) + EPS)
           * vec[1:2, :MLA_KV_LORA]).astype(BF16)
    kv = _dot(kvn, wukv_ref[...])
    kpe = proj(_C_KPE, _C_KPEP)
    kpep = proj(_C_KPEP, _C_QD)
    cm = cm_ref[...]
    sm = sm_ref[...]
    nslot = MLA_HEADS * LANE
    q_out = _head_norm_rope(q2[:, :nslot], q2[:, nslot:], cm * vec[2:3, :LANE], sm * vec[3:4, :LANE],
                            MLA_HEADS, 1.0 / MLA_QK, MLA_QK ** -0.5)
    gkc = cm * vec[4:5, :LANE]
    gks = sm * vec[5:6, :LANE]
    for h in range(MLA_HEADS):
        qm_ref[:, h * LANE:(h + 1) * LANE] = q_out[h].astype(qm_ref.dtype)
        ks = kv[:, h * LANE:(h + 1) * LANE] + kpe
        ss = jnp.sum(ks * ks, axis=-1, keepdims=True) * (1.0 / MLA_QK)
        s = lax.rsqrt(ss + EPS)
        km_ref[:, h * LANE:(h + 1) * LANE] = (s * (ks * gkc + kpep * gks)).astype(km_ref.dtype)
    vm_ref[...] = kv[:, nslot:].astype(vm_ref.dtype)

    cd = cd_ref[...]
    sd = sd_ref[...]
    qd = proj(_C_QD, _C_QDP)
    qdp = proj(_C_QDP, _C_KD)
    qd_out = _head_norm_rope(qd, qdp, cd * vec[6:7, :LANE], sd * vec[7:8, :LANE],
                             DSA_HEADS, 1.0 / DSA_HEAD_DIM, DSA_HEAD_DIM ** -0.5)
    for h in range(DSA_HEADS):
        qd_ref[:, h * LANE:(h + 1) * LANE] = qd_out[h].astype(qd_ref.dtype)
    kd = proj(_C_KD, _C_KDP)
    kdp = proj(_C_KDP, _C_VD)
    kd_out = _head_norm_rope(kd, kdp, cd * vec[8:9, :LANE], sd * vec[9:10, :LANE],
                             1, 1.0 / DSA_HEAD_DIM, 1.0)
    kd_ref[...] = kd_out[0].astype(kd_ref.dtype)
    vd_ref[...] = proj(_C_VD, _C_QI).astype(vd_ref.dtype)

    ci = jnp.concatenate([ci_ref[...]] * 2, axis=1)
    si = jnp.concatenate([si_ref[...]] * 2, axis=1)
    qi_ref[...] = (proj(_C_QI, _C_QIP) * ci + proj(_C_QIP, _C_KI) * si).astype(qi_ref.dtype)
    ki_ref[...] = (proj(_C_KI, _C_KIP) * ci + proj(_C_KIP, _C_WI) * si).astype(ki_ref.dtype)
    wi_ref[...] = proj(_C_WI, _C_US)

    us_ref[...] = proj(_C_US, _C_END).astype(us_ref.dtype)


def _inproj_call(x, mod, g, lw, tabs, seq, tm):
    n, d = x.shape
    row = lambda w: pl.BlockSpec((tm, w), lambda i: (i, 0))
    out_widths = [(D_RNN, BF16), (D_RNN, BF16), (MLA_HEADS * LANE, BF16), (MLA_HEADS * LANE, BF16),
                  (MLA_HEADS * MLA_V, BF16), (DSA_HEADS * LANE, BF16), (LANE, BF16), (LANE, BF16),
                  (IDX_HEADS * IDX_DIM, BF16), (IDX_HEADS * IDX_DIM, BF16), (LANE, F32), (D_S5, BF16)]
    return pl.pallas_call(
        _inproj_kernel,
        grid=(n // tm,),
        in_specs=[
            row(d),
            pl.BlockSpec((1, N_ADA, d), lambda i: (i * tm // seq, 0, 0)),
            _resident((1, d)),
            _resident(lw["w_all"].shape),
            _resident(lw["wuq"].shape),
            _resident(lw["wukv"].shape),
            _resident(lw["vec"].shape),
        ] + [row(LANE)] * 6,
        out_specs=[row(w) for w, _ in out_widths],
        out_shape=[jax.ShapeDtypeStruct((n, w), dt) for w, dt in out_widths],
        compiler_params=_params(("arbitrary",)),
        name="in_proj",
    )(x, mod, g, lw["w_all"], lw["wuq"], lw["wukv"], lw["vec"], *tabs)


def _rglru_kernel(tc, x_ref, gate_ref, cw_ref, vec_ref, wa_ref, wx_ref, o_ref, xs_ref, h_ref):
    j = pl.program_id(1)

    @pl.when(j == 0)
    def _():
        xs_ref[0:8, :] = jnp.zeros((8, D_RNN), F32)
        h_ref[...] = jnp.zeros(h_ref.shape, F32)

    xs_ref[8:8 + tc, :] = x_ref[...].astype(F32)
    cw = cw_ref[...]
    vec = vec_ref[...]
    xc = vec[0:1]
    for k in range(CONV_WIDTH):
        xc = xc + cw[k:k + 1] * xs_ref[pl.ds(8 - (CONV_WIDTH - 1) + k, tc), :]
    xs_ref[0:8, :] = xs_ref[tc:tc + 8, :]

    xb = xc.astype(BF16)
    r = _sigmoid(_dot(xb, wa_ref[...]) + vec[1:2])
    ig = _sigmoid(_dot(xb, wx_ref[...]) + vec[2:3])
    nl = -vec[3:4]
    softplus = jnp.maximum(nl, 0.0) + jnp.log(1.0 + jnp.exp(-jnp.abs(nl)))
    log_a = (-LRU_C) * r * softplus
    a = jnp.exp(log_a)
    z = 2.0 * log_a
    series = -z * (1.0 + z * (0.5 + z * (1.0 / 6.0 + z * (1.0 / 24.0 + z * (1.0 / 120.0 + z * (1.0 / 720.0))))))
    nem1 = jnp.where(z > -0.25, series, 1.0 - jnp.exp(z))
    b = jnp.sqrt(nem1) * ig * xc

    rowi = lax.broadcasted_iota(jnp.int32, (tc, D_RNN), 0)
    d = 1
    while d < tc:
        keep = rowi >= d
        a_s = jnp.where(keep, pltpu.roll(a, d, 0), 1.0)
        b_s = jnp.where(keep, pltpu.roll(b, d, 0), 0.0)
        b = a * b_s + b
        a = a * a_s
        d *= 2
    h = b + a * h_ref[...]
    h_ref[...] = h[tc - 1:tc, :]
    o_ref[...] = (h * _gelu_tanh(gate_ref[...].astype(F32))).astype(o_ref.dtype)


def _rglru_call(xr, gr, lw, bsz, seq, tc):
    n = xr.shape[0]
    nt = seq // tc
    row = pl.BlockSpec((tc, D_RNN), lambda b, j: (b * nt + j, 0))
    return pl.pallas_call(
        functools.partial(_rglru_kernel, tc),
        grid=(bsz, nt),
        in_specs=[row, row,
                  _resident((CONV_WIDTH, D_RNN)), _resident((4, D_RNN)),
                  _resident((D_RNN, D_RNN)), _resident((D_RNN, D_RNN))],
        out_specs=row,
        out_shape=jax.ShapeDtypeStruct((n, D_RNN), BF16),
        scratch_shapes=[pltpu.VMEM((tc + 8, D_RNN), F32), pltpu.VMEM((1, D_RNN), F32)],
        compiler_params=_params(("arbitrary", "arbitrary")),
        name="rglru",
    )(xr, gr, lw["conv_w"], lw["rg_vec"], lw["rg_wa"], lw["rg_wx"])


def _flash_kernel(bq, q_ref, k_ref, v_ref, o_ref, m_ref, l_ref, acc_ref):
    i = pl.program_id(2)
    j = pl.program_id(3)

    @pl.when(j == 0)
    def _():
        m_ref[...] = jnp.full(m_ref.shape, NEG_BIG, F32)
        l_ref[...] = jnp.zeros(l_ref.shape, F32)
        acc_ref[...] = jnp.zeros(acc_ref.shape, F32)

    def step(masked):
        v = v_ref[...]
        for hh in range(2):
            q = q_ref[:, hh * LANE:(hh + 1) * LANE]
            k = k_ref[:, hh * LANE:(hh + 1) * LANE]
            s = _dot_nt(q, k)
            if masked:
                rowi = lax.broadcasted_iota(jnp.int32, s.shape, 0)
                coli = lax.broadcasted_iota(jnp.int32, s.shape, 1)
                s = jnp.where(coli <= rowi, s, NEG_BIG)
            m_prev = m_ref[hh]
            m_new = jnp.maximum(m_prev, jnp.max(s, axis=1, keepdims=True))
            alpha = jnp.exp(m_prev - m_new)
            p = jnp.exp(s - m_new)
            l_ref[hh] = alpha * l_ref[hh] + jnp.sum(p, axis=1, keepdims=True)
            acc_ref[hh] = alpha * acc_ref[hh] + _dot(p.astype(BF16), v)
            m_ref[hh] = m_new

    @pl.when(j < i)
    def _():
        step(False)

    @pl.when(j == i)
    def _():
        step(True)
        o0 = acc_ref[0] / l_ref[0]
        o1 = acc_ref[1] / l_ref[1]
        lane = lax.broadcasted_iota(jnp.int32, o0.shape, 1)
        o_ref[...] = jnp.where(lane < MLA_V, o0, o1).astype(o_ref.dtype)


def _flash_call(q, k, v, bsz, seq, bq):
    n = q.shape[0]
    nq = seq // bq
    npair = MLA_HEADS // 2
    return pl.pallas_call(
        functools.partial(_flash_kernel, bq),
        grid=(bsz, npair, nq, nq),
        in_specs=[
            pl.BlockSpec((bq, 2 * LANE), lambda b, p, i, j: (b * nq + i, p)),
            pl.BlockSpec((bq, 2 * LANE), lambda b, p, i, j: (b * nq + jnp.minimum(j, i), p)),
            pl.BlockSpec((bq, 2 * MLA_V), lambda b, p, i, j: (b * nq + jnp.minimum(j, i), p)),
        ],
        out_specs=pl.BlockSpec((bq, 2 * MLA_V), lambda b, p, i, j: (b * nq + i, p)),
        out_shape=jax.ShapeDtypeStruct((n, MLA_HEADS * MLA_V), BF16),
        scratch_shapes=[pltpu.VMEM((2, bq, 1), F32), pltpu.VMEM((2, bq, 1), F32),
                        pltpu.VMEM((2, bq, 2 * MLA_V), F32)],
        compiler_params=_params(("arbitrary",) * 4),
        name="mla_flash",
    )(q, k, v)


_DSA_QB = 128
_INT_MIN = -2 ** 31


def _dsa_body(ns, topk, i, q_ref, k_ref, v_ref, qi_ref, ki_ref, w_ref, tri_ref, o_ref):
    qb = _DSA_QB
    qi = qi_ref[...]
    ki = ki_ref[0:ns, :]
    w = w_ref[...]
    head_of_lane = jnp.right_shift(lax.broadcasted_iota(jnp.int32, qi.shape, 1), int(math.log2(IDX_DIM)))
    score = jnp.zeros((qb, ns), F32)
    for h in range(IDX_HEADS):
        lhs = jnp.where(head_of_lane == h, qi, jnp.zeros_like(qi))
        score = score + w[:, h:h + 1] * jnp.maximum(_dot_nt(lhs, ki), 0.0)

    qpos = lax.broadcasted_iota(jnp.int32, (qb, ns), 0) + i * qb
    kpos = lax.broadcasted_iota(jnp.int32, (qb, ns), 1)
    bits = pltpu.bitcast(score + 0.0, jnp.int32)
    key = jnp.where(bits < 0, bits ^ jnp.int32(0x7FFFFFFF), bits)
    key = jnp.where(kpos <= qpos, key, jnp.int32(_INT_MIN))
    kk = jnp.minimum(lax.broadcasted_iota(jnp.int32, (qb, 1), 0) + (i * qb + 1), topk).astype(F32)

    def count_ge(c):
        return jnp.sum(jnp.where(key >= c, 1.0, 0.0), axis=1, keepdims=True)

    ans0 = jnp.where(count_ge(jnp.zeros((qb, 1), jnp.int32)) >= kk, jnp.int32(0), jnp.int32(_INT_MIN))

    def body(it, ans):
        c = ans + lax.shift_left(jnp.int32(1), jnp.int32(30) - it)
        return jnp.where(count_ge(c) >= kk, c, ans)

    thr = lax.fori_loop(0, 31, body, ans0)

    need = kk - jnp.sum(jnp.where(key > thr, 1.0, 0.0), axis=1, keepdims=True)
    tri = tri_ref[...]
    run = jnp.zeros((qb, 1), F32)
    bias_chunks = []
    for c in range(ns // LANE):
        kc = key[:, c * LANE:(c + 1) * LANE]
        e = jnp.where(kc == thr, 1.0, 0.0)
        before = _dot(e.astype(BF16), tri) + run
        take = jnp.where(kc > thr, 1.0, jnp.where(before < need, e, 0.0))
        bias_chunks.append(jnp.where(take > 0.5, 0.0, NEG_BIG))
        run = run + jnp.sum(e, axis=1, keepdims=True)
    bias = jnp.concatenate(bias_chunks, axis=1)

    k = k_ref[0:ns, :]
    v = v_ref[0:ns, :]
    lane = lax.broadcasted_iota(jnp.int32, (qb, LANE), 1)
    prev = None
    for h in range(DSA_HEADS):
        q = q_ref[:, h * LANE:(h + 1) * LANE]
        s = _dot_nt(q, k) + bias
        m = jnp.max(s, axis=1, keepdims=True)
        p = jnp.exp(s - m)
        l = jnp.sum(p, axis=1, keepdims=True)
        o = _dot(p.astype(BF16), v) / l
        if h % 2 == 0:
            prev = o
        else:
            o_ref[:, (h // 2) * LANE:(h // 2 + 1) * LANE] = jnp.where(
                lane < DSA_HEAD_DIM, prev, o).astype(o_ref.dtype)


def _dsa_kernel(seq, bucket, topk, q_ref, k_ref, v_ref, qi_ref, ki_ref, w_ref, tri_ref, o_ref):
    i = pl.program_id(1)
    for bk in range(seq // bucket):
        @pl.when((i * _DSA_QB) // bucket == bk)
        def _(bk=bk):
            _dsa_body((bk + 1) * bucket, topk, i, q_ref, k_ref, v_ref, qi_ref, ki_ref, w_ref,
                      tri_ref, o_ref)


def _dsa_call(qd, kd, vd, qi, ki, wi, tri, bsz, seq):
    n = qd.shape[0]
    qb = _DSA_QB
    nq = seq // qb
    bucket = min(512, seq)
    topk = min(TOPK_MAX, seq // 4)
    qrow = lambda w: pl.BlockSpec((qb, w), lambda b, i: (b * nq + i, 0))
    full = lambda w: pl.BlockSpec((seq, w), lambda b, i: (b, 0))
    return pl.pallas_call(
        functools.partial(_dsa_kernel, seq, bucket, topk),
        grid=(bsz, nq),
        in_specs=[qrow(DSA_HEADS * LANE), full(LANE), full(LANE),
                  qrow(IDX_HEADS * IDX_DIM), full(IDX_HEADS * IDX_DIM), qrow(LANE),
                  _resident((LANE, LANE))],
        out_specs=qrow(DSA_HEADS * DSA_HEAD_DIM),
        out_shape=jax.ShapeDtypeStruct((n, DSA_HEADS * DSA_HEAD_DIM), BF16),
        compiler_params=_params(("arbitrary", "arbitrary")),
        name="dsa",
    )(qd, kd, vd, qi, ki, wi, tri)


def _s5_kernel(rows, u_ref, w_ref, lam_ref, d_ref, o_ref):
    cw = S5_CHUNK * LANE
    half = S5_GB * S5_STATE
    rowi = lax.broadcasted_iota(jnp.int32, (rows, half), 0)
    for gb in range(S5_NGB):
        u = jnp.concatenate(
            [u_ref[:, a * D_S5 + gb * LANE:a * D_S5 + (gb + 1) * LANE] for a in range(S5_CHUNK)], axis=1)
        y = _dot(u, w_ref[gb, :, 0:cw])
        st = _dot(u, w_ref[gb, :, cw:2 * cw])
        xr = st[:, :half]
        xi = st[:, half:]
        lam = lam_ref[gb]
        lr = lam[0:1]
        li = lam[1:2]
        d = 1
        while d < rows:
            keep = rowi >= d
            sr = jnp.where(keep, pltpu.roll(xr, d, 0), 0.0)
            si = jnp.where(keep, pltpu.roll(xi, d, 0), 0.0)
            xr, xi = xr + lr * sr - li * si, xi + lr * si + li * sr
            lr, li = lr * lr - li * li, 2.0 * lr * li
            d *= 2
        keep = rowi >= 1
        pr = jnp.where(keep, pltpu.roll(xr, 1, 0), 0.0)
        pi = jnp.where(keep, pltpu.roll(xi, 1, 0), 0.0)
        xp = jnp.concatenate([pr, pi], axis=1).astype(BF16)
        y = y + _dot(xp, w_ref[gb, :, 2 * cw:3 * cw]) + d_ref[gb] * u.astype(F32)
        yg = _gelu_tanh(y).astype(o_ref.dtype)
        for a in range(S5_CHUNK):
            o_ref[:, a * D_S5 + gb * LANE:a * D_S5 + (gb + 1) * LANE] = yg[:, a * LANE:(a + 1) * LANE]


def _s5_call(us, lw, bsz, seq):
    n = us.shape[0]
    rows = seq // S5_CHUNK
    width = S5_CHUNK * D_S5
    blk = pl.BlockSpec((rows, width), lambda b: (b, 0))
    out = pl.pallas_call(
        functools.partial(_s5_kernel, rows),
        grid=(bsz,),
        in_specs=[blk, _resident(lw["s5_w"].shape), _resident(lw["s5_lam"].shape),
                  _resident(lw["s5_d"].shape)],
        out_specs=blk,
        out_shape=jax.ShapeDtypeStruct((n // S5_CHUNK, width), BF16),
        compiler_params=_params(("arbitrary",)),
        name="s5",
    )(us.reshape(n // S5_CHUNK, width), lw["s5_w"], lw["s5_lam"], lw["s5_d"])
    return out.reshape(n, D_S5)


def _merge_kernel(x_ref, mod_ref, g_ref, ya_ref, yb_ref, yc_ref, yd_ref,
                  wg_ref, wb_ref, wglu_ref, bglu_ref, wout_ref, o_ref):
    x = x_ref[...]
    m = mod_ref[0]
    d = x.shape[1]
    u = _rms_mod(x, g_ref[...], m[3:4], m[4:5]).astype(BF16)
    yd = yd_ref[...]
    ydg = (yd.astype(F32) * _sigmoid(_dot(yd, wglu_ref[...]) + bglu_ref[...])).astype(BF16)
    ys = (ya_ref[...], yb_ref[...], yc_ref[...], ydg)
    merged = jnp.zeros(x.shape, F32)
    for nb in range(N_BRANCH):
        gate = _sigmoid(_dot(u, wg_ref[:, nb * d:(nb + 1) * d]))
        merged = merged + gate * _dot(ys[nb], wb_ref[nb])
    o_ref[...] = x + (1.0 + m[5:6]) * _dot(merged.astype(BF16), wout_ref[...])


def _merge_call(x, mod, g, ya, yb, yc, yd, lw, seq, tm):
    n, d = x.shape
    row = lambda w: pl.BlockSpec((tm, w), lambda i: (i, 0))
    return pl.pallas_call(
        _merge_kernel,
        grid=(n // tm,),
        in_specs=[row(d), pl.BlockSpec((1, N_ADA, d), lambda i: (i * tm // seq, 0, 0)), _resident((1, d)),
                  row(BRANCH_W), row(BRANCH_W), row(BRANCH_W), row(BRANCH_W),
                  _resident((d, N_BRANCH * d)), _resident((N_BRANCH, BRANCH_W, d)),
                  _resident((D_S5, D_S5)), _resident((1, D_S5)), _resident((d, d))],
        out_specs=row(d),
        out_shape=jax.ShapeDtypeStruct((n, d), F32),
        compiler_params=_params(("arbitrary",)),
        name="merge",
    )(x, mod, g, ya, yb, yc, yd, lw["w_gate"], lw["w_branch"], lw["w_glu"], lw["b_glu"], lw["w_out"])


def _partner_cols(w, nheads, hdim, rot):
    kdim = w.shape[0]
    wh = w.reshape(kdim, nheads, hdim)
    half = rot // 2
    p = jnp.concatenate([-wh[..., half:rot], wh[..., :half],
                         jnp.zeros((kdim, nheads, hdim - rot), w.dtype)], axis=-1)
    return p.reshape(kdim, nheads * hdim)


def _to_slots(w, nheads, hdim):
    kdim = w.shape[0]
    wh = w.reshape(kdim, nheads, hdim)
    wh = jnp.pad(wh, ((0, 0), (0, 0), (0, LANE - hdim)))
    return wh.reshape(kdim, nheads * LANE)


def _pad_lane(v, width=LANE):
    return jnp.pad(v, [(0, 0)] * (v.ndim - 1) + [(0, width - v.shape[-1])])


def _partner_vec(g, rot):
    half = rot // 2
    return jnp.concatenate([g[half:rot], g[:half], jnp.zeros((g.shape[0] - rot,), g.dtype)])


def _rope_table(positions, rot, period):
    inv = ROPE_THETA ** (-jnp.arange(0, rot, 2, dtype=F32) / rot)
    ang = positions.astype(F32)[..., None] * inv
    cos, sin = jnp.cos(ang), jnp.sin(ang)
    one = jnp.ones(ang.shape[:-1] + (period - rot,), F32)
    ct = jnp.concatenate([cos, cos, one], axis=-1)
    st = jnp.concatenate([sin, sin, 0.0 * one], axis=-1)
    reps = LANE // period
    ct = jnp.tile(ct, (1, 1, reps)).reshape(-1, LANE)
    st = jnp.tile(st, (1, 1, reps)).reshape(-1, LANE)
    return ct, st


def _s5_weights(lam_re, lam_im, log_dt, b_re, b_im, c_re, c_im, dvec):
    hp = lax.Precision.HIGHEST
    g, p = lam_re.shape
    ch = S5_CHUNK
    dt = jnp.exp(log_dt)[:, None]
    mag = jnp.exp(lam_re * dt)
    ar, ai = mag * jnp.cos(lam_im * dt), mag * jnp.sin(lam_im * dt)
    den = lam_re * lam_re + lam_im * lam_im
    nr, ni = ar - 1.0, ai
    f_re = (nr * lam_re + ni * lam_im) / den
    f_im = (ni * lam_re - nr * lam_im) / den
    bb_re = f_re[..., None] * b_re - f_im[..., None] * b_im
    bb_im = f_re[..., None] * b_im + f_im[..., None] * b_re
    pr, pi = [jnp.ones_like(ar)], [jnp.zeros_like(ar)]
    for _ in range(ch):
        pr, pi = pr + [pr[-1] * ar - pi[-1] * ai], pi + [pr[-1] * ai + pi[-1] * ar]
    pr, pi = jnp.stack(pr), jnp.stack(pi)
    mr = pr[..., None] * bb_re - pi[..., None] * bb_im
    mi = pr[..., None] * bb_im + pi[..., None] * bb_re
    kern = (jnp.einsum('gjp,tgpi->tgji', c_re, mr, precision=hp)
            - jnp.einsum('gjp,tgpi->tgji', c_im, mi, precision=hp))
    eye = jnp.eye(S5_GB, dtype=F32)
    a_in = jnp.arange(ch)[:, None]
    a_out = jnp.arange(ch)[None, :]
    lag = jnp.clip(a_out - a_in, 0, ch)
    toe = kern[lag] * (a_out >= a_in)[..., None, None, None].astype(F32)
    toe = toe.reshape(ch, ch, S5_NGB, S5_GB, S5_GROUP, S5_GROUP)
    w_toe = jnp.einsum('xyngji,gh->nxgiyhj', toe, eye).reshape(S5_NGB, ch * LANE, ch * LANE)
    rev = jnp.arange(ch - 1, -1, -1)
    ms = jnp.stack([mr[rev], mi[rev]], axis=1)
    ms = ms.reshape(ch, 2, S5_NGB, S5_GB, p, S5_GROUP)
    w_st = jnp.einsum('acngpi,gh->nagichp', ms, eye).reshape(S5_NGB, ch * LANE, 2 * S5_GB * p)
    pr1, pi1 = pr[1:], pi[1:]
    co_re = c_re[None] * pr1[:, :, None, :] - c_im[None] * pi1[:, :, None, :]
    co_im = -(c_re[None] * pi1[:, :, None, :] + c_im[None] * pr1[:, :, None, :])
    co = jnp.stack([co_re, co_im], axis=0).reshape(2, ch, S5_NGB, S5_GB, S5_GROUP, p)
    w_out = jnp.einsum('cyngjp,gh->ncgpyhj', co, eye).reshape(S5_NGB, 2 * S5_GB * p, ch * LANE)
    w_all = jnp.concatenate([w_toe, w_st, w_out], axis=2).astype(BF16)
    lam_c = jnp.stack([pr[ch].reshape(S5_NGB, S5_GB * p), pi[ch].reshape(S5_NGB, S5_GB * p)], axis=1)
    d_t = jnp.tile(dvec.reshape(S5_NGB, 1, LANE), (1, 1, ch))
    return w_all, lam_c, d_t


def _block_diag(w):
    h, a, b = w.shape
    eye = jnp.eye(h, dtype=w.dtype)
    return jnp.einsum('hij,hg->higj', w, eye).reshape(h * a, h * b)


def _prep_layer(l, p):
    w_in = p["w_in"][l]
    offs = [0]
    for s in IN_SPLITS:
        offs.append(offs[-1] + s)
    seg = [w_in[:, offs[k]:offs[k + 1]] for k in range(len(IN_SPLITS))]
    (w_xr, w_gr, w_ql, w_kvl, w_kpe, w_qd, w_kd, w_vd, w_qi, w_ki, w_wi, w_us, w_gate) = seg
    rep = lambda w: jnp.tile(w, (1, IDX_HEADS))
    cols = [
        w_xr, w_gr, w_ql, w_kvl,
        _pad_lane(w_kpe), _pad_lane(_partner_cols(w_kpe, 1, MLA_ROPE, MLA_ROPE)),
        _to_slots(w_qd, DSA_HEADS, DSA_HEAD_DIM),
        _to_slots(_partner_cols(w_qd, DSA_HEADS, DSA_HEAD_DIM, DSA_ROT), DSA_HEADS, DSA_HEAD_DIM),
        _pad_lane(w_kd), _pad_lane(_partner_cols(w_kd, 1, DSA_HEAD_DIM, DSA_ROT)),
        jnp.concatenate([w_vd, w_vd], axis=1),
        w_qi, _partner_cols(w_qi, IDX_HEADS, IDX_DIM, IDX_ROT),
        rep(w_ki), rep(_partner_cols(w_ki, 1, IDX_DIM, IDX_ROT)),
        _pad_lane(w_wi), w_us,
    ]
    w_all = jnp.concatenate(cols, axis=1).astype(BF16)
    assert w_all.shape[1] == _C_END

    w_uq = p["mla_w_uq"][l]
    wuq = jnp.concatenate([_to_slots(w_uq, MLA_HEADS, MLA_QK),
                           _to_slots(_partner_cols(w_uq, MLA_HEADS, MLA_QK, MLA_ROPE), MLA_HEADS, MLA_QK)],
                          axis=1).astype(BF16)
    wkv = p["mla_w_ukv"][l].reshape(MLA_KV_LORA, MLA_HEADS, MLA_NOPE + MLA_V)
    kn = jnp.pad(wkv[..., :MLA_NOPE], ((0, 0), (0, 0), (MLA_ROPE, LANE - MLA_QK)))
    wukv = jnp.concatenate([kn.reshape(MLA_KV_LORA, MLA_HEADS * LANE),
                            wkv[..., MLA_NOPE:].reshape(MLA_KV_LORA, MLA_HEADS * MLA_V)], axis=1).astype(BF16)

    gq, gk = p["mla_qk_gain"][l, 0], p["mla_qk_gain"][l, 1]
    dq, dk = p["dsa_qk_gain"][l, 0], p["dsa_qk_gain"][l, 1]
    vec_rows = [p["mla_q_norm"][l], p["mla_kv_norm"][l],
                gq, _partner_vec(gq, MLA_ROPE), gk, _partner_vec(gk, MLA_ROPE),
                dq, _partner_vec(dq, DSA_ROT), dk, _partner_vec(dk, DSA_ROT)]
    vec = jnp.stack([_pad_lane(v, MLA_Q_LORA) for v in vec_rows])

    s5_w, s5_lam, s5_d = _s5_weights(p["s5_lambda_re"][l], p["s5_lambda_im"][l], p["s5_log_dt"][l],
                                     p["s5_b_re"][l], p["s5_b_im"][l], p["s5_c_re"][l], p["s5_c_im"][l],
                                     p["s5_d"][l])

    nchunk = D_FF // 256
    ffn = []
    for j in range(2):
        w1 = p["ffn_w1"][l, j].astype(BF16).reshape(D_MODEL, nchunk, 256).transpose(1, 0, 2)
        w3 = p["ffn_w3"][l, j].astype(BF16).reshape(D_MODEL, nchunk, 256).transpose(1, 0, 2)
        w2 = p["ffn_w2"][l, j].astype(BF16).reshape(nchunk, 256, D_MODEL)
        ffn.append((w1, w3, w2))

    return dict(
        w_all=w_all, wuq=wuq, wukv=wukv, vec=vec,
        conv_w=p["conv_w"][l],
        rg_vec=jnp.stack([p["conv_b"][l], p["rg_ba"][l], p["rg_bx"][l], p["rg_lambda"][l]]),
        rg_wa=_block_diag(p["rg_wa"][l]).astype(BF16),
        rg_wx=_block_diag(p["rg_wx"][l]).astype(BF16),
        s5_w=s5_w, s5_lam=s5_lam, s5_d=s5_d,
        w_gate=w_gate.astype(BF16),
        w_branch=p["w_branch"][l].astype(BF16),
        w_glu=p["s5_w_glu"][l].astype(BF16),
        b_glu=p["s5_b_glu"][l].reshape(1, D_S5),
        w_out=p["w_out"][l].astype(BF16),
        ffn=ffn,
        norm_g=p["norm_g"][l],
    )


def _pick_tile(n, pref):
    t = pref
    while n % t:
        t //= 2
    return t


def kernel(x, c, positions, ada_w, ada_b, norm_g, ffn_w1, ffn_w3, ffn_w2, w_in,
           conv_w, conv_b, rg_wa, rg_ba, rg_wx, rg_bx, rg_lambda,
           mla_q_norm, mla_w_uq, mla_kv_norm, mla_w_ukv, mla_qk_gain, dsa_qk_gain,
           s5_lambda_re, s5_lambda_im, s5_log_dt, s5_b_re, s5_b_im, s5_c_re, s5_c_im,
           s5_d, s5_w_glu, s5_b_glu, w_branch, w_out):
    p = dict(norm_g=norm_g, ffn_w1=ffn_w1, ffn_w3=ffn_w3, ffn_w2=ffn_w2, w_in=w_in,
             conv_w=conv_w, conv_b=conv_b, rg_wa=rg_wa, rg_ba=rg_ba, rg_wx=rg_wx, rg_bx=rg_bx,
             rg_lambda=rg_lambda, mla_q_norm=mla_q_norm, mla_w_uq=mla_w_uq, mla_kv_norm=mla_kv_norm,
             mla_w_ukv=mla_w_ukv, mla_qk_gain=mla_qk_gain, dsa_qk_gain=dsa_qk_gain,
             s5_lambda_re=s5_lambda_re, s5_lambda_im=s5_lambda_im, s5_log_dt=s5_log_dt,
             s5_b_re=s5_b_re, s5_b_im=s5_b_im, s5_c_re=s5_c_re, s5_c_im=s5_c_im, s5_d=s5_d,
             s5_w_glu=s5_w_glu, s5_b_glu=s5_b_glu, w_branch=w_branch, w_out=w_out)
    bsz, seq, d = x.shape
    n = bsz * seq
    depth = ada_w.shape[0]
    assert seq % (S5_CHUNK * 8) == 0 and seq % _DSA_QB == 0

    tm = _pick_tile(seq, 512)
    tc = _pick_tile(seq, 256)
    bq = _pick_tile(seq, 512)

    mod = _ada_call(c, ada_w, ada_b)
    tabs = (_rope_table(positions, MLA_ROPE, LANE) + _rope_table(positions, DSA_ROT, LANE)
            + _rope_table(positions, IDX_ROT, IDX_DIM))
    tri = (jnp.arange(LANE)[:, None] < jnp.arange(LANE)[None, :]).astype(BF16)

    xf = x.reshape(n, d)
    for l in range(depth):
        lw = _prep_layer(l, p)
        g = lw["norm_g"]
        ml = mod[l]
        xf = _ffn_call(xf, ml, g[0:1], *lw["ffn"][0], 0, seq, tm)
        (xr, gr, qm, km, vm, qd, kd, vd, qi, ki, wi, us) = _inproj_call(xf, ml, g[1:2], lw, tabs, seq, tm)
        ya = _rglru_call(xr, gr, lw, bsz, seq, tc)
        yb = _flash_call(qm, km, vm, bsz, seq, bq)
        yc = _dsa_call(qd, kd, vd, qi, ki, wi, tri, bsz, seq)
        yd = _s5_call(us, lw, bsz, seq)
        xf = _merge_call(xf, ml, g[1:2], ya, yb, yc, yd, lw, seq, tm)
        xf = _ffn_call(xf, ml, g[2:3], *lw["ffn"][1], 6, seq, tm)
    return xf.reshape(bsz, seq, d)
```

```python
import functools
import math

import jax
import jax.numpy as jnp
from jax import lax
from jax.experimental import pallas as pl
from jax.experimental.pallas import tpu as pltpu

F32 = jnp.float32
BF16 = jnp.bfloat16

D_MODEL = 1024
EPS = 1e-6
ROPE_THETA = 500000.0
D_FF = 2816
N_ADA = 9

D_RNN = 512
RNN_HEADS = 8
RNN_HEAD_DIM = D_RNN // RNN_HEADS
CONV_WIDTH = 4
LRU_C = 8.0

MLA_HEADS = 8
MLA_NOPE = 64
MLA_ROPE = 32
MLA_V = 64
MLA_QK = MLA_ROPE + MLA_NOPE
MLA_Q_LORA = 256
MLA_KV_LORA = 128

DSA_HEADS = 8
DSA_HEAD_DIM = 64
DSA_ROT = DSA_HEAD_DIM // 4
IDX_HEADS = 8
IDX_DIM = 32
IDX_ROT = IDX_DIM // 4
TOPK_MAX = 256

S5_GROUP = 16
S5_GROUPS = 32
D_S5 = S5_GROUP * S5_GROUPS
S5_STATE = 64
S5_CHUNK = 8
S5_GB = 8
S5_NGB = S5_GROUPS // S5_GB

N_BRANCH = 4
BRANCH_W = 512
IN_SPLITS = (D_RNN, D_RNN, MLA_Q_LORA, MLA_KV_LORA, MLA_ROPE,
             DSA_HEADS * DSA_HEAD_DIM, DSA_HEAD_DIM, DSA_HEAD_DIM,
             IDX_HEADS * IDX_DIM, IDX_DIM, IDX_HEADS, D_S5, N_BRANCH * D_MODEL)

LANE = 128
NEG_BIG = -1e30
VMEM_LIMIT = 56 * 1024 * 1024

_C_XR = 0
_C_GR = _C_XR + D_RNN
_C_QL = _C_GR + D_RNN
_C_KVL = _C_QL + MLA_Q_LORA
_C_KPE = _C_KVL + MLA_KV_LORA
_C_KPEP = _C_KPE + LANE
_C_QD = _C_KPEP + LANE
_C_QDP = _C_QD + DSA_HEADS * LANE
_C_KD = _C_QDP + DSA_HEADS * LANE
_C_KDP = _C_KD + LANE
_C_VD = _C_KDP + LANE
_C_QI = _C_VD + LANE
_C_QIP = _C_QI + IDX_HEADS * IDX_DIM
_C_KI = _C_QIP + IDX_HEADS * IDX_DIM
_C_KIP = _C_KI + IDX_HEADS * IDX_DIM
_C_WI = _C_KIP + IDX_HEADS * IDX_DIM
_C_US = _C_WI + LANE
_C_END = _C_US + D_S5


def _dot(a, b):
    return jnp.dot(a, b, preferred_element_type=F32)


def _dot_nt(a, b):
    return lax.dot_general(a, b, (((1,), (1,)), ((), ())), preferred_element_type=F32)


def _sigmoid(x):
    return jax.nn.sigmoid(x)


def _gelu_tanh(x):
    return 0.5 * x * (1.0 + jnp.tanh(0.7978845608028654 * (x + 0.044715 * (x * x * x))))


def _rms_mod(x, g, shift, scale):
    ms = jnp.mean(x * x, axis=-1, keepdims=True)
    y = x * lax.rsqrt(ms + EPS) * g
    return y * (1.0 + scale) + shift


def _resident(shape):
    return pl.BlockSpec(shape, lambda *_: (0,) * len(shape), pipeline_mode=pl.Buffered(1))


def _params(sem):
    return pltpu.CompilerParams(dimension_semantics=sem, vmem_limit_bytes=VMEM_LIMIT)


def _ada_kernel(c_ref, w_ref, b_ref, o_ref):
    c = c_ref[...]
    a = c * _sigmoid(c)
    w = w_ref[0]
    a_hi = a.astype(BF16)
    a_lo = (a - a_hi.astype(F32)).astype(BF16)
    w_hi = w.astype(BF16)
    w_lo = (w - w_hi.astype(F32)).astype(BF16)
    o_ref[0] = _dot(a_hi, w_hi) + _dot(a_lo, w_hi) + _dot(a_hi, w_lo) + b_ref[0]


def _ada_call(c, ada_w, ada_b):
    nl, d, n9 = ada_w.shape
    b = c.shape[0]
    tn = 1024
    out = pl.pallas_call(
        _ada_kernel,
        grid=(nl, n9 // tn),
        in_specs=[
            pl.BlockSpec((b, d), lambda l, j: (0, 0)),
            pl.BlockSpec((1, d, tn), lambda l, j: (l, 0, j)),
            pl.BlockSpec((1, 1, tn), lambda l, j: (l, 0, j)),
        ],
        out_specs=pl.BlockSpec((1, b, tn), lambda l, j: (l, 0, j)),
        out_shape=jax.ShapeDtypeStruct((nl, b, n9), F32),
        compiler_params=_params(("arbitrary", "arbitrary")),
        name="ada_mod",
    )(c, ada_w, ada_b.reshape(nl, 1, n9))
    return out.reshape(nl, b, N_ADA, d)


_FFN_CHUNK = 256


def _ffn_kernel(row0, x_ref, mod_ref, g_ref, w1_ref, w3_ref, w2_ref, o_ref, h_ref):
    x = x_ref[...]
    m = mod_ref[0]
    u = _rms_mod(x, g_ref[...], m[row0:row0 + 1], m[row0 + 1:row0 + 2]).astype(BF16)
    for c in range(0, w1_ref.shape[1], _FFN_CHUNK):
        h1 = _dot(u, w1_ref[:, c:c + _FFN_CHUNK])
        h3 = _dot(u, w3_ref[:, c:c + _FFN_CHUNK])
        h_ref[:, c:c + _FFN_CHUNK] = (h1 * _sigmoid(h1) * h3).astype(BF16)
    o_ref[...] = x + 0.5 * (1.0 + m[row0 + 2:row0 + 3]) * _dot(h_ref[...], w2_ref[...])


def _ffn_call(x, mod, g, w1, w3, w2, row0, seq, tm):
    n, d = x.shape
    f = w1.shape[1]
    return pl.pallas_call(
        functools.partial(_ffn_kernel, row0),
        grid=(n // tm,),
        in_specs=[
            pl.BlockSpec((tm, d), lambda i: (i, 0)),
            pl.BlockSpec((1, N_ADA, d), lambda i: (i * tm // seq, 0, 0)),
            _resident((1, d)),
            _resident((d, f)),
            _resident((d, f)),
            _resident((f, d)),
        ],
        out_specs=pl.BlockSpec((tm, d), lambda i: (i, 0)),
        out_shape=jax.ShapeDtypeStruct((n, d), F32),
        scratch_shapes=[pltpu.VMEM((tm, f), BF16)],
        compiler_params=_params(("arbitrary",)),
        name="ffn",
    )(x, mod, g, w1, w3, w2)


def _head_norm_rope(xs, xp, gc, gs, nheads, inv_dim, scale):
    outs = []
    for h in range(nheads):
        a = xs[:, h * LANE:(h + 1) * LANE]
        p = xp[:, h * LANE:(h + 1) * LANE]
        ss = jnp.sum(a * a, axis=-1, keepdims=True) * inv_dim
        s = lax.rsqrt(ss + EPS) * scale
        outs.append(s * (a * gc + p * gs))
    return outs


def _inproj_kernel(x_ref, mod_ref, g_ref, w_ref, wuq_ref, wukv_ref, vec_ref,
                   cm_ref, sm_ref, cd_ref, sd_ref, ci_ref, si_ref,
                   xr_ref, gr_ref, qm_ref, km_ref, vm_ref, qd_ref, kd_ref, vd_ref,
                   qi_ref, ki_ref, wi_ref, us_ref):
    x = x_ref[...]
    m = mod_ref[0]
    u = _rms_mod(x, g_ref[...], m[3:4], m[4:5]).astype(BF16)
    vec = vec_ref[...]

    def proj(lo, hi):
        return _dot(u, w_ref[:, lo:hi])

    xr_ref[...] = proj(_C_XR, _C_GR).astype(xr_ref.dtype)
    gr_ref[...] = proj(_C_GR, _C_QL).astype(gr_ref.dtype)

    ql = proj(_C_QL, _C_KVL)
    qn = (ql * lax.rsqrt(jnp.mean(ql * ql, axis=-1, keepdims=True) + EPS)
          * vec[0:1, :MLA_Q_LORA]).astype(BF16)
    q2 = _dot(qn, wuq_ref[...])
    kvl = proj(_C_KVL, _C_KPE)
    kvn = (kvl * lax.rsqrt(jnp.mean(kvl * kvl, axis=-1, keepdims=True) + EPS)
           * vec[1:2, :MLA_KV_LORA]).astype(BF16)
    kv = _dot(kvn, wukv_ref[...])
    kpe = proj(_C_KPE, _C_KPEP)
    kpep = proj(_C_KPEP, _C_QD)
    cm = cm_ref[...]
    sm = sm_ref[...]
    nslot = MLA_HEADS * LANE
    q_out = _head_norm_rope(q2[:, :nslot], q2[:, nslot:], cm * vec[2:3, :LANE], sm * vec[3:4, :LANE],
                            MLA_HEADS, 1.0 / MLA_QK, MLA_QK ** -0.5)
    gkc = cm * vec[4:5, :LANE]
    gks = sm * vec[5:6, :LANE]
    for h in range(MLA_HEADS):
        qm_ref[:, h * LANE:(h + 1) * LANE] = q_out[h].astype(qm_ref.dtype)
        ks = kv[:, h * LANE:(h + 1) * LANE] + kpe
        ss = jnp.sum(ks * ks, axis=-1, keepdims=True) * (1.0 / MLA_QK)
        s = lax.rsqrt(ss + EPS)
        km_ref[:, h * LANE:(h + 1) * LANE] = (s * (ks * gkc + kpep * gks)).astype(km_ref.dtype)
    vlane = lax.broadcasted_iota(jnp.int32, (1, nslot), 1)
    v_is_low = (vlane // LANE) % 2 == 0
    ones_half = jnp.where(((vlane % LANE) < MLA_V) == v_is_low, 0.0, 1.0)
    vm_ref[...] = (kv[:, nslot:] + ones_half).astype(vm_ref.dtype)

    cd = cd_ref[...]
    sd = sd_ref[...]
    qd = proj(_C_QD, _C_QDP)
    qdp = proj(_C_QDP, _C_KD)
    qd_out = _head_norm_rope(qd, qdp, cd * vec[6:7, :LANE], sd * vec[7:8, :LANE],
                             DSA_HEADS, 1.0 / DSA_HEAD_DIM, DSA_HEAD_DIM ** -0.5)
    for h in range(DSA_HEADS):
        qd_ref[:, h * LANE:(h + 1) * LANE] = qd_out[h].astype(qd_ref.dtype)
    kd = proj(_C_KD, _C_KDP)
    kdp = proj(_C_KDP, _C_VD)
    kd_out = _head_norm_rope(kd, kdp, cd * vec[8:9, :LANE], sd * vec[9:10, :LANE],
                             1, 1.0 / DSA_HEAD_DIM, 1.0)
    kd_ref[...] = kd_out[0].astype(kd_ref.dtype)
    dlane = lax.broadcasted_iota(jnp.int32, (1, LANE), 1)
    vd_ref[...] = (proj(_C_VD, _C_QI) + jnp.where(dlane < DSA_HEAD_DIM, 0.0, 1.0)).astype(vd_ref.dtype)

    ci = jnp.concatenate([ci_ref[...]] * 2, axis=1)
    si = jnp.concatenate([si_ref[...]] * 2, axis=1)
    qi_ref[...] = (proj(_C_QI, _C_QIP) * ci + proj(_C_QIP, _C_KI) * si).astype(qi_ref.dtype)
    ki_ref[...] = (proj(_C_KI, _C_KIP) * ci + proj(_C_KIP, _C_WI) * si).astype(ki_ref.dtype)
    wi_ref[...] = proj(_C_WI, _C_US)

    us_ref[...] = proj(_C_US, _C_END).astype(us_ref.dtype)


def _inproj_call(x, mod, g, lw, tabs, seq, tm):
    n, d = x.shape
    row = lambda w: pl.BlockSpec((tm, w), lambda i: (i, 0))
    out_widths = [(D_RNN, BF16), (D_RNN, BF16), (MLA_HEADS * LANE, BF16), (MLA_HEADS * LANE, BF16),
                  (MLA_HEADS * LANE, BF16), (DSA_HEADS * LANE, BF16), (LANE, BF16), (LANE, BF16),
                  (IDX_HEADS * IDX_DIM, BF16), (IDX_HEADS * IDX_DIM, BF16), (LANE, F32), (D_S5, BF16)]
    return pl.pallas_call(
        _inproj_kernel,
        grid=(n // tm,),
        in_specs=[
            row(d),
            pl.BlockSpec((1, N_ADA, d), lambda i: (i * tm // seq, 0, 0)),
            _resident((1, d)),
            _resident(lw["w_all"].shape),
            _resident(lw["wuq"].shape),
            _resident(lw["wukv"].shape),
            _resident(lw["vec"].shape),
        ] + [row(LANE)] * 6,
        out_specs=[row(w) for w, _ in out_widths],
        out_shape=[jax.ShapeDtypeStruct((n, w), dt) for w, dt in out_widths],
        compiler_params=_params(("arbitrary",)),
        name="in_proj",
    )(x, mod, g, lw["w_all"], lw["wuq"], lw["wukv"], lw["vec"], *tabs)


def _rglru_kernel(tc, x_ref, gate_ref, cw_ref, vec_ref, wa_ref, wx_ref, o_ref, xs_ref, h_ref):
    j = pl.program_id(1)

    @pl.when(j == 0)
    def _():
        xs_ref[0:8, :] = jnp.zeros((8, D_RNN), F32)
        h_ref[...] = jnp.zeros(h_ref.shape, F32)

    xs_ref[8:8 + tc, :] = x_ref[...].astype(F32)
    cw = cw_ref[...]
    vec = vec_ref[...]
    xc = vec[0:1]
    for k in range(CONV_WIDTH):
        xc = xc + cw[k:k + 1] * xs_ref[pl.ds(8 - (CONV_WIDTH - 1) + k, tc), :]
    xs_ref[0:8, :] = xs_ref[tc:tc + 8, :]

    xb = xc.astype(BF16)
    r = _sigmoid(_dot(xb, wa_ref[...]) + vec[1:2])
    ig = _sigmoid(_dot(xb, wx_ref[...]) + vec[2:3])
    nl = -vec[3:4]
    softplus = jnp.maximum(nl, 0.0) + jnp.log(1.0 + jnp.exp(-jnp.abs(nl)))
    log_a = (-LRU_C) * r * softplus
    a = jnp.exp(log_a)
    z = 2.0 * log_a
    series = -z * (1.0 + z * (0.5 + z * (1.0 / 6.0 + z * (1.0 / 24.0 + z * (1.0 / 120.0 + z * (1.0 / 720.0))))))
    nem1 = jnp.where(z > -0.25, series, 1.0 - jnp.exp(z))
    b = jnp.sqrt(nem1) * ig * xc

    rowi = lax.broadcasted_iota(jnp.int32, (tc, D_RNN), 0)
    d = 1
    while d < tc:
        keep = rowi >= d
        a_s = jnp.where(keep, pltpu.roll(a, d, 0), 1.0)
        b_s = jnp.where(keep, pltpu.roll(b, d, 0), 0.0)
        b = a * b_s + b
        a = a * a_s
        d *= 2
    h = b + a * h_ref[...]
    h_ref[...] = h[tc - 1:tc, :]
    o_ref[...] = (h * _gelu_tanh(gate_ref[...].astype(F32))).astype(o_ref.dtype)


def _rglru_call(xr, gr, lw, bsz, seq, tc):
    n = xr.shape[0]
    nt = seq // tc
    row = pl.BlockSpec((tc, D_RNN), lambda b, j: (b * nt + j, 0))
    return pl.pallas_call(
        functools.partial(_rglru_kernel, tc),
        grid=(bsz, nt),
        in_specs=[row, row,
                  _resident((CONV_WIDTH, D_RNN)), _resident((4, D_RNN)),
                  _resident((D_RNN, D_RNN)), _resident((D_RNN, D_RNN))],
        out_specs=row,
        out_shape=jax.ShapeDtypeStruct((n, D_RNN), BF16),
        scratch_shapes=[pltpu.VMEM((tc + 8, D_RNN), F32), pltpu.VMEM((1, D_RNN), F32)],
        compiler_params=_params(("arbitrary", "arbitrary")),
        name="rglru",
    )(xr, gr, lw["conv_w"], lw["rg_vec"], lw["rg_wa"], lw["rg_wx"])


def _flash_kernel(q_ref, k_ref, v_ref, o_ref, m_ref, acc_ref):
    i = pl.program_id(1)
    j = pl.program_id(2)
    bq, bk = q_ref.shape[0], k_ref.shape[0]

    @pl.when(j == 0)
    def _():
        m_ref[...] = jnp.full(m_ref.shape, NEG_BIG, F32)
        acc_ref[...] = jnp.zeros(acc_ref.shape, F32)

    def step(masked):
        if masked:
            causal = (lax.broadcasted_iota(jnp.int32, (bq, bk), 1)
                      <= lax.broadcasted_iota(jnp.int32, (bq, bk), 0))
        m_out, acc_out = [], []
        for h in range(MLA_HEADS):
            hs = slice(h * LANE, (h + 1) * LANE)
            s = _dot_nt(q_ref[:, hs], k_ref[:, hs])
            if masked:
                s = jnp.where(causal, s, NEG_BIG)
            m_prev = m_ref[h]
            m_new = jnp.maximum(m_prev, jnp.max(s, axis=1, keepdims=True))
            p = jnp.exp(s - jnp.concatenate([m_new] * (bk // LANE), axis=1))
            acc_out.append(jnp.exp(m_prev - m_new) * acc_ref[h] + _dot(p.astype(BF16), v_ref[:, hs]))
            m_out.append(m_new)
        m_ref[...] = jnp.stack(m_out)
        acc_ref[...] = jnp.stack(acc_out)

    @pl.when(j < i)
    def _():
        step(False)

    @pl.when(j == i)
    def _():
        step(True)
        lane = lax.broadcasted_iota(jnp.int32, (bq, LANE), 1)
        outs = []
        for pr in range(MLA_HEADS // 2):
            even = acc_ref[2 * pr]
            odd = acc_ref[2 * pr + 1]
            outs.append(jnp.where(lane < MLA_V, even / pltpu.roll(even, MLA_V, 1),
                                  odd / pltpu.roll(odd, MLA_V, 1)))
        o_ref[...] = jnp.concatenate(outs, axis=1).astype(o_ref.dtype)


def _flash_call(q, k, v, bsz, seq, bq):
    n = q.shape[0]
    nq = seq // bq
    kv_row = lambda b, i, j: (b * nq + jnp.minimum(j, i), 0)
    return pl.pallas_call(
        _flash_kernel,
        grid=(bsz, nq, nq),
        in_specs=[
            pl.BlockSpec((bq, MLA_HEADS * LANE), lambda b, i, j: (b * nq + i, 0)),
            pl.BlockSpec((bq, MLA_HEADS * LANE), kv_row),
            pl.BlockSpec((bq, MLA_HEADS * LANE), kv_row),
        ],
        out_specs=pl.BlockSpec((bq, MLA_HEADS * MLA_V), lambda b, i, j: (b * nq + i, 0)),
        out_shape=jax.ShapeDtypeStruct((n, MLA_HEADS * MLA_V), BF16),
        scratch_shapes=[pltpu.VMEM((MLA_HEADS, bq, LANE), F32), pltpu.VMEM((MLA_HEADS, bq, LANE), F32)],
        compiler_params=_params(("arbitrary",) * 3),
        name="mla_flash",
    )(q, k, v)


_DSA_QB = 128
_DSA_HG = 4
_DSA_CHAINS = 4
_INT_MIN = -2 ** 31


def _dsa_body(ns, topk, i, q_ref, k_ref, v_ref, qi_ref, ki_ref, w_ref, tri_ref, o_ref):
    qb = _DSA_QB
    hg = _DSA_HG
    qi = qi_ref[...]
    ki = ki_ref[0:ns, :]
    w = w_ref[...]
    head_of_lane = jnp.right_shift(lax.broadcasted_iota(jnp.int32, qi.shape, 1), int(math.log2(IDX_DIM)))
    zero = jnp.zeros_like(qi)
    score = jnp.zeros((qb, ns), F32)
    for g in range(0, IDX_HEADS, hg):
        lhs = jnp.concatenate([jnp.where(head_of_lane == h, qi, zero) for h in range(g, g + hg)], axis=0)
        rel = jnp.maximum(_dot_nt(lhs, ki), 0.0)
        for t in range(hg):
            score = score + w[:, g + t:g + t + 1] * rel[t * qb:(t + 1) * qb]

    qpos = lax.broadcasted_iota(jnp.int32, (qb, ns), 0) + i * qb
    kpos = lax.broadcasted_iota(jnp.int32, (qb, ns), 1)
    bits = pltpu.bitcast(score + 0.0, jnp.int32)
    key = jnp.where(bits < 0, bits ^ jnp.int32(0x7FFFFFFF), bits)
    key = jnp.where(kpos <= qpos, key, jnp.int32(_INT_MIN))
    kk = jnp.minimum(lax.broadcasted_iota(jnp.int32, (qb, 1), 0) + (i * qb + 1), topk).astype(F32)

    rows = qb // _DSA_CHAINS
    keys = [key[r * rows:(r + 1) * rows] for r in range(_DSA_CHAINS)]
    kks = [kk[r * rows:(r + 1) * rows] for r in range(_DSA_CHAINS)]

    def body(it, thrs):
        bit = lax.shift_left(jnp.int32(1), jnp.int32(31) - it)
        out = []
        for kr, kkr, t in zip(keys, kks, thrs):
            c = t + bit
            cnt = jnp.sum(jnp.where(kr >= c, 1.0, 0.0), axis=1, keepdims=True)
            out.append(jnp.where(cnt >= kkr, c, t))
        return tuple(out)

    thrs = lax.fori_loop(0, 32, body, tuple(jnp.full((rows, 1), _INT_MIN, jnp.int32) for _ in keys), unroll=2)
    thr = jnp.concatenate(thrs, axis=0)

    nchunk = ns // LANE
    need = kk - jnp.sum(jnp.where(key > thr, 1.0, 0.0), axis=1, keepdims=True)
    eqs = [jnp.where(key[:, c * LANE:(c + 1) * LANE] == thr, 1.0, 0.0) for c in range(nchunk)]
    before_all = _dot(jnp.concatenate(eqs, axis=0).astype(BF16), tri_ref[...])
    run = jnp.zeros((qb, 1), F32)
    bias_chunks = []
    for c in range(nchunk):
        before = before_all[c * qb:(c + 1) * qb] + run
        take = jnp.where(key[:, c * LANE:(c + 1) * LANE] > thr, 1.0, jnp.where(before < need, eqs[c], 0.0))
        bias_chunks.append(jnp.where(take > 0.5, 0.0, NEG_BIG))
        run = run + jnp.sum(eqs[c], axis=1, keepdims=True)
    bias = jnp.concatenate(bias_chunks, axis=1)

    k = k_ref[0:ns, :]
    v = v_ref[0:ns, :]
    lane = lax.broadcasted_iota(jnp.int32, (qb, LANE), 1)
    for g in range(0, DSA_HEADS, hg):
        q = jnp.concatenate([q_ref[:, h * LANE:(h + 1) * LANE] for h in range(g, g + hg)], axis=0)
        s = _dot_nt(q, k).reshape(hg, qb, ns) + bias[None]
        p = jnp.exp(s - jnp.max(s, axis=2, keepdims=True))
        o = _dot(p.reshape(hg * qb, ns).astype(BF16), v)
        for t in range(0, hg, 2):
            even = o[t * qb:(t + 1) * qb]
            odd = o[(t + 1) * qb:(t + 2) * qb]
            pr = (g + t) // 2
            o_ref[:, pr * LANE:(pr + 1) * LANE] = jnp.where(
                lane < DSA_HEAD_DIM, even / pltpu.roll(even, DSA_HEAD_DIM, 1),
                pltpu.roll(odd, DSA_HEAD_DIM, 1) / odd).astype(o_ref.dtype)


def _dsa_kernel(seq, bucket, topk, q_ref, k_ref, v_ref, qi_ref, ki_ref, w_ref, tri_ref, o_ref):
    i = pl.program_id(1)
    for bk in range(seq // bucket):
        @pl.when((i * _DSA_QB) // bucket == bk)
        def _(bk=bk):
            _dsa_body((bk + 1) * bucket, topk, i, q_ref, k_ref, v_ref, qi_ref, ki_ref, w_ref,
                      tri_ref, o_ref)


def _dsa_call(qd, kd, vd, qi, ki, wi, tri, bsz, seq):
    n = qd.shape[0]
    qb = _DSA_QB
    nq = seq // qb
    bucket = min(512, seq)
    topk = min(TOPK_MAX, seq // 4)
    qrow = lambda w: pl.BlockSpec((qb, w), lambda b, i: (b * nq + i, 0))
    full = lambda w: pl.BlockSpec((seq, w), lambda b, i: (b, 0))
    return pl.pallas_call(
        functools.partial(_dsa_kernel, seq, bucket, topk),
        grid=(bsz, nq),
        in_specs=[qrow(DSA_HEADS * LANE), full(LANE), full(LANE),
                  qrow(IDX_HEADS * IDX_DIM), full(IDX_HEADS * IDX_DIM), qrow(LANE),
                  _resident((LANE, LANE))],
        out_specs=qrow(DSA_HEADS * DSA_HEAD_DIM),
        out_shape=jax.ShapeDtypeStruct((n, DSA_HEADS * DSA_HEAD_DIM), BF16),
        compiler_params=_params(("arbitrary", "arbitrary")),
        name="dsa",
    )(qd, kd, vd, qi, ki, wi, tri)


def _s5_kernel(rows, u_ref, w_ref, lam_ref, d_ref, o_ref):
    cw = S5_CHUNK * LANE
    half = S5_GB * S5_STATE
    rowi = lax.broadcasted_iota(jnp.int32, (rows, half), 0)
    for gb in range(S5_NGB):
        u = jnp.concatenate(
            [u_ref[:, a * D_S5 + gb * LANE:a * D_S5 + (gb + 1) * LANE] for a in range(S5_CHUNK)], axis=1)
        y = _dot(u, w_ref[gb, :, 0:cw])
        st = _dot(u, w_ref[gb, :, cw:2 * cw])
        xr = st[:, :half]
        xi = st[:, half:]
        lam = lam_ref[gb]
        lr = lam[0:1]
        li = lam[1:2]
        d = 1
        while d < rows:
            keep = rowi >= d
            sr = jnp.where(keep, pltpu.roll(xr, d, 0), 0.0)
            si = jnp.where(keep, pltpu.roll(xi, d, 0), 0.0)
            xr, xi = xr + lr * sr - li * si, xi + lr * si + li * sr
            lr, li = lr * lr - li * li, 2.0 * lr * li
            d *= 2
        keep = rowi >= 1
        pr = jnp.where(keep, pltpu.roll(xr, 1, 0), 0.0)
        pi = jnp.where(keep, pltpu.roll(xi, 1, 0), 0.0)
        xp = jnp.concatenate([pr, pi], axis=1).astype(BF16)
        y = y + _dot(xp, w_ref[gb, :, 2 * cw:3 * cw]) + d_ref[gb] * u.astype(F32)
        yg = _gelu_tanh(y).astype(o_ref.dtype)
        for a in range(S5_CHUNK):
            o_ref[:, a * D_S5 + gb * LANE:a * D_S5 + (gb + 1) * LANE] = yg[:, a * LANE:(a + 1) * LANE]


def _s5_call(us, lw, bsz, seq):
    n = us.shape[0]
    rows = seq // S5_CHUNK
    width = S5_CHUNK * D_S5
    blk = pl.BlockSpec((rows, width), lambda b: (b, 0))
    out = pl.pallas_call(
        functools.partial(_s5_kernel, rows),
        grid=(bsz,),
        in_specs=[blk, _resident(lw["s5_w"].shape), _resident(lw["s5_lam"].shape),
                  _resident(lw["s5_d"].shape)],
        out_specs=blk,
        out_shape=jax.ShapeDtypeStruct((n // S5_CHUNK, width), BF16),
        compiler_params=_params(("arbitrary",)),
        name="s5",
    )(us.reshape(n // S5_CHUNK, width), lw["s5_w"], lw["s5_lam"], lw["s5_d"])
    return out.reshape(n, D_S5)


def _merge_kernel(x_ref, mod_ref, g_ref, ya_ref, yb_ref, yc_ref, yd_ref,
                  wg_ref, wb_ref, wglu_ref, bglu_ref, wout_ref, o_ref):
    x = x_ref[...]
    m = mod_ref[0]
    d = x.shape[1]
    u = _rms_mod(x, g_ref[...], m[3:4], m[4:5]).astype(BF16)
    yd = yd_ref[...]
    ydg = (yd.astype(F32) * _sigmoid(_dot(yd, wglu_ref[...]) + bglu_ref[...])).astype(BF16)
    ys = (ya_ref[...], yb_ref[...], yc_ref[...], ydg)
    merged = jnp.zeros(x.shape, F32)
    for nb in range(N_BRANCH):
        gate = _sigmoid(_dot(u, wg_ref[:, nb * d:(nb + 1) * d]))
        merged = merged + gate * _dot(ys[nb], wb_ref[nb])
    o_ref[...] = x + (1.0 + m[5:6]) * _dot(merged.astype(BF16), wout_ref[...])


def _merge_call(x, mod, g, ya, yb, yc, yd, lw, seq, tm):
    n, d = x.shape
    row = lambda w: pl.BlockSpec((tm, w), lambda i: (i, 0))
    return pl.pallas_call(
        _merge_kernel,
        grid=(n // tm,),
        in_specs=[row(d), pl.BlockSpec((1, N_ADA, d), lambda i: (i * tm // seq, 0, 0)), _resident((1, d)),
                  row(BRANCH_W), row(BRANCH_W), row(BRANCH_W), row(BRANCH_W),
                  _resident((d, N_BRANCH * d)), _resident((N_BRANCH, BRANCH_W, d)),
                  _resident((D_S5, D_S5)), _resident((1, D_S5)), _resident((d, d))],
        out_specs=row(d),
        out_shape=jax.ShapeDtypeStruct((n, d), F32),
        compiler_params=_params(("arbitrary",)),
        name="merge",
    )(x, mod, g, ya, yb, yc, yd, lw["w_gate"], lw["w_branch"], lw["w_glu"], lw["b_glu"], lw["w_out"])


def _partner_cols(w, nheads, hdim, rot):
    kdim = w.shape[0]
    wh = w.reshape(kdim, nheads, hdim)
    half = rot // 2
    p = jnp.concatenate([-wh[..., half:rot], wh[..., :half],
                         jnp.zeros((kdim, nheads, hdim - rot), w.dtype)], axis=-1)
    return p.reshape(kdim, nheads * hdim)


def _to_slots(w, nheads, hdim):
    kdim = w.shape[0]
    wh = w.reshape(kdim, nheads, hdim)
    wh = jnp.pad(wh, ((0, 0), (0, 0), (0, LANE - hdim)))
    return wh.reshape(kdim, nheads * LANE)


def _pad_lane(v, width=LANE):
    return jnp.pad(v, [(0, 0)] * (v.ndim - 1) + [(0, width - v.shape[-1])])


def _partner_vec(g, rot):
    half = rot // 2
    return jnp.concatenate([g[half:rot], g[:half], jnp.zeros((g.shape[0] - rot,), g.dtype)])


def _rope_table(positions, rot, period):
    inv = ROPE_THETA ** (-jnp.arange(0, rot, 2, dtype=F32) / rot)
    ang = positions.astype(F32)[..., None] * inv
    cos, sin = jnp.cos(ang), jnp.sin(ang)
    one = jnp.ones(ang.shape[:-1] + (period - rot,), F32)
    ct = jnp.concatenate([cos, cos, one], axis=-1)
    st = jnp.concatenate([sin, sin, 0.0 * one], axis=-1)
    reps = LANE // period
    ct = jnp.tile(ct, (1, 1, reps)).reshape(-1, LANE)
    st = jnp.tile(st, (1, 1, reps)).reshape(-1, LANE)
    return ct, st


def _s5_weights(lam_re, lam_im, log_dt, b_re, b_im, c_re, c_im, dvec):
    hp = lax.Precision.HIGHEST
    g, p = lam_re.shape
    ch = S5_CHUNK
    dt = jnp.exp(log_dt)[:, None]
    mag = jnp.exp(lam_re * dt)
    ar, ai = mag * jnp.cos(lam_im * dt), mag * jnp.sin(lam_im * dt)
    den = lam_re * lam_re + lam_im * lam_im
    nr, ni = ar - 1.0, ai
    f_re = (nr * lam_re + ni * lam_im) / den
    f_im = (ni * lam_re - nr * lam_im) / den
    bb_re = f_re[..., None] * b_re - f_im[..., None] * b_im
    bb_im = f_re[..., None] * b_im + f_im[..., None] * b_re
    pr, pi = [jnp.ones_like(ar)], [jnp.zeros_like(ar)]
    for _ in range(ch):
        pr, pi = pr + [pr[-1] * ar - pi[-1] * ai], pi + [pr[-1] * ai + pi[-1] * ar]
    pr, pi = jnp.stack(pr), jnp.stack(pi)
    mr = pr[..., None] * bb_re - pi[..., None] * bb_im
    mi = pr[..., None] * bb_im + pi[..., None] * bb_re
    kern = (jnp.einsum('gjp,tgpi->tgji', c_re, mr, precision=hp)
            - jnp.einsum('gjp,tgpi->tgji', c_im, mi, precision=hp))
    eye = jnp.eye(S5_GB, dtype=F32)
    a_in = jnp.arange(ch)[:, None]
    a_out = jnp.arange(ch)[None, :]
    lag = jnp.clip(a_out - a_in, 0, ch)
    toe = kern[lag] * (a_out >= a_in)[..., None, None, None].astype(F32)
    toe = toe.reshape(ch, ch, S5_NGB, S5_GB, S5_GROUP, S5_GROUP)
    w_toe = jnp.einsum('xyngji,gh->nxgiyhj', toe, eye).reshape(S5_NGB, ch * LANE, ch * LANE)
    rev = jnp.arange(ch - 1, -1, -1)
    ms = jnp.stack([mr[rev], mi[rev]], axis=1)
    ms = ms.reshape(ch, 2, S5_NGB, S5_GB, p, S5_GROUP)
    w_st = jnp.einsum('acngpi,gh->nagichp', ms, eye).reshape(S5_NGB, ch * LANE, 2 * S5_GB * p)
    pr1, pi1 = pr[1:], pi[1:]
    co_re = c_re[None] * pr1[:, :, None, :] - c_im[None] * pi1[:, :, None, :]
    co_im = -(c_re[None] * pi1[:, :, None, :] + c_im[None] * pr1[:, :, None, :])
    co = jnp.stack([co_re, co_im], axis=0).reshape(2, ch, S5_NGB, S5_GB, S5_GROUP, p)
    w_out = jnp.einsum('cyngjp,gh->ncgpyhj', co, eye).reshape(S5_NGB, 2 * S5_GB * p, ch * LANE)
    w_all = jnp.concatenate([w_toe, w_st, w_out], axis=2).astype(BF16)
    lam_c = jnp.stack([pr[ch].reshape(S5_NGB, S5_GB * p), pi[ch].reshape(S5_NGB, S5_GB * p)], axis=1)
    d_t = jnp.tile(dvec.reshape(S5_NGB, 1, LANE), (1, 1, ch))
    return w_all, lam_c, d_t


def _block_diag(w):
    h, a, b = w.shape
    eye = jnp.eye(h, dtype=w.dtype)
    return jnp.einsum('hij,hg->higj', w, eye).reshape(h * a, h * b)


def _prep_layer(l, p):
    w_in = p["w_in"][l]
    offs = [0]
    for s in IN_SPLITS:
        offs.append(offs[-1] + s)
    seg = [w_in[:, offs[k]:offs[k + 1]] for k in range(len(IN_SPLITS))]
    (w_xr, w_gr, w_ql, w_kvl, w_kpe, w_qd, w_kd, w_vd, w_qi, w_ki, w_wi, w_us, w_gate) = seg
    rep = lambda w: jnp.tile(w, (1, IDX_HEADS))
    cols = [
        w_xr, w_gr, w_ql, w_kvl,
        _pad_lane(w_kpe), _pad_lane(_partner_cols(w_kpe, 1, MLA_ROPE, MLA_ROPE)),
        _to_slots(w_qd, DSA_HEADS, DSA_HEAD_DIM),
        _to_slots(_partner_cols(w_qd, DSA_HEADS, DSA_HEAD_DIM, DSA_ROT), DSA_HEADS, DSA_HEAD_DIM),
        _pad_lane(w_kd), _pad_lane(_partner_cols(w_kd, 1, DSA_HEAD_DIM, DSA_ROT)),
        _pad_lane(w_vd),
        w_qi, _partner_cols(w_qi, IDX_HEADS, IDX_DIM, IDX_ROT),
        rep(w_ki), rep(_partner_cols(w_ki, 1, IDX_DIM, IDX_ROT)),
        _pad_lane(w_wi), w_us,
    ]
    w_all = jnp.concatenate(cols, axis=1).astype(BF16)
    assert w_all.shape[1] == _C_END

    w_uq = p["mla_w_uq"][l]
    wuq = jnp.concatenate([_to_slots(w_uq, MLA_HEADS, MLA_QK),
                           _to_slots(_partner_cols(w_uq, MLA_HEADS, MLA_QK, MLA_ROPE), MLA_HEADS, MLA_QK)],
                          axis=1).astype(BF16)
    wkv = p["mla_w_ukv"][l].reshape(MLA_KV_LORA, MLA_HEADS, MLA_NOPE + MLA_V)
    kn = jnp.pad(wkv[..., :MLA_NOPE], ((0, 0), (0, 0), (MLA_ROPE, LANE - MLA_QK)))
    wv = wkv[..., MLA_NOPE:].reshape(MLA_KV_LORA, MLA_HEADS // 2, 2, MLA_V)
    zv = jnp.zeros_like(wv[:, :, 0])
    wv = jnp.stack([jnp.concatenate([wv[:, :, 0], zv], axis=-1),
                    jnp.concatenate([zv, wv[:, :, 1]], axis=-1)], axis=2)
    wukv = jnp.concatenate([kn.reshape(MLA_KV_LORA, MLA_HEADS * LANE),
                            wv.reshape(MLA_KV_LORA, MLA_HEADS * LANE)], axis=1).astype(BF16)

    gq, gk = p["mla_qk_gain"][l, 0], p["mla_qk_gain"][l, 1]
    dq, dk = p["dsa_qk_gain"][l, 0], p["dsa_qk_gain"][l, 1]
    vec_rows = [p["mla_q_norm"][l], p["mla_kv_norm"][l],
                gq, _partner_vec(gq, MLA_ROPE), gk, _partner_vec(gk, MLA_ROPE),
                dq, _partner_vec(dq, DSA_ROT), dk, _partner_vec(dk, DSA_ROT)]
    vec = jnp.stack([_pad_lane(v, MLA_Q_LORA) for v in vec_rows])

    s5_w, s5_lam, s5_d = _s5_weights(p["s5_lambda_re"][l], p["s5_lambda_im"][l], p["s5_log_dt"][l],
                                     p["s5_b_re"][l], p["s5_b_im"][l], p["s5_c_re"][l], p["s5_c_im"][l],
                                     p["s5_d"][l])

    ffn = [(p["ffn_w1"][l, j].astype(BF16), p["ffn_w3"][l, j].astype(BF16), p["ffn_w2"][l, j].astype(BF16))
           for j in range(2)]

    return dict(
        w_all=w_all, wuq=wuq, wukv=wukv, vec=vec,
        conv_w=p["conv_w"][l],
        rg_vec=jnp.stack([p["conv_b"][l], p["rg_ba"][l], p["rg_bx"][l], p["rg_lambda"][l]]),
        rg_wa=_block_diag(p["rg_wa"][l]).astype(BF16),
        rg_wx=_block_diag(p["rg_wx"][l]).astype(BF16),
        s5_w=s5_w, s5_lam=s5_lam, s5_d=s5_d,
        w_gate=w_gate.astype(BF16),
        w_branch=p["w_branch"][l].astype(BF16),
        w_glu=p["s5_w_glu"][l].astype(BF16),
        b_glu=p["s5_b_glu"][l].reshape(1, D_S5),
        w_out=p["w_out"][l].astype(BF16),
        ffn=ffn,
        norm_g=p["norm_g"][l],
    )


def _pick_tile(n, pref):
    t = pref
    while n % t:
        t //= 2
    return t


def kernel(x, c, positions, ada_w, ada_b, norm_g, ffn_w1, ffn_w3, ffn_w2, w_in,
           conv_w, conv_b, rg_wa, rg_ba, rg_wx, rg_bx, rg_lambda,
           mla_q_norm, mla_w_uq, mla_kv_norm, mla_w_ukv, mla_qk_gain, dsa_qk_gain,
           s5_lambda_re, s5_lambda_im, s5_log_dt, s5_b_re, s5_b_im, s5_c_re, s5_c_im,
           s5_d, s5_w_glu, s5_b_glu, w_branch, w_out):
    p = dict(norm_g=norm_g, ffn_w1=ffn_w1, ffn_w3=ffn_w3, ffn_w2=ffn_w2, w_in=w_in,
             conv_w=conv_w, conv_b=conv_b, rg_wa=rg_wa, rg_ba=rg_ba, rg_wx=rg_wx, rg_bx=rg_bx,
             rg_lambda=rg_lambda, mla_q_norm=mla_q_norm, mla_w_uq=mla_w_uq, mla_kv_norm=mla_kv_norm,
             mla_w_ukv=mla_w_ukv, mla_qk_gain=mla_qk_gain, dsa_qk_gain=dsa_qk_gain,
             s5_lambda_re=s5_lambda_re, s5_lambda_im=s5_lambda_im, s5_log_dt=s5_log_dt,
             s5_b_re=s5_b_re, s5_b_im=s5_b_im, s5_c_re=s5_c_re, s5_c_im=s5_c_im, s5_d=s5_d,
             s5_w_glu=s5_w_glu, s5_b_glu=s5_b_glu, w_branch=w_branch, w_out=w_out)
    bsz, seq, d = x.shape
    n = bsz * seq
    depth = ada_w.shape[0]
    assert seq % (S5_CHUNK * 8) == 0 and seq % _DSA_QB == 0

    tm = _pick_tile(seq, 512)
    tc = _pick_tile(seq, 256)
    bq = _pick_tile(seq, 512)

    mod = _ada_call(c, ada_w, ada_b)
    tabs = (_rope_table(positions, MLA_ROPE, LANE) + _rope_table(positions, DSA_ROT, LANE)
            + _rope_table(positions, IDX_ROT, IDX_DIM))
    tri = (jnp.arange(LANE)[:, None] < jnp.arange(LANE)[None, :]).astype(BF16)

    xf = x.reshape(n, d)
    for l in range(depth):
        lw = _prep_layer(l, p)
        g = lw["norm_g"]
        ml = mod[l]
        xf = _ffn_call(xf, ml, g[0:1], *lw["ffn"][0], 0, seq, tm)
        (xr, gr, qm, km, vm, qd, kd, vd, qi, ki, wi, us) = _inproj_call(xf, ml, g[1:2], lw, tabs, seq, tm)
        ya = _rglru_call(xr, gr, lw, bsz, seq, tc)
        yb = _flash_call(qm, km, vm, bsz, seq, bq)
        yc = _dsa_call(qd, kd, vd, qi, ki, wi, tri, bsz, seq)
        yd = _s5_call(us, lw, bsz, seq)
        xf = _merge_call(xf, ml, g[1:2], ya, yb, yc, yd, lw, seq, tm)
        xf = _ffn_call(xf, ml, g[2:3], *lw["ffn"][1], 6, seq, tm)
    return xf.reshape(bsz, seq, d)
```

```python
import functools
import math

import jax
import jax.numpy as jnp
from jax import lax
from jax.experimental import pallas as pl
from jax.experimental.pallas import tpu as pltpu

F32 = jnp.float32
BF16 = jnp.bfloat16

D_MODEL = 1024
EPS = 1e-6
ROPE_THETA = 500000.0
D_FF = 2816
N_ADA = 9

D_RNN = 512
RNN_HEADS = 8
RNN_HEAD_DIM = D_RNN // RNN_HEADS
CONV_WIDTH = 4
LRU_C = 8.0

MLA_HEADS = 8
MLA_NOPE = 64
MLA_ROPE = 32
MLA_V = 64
MLA_QK = MLA_ROPE + MLA_NOPE
MLA_Q_LORA = 256
MLA_KV_LORA = 128

DSA_HEADS = 8
DSA_HEAD_DIM = 64
DSA_ROT = DSA_HEAD_DIM // 4
IDX_HEADS = 8
IDX_DIM = 32
IDX_ROT = IDX_DIM // 4
TOPK_MAX = 256

S5_GROUP = 16
S5_GROUPS = 32
D_S5 = S5_GROUP * S5_GROUPS
S5_STATE = 64
S5_CHUNK = 8
S5_GB = 8
S5_NGB = S5_GROUPS // S5_GB

N_BRANCH = 4
BRANCH_W = 512
IN_SPLITS = (D_RNN, D_RNN, MLA_Q_LORA, MLA_KV_LORA, MLA_ROPE,
             DSA_HEADS * DSA_HEAD_DIM, DSA_HEAD_DIM, DSA_HEAD_DIM,
             IDX_HEADS * IDX_DIM, IDX_DIM, IDX_HEADS, D_S5, N_BRANCH * D_MODEL)

LANE = 128
NEG_BIG = -1e30
VMEM_LIMIT = 56 * 1024 * 1024

_C_XR = 0
_C_GR = _C_XR + D_RNN
_C_QL = _C_GR + D_RNN
_C_KVL = _C_QL + MLA_Q_LORA
_C_KPE = _C_KVL + MLA_KV_LORA
_C_KPEP = _C_KPE + LANE
_C_QD = _C_KPEP + LANE
_C_QDP = _C_QD + DSA_HEADS * LANE
_C_KD = _C_QDP + DSA_HEADS * LANE
_C_KDP = _C_KD + LANE
_C_VD = _C_KDP + LANE
_C_QI = _C_VD + LANE
_C_QIP = _C_QI + IDX_HEADS * IDX_DIM
_C_KI = _C_QIP + IDX_HEADS * IDX_DIM
_C_KIP = _C_KI + IDX_HEADS * IDX_DIM
_C_WI = _C_KIP + IDX_HEADS * IDX_DIM
_C_US = _C_WI + LANE
_C_END = _C_US + D_S5


def _dot(a, b):
    return jnp.dot(a, b, preferred_element_type=F32)


def _dot_nt(a, b):
    return lax.dot_general(a, b, (((1,), (1,)), ((), ())), preferred_element_type=F32)


def _sigmoid(x):
    return jax.nn.sigmoid(x)


def _gelu_tanh(x):
    return 0.5 * x * (1.0 + jnp.tanh(0.7978845608028654 * (x + 0.044715 * (x * x * x))))


def _rms_mod(x, g, shift, scale):
    ms = jnp.mean(x * x, axis=-1, keepdims=True)
    y = x * lax.rsqrt(ms + EPS) * g
    return y * (1.0 + scale) + shift


def _resident(shape):
    return pl.BlockSpec(shape, lambda *_: (0,) * len(shape), pipeline_mode=pl.Buffered(1))


def _params(sem):
    return pltpu.CompilerParams(dimension_semantics=sem, vmem_limit_bytes=VMEM_LIMIT)


def _ada_kernel(c_ref, w_ref, b_ref, o_ref):
    c = c_ref[...]
    a = c * _sigmoid(c)
    w = w_ref[0]
    a_hi = a.astype(BF16)
    a_lo = (a - a_hi.astype(F32)).astype(BF16)
    w_hi = w.astype(BF16)
    w_lo = (w - w_hi.astype(F32)).astype(BF16)
    o_ref[0] = _dot(a_hi, w_hi) + _dot(a_lo, w_hi) + _dot(a_hi, w_lo) + b_ref[0]


def _ada_call(c, ada_w, ada_b):
    nl, d, n9 = ada_w.shape
    b = c.shape[0]
    tn = 1024
    out = pl.pallas_call(
        _ada_kernel,
        grid=(nl, n9 // tn),
        in_specs=[
            pl.BlockSpec((b, d), lambda l, j: (0, 0)),
            pl.BlockSpec((1, d, tn), lambda l, j: (l, 0, j)),
            pl.BlockSpec((1, 1, tn), lambda l, j: (l, 0, j)),
        ],
        out_specs=pl.BlockSpec((1, b, tn), lambda l, j: (l, 0, j)),
        out_shape=jax.ShapeDtypeStruct((nl, b, n9), F32),
        compiler_params=_params(("arbitrary", "arbitrary")),
        name="ada_mod",
    )(c, ada_w, ada_b.reshape(nl, 1, n9))
    return out.reshape(nl, b, N_ADA, d)


_FFN_CHUNK = 256


def _ffn_kernel(row0, x_ref, mod_ref, g_ref, w1_ref, w3_ref, w2_ref, o_ref, h_ref):
    x = x_ref[...]
    m = mod_ref[0]
    u = _rms_mod(x, g_ref[...], m[row0:row0 + 1], m[row0 + 1:row0 + 2]).astype(BF16)
    for c in range(0, w1_ref.shape[1], _FFN_CHUNK):
        h1 = _dot(u, w1_ref[:, c:c + _FFN_CHUNK])
        h3 = _dot(u, w3_ref[:, c:c + _FFN_CHUNK])
        h_ref[:, c:c + _FFN_CHUNK] = (h1 * _sigmoid(h1) * h3).astype(BF16)
    o_ref[...] = x + 0.5 * (1.0 + m[row0 + 2:row0 + 3]) * _dot(h_ref[...], w2_ref[...])


def _ffn_call(x, mod, g, w1, w3, w2, row0, seq, tm):
    n, d = x.shape
    f = w1.shape[1]
    return pl.pallas_call(
        functools.partial(_ffn_kernel, row0),
        grid=(n // tm,),
        in_specs=[
            pl.BlockSpec((tm, d), lambda i: (i, 0)),
            pl.BlockSpec((1, N_ADA, d), lambda i: (i * tm // seq, 0, 0)),
            _resident((1, d)),
            _resident((d, f)),
            _resident((d, f)),
            _resident((f, d)),
        ],
        out_specs=pl.BlockSpec((tm, d), lambda i: (i, 0)),
        out_shape=jax.ShapeDtypeStruct((n, d), F32),
        scratch_shapes=[pltpu.VMEM((tm, f), BF16)],
        compiler_params=_params(("arbitrary",)),
        name="ffn",
    )(x, mod, g, w1, w3, w2)


def _head_norm_rope(xs, xp, gc, gs, nheads, inv_dim, scale):
    outs = []
    for h in range(nheads):
        a = xs[:, h * LANE:(h + 1) * LANE]
        p = xp[:, h * LANE:(h + 1) * LANE]
        ss = jnp.sum(a * a, axis=-1, keepdims=True) * inv_dim
        s = lax.rsqrt(ss + EPS) * scale
        outs.append(s * (a * gc + p * gs))
    return outs


def _inproj_kernel(x_ref, mod_ref, g_ref, w_ref, wuq_ref, wukv_ref, vec_ref,
                   cm_ref, sm_ref, cd_ref, sd_ref, ci_ref, si_ref,
                   xr_ref, gr_ref, qm_ref, km_ref, vm_ref, qd_ref, kd_ref, vd_ref,
                   qi_ref, ki_ref, wi_ref, us_ref):
    x = x_ref[...]
    m = mod_ref[0]
    u = _rms_mod(x, g_ref[...], m[3:4], m[4:5]).astype(BF16)
    vec = vec_ref[...]

    def proj(lo, hi):
        return _dot(u, w_ref[:, lo:hi])

    xr_ref[...] = proj(_C_XR, _C_GR).astype(xr_ref.dtype)
    gr_ref[...] = proj(_C_GR, _C_QL).astype(gr_ref.dtype)

    ql = proj(_C_QL, _C_KVL)
    qn = (ql * lax.rsqrt(jnp.mean(ql * ql, axis=-1, keepdims=True) + EPS)
          * vec[0:1, :MLA_Q_LORA]).astype(BF16)
    q2 = _dot(qn, wuq_ref[...])
    kvl = proj(_C_KVL, _C_KPE)
    kvn = (kvl * lax.rsqrt(jnp.mean(kvl * kvl, axis=-1, keepdims=True) + EPS)
           * vec[1:2, :MLA_KV_LORA]).astype(BF16)
    kv = _dot(kvn, wukv_ref[...])
    kpe = proj(_C_KPE, _C_KPEP)
    kpep = proj(_C_KPEP, _C_QD)
    cm = cm_ref[...]
    sm = sm_ref[...]
    nslot = MLA_HEADS * LANE
    q_out = _head_norm_rope(q2[:, :nslot], q2[:, nslot:], cm * vec[2:3, :LANE], sm * vec[3:4, :LANE],
                            MLA_HEADS, 1.0 / MLA_QK, MLA_QK ** -0.5)
    gkc = cm * vec[4:5, :LANE]
    gks = sm * vec[5:6, :LANE]
    for h in range(MLA_HEADS):
        qm_ref[:, h * LANE:(h + 1) * LANE] = q_out[h].astype(qm_ref.dtype)
        ks = kv[:, h * LANE:(h + 1) * LANE] + kpe
        ss = jnp.sum(ks * ks, axis=-1, keepdims=True) * (1.0 / MLA_QK)
        s = lax.rsqrt(ss + EPS)
        km_ref[:, h * LANE:(h + 1) * LANE] = (s * (ks * gkc + kpep * gks)).astype(km_ref.dtype)
    vlane = lax.broadcasted_iota(jnp.int32, (1, nslot), 1)
    v_is_low = (vlane // LANE) % 2 == 0
    ones_half = jnp.where(((vlane % LANE) < MLA_V) == v_is_low, 0.0, 1.0)
    vm_ref[...] = (kv[:, nslot:] + ones_half).astype(vm_ref.dtype)

    cd = cd_ref[...]
    sd = sd_ref[...]
    qd = proj(_C_QD, _C_QDP)
    qdp = proj(_C_QDP, _C_KD)
    qd_out = _head_norm_rope(qd, qdp, cd * vec[6:7, :LANE], sd * vec[7:8, :LANE],
                             DSA_HEADS, 1.0 / DSA_HEAD_DIM, DSA_HEAD_DIM ** -0.5)
    for h in range(DSA_HEADS):
        qd_ref[:, h * LANE:(h + 1) * LANE] = qd_out[h].astype(qd_ref.dtype)
    kd = proj(_C_KD, _C_KDP)
    kdp = proj(_C_KDP, _C_VD)
    kd_out = _head_norm_rope(kd, kdp, cd * vec[8:9, :LANE], sd * vec[9:10, :LANE],
                             1, 1.0 / DSA_HEAD_DIM, 1.0)
    kd_ref[...] = kd_out[0].astype(kd_ref.dtype)
    dlane = lax.broadcasted_iota(jnp.int32, (1, LANE), 1)
    vd_ref[...] = (proj(_C_VD, _C_QI) + jnp.where(dlane < DSA_HEAD_DIM, 0.0, 1.0)).astype(vd_ref.dtype)

    ci = jnp.concatenate([ci_ref[...]] * 2, axis=1)
    si = jnp.concatenate([si_ref[...]] * 2, axis=1)
    qi_ref[...] = (proj(_C_QI, _C_QIP) * ci + proj(_C_QIP, _C_KI) * si).astype(qi_ref.dtype)
    ki_ref[...] = (proj(_C_KI, _C_KIP) * ci + proj(_C_KIP, _C_WI) * si).astype(ki_ref.dtype)
    wi_ref[...] = proj(_C_WI, _C_US)

    us_ref[...] = proj(_C_US, _C_END).astype(us_ref.dtype)


def _inproj_call(x, mod, g, lw, tabs, seq, tm):
    n, d = x.shape
    row = lambda w: pl.BlockSpec((tm, w), lambda i: (i, 0))
    out_widths = [(D_RNN, BF16), (D_RNN, BF16), (MLA_HEADS * LANE, BF16), (MLA_HEADS * LANE, BF16),
                  (MLA_HEADS * LANE, BF16), (DSA_HEADS * LANE, BF16), (LANE, BF16), (LANE, BF16),
                  (IDX_HEADS * IDX_DIM, BF16), (IDX_HEADS * IDX_DIM, BF16), (LANE, F32), (D_S5, BF16)]
    return pl.pallas_call(
        _inproj_kernel,
        grid=(n // tm,),
        in_specs=[
            row(d),
            pl.BlockSpec((1, N_ADA, d), lambda i: (i * tm // seq, 0, 0)),
            _resident((1, d)),
            _resident(lw["w_all"].shape),
            _resident(lw["wuq"].shape),
            _resident(lw["wukv"].shape),
            _resident(lw["vec"].shape),
        ] + [row(LANE)] * 6,
        out_specs=[row(w) for w, _ in out_widths],
        out_shape=[jax.ShapeDtypeStruct((n, w), dt) for w, dt in out_widths],
        compiler_params=_params(("arbitrary",)),
        name="in_proj",
    )(x, mod, g, lw["w_all"], lw["wuq"], lw["wukv"], lw["vec"], *tabs)


def _rglru_kernel(tc, x_ref, gate_ref, cw_ref, vec_ref, wa_ref, wx_ref, o_ref, xs_ref, h_ref):
    j = pl.program_id(1)

    @pl.when(j == 0)
    def _():
        xs_ref[0:8, :] = jnp.zeros((8, D_RNN), F32)
        h_ref[...] = jnp.zeros(h_ref.shape, F32)

    xs_ref[8:8 + tc, :] = x_ref[...].astype(F32)
    cw = cw_ref[...]
    vec = vec_ref[...]
    xc = vec[0:1]
    for k in range(CONV_WIDTH):
        xc = xc + cw[k:k + 1] * xs_ref[pl.ds(8 - (CONV_WIDTH - 1) + k, tc), :]
    xs_ref[0:8, :] = xs_ref[tc:tc + 8, :]

    xb = xc.astype(BF16)
    r = _sigmoid(_dot(xb, wa_ref[...]) + vec[1:2])
    ig = _sigmoid(_dot(xb, wx_ref[...]) + vec[2:3])
    nl = -vec[3:4]
    softplus = jnp.maximum(nl, 0.0) + jnp.log(1.0 + jnp.exp(-jnp.abs(nl)))
    log_a = (-LRU_C) * r * softplus
    a = jnp.exp(log_a)
    z = 2.0 * log_a
    series = -z * (1.0 + z * (0.5 + z * (1.0 / 6.0 + z * (1.0 / 24.0 + z * (1.0 / 120.0 + z * (1.0 / 720.0))))))
    nem1 = jnp.where(z > -0.25, series, 1.0 - jnp.exp(z))
    b = jnp.sqrt(nem1) * ig * xc

    rowi = lax.broadcasted_iota(jnp.int32, (tc, D_RNN), 0)
    d = 1
    while d < tc:
        keep = rowi >= d
        a_s = jnp.where(keep, pltpu.roll(a, d, 0), 1.0)
        b_s = jnp.where(keep, pltpu.roll(b, d, 0), 0.0)
        b = a * b_s + b
        a = a * a_s
        d *= 2
    h = b + a * h_ref[...]
    h_ref[...] = h[tc - 1:tc, :]
    o_ref[...] = (h * _gelu_tanh(gate_ref[...].astype(F32))).astype(o_ref.dtype)


def _rglru_call(xr, gr, lw, bsz, seq, tc):
    n = xr.shape[0]
    nt = seq // tc
    row = pl.BlockSpec((tc, D_RNN), lambda b, j: (b * nt + j, 0))
    return pl.pallas_call(
        functools.partial(_rglru_kernel, tc),
        grid=(bsz, nt),
        in_specs=[row, row,
                  _resident((CONV_WIDTH, D_RNN)), _resident((4, D_RNN)),
                  _resident((D_RNN, D_RNN)), _resident((D_RNN, D_RNN))],
        out_specs=row,
        out_shape=jax.ShapeDtypeStruct((n, D_RNN), BF16),
        scratch_shapes=[pltpu.VMEM((tc + 8, D_RNN), F32), pltpu.VMEM((1, D_RNN), F32)],
        compiler_params=_params(("arbitrary", "arbitrary")),
        name="rglru",
    )(xr, gr, lw["conv_w"], lw["rg_vec"], lw["rg_wa"], lw["rg_wx"])


def _flash_kernel(q_ref, k_ref, v_ref, o_ref, m_ref, acc_ref):
    i = pl.program_id(1)
    j = pl.program_id(2)
    bq, bk = q_ref.shape[0], k_ref.shape[0]

    @pl.when(j == 0)
    def _():
        m_ref[...] = jnp.full(m_ref.shape, NEG_BIG, F32)
        acc_ref[...] = jnp.zeros(acc_ref.shape, F32)

    def step(masked):
        if masked:
            causal = (lax.broadcasted_iota(jnp.int32, (bq, bk), 1)
                      <= lax.broadcasted_iota(jnp.int32, (bq, bk), 0))
        m_out, acc_out = [], []
        for h in range(MLA_HEADS):
            hs = slice(h * LANE, (h + 1) * LANE)
            s = _dot_nt(q_ref[:, hs], k_ref[:, hs])
            if masked:
                s = jnp.where(causal, s, NEG_BIG)
            m_prev = m_ref[h]
            m_new = jnp.maximum(m_prev, jnp.max(s, axis=1, keepdims=True))
            p = jnp.exp(s - jnp.concatenate([m_new] * (bk // LANE), axis=1))
            acc_out.append(jnp.exp(m_prev - m_new) * acc_ref[h] + _dot(p.astype(BF16), v_ref[:, hs]))
            m_out.append(m_new)
        m_ref[...] = jnp.stack(m_out)
        acc_ref[...] = jnp.stack(acc_out)

    @pl.when(j < i)
    def _():
        step(False)

    @pl.when(j == i)
    def _():
        step(True)
        lane = lax.broadcasted_iota(jnp.int32, (bq, LANE), 1)
        outs = []
        for pr in range(MLA_HEADS // 2):
            even = acc_ref[2 * pr]
            odd = acc_ref[2 * pr + 1]
            outs.append(jnp.where(lane < MLA_V, even / pltpu.roll(even, MLA_V, 1),
                                  odd / pltpu.roll(odd, MLA_V, 1)))
        o_ref[...] = jnp.concatenate(outs, axis=1).astype(o_ref.dtype)


def _flash_call(q, k, v, bsz, seq, bq):
    n = q.shape[0]
    nq = seq // bq
    kv_row = lambda b, i, j: (b * nq + jnp.minimum(j, i), 0)
    return pl.pallas_call(
        _flash_kernel,
        grid=(bsz, nq, nq),
        in_specs=[
            pl.BlockSpec((bq, MLA_HEADS * LANE), lambda b, i, j: (b * nq + i, 0)),
            pl.BlockSpec((bq, MLA_HEADS * LANE), kv_row),
            pl.BlockSpec((bq, MLA_HEADS * LANE), kv_row),
        ],
        out_specs=pl.BlockSpec((bq, MLA_HEADS * MLA_V), lambda b, i, j: (b * nq + i, 0)),
        out_shape=jax.ShapeDtypeStruct((n, MLA_HEADS * MLA_V), BF16),
        scratch_shapes=[pltpu.VMEM((MLA_HEADS, bq, LANE), F32), pltpu.VMEM((MLA_HEADS, bq, LANE), F32)],
        compiler_params=_params(("arbitrary",) * 3),
        name="mla_flash",
    )(q, k, v)


_DSA_QB = 256
_DSA_HG = 2
_DSA_CHAINS = 8
_INT_MIN = -2 ** 31


def _dsa_body(ns, topk, i, q_ref, k_ref, v_ref, qi_ref, ki_ref, w_ref, tri_ref, o_ref):
    qb = _DSA_QB
    hg = _DSA_HG
    qi = qi_ref[...]
    ki = ki_ref[0:ns, :]
    w = w_ref[...]
    head_of_lane = jnp.right_shift(lax.broadcasted_iota(jnp.int32, qi.shape, 1), int(math.log2(IDX_DIM)))
    zero = jnp.zeros_like(qi)
    score = jnp.zeros((qb, ns), F32)
    for g in range(0, IDX_HEADS, hg):
        lhs = jnp.concatenate([jnp.where(head_of_lane == h, qi, zero) for h in range(g, g + hg)], axis=0)
        rel = jnp.maximum(_dot_nt(lhs, ki), 0.0)
        for t in range(hg):
            score = score + w[:, g + t:g + t + 1] * rel[t * qb:(t + 1) * qb]

    qpos = lax.broadcasted_iota(jnp.int32, (qb, ns), 0) + i * qb
    kpos = lax.broadcasted_iota(jnp.int32, (qb, ns), 1)
    bits = pltpu.bitcast(score + 0.0, jnp.int32)
    key = jnp.where(bits < 0, bits ^ jnp.int32(0x7FFFFFFF), bits)
    key = jnp.where(kpos <= qpos, key, jnp.int32(_INT_MIN))
    kk = jnp.minimum(lax.broadcasted_iota(jnp.int32, (qb, 1), 0) + (i * qb + 1), topk).astype(F32)

    rows = qb // _DSA_CHAINS
    keys = [key[r * rows:(r + 1) * rows] for r in range(_DSA_CHAINS)]
    kks = [kk[r * rows:(r + 1) * rows] for r in range(_DSA_CHAINS)]

    def body(it, thrs):
        bit = lax.shift_left(jnp.int32(1), jnp.int32(31) - it)
        out = []
        for kr, kkr, t in zip(keys, kks, thrs):
            c = t + bit
            cnt = jnp.sum(jnp.where(kr >= c, 1.0, 0.0), axis=1, keepdims=True)
            out.append(jnp.where(cnt >= kkr, c, t))
        return tuple(out)

    thrs = lax.fori_loop(0, 32, body, tuple(jnp.full((rows, 1), _INT_MIN, jnp.int32) for _ in keys), unroll=2)
    thr = jnp.concatenate(thrs, axis=0)

    nchunk = ns // LANE
    need = kk - jnp.sum(jnp.where(key > thr, 1.0, 0.0), axis=1, keepdims=True)
    eqs = [jnp.where(key[:, c * LANE:(c + 1) * LANE] == thr, 1.0, 0.0) for c in range(nchunk)]
    before_all = _dot(jnp.concatenate(eqs, axis=0).astype(BF16), tri_ref[...])
    run = jnp.zeros((qb, 1), F32)
    bias_chunks = []
    for c in range(nchunk):
        before = before_all[c * qb:(c + 1) * qb] + run
        take = jnp.where(key[:, c * LANE:(c + 1) * LANE] > thr, 1.0, jnp.where(before < need, eqs[c], 0.0))
        bias_chunks.append(jnp.where(take > 0.5, 0.0, NEG_BIG))
        run = run + jnp.sum(eqs[c], axis=1, keepdims=True)
    bias = jnp.concatenate(bias_chunks, axis=1)

    k = k_ref[0:ns, :]
    v = v_ref[0:ns, :]
    lane = lax.broadcasted_iota(jnp.int32, (qb, LANE), 1)
    for g in range(0, DSA_HEADS, hg):
        q = jnp.concatenate([q_ref[:, h * LANE:(h + 1) * LANE] for h in range(g, g + hg)], axis=0)
        s = _dot_nt(q, k).reshape(hg, qb, ns) + bias[None]
        p = jnp.exp(s - jnp.max(s, axis=2, keepdims=True))
        o = _dot(p.reshape(hg * qb, ns).astype(BF16), v)
        for t in range(0, hg, 2):
            even = o[t * qb:(t + 1) * qb]
            odd = o[(t + 1) * qb:(t + 2) * qb]
            pr = (g + t) // 2
            o_ref[:, pr * LANE:(pr + 1) * LANE] = jnp.where(
                lane < DSA_HEAD_DIM, even / pltpu.roll(even, DSA_HEAD_DIM, 1),
                pltpu.roll(odd, DSA_HEAD_DIM, 1) / odd).astype(o_ref.dtype)


def _dsa_kernel(seq, bucket, topk, q_ref, k_ref, v_ref, qi_ref, ki_ref, w_ref, tri_ref, o_ref):
    i = pl.program_id(1)
    for bk in range(seq // bucket):
        @pl.when((i * _DSA_QB) // bucket == bk)
        def _(bk=bk):
            _dsa_body((bk + 1) * bucket, topk, i, q_ref, k_ref, v_ref, qi_ref, ki_ref, w_ref,
                      tri_ref, o_ref)


def _dsa_call(qd, kd, vd, qi, ki, wi, tri, bsz, seq):
    n = qd.shape[0]
    qb = _DSA_QB
    nq = seq // qb
    bucket = min(512, seq)
    topk = min(TOPK_MAX, seq // 4)
    qrow = lambda w: pl.BlockSpec((qb, w), lambda b, i: (b * nq + i, 0))
    full = lambda w: pl.BlockSpec((seq, w), lambda b, i: (b, 0))
    return pl.pallas_call(
        functools.partial(_dsa_kernel, seq, bucket, topk),
        grid=(bsz, nq),
        in_specs=[qrow(DSA_HEADS * LANE), full(LANE), full(LANE),
                  qrow(IDX_HEADS * IDX_DIM), full(IDX_HEADS * IDX_DIM), qrow(LANE),
                  _resident((LANE, LANE))],
        out_specs=qrow(DSA_HEADS * DSA_HEAD_DIM),
        out_shape=jax.ShapeDtypeStruct((n, DSA_HEADS * DSA_HEAD_DIM), BF16),
        compiler_params=_params(("arbitrary", "arbitrary")),
        name="dsa",
    )(qd, kd, vd, qi, ki, wi, tri)


def _s5_kernel(rows, u_ref, wt_ref, ws_ref, wo_ref, lam_ref, d_ref, o_ref, uf_ref, yf_ref):
    half = S5_GB * S5_STATE
    rowi = lax.broadcasted_iota(jnp.int32, (rows, half), 0)
    for gb in range(S5_NGB):
        gs = slice(gb * LANE, (gb + 1) * LANE)
        uf_ref[gb] = u_ref[:, gs].astype(F32)
        uf = jnp.concatenate([uf_ref[gb, pl.ds(a, rows, stride=S5_CHUNK), :] for a in range(S5_CHUNK)], axis=1)
        u = uf.astype(BF16)
        y = _dot(u, wt_ref[gb])
        st = _dot(u, ws_ref[gb])
        xr = st[:, :half]
        xi = st[:, half:]
        lam = lam_ref[gb]
        lr = lam[0:1]
        li = lam[1:2]
        d = 1
        while d < rows:
            keep = rowi >= d
            sr = jnp.where(keep, pltpu.roll(xr, d, 0), 0.0)
            si = jnp.where(keep, pltpu.roll(xi, d, 0), 0.0)
            xr, xi = xr + lr * sr - li * si, xi + lr * si + li * sr
            lr, li = lr * lr - li * li, 2.0 * lr * li
            d *= 2
        keep = rowi >= 1
        pr = jnp.where(keep, pltpu.roll(xr, 1, 0), 0.0)
        pi = jnp.where(keep, pltpu.roll(xi, 1, 0), 0.0)
        xp = jnp.concatenate([pr, pi], axis=1).astype(BF16)
        yg = _gelu_tanh(y + _dot(xp, wo_ref[gb]) + d_ref[gb] * uf)
        for a in range(S5_CHUNK):
            yf_ref[gb, pl.ds(a, rows, stride=S5_CHUNK), :] = yg[:, a * LANE:(a + 1) * LANE]
        o_ref[:, gs] = yf_ref[gb].astype(o_ref.dtype)


def _s5_call(us, lw, bsz, seq):
    n = us.shape[0]
    blk = pl.BlockSpec((seq, D_S5), lambda b: (b, 0))
    return pl.pallas_call(
        functools.partial(_s5_kernel, seq // S5_CHUNK),
        grid=(bsz,),
        in_specs=[blk] + [_resident(w.shape) for w in lw["s5_w"]] + [
            _resident(lw["s5_lam"].shape), _resident(lw["s5_d"].shape)],
        out_specs=blk,
        out_shape=jax.ShapeDtypeStruct((n, D_S5), BF16),
        scratch_shapes=[pltpu.VMEM((S5_NGB, seq, LANE), F32), pltpu.VMEM((S5_NGB, seq, LANE), F32)],
        compiler_params=_params(("arbitrary",)),
        name="s5",
    )(us, *lw["s5_w"], lw["s5_lam"], lw["s5_d"])


def _merge_kernel(x_ref, mod_ref, g_ref, ya_ref, yb_ref, yc_ref, yd_ref,
                  wg_ref, wb_ref, wglu_ref, bglu_ref, wout_ref, o_ref):
    x = x_ref[...]
    m = mod_ref[0]
    d = x.shape[1]
    u = _rms_mod(x, g_ref[...], m[3:4], m[4:5]).astype(BF16)
    yd = yd_ref[...]
    ydg = (yd.astype(F32) * _sigmoid(_dot(yd, wglu_ref[...]) + bglu_ref[...])).astype(BF16)
    ys = (ya_ref[...], yb_ref[...], yc_ref[...], ydg)
    merged = jnp.zeros(x.shape, F32)
    for nb in range(N_BRANCH):
        gate = _sigmoid(_dot(u, wg_ref[:, nb * d:(nb + 1) * d]))
        merged = merged + gate * _dot(ys[nb], wb_ref[nb])
    o_ref[...] = x + (1.0 + m[5:6]) * _dot(merged.astype(BF16), wout_ref[...])


def _merge_call(x, mod, g, ya, yb, yc, yd, lw, seq, tm):
    n, d = x.shape
    row = lambda w: pl.BlockSpec((tm, w), lambda i: (i, 0))
    return pl.pallas_call(
        _merge_kernel,
        grid=(n // tm,),
        in_specs=[row(d), pl.BlockSpec((1, N_ADA, d), lambda i: (i * tm // seq, 0, 0)), _resident((1, d)),
                  row(BRANCH_W), row(BRANCH_W), row(BRANCH_W), row(BRANCH_W),
                  _resident((d, N_BRANCH * d)), _resident((N_BRANCH, BRANCH_W, d)),
                  _resident((D_S5, D_S5)), _resident((1, D_S5)), _resident((d, d))],
        out_specs=row(d),
        out_shape=jax.ShapeDtypeStruct((n, d), F32),
        compiler_params=_params(("arbitrary",)),
        name="merge",
    )(x, mod, g, ya, yb, yc, yd, lw["w_gate"], lw["w_branch"], lw["w_glu"], lw["b_glu"], lw["w_out"])


def _partner_cols(w, nheads, hdim, rot):
    kdim = w.shape[0]
    wh = w.reshape(kdim, nheads, hdim)
    half = rot // 2
    p = jnp.concatenate([-wh[..., half:rot], wh[..., :half],
                         jnp.zeros((kdim, nheads, hdim - rot), w.dtype)], axis=-1)
    return p.reshape(kdim, nheads * hdim)


def _to_slots(w, nheads, hdim):
    kdim = w.shape[0]
    wh = w.reshape(kdim, nheads, hdim)
    wh = jnp.pad(wh, ((0, 0), (0, 0), (0, LANE - hdim)))
    return wh.reshape(kdim, nheads * LANE)


def _pad_lane(v, width=LANE):
    return jnp.pad(v, [(0, 0)] * (v.ndim - 1) + [(0, width - v.shape[-1])])


def _partner_vec(g, rot):
    half = rot // 2
    return jnp.concatenate([g[half:rot], g[:half], jnp.zeros((g.shape[0] - rot,), g.dtype)])


def _rope_table(positions, rot, period):
    inv = ROPE_THETA ** (-jnp.arange(0, rot, 2, dtype=F32) / rot)
    ang = positions.astype(F32)[..., None] * inv
    cos, sin = jnp.cos(ang), jnp.sin(ang)
    one = jnp.ones(ang.shape[:-1] + (period - rot,), F32)
    ct = jnp.concatenate([cos, cos, one], axis=-1)
    st = jnp.concatenate([sin, sin, 0.0 * one], axis=-1)
    reps = LANE // period
    ct = jnp.tile(ct, (1, 1, reps)).reshape(-1, LANE)
    st = jnp.tile(st, (1, 1, reps)).reshape(-1, LANE)
    return ct, st


def _s5_weights(lam_re, lam_im, log_dt, b_re, b_im, c_re, c_im, dvec):
    hp = lax.Precision.HIGHEST
    g, p = lam_re.shape
    ch = S5_CHUNK
    dt = jnp.exp(log_dt)[:, None]
    mag = jnp.exp(lam_re * dt)
    ar, ai = mag * jnp.cos(lam_im * dt), mag * jnp.sin(lam_im * dt)
    den = lam_re * lam_re + lam_im * lam_im
    nr, ni = ar - 1.0, ai
    f_re = (nr * lam_re + ni * lam_im) / den
    f_im = (ni * lam_re - nr * lam_im) / den
    bb_re = f_re[..., None] * b_re - f_im[..., None] * b_im
    bb_im = f_re[..., None] * b_im + f_im[..., None] * b_re
    pr, pi = [jnp.ones_like(ar)], [jnp.zeros_like(ar)]
    for _ in range(ch):
        pr, pi = pr + [pr[-1] * ar - pi[-1] * ai], pi + [pr[-1] * ai + pi[-1] * ar]
    pr, pi = jnp.stack(pr), jnp.stack(pi)
    mr = pr[..., None] * bb_re - pi[..., None] * bb_im
    mi = pr[..., None] * bb_im + pi[..., None] * bb_re
    kern = (jnp.einsum('gjp,tgpi->tgji', c_re, mr, precision=hp)
            - jnp.einsum('gjp,tgpi->tgji', c_im, mi, precision=hp))
    eye = jnp.eye(S5_GB, dtype=F32)

    def group_diag(x):
        x = x.reshape(x.shape[:-3] + (S5_NGB, S5_GB) + x.shape[-2:])
        y = x[..., :, :, None, :] * eye[:, None, :, None]
        return y.reshape(x.shape[:-3] + (S5_GB * x.shape[-2], S5_GB * x.shape[-1]))

    a_in = jnp.arange(ch)[:, None]
    a_out = jnp.arange(ch)[None, :]
    lag = jnp.clip(a_out - a_in, 0, ch)
    bd_k = group_diag(kern.transpose(0, 1, 3, 2)).astype(BF16)
    toe = bd_k[lag] * (a_out >= a_in)[..., None, None, None].astype(BF16)
    w_toe = toe.transpose(2, 0, 3, 1, 4).reshape(S5_NGB, ch * LANE, ch * LANE)
    rev = jnp.arange(ch - 1, -1, -1)
    st = jnp.concatenate([group_diag(mr[:ch].transpose(0, 1, 3, 2)), group_diag(mi[:ch].transpose(0, 1, 3, 2))],
                         axis=-1).astype(BF16)
    w_st = st[rev].transpose(1, 0, 2, 3).reshape(S5_NGB, ch * LANE, 2 * S5_GB * p)
    pr1, pi1 = pr[1:], pi[1:]
    co_re = c_re[None] * pr1[:, :, None, :] - c_im[None] * pi1[:, :, None, :]
    co_im = -(c_re[None] * pi1[:, :, None, :] + c_im[None] * pr1[:, :, None, :])
    co = jnp.stack([group_diag(co_re.transpose(0, 1, 3, 2)), group_diag(co_im.transpose(0, 1, 3, 2))],
                   axis=2).astype(BF16)
    w_out = co.reshape(ch, S5_NGB, 2 * S5_GB * p, LANE).transpose(1, 2, 0, 3).reshape(
        S5_NGB, 2 * S5_GB * p, ch * LANE)
    w_all = (w_toe, w_st, w_out)
    lam_c = jnp.stack([pr[ch].reshape(S5_NGB, S5_GB * p), pi[ch].reshape(S5_NGB, S5_GB * p)], axis=1)
    d_t = jnp.tile(dvec.reshape(S5_NGB, 1, LANE), (1, 1, ch))
    return w_all, lam_c, d_t


def _block_diag(w):
    h, a, b = w.shape
    eye = jnp.eye(h, dtype=w.dtype)
    return jnp.einsum('hij,hg->higj', w, eye).reshape(h * a, h * b)


def _prep_layer(l, p):
    w_in = p["w_in"][l]
    offs = [0]
    for s in IN_SPLITS:
        offs.append(offs[-1] + s)
    seg = [w_in[:, offs[k]:offs[k + 1]] for k in range(len(IN_SPLITS))]
    (w_xr, w_gr, w_ql, w_kvl, w_kpe, w_qd, w_kd, w_vd, w_qi, w_ki, w_wi, w_us, w_gate) = seg
    rep = lambda w: jnp.tile(w, (1, IDX_HEADS))
    cols = [
        w_xr, w_gr, w_ql, w_kvl,
        _pad_lane(w_kpe), _pad_lane(_partner_cols(w_kpe, 1, MLA_ROPE, MLA_ROPE)),
        _to_slots(w_qd, DSA_HEADS, DSA_HEAD_DIM),
        _to_slots(_partner_cols(w_qd, DSA_HEADS, DSA_HEAD_DIM, DSA_ROT), DSA_HEADS, DSA_HEAD_DIM),
        _pad_lane(w_kd), _pad_lane(_partner_cols(w_kd, 1, DSA_HEAD_DIM, DSA_ROT)),
        _pad_lane(w_vd),
        w_qi, _partner_cols(w_qi, IDX_HEADS, IDX_DIM, IDX_ROT),
        rep(w_ki), rep(_partner_cols(w_ki, 1, IDX_DIM, IDX_ROT)),
        _pad_lane(w_wi), w_us,
    ]
    w_all = jnp.concatenate(cols, axis=1).astype(BF16)
    assert w_all.shape[1] == _C_END

    w_uq = p["mla_w_uq"][l]
    wuq = jnp.concatenate([_to_slots(w_uq, MLA_HEADS, MLA_QK),
                           _to_slots(_partner_cols(w_uq, MLA_HEADS, MLA_QK, MLA_ROPE), MLA_HEADS, MLA_QK)],
                          axis=1).astype(BF16)
    wkv = p["mla_w_ukv"][l].reshape(MLA_KV_LORA, MLA_HEADS, MLA_NOPE + MLA_V)
    kn = jnp.pad(wkv[..., :MLA_NOPE], ((0, 0), (0, 0), (MLA_ROPE, LANE - MLA_QK)))
    wv = wkv[..., MLA_NOPE:].reshape(MLA_KV_LORA, MLA_HEADS // 2, 2, MLA_V)
    zv = jnp.zeros_like(wv[:, :, 0])
    wv = jnp.stack([jnp.concatenate([wv[:, :, 0], zv], axis=-1),
                    jnp.concatenate([zv, wv[:, :, 1]], axis=-1)], axis=2)
    wukv = jnp.concatenate([kn.reshape(MLA_KV_LORA, MLA_HEADS * LANE),
                            wv.reshape(MLA_KV_LORA, MLA_HEADS * LANE)], axis=1).astype(BF16)

    gq, gk = p["mla_qk_gain"][l, 0], p["mla_qk_gain"][l, 1]
    dq, dk = p["dsa_qk_gain"][l, 0], p["dsa_qk_gain"][l, 1]
    vec_rows = [p["mla_q_norm"][l], p["mla_kv_norm"][l],
                gq, _partner_vec(gq, MLA_ROPE), gk, _partner_vec(gk, MLA_ROPE),
                dq, _partner_vec(dq, DSA_ROT), dk, _partner_vec(dk, DSA_ROT)]
    vec = jnp.stack([_pad_lane(v, MLA_Q_LORA) for v in vec_rows])

    s5_w, s5_lam, s5_d = _s5_weights(p["s5_lambda_re"][l], p["s5_lambda_im"][l], p["s5_log_dt"][l],
                                     p["s5_b_re"][l], p["s5_b_im"][l], p["s5_c_re"][l], p["s5_c_im"][l],
                                     p["s5_d"][l])

    ffn = [(p["ffn_w1"][l, j].astype(BF16), p["ffn_w3"][l, j].astype(BF16), p["ffn_w2"][l, j].astype(BF16))
           for j in range(2)]

    return dict(
        w_all=w_all, wuq=wuq, wukv=wukv, vec=vec,
        conv_w=p["conv_w"][l],
        rg_vec=jnp.stack([p["conv_b"][l], p["rg_ba"][l], p["rg_bx"][l], p["rg_lambda"][l]]),
        rg_wa=_block_diag(p["rg_wa"][l]).astype(BF16),
        rg_wx=_block_diag(p["rg_wx"][l]).astype(BF16),
        s5_w=s5_w, s5_lam=s5_lam, s5_d=s5_d,
        w_gate=w_gate.astype(BF16),
        w_branch=p["w_branch"][l].astype(BF16),
        w_glu=p["s5_w_glu"][l].astype(BF16),
        b_glu=p["s5_b_glu"][l].reshape(1, D_S5),
        w_out=p["w_out"][l].astype(BF16),
        ffn=ffn,
        norm_g=p["norm_g"][l],
    )


def _pick_tile(n, pref):
    t = pref
    while n % t:
        t //= 2
    return t


def kernel(x, c, positions, ada_w, ada_b, norm_g, ffn_w1, ffn_w3, ffn_w2, w_in,
           conv_w, conv_b, rg_wa, rg_ba, rg_wx, rg_bx, rg_lambda,
           mla_q_norm, mla_w_uq, mla_kv_norm, mla_w_ukv, mla_qk_gain, dsa_qk_gain,
           s5_lambda_re, s5_lambda_im, s5_log_dt, s5_b_re, s5_b_im, s5_c_re, s5_c_im,
           s5_d, s5_w_glu, s5_b_glu, w_branch, w_out):
    p = dict(norm_g=norm_g, ffn_w1=ffn_w1, ffn_w3=ffn_w3, ffn_w2=ffn_w2, w_in=w_in,
             conv_w=conv_w, conv_b=conv_b, rg_wa=rg_wa, rg_ba=rg_ba, rg_wx=rg_wx, rg_bx=rg_bx,
             rg_lambda=rg_lambda, mla_q_norm=mla_q_norm, mla_w_uq=mla_w_uq, mla_kv_norm=mla_kv_norm,
             mla_w_ukv=mla_w_ukv, mla_qk_gain=mla_qk_gain, dsa_qk_gain=dsa_qk_gain,
             s5_lambda_re=s5_lambda_re, s5_lambda_im=s5_lambda_im, s5_log_dt=s5_log_dt,
             s5_b_re=s5_b_re, s5_b_im=s5_b_im, s5_c_re=s5_c_re, s5_c_im=s5_c_im, s5_d=s5_d,
             s5_w_glu=s5_w_glu, s5_b_glu=s5_b_glu, w_branch=w_branch, w_out=w_out)
    bsz, seq, d = x.shape
    n = bsz * seq
    depth = ada_w.shape[0]
    assert seq % (S5_CHUNK * 8) == 0 and seq % _DSA_QB == 0

    tm = _pick_tile(seq, 512)
    tc = _pick_tile(seq, 256)
    bq = _pick_tile(seq, 512)

    mod = _ada_call(c, ada_w, ada_b)
    tabs = (_rope_table(positions, MLA_ROPE, LANE) + _rope_table(positions, DSA_ROT, LANE)
            + _rope_table(positions, IDX_ROT, IDX_DIM))
    tri = (jnp.arange(LANE)[:, None] < jnp.arange(LANE)[None, :]).astype(BF16)

    xf = x.reshape(n, d)
    for l in range(depth):
        lw = _prep_layer(l, p)
        g = lw["norm_g"]
        ml = mod[l]
        xf = _ffn_call(xf, ml, g[0:1], *lw["ffn"][0], 0, seq, tm)
        (xr, gr, qm, km, vm, qd, kd, vd, qi, ki, wi, us) = _inproj_call(xf, ml, g[1:2], lw, tabs, seq, tm)
        ya = _rglru_call(xr, gr, lw, bsz, seq, tc)
        yb = _flash_call(qm, km, vm, bsz, seq, bq)
        yc = _dsa_call(qd, kd, vd, qi, ki, wi, tri, bsz, seq)
        yd = _s5_call(us, lw, bsz, seq)
        xf = _merge_call(xf, ml, g[1:2], ya, yb, yc, yd, lw, seq, tm)
        xf = _ffn_call(xf, ml, g[2:3], *lw["ffn"][1], 6, seq, tm)
    return xf.reshape(bsz, seq, d)
```

```python
import functools
import math

import jax
import jax.numpy as jnp
from jax import lax
from jax.experimental import pallas as pl
from jax.experimental.pallas import tpu as pltpu

F32 = jnp.float32
BF16 = jnp.bfloat16

D_MODEL = 1024
EPS = 1e-6
ROPE_THETA = 500000.0
D_FF = 2816
N_ADA = 9

D_RNN = 512
RNN_HEADS = 8
RNN_HEAD_DIM = D_RNN // RNN_HEADS
CONV_WIDTH = 4
LRU_C = 8.0

MLA_HEADS = 8
MLA_NOPE = 64
MLA_ROPE = 32
MLA_V = 64
MLA_QK = MLA_ROPE + MLA_NOPE
MLA_Q_LORA = 256
MLA_KV_LORA = 128

DSA_HEADS = 8
DSA_HEAD_DIM = 64
DSA_ROT = DSA_HEAD_DIM // 4
IDX_HEADS = 8
IDX_DIM = 32
IDX_ROT = IDX_DIM // 4
TOPK_MAX = 256

S5_GROUP = 16
S5_GROUPS = 32
D_S5 = S5_GROUP * S5_GROUPS
S5_STATE = 64
S5_CHUNK = 8
S5_GB = 8
S5_NGB = S5_GROUPS // S5_GB

N_BRANCH = 4
BRANCH_W = 512
IN_SPLITS = (D_RNN, D_RNN, MLA_Q_LORA, MLA_KV_LORA, MLA_ROPE,
             DSA_HEADS * DSA_HEAD_DIM, DSA_HEAD_DIM, DSA_HEAD_DIM,
             IDX_HEADS * IDX_DIM, IDX_DIM, IDX_HEADS, D_S5, N_BRANCH * D_MODEL)

LANE = 128
NEG_BIG = -1e30
VMEM_LIMIT = 56 * 1024 * 1024

_C_XR = 0
_C_GR = _C_XR + D_RNN
_C_QL = _C_GR + D_RNN
_C_KVL = _C_QL + MLA_Q_LORA
_C_KPE = _C_KVL + MLA_KV_LORA
_C_QD = _C_KPE + LANE
_C_KD = _C_QD + DSA_HEADS * DSA_HEAD_DIM
_C_VD = _C_KD + LANE
_C_QI = _C_VD + LANE
_C_KI = _C_QI + IDX_HEADS * IDX_DIM
_C_WI = _C_KI + IDX_HEADS * IDX_DIM
_C_US = _C_WI + LANE
_C_END = _C_US + D_S5


def _dot(a, b):
    return jnp.dot(a, b, preferred_element_type=F32)


def _dot_nt(a, b):
    return lax.dot_general(a, b, (((1,), (1,)), ((), ())), preferred_element_type=F32)


def _sigmoid(x):
    return jax.nn.sigmoid(x)


def _gelu_tanh(x):
    return 0.5 * x * (1.0 + jnp.tanh(0.7978845608028654 * (x + 0.044715 * (x * x * x))))


def _rms_mod(x, g, shift, scale):
    ms = jnp.mean(x * x, axis=-1, keepdims=True)
    y = x * lax.rsqrt(ms + EPS) * g
    return y * (1.0 + scale) + shift


def _resident(shape):
    return pl.BlockSpec(shape, lambda *_: (0,) * len(shape), pipeline_mode=pl.Buffered(1))


def _layer(tail, *lead):
    return pl.BlockSpec((None,) * len(lead) + tuple(tail), lambda *_: tuple(lead) + (0,) * len(tail),
                        pipeline_mode=pl.Buffered(1))


def _mod_spec(l, tm, seq, d):
    return pl.BlockSpec((None, 1, N_ADA, d), lambda i: (l, i * tm // seq, 0, 0))


def _params(sem):
    return pltpu.CompilerParams(dimension_semantics=sem, vmem_limit_bytes=VMEM_LIMIT)


def _ada_kernel(c_ref, w_ref, b_ref, o_ref):
    c = c_ref[...]
    a = c * _sigmoid(c)
    w = w_ref[0]
    a_hi = a.astype(BF16)
    a_lo = (a - a_hi.astype(F32)).astype(BF16)
    w_hi = w.astype(BF16)
    w_lo = (w - w_hi.astype(F32)).astype(BF16)
    o_ref[0] = _dot(a_hi, w_hi) + _dot(a_lo, w_hi) + _dot(a_hi, w_lo) + b_ref[0]


def _ada_call(c, ada_w, ada_b):
    nl, d, n9 = ada_w.shape
    b = c.shape[0]
    tn = 1024
    out = pl.pallas_call(
        _ada_kernel,
        grid=(nl, n9 // tn),
        in_specs=[
            pl.BlockSpec((b, d), lambda l, j: (0, 0)),
            pl.BlockSpec((1, d, tn), lambda l, j: (l, 0, j)),
            pl.BlockSpec((1, 1, tn), lambda l, j: (l, 0, j)),
        ],
        out_specs=pl.BlockSpec((1, b, tn), lambda l, j: (l, 0, j)),
        out_shape=jax.ShapeDtypeStruct((nl, b, n9), F32),
        compiler_params=_params(("arbitrary", "arbitrary")),
        name="ada_mod",
    )(c, ada_w, ada_b.reshape(nl, 1, n9))
    return out.reshape(nl, b, N_ADA, d)


_FFN_CHUNK = 256


def _ffn_kernel(row0, x_ref, mod_ref, g_ref, w1_ref, w3_ref, w2_ref, o_ref, h_ref):
    x = x_ref[...]
    m = mod_ref[0]
    u = _rms_mod(x, g_ref[...], m[row0:row0 + 1], m[row0 + 1:row0 + 2]).astype(BF16)
    for c in range(0, w1_ref.shape[1], _FFN_CHUNK):
        h1 = _dot(u, w1_ref[:, c:c + _FFN_CHUNK])
        h3 = _dot(u, w3_ref[:, c:c + _FFN_CHUNK])
        h_ref[:, c:c + _FFN_CHUNK] = (h1 * _sigmoid(h1) * h3).astype(BF16)
    o_ref[...] = x + 0.5 * (1.0 + m[row0 + 2:row0 + 3]) * _dot(h_ref[...], w2_ref[...])


def _ffn_call(x, mod, pw, l, j, seq, tm):
    n, d = x.shape
    f = pw["ffn_w1"].shape[-1]
    return pl.pallas_call(
        functools.partial(_ffn_kernel, 6 * j),
        grid=(n // tm,),
        in_specs=[
            pl.BlockSpec((tm, d), lambda i: (i, 0)),
            _mod_spec(l, tm, seq, d),
            _layer((1, d), l, 2 * j),
            _layer((d, f), l, j), _layer((d, f), l, j), _layer((f, d), l, j),
        ],
        out_specs=pl.BlockSpec((tm, d), lambda i: (i, 0)),
        out_shape=jax.ShapeDtypeStruct((n, d), F32),
        scratch_shapes=[pltpu.VMEM((tm, f), BF16)],
        compiler_params=_params(("arbitrary",)),
        name="ffn",
    )(x, mod, pw["norm_g"], pw["ffn_w1"], pw["ffn_w3"], pw["ffn_w2"])


def _rope_tiles(tiles, cos, sin, rot):
    rows = tiles[0].shape[0]
    partner = _dot(jnp.concatenate(tiles, axis=0).astype(BF16), rot)
    return [t * cos + partner[i * rows:(i + 1) * rows] * sin for i, t in enumerate(tiles)]


def _inproj_kernel(x_ref, mod_ref, g_ref, w_ref, wuq_ref, wukv_ref, vec_ref, rot_ref,
                   cm_ref, sm_ref, cd_ref, sd_ref, ci_ref, si_ref,
                   xr_ref, gr_ref, qm_ref, km_ref, vm_ref, qd_ref, kd_ref, vd_ref,
                   qi_ref, ki_ref, wi_ref, us_ref):
    x = x_ref[...]
    m = mod_ref[0]
    u = _rms_mod(x, g_ref[...], m[3:4], m[4:5]).astype(BF16)
    vec = vec_ref[...]
    lane = lax.broadcasted_iota(jnp.int32, (1, LANE), 1)
    z = _dot(u, w_ref[...])

    xr_ref[...] = z[:, _C_XR:_C_GR].astype(xr_ref.dtype)
    gr_ref[...] = z[:, _C_GR:_C_QL].astype(gr_ref.dtype)
    wi_ref[...] = z[:, _C_WI:_C_US]
    us_ref[...] = z[:, _C_US:_C_END].astype(us_ref.dtype)

    ql = z[:, _C_QL:_C_KVL]
    qn = (ql * lax.rsqrt(jnp.mean(ql * ql, axis=-1, keepdims=True) + EPS)
          * vec[0:1, :MLA_Q_LORA]).astype(BF16)
    q2 = _dot(qn, wuq_ref[...])
    kvl = z[:, _C_KVL:_C_KPE]
    kvn = (kvl * lax.rsqrt(jnp.mean(kvl * kvl, axis=-1, keepdims=True) + EPS)
           * vec[1:2, :MLA_KV_LORA]).astype(BF16)
    kv = _dot(kvn, wukv_ref[...])
    kpe = z[:, _C_KPE:_C_QD]
    gq = vec[2:3, :LANE]
    gk = vec[3:4, :LANE]
    roped = _rope_tiles([kpe * gk] + [q2[:, h * LANE:(h + 1) * LANE] * gq for h in range(MLA_HEADS)],
                        cm_ref[...], sm_ref[...], rot_ref[0])
    kpe_rot = roped[0]
    kpe_ss = jnp.sum(kpe * kpe, axis=-1, keepdims=True)
    q_tiles, k_tiles = [], []
    for h in range(MLA_HEADS):
        hs = slice(h * LANE, (h + 1) * LANE)
        qs = q2[:, hs]
        s = lax.rsqrt(jnp.sum(qs * qs, axis=-1, keepdims=True) * (1.0 / MLA_QK) + EPS) * MLA_QK ** -0.5
        q_tiles.append((s * roped[1 + h]).astype(qm_ref.dtype))
        kn = kv[:, hs]
        s = lax.rsqrt((jnp.sum(kn * kn, axis=-1, keepdims=True) + kpe_ss) * (1.0 / MLA_QK) + EPS)
        k_tiles.append((s * (kn * gk + kpe_rot)).astype(km_ref.dtype))
    qm_ref[...] = jnp.concatenate(q_tiles, axis=1)
    km_ref[...] = jnp.concatenate(k_tiles, axis=1)
    nslot = MLA_HEADS * LANE
    vlane = lax.broadcasted_iota(jnp.int32, (1, nslot), 1)
    v_is_low = (vlane // LANE) % 2 == 0
    ones_half = jnp.where(((vlane % LANE) < MLA_V) == v_is_low, 0.0, 1.0)
    vm_ref[...] = (kv[:, nslot:] + ones_half).astype(vm_ref.dtype)

    gd = vec[4:5, :LANE]
    low = lane < DSA_HEAD_DIM
    kd = z[:, _C_KD:_C_VD]
    xqs = [z[:, _C_QD + t * LANE:_C_QD + (t + 1) * LANE] for t in range(DSA_HEADS // 2)]
    roped = _rope_tiles([xq * gd for xq in xqs] + [kd * vec[5:6, :LANE]], cd_ref[...], sd_ref[...], rot_ref[1])
    d_tiles = []
    for t in range(DSA_HEADS // 2):
        x2 = xqs[t] * xqs[t]
        ss_even = jnp.sum(jnp.where(low, x2, 0.0), axis=-1, keepdims=True)
        ss_odd = jnp.sum(jnp.where(low, 0.0, x2), axis=-1, keepdims=True)
        y = roped[t]
        s_even = lax.rsqrt(ss_even * (1.0 / DSA_HEAD_DIM) + EPS) * DSA_HEAD_DIM ** -0.5
        s_odd = lax.rsqrt(ss_odd * (1.0 / DSA_HEAD_DIM) + EPS) * DSA_HEAD_DIM ** -0.5
        d_tiles.append(jnp.where(low, y * s_even, 0.0).astype(qd_ref.dtype))
        d_tiles.append(jnp.where(low, pltpu.roll(y, DSA_HEAD_DIM, 1) * s_odd, 0.0).astype(qd_ref.dtype))
    qd_ref[...] = jnp.concatenate(d_tiles, axis=1)
    s = lax.rsqrt(jnp.sum(kd * kd, axis=-1, keepdims=True) * (1.0 / DSA_HEAD_DIM) + EPS)
    kd_ref[...] = (s * roped[-1]).astype(kd_ref.dtype)
    vd_ref[...] = (z[:, _C_VD:_C_QI] + jnp.where(low, 0.0, 1.0)).astype(vd_ref.dtype)

    ntile = IDX_HEADS * IDX_DIM // LANE
    roped = _rope_tiles([z[:, c + t * LANE:c + (t + 1) * LANE] for c in (_C_QI, _C_KI) for t in range(ntile)],
                        ci_ref[...], si_ref[...], rot_ref[2])
    qi_ref[...] = jnp.concatenate(roped[:ntile], axis=1).astype(qi_ref.dtype)
    ki_ref[...] = jnp.concatenate(roped[ntile:], axis=1).astype(ki_ref.dtype)


def _inproj_call(x, mod, pw, l, tabs, seq, tm):
    n, d = x.shape
    row = lambda w: pl.BlockSpec((tm, w), lambda i: (i, 0))
    out_widths = [(D_RNN, BF16), (D_RNN, BF16), (MLA_HEADS * LANE, BF16), (MLA_HEADS * LANE, BF16),
                  (MLA_HEADS * LANE, BF16), (DSA_HEADS * LANE, BF16), (LANE, BF16), (LANE, BF16),
                  (IDX_HEADS * IDX_DIM, BF16), (IDX_HEADS * IDX_DIM, BF16), (LANE, F32), (D_S5, BF16)]
    return pl.pallas_call(
        _inproj_kernel,
        grid=(n // tm,),
        in_specs=[
            row(d), _mod_spec(l, tm, seq, d), _layer((1, d), l, 1),
            _layer(pw["w_all"].shape[1:], l), _layer(pw["wuq"].shape[1:], l),
            _layer(pw["wukv"].shape[1:], l), _layer(pw["vec"].shape[1:], l),
            _resident(pw["rot"].shape),
        ] + [row(LANE)] * 6,
        out_specs=[row(w) for w, _ in out_widths],
        out_shape=[jax.ShapeDtypeStruct((n, w), dt) for w, dt in out_widths],
        compiler_params=_params(("arbitrary",)),
        name="in_proj",
    )(x, mod, pw["norm_g"], pw["w_all"], pw["wuq"], pw["wukv"], pw["vec"], pw["rot"], *tabs)


def _rglru_kernel(tc, x_ref, gate_ref, cw_ref, vec_ref, wa_ref, wx_ref, o_ref, xs_ref, h_ref):
    j = pl.program_id(1)

    @pl.when(j == 0)
    def _():
        xs_ref[0:8, :] = jnp.zeros((8, D_RNN), F32)
        h_ref[...] = jnp.zeros(h_ref.shape, F32)

    xs_ref[8:8 + tc, :] = x_ref[...].astype(F32)
    cw = cw_ref[...]
    vec = vec_ref[...]
    xc = vec[0:1]
    for k in range(CONV_WIDTH):
        xc = xc + cw[k:k + 1] * xs_ref[pl.ds(8 - (CONV_WIDTH - 1) + k, tc), :]
    xs_ref[0:8, :] = xs_ref[tc:tc + 8, :]

    xb = xc.astype(BF16)
    r = _sigmoid(_dot(xb, wa_ref[...]) + vec[1:2])
    ig = _sigmoid(_dot(xb, wx_ref[...]) + vec[2:3])
    nl = -vec[3:4]
    softplus = jnp.maximum(nl, 0.0) + jnp.log(1.0 + jnp.exp(-jnp.abs(nl)))
    log_a = (-LRU_C) * r * softplus
    a = jnp.exp(log_a)
    z = 2.0 * log_a
    series = -z * (1.0 + z * (0.5 + z * (1.0 / 6.0 + z * (1.0 / 24.0 + z * (1.0 / 120.0 + z * (1.0 / 720.0))))))
    nem1 = jnp.where(z > -0.25, series, 1.0 - jnp.exp(z))
    b = jnp.sqrt(nem1) * ig * xc

    rowi = lax.broadcasted_iota(jnp.int32, (tc, D_RNN), 0)
    d = 1
    while d < tc:
        keep = rowi >= d
        a_s = jnp.where(keep, pltpu.roll(a, d, 0), 1.0)
        b_s = jnp.where(keep, pltpu.roll(b, d, 0), 0.0)
        b = a * b_s + b
        a = a * a_s
        d *= 2
    h = b + a * h_ref[...]
    h_ref[...] = h[tc - 1:tc, :]
    o_ref[...] = (h * _gelu_tanh(gate_ref[...].astype(F32))).astype(o_ref.dtype)


def _rglru_call(xr, gr, pw, l, bsz, seq, tc):
    n = xr.shape[0]
    nt = seq // tc
    row = pl.BlockSpec((tc, D_RNN), lambda b, j: (b * nt + j, 0))
    return pl.pallas_call(
        functools.partial(_rglru_kernel, tc),
        grid=(bsz, nt),
        in_specs=[row, row,
                  _layer((CONV_WIDTH, D_RNN), l), _layer((4, D_RNN), l),
                  _layer((D_RNN, D_RNN), l), _layer((D_RNN, D_RNN), l)],
        out_specs=row,
        out_shape=jax.ShapeDtypeStruct((n, D_RNN), BF16),
        scratch_shapes=[pltpu.VMEM((tc + 8, D_RNN), F32), pltpu.VMEM((1, D_RNN), F32)],
        compiler_params=_params(("arbitrary", "arbitrary")),
        name="rglru",
    )(xr, gr, pw["conv_w"], pw["rg_vec"], pw["rg_wa"], pw["rg_wx"])


def _flash_kernel(q_ref, k_ref, v_ref, o_ref, m_ref, acc_ref):
    i = pl.program_id(1)
    j = pl.program_id(2)
    bq, bk = q_ref.shape[0], k_ref.shape[0]

    @pl.when(j == 0)
    def _():
        m_ref[...] = jnp.full(m_ref.shape, NEG_BIG, F32)
        acc_ref[...] = jnp.zeros(acc_ref.shape, F32)

    def step(masked):
        if masked:
            causal = (lax.broadcasted_iota(jnp.int32, (bq, bk), 1)
                      <= lax.broadcasted_iota(jnp.int32, (bq, bk), 0))
        m_out, acc_out = [], []
        for h in range(MLA_HEADS):
            hs = slice(h * LANE, (h + 1) * LANE)
            s = _dot_nt(q_ref[:, hs], k_ref[:, hs])
            if masked:
                s = jnp.where(causal, s, NEG_BIG)
            m_prev = m_ref[h]
            m_new = jnp.maximum(m_prev, jnp.max(s, axis=1, keepdims=True))
            p = jnp.exp(s - jnp.concatenate([m_new] * (bk // LANE), axis=1))
            acc_out.append(jnp.exp(m_prev - m_new) * acc_ref[h] + _dot(p.astype(BF16), v_ref[:, hs]))
            m_out.append(m_new)
        m_ref[...] = jnp.stack(m_out)
        acc_ref[...] = jnp.stack(acc_out)

    @pl.when(j < i)
    def _():
        step(False)

    @pl.when(j == i)
    def _():
        step(True)
        lane = lax.broadcasted_iota(jnp.int32, (bq, LANE), 1)
        outs = []
        for pr in range(MLA_HEADS // 2):
            even = acc_ref[2 * pr]
            odd = acc_ref[2 * pr + 1]
            outs.append(jnp.where(lane < MLA_V, even / pltpu.roll(even, MLA_V, 1),
                                  odd / pltpu.roll(odd, MLA_V, 1)))
        o_ref[...] = jnp.concatenate(outs, axis=1).astype(o_ref.dtype)


def _flash_call(q, k, v, bsz, seq, bq):
    n = q.shape[0]
    nq = seq // bq
    kv_row = lambda b, i, j: (b * nq + jnp.minimum(j, i), 0)
    return pl.pallas_call(
        _flash_kernel,
        grid=(bsz, nq, nq),
        in_specs=[
            pl.BlockSpec((bq, MLA_HEADS * LANE), lambda b, i, j: (b * nq + i, 0)),
            pl.BlockSpec((bq, MLA_HEADS * LANE), kv_row),
            pl.BlockSpec((bq, MLA_HEADS * LANE), kv_row),
        ],
        out_specs=pl.BlockSpec((bq, MLA_HEADS * MLA_V), lambda b, i, j: (b * nq + i, 0)),
        out_shape=jax.ShapeDtypeStruct((n, MLA_HEADS * MLA_V), BF16),
        scratch_shapes=[pltpu.VMEM((MLA_HEADS, bq, LANE), F32), pltpu.VMEM((MLA_HEADS, bq, LANE), F32)],
        compiler_params=_params(("arbitrary",) * 3),
        name="mla_flash",
    )(q, k, v)


_DSA_QB = 256
_DSA_HG = 2
_DSA_CHAINS = 8
_INT_MIN = -2 ** 31


def _dsa_body(ns, topk, i, q_ref, k_ref, v_ref, qi_ref, ki_ref, w_ref, tri_ref, o_ref):
    qb = _DSA_QB
    hg = _DSA_HG
    qi = qi_ref[...]
    ki = ki_ref[0:ns, :]
    w = w_ref[...]
    head_of_lane = jnp.right_shift(lax.broadcasted_iota(jnp.int32, qi.shape, 1), int(math.log2(IDX_DIM)))
    zero = jnp.zeros_like(qi)
    score = jnp.zeros((qb, ns), F32)
    for g in range(0, IDX_HEADS, hg):
        lhs = jnp.concatenate([jnp.where(head_of_lane == h, qi, zero) for h in range(g, g + hg)], axis=0)
        rel = jnp.maximum(_dot_nt(lhs, ki), 0.0)
        for t in range(hg):
            score = score + w[:, g + t:g + t + 1] * rel[t * qb:(t + 1) * qb]

    qpos = lax.broadcasted_iota(jnp.int32, (qb, ns), 0) + i * qb
    kpos = lax.broadcasted_iota(jnp.int32, (qb, ns), 1)
    bits = pltpu.bitcast(score + 0.0, jnp.int32)
    key = jnp.where(bits < 0, bits ^ jnp.int32(0x7FFFFFFF), bits)
    key = jnp.where(kpos <= qpos, key, jnp.int32(_INT_MIN))
    kk = jnp.minimum(lax.broadcasted_iota(jnp.int32, (qb, 1), 0) + (i * qb + 1), topk).astype(F32)

    rows = qb // _DSA_CHAINS
    keys = [key[r * rows:(r + 1) * rows] for r in range(_DSA_CHAINS)]
    kks = [kk[r * rows:(r + 1) * rows] for r in range(_DSA_CHAINS)]

    def body(it, thrs):
        bit = lax.shift_left(jnp.int32(1), jnp.int32(31) - it)
        out = []
        for kr, kkr, t in zip(keys, kks, thrs):
            c = t + bit
            cnt = jnp.sum(jnp.where(kr >= c, 1.0, 0.0), axis=1, keepdims=True)
            out.append(jnp.where(cnt >= kkr, c, t))
        return tuple(out)

    thrs = lax.fori_loop(0, 32, body, tuple(jnp.full((rows, 1), _INT_MIN, jnp.int32) for _ in keys), unroll=2)
    thr = jnp.concatenate(thrs, axis=0)

    nchunk = ns // LANE
    need = kk - jnp.sum(jnp.where(key > thr, 1.0, 0.0), axis=1, keepdims=True)
    eqs = [jnp.where(key[:, c * LANE:(c + 1) * LANE] == thr, 1.0, 0.0) for c in range(nchunk)]
    before_all = _dot(jnp.concatenate(eqs, axis=0).astype(BF16), tri_ref[...])
    run = jnp.zeros((qb, 1), F32)
    bias_chunks = []
    for c in range(nchunk):
        before = before_all[c * qb:(c + 1) * qb] + run
        take = jnp.where(key[:, c * LANE:(c + 1) * LANE] > thr, 1.0, jnp.where(before < need, eqs[c], 0.0))
        bias_chunks.append(jnp.where(take > 0.5, 0.0, NEG_BIG))
        run = run + jnp.sum(eqs[c], axis=1, keepdims=True)
    bias = jnp.concatenate(bias_chunks, axis=1)

    k = k_ref[0:ns, :]
    v = v_ref[0:ns, :]
    lane = lax.broadcasted_iota(jnp.int32, (qb, LANE), 1)
    for g in range(0, DSA_HEADS, hg):
        q = jnp.concatenate([q_ref[:, h * LANE:(h + 1) * LANE] for h in range(g, g + hg)], axis=0)
        s = _dot_nt(q, k).reshape(hg, qb, ns) + bias[None]
        p = jnp.exp(s - jnp.max(s, axis=2, keepdims=True))
        o = _dot(p.reshape(hg * qb, ns).astype(BF16), v)
        for t in range(0, hg, 2):
            even = o[t * qb:(t + 1) * qb]
            odd = o[(t + 1) * qb:(t + 2) * qb]
            pr = (g + t) // 2
            o_ref[:, pr * LANE:(pr + 1) * LANE] = jnp.where(
                lane < DSA_HEAD_DIM, even / pltpu.roll(even, DSA_HEAD_DIM, 1),
                pltpu.roll(odd, DSA_HEAD_DIM, 1) / odd).astype(o_ref.dtype)


def _dsa_kernel(seq, bucket, topk, q_ref, k_ref, v_ref, qi_ref, ki_ref, w_ref, tri_ref, o_ref):
    i = pl.program_id(1)
    for bk in range(seq // bucket):
        @pl.when((i * _DSA_QB) // bucket == bk)
        def _(bk=bk):
            _dsa_body((bk + 1) * bucket, topk, i, q_ref, k_ref, v_ref, qi_ref, ki_ref, w_ref,
                      tri_ref, o_ref)


def _dsa_call(qd, kd, vd, qi, ki, wi, tri, bsz, seq):
    n = qd.shape[0]
    qb = _DSA_QB
    nq = seq // qb
    bucket = min(512, seq)
    topk = min(TOPK_MAX, seq // 4)
    qrow = lambda w: pl.BlockSpec((qb, w), lambda b, i: (b * nq + i, 0))
    full = lambda w: pl.BlockSpec((seq, w), lambda b, i: (b, 0))
    return pl.pallas_call(
        functools.partial(_dsa_kernel, seq, bucket, topk),
        grid=(bsz, nq),
        in_specs=[qrow(DSA_HEADS * LANE), full(LANE), full(LANE),
                  qrow(IDX_HEADS * IDX_DIM), full(IDX_HEADS * IDX_DIM), qrow(LANE),
                  _resident((LANE, LANE))],
        out_specs=qrow(DSA_HEADS * DSA_HEAD_DIM),
        out_shape=jax.ShapeDtypeStruct((n, DSA_HEADS * DSA_HEAD_DIM), BF16),
        compiler_params=_params(("arbitrary", "arbitrary")),
        name="dsa",
    )(qd, kd, vd, qi, ki, wi, tri)


def _s5_kernel(rows, u_ref, wt_ref, ws_ref, wo_ref, lam_ref, d_ref, o_ref, uf_ref, yf_ref):
    half = S5_GB * S5_STATE
    rowi = lax.broadcasted_iota(jnp.int32, (rows, half), 0)
    for gb in range(S5_NGB):
        gs = slice(gb * LANE, (gb + 1) * LANE)
        uf_ref[gb] = u_ref[:, gs].astype(F32)
        uf = jnp.concatenate([uf_ref[gb, pl.ds(a, rows, stride=S5_CHUNK), :] for a in range(S5_CHUNK)], axis=1)
        u = uf.astype(BF16)
        y = _dot(u, wt_ref[gb])
        st = _dot(u, ws_ref[gb])
        xr = st[:, :half]
        xi = st[:, half:]
        lam = lam_ref[gb]
        lr = lam[0:1]
        li = lam[1:2]
        d = 1
        while d < rows:
            keep = rowi >= d
            sr = jnp.where(keep, pltpu.roll(xr, d, 0), 0.0)
            si = jnp.where(keep, pltpu.roll(xi, d, 0), 0.0)
            xr, xi = xr + lr * sr - li * si, xi + lr * si + li * sr
            lr, li = lr * lr - li * li, 2.0 * lr * li
            d *= 2
        keep = rowi >= 1
        pr = jnp.where(keep, pltpu.roll(xr, 1, 0), 0.0)
        pi = jnp.where(keep, pltpu.roll(xi, 1, 0), 0.0)
        xp = jnp.concatenate([pr, pi], axis=1).astype(BF16)
        yg = _gelu_tanh(y + _dot(xp, wo_ref[gb]) + d_ref[gb] * uf)
        for a in range(S5_CHUNK):
            yf_ref[gb, pl.ds(a, rows, stride=S5_CHUNK), :] = yg[:, a * LANE:(a + 1) * LANE]
        o_ref[:, gs] = yf_ref[gb].astype(o_ref.dtype)


def _s5_call(us, pw, l, bsz, seq):
    n = us.shape[0]
    blk = pl.BlockSpec((seq, D_S5), lambda b: (b, 0))
    wshape = pw["s5_wt"].shape[1:]
    return pl.pallas_call(
        functools.partial(_s5_kernel, seq // S5_CHUNK),
        grid=(bsz,),
        in_specs=[blk, _layer(wshape, l), _layer(wshape, l), _layer(wshape, l),
                  _layer(pw["s5_lam"].shape[1:], l), _layer(pw["s5_d"].shape[1:], l)],
        out_specs=blk,
        out_shape=jax.ShapeDtypeStruct((n, D_S5), BF16),
        scratch_shapes=[pltpu.VMEM((S5_NGB, seq, LANE), F32), pltpu.VMEM((S5_NGB, seq, LANE), F32)],
        compiler_params=_params(("arbitrary",)),
        name="s5",
    )(us, pw["s5_wt"], pw["s5_ws"], pw["s5_wo"], pw["s5_lam"], pw["s5_d"])


def _merge_kernel(x_ref, mod_ref, g_ref, ya_ref, yb_ref, yc_ref, yd_ref,
                  wg_ref, wb_ref, wglu_ref, bglu_ref, wout_ref, o_ref):
    x = x_ref[...]
    m = mod_ref[0]
    d = x.shape[1]
    u = _rms_mod(x, g_ref[...], m[3:4], m[4:5]).astype(BF16)
    yd = yd_ref[...]
    ydg = (yd.astype(F32) * _sigmoid(_dot(yd, wglu_ref[...]) + bglu_ref[...])).astype(BF16)
    ys = (ya_ref[...], yb_ref[...], yc_ref[...], ydg)
    merged = jnp.zeros(x.shape, F32)
    for nb in range(N_BRANCH):
        gate = _sigmoid(_dot(u, wg_ref[:, nb * d:(nb + 1) * d]))
        merged = merged + gate * _dot(ys[nb], wb_ref[nb])
    o_ref[...] = x + (1.0 + m[5:6]) * _dot(merged.astype(BF16), wout_ref[...])


def _merge_call(x, mod, ya, yb, yc, yd, pw, l, seq, tm):
    n, d = x.shape
    row = lambda w: pl.BlockSpec((tm, w), lambda i: (i, 0))
    return pl.pallas_call(
        _merge_kernel,
        grid=(n // tm,),
        in_specs=[row(d), _mod_spec(l, tm, seq, d), _layer((1, d), l, 1),
                  row(BRANCH_W), row(BRANCH_W), row(BRANCH_W), row(BRANCH_W),
                  _layer((d, N_BRANCH * d), l), _layer((N_BRANCH, BRANCH_W, d), l),
                  _layer((D_S5, D_S5), l), _layer((1, D_S5), l), _layer((d, d), l)],
        out_specs=row(d),
        out_shape=jax.ShapeDtypeStruct((n, d), F32),
        compiler_params=_params(("arbitrary",)),
        name="merge",
    )(x, mod, pw["norm_g"], ya, yb, yc, yd, pw["w_gate"], pw["w_branch"], pw["w_glu"], pw["b_glu"], pw["w_out"])


def _to_slots(w, nheads, hdim):
    lead = w.shape[:-1]
    wh = w.reshape(lead + (nheads, hdim))
    wh = jnp.pad(wh, [(0, 0)] * (len(lead) + 1) + [(0, LANE - hdim)])
    return wh.reshape(lead + (nheads * LANE,))


def _pad_lane(v, width=LANE):
    return jnp.pad(v, [(0, 0)] * (v.ndim - 1) + [(0, width - v.shape[-1])])


def _rope_table(positions, rot, period):
    j = jnp.arange(LANE) % period
    inv = jnp.where(j < rot, ROPE_THETA ** (-(2 * (j % (rot // 2))).astype(F32) / rot), 0.0)
    ang = positions.astype(F32).reshape(-1, 1) * inv[None, :]
    return jnp.cos(ang), jnp.sin(ang)


def _rot_matrix(rot, period):
    src = jnp.arange(LANE)[:, None]
    dst = jnp.arange(LANE)[None, :]
    j = dst % period
    half = rot // 2
    return (jnp.where((j < half) & (src == dst + half), -1.0, 0.0)
            + jnp.where((j >= half) & (j < rot) & (src == dst - half), 1.0, 0.0)).astype(BF16)


def _s5_weights(lam_re, lam_im, log_dt, b_re, b_im, c_re, c_im, dvec):
    hp = lax.Precision.HIGHEST
    g, p = lam_re.shape
    ch = S5_CHUNK
    dt = jnp.exp(log_dt)[:, None]
    mag = jnp.exp(lam_re * dt)
    ar, ai = mag * jnp.cos(lam_im * dt), mag * jnp.sin(lam_im * dt)
    den = lam_re * lam_re + lam_im * lam_im
    nr, ni = ar - 1.0, ai
    f_re = (nr * lam_re + ni * lam_im) / den
    f_im = (ni * lam_re - nr * lam_im) / den
    bb_re = f_re[..., None] * b_re - f_im[..., None] * b_im
    bb_im = f_re[..., None] * b_im + f_im[..., None] * b_re
    pr, pi = [jnp.ones_like(ar)], [jnp.zeros_like(ar)]
    for _ in range(ch):
        pr, pi = pr + [pr[-1] * ar - pi[-1] * ai], pi + [pr[-1] * ai + pi[-1] * ar]
    pr, pi = jnp.stack(pr), jnp.stack(pi)
    mr = pr[..., None] * bb_re - pi[..., None] * bb_im
    mi = pr[..., None] * bb_im + pi[..., None] * bb_re
    kern = (jnp.einsum('gjp,tgpi->tgji', c_re, mr, precision=hp)
            - jnp.einsum('gjp,tgpi->tgji', c_im, mi, precision=hp))
    eye = jnp.eye(S5_GB, dtype=F32)

    def group_diag(x):
        x = x.reshape(x.shape[:-3] + (S5_NGB, S5_GB) + x.shape[-2:])
        y = x[..., :, :, None, :] * eye[:, None, :, None]
        return y.reshape(x.shape[:-3] + (S5_GB * x.shape[-2], S5_GB * x.shape[-1]))

    a_in = jnp.arange(ch)[:, None]
    a_out = jnp.arange(ch)[None, :]
    lag = jnp.clip(a_out - a_in, 0, ch)
    bd_k = group_diag(kern.transpose(0, 1, 3, 2)).astype(BF16)
    toe = bd_k[lag] * (a_out >= a_in)[..., None, None, None].astype(BF16)
    w_toe = toe.transpose(2, 0, 3, 1, 4).reshape(S5_NGB, ch * LANE, ch * LANE)
    rev = jnp.arange(ch - 1, -1, -1)
    st = jnp.concatenate([group_diag(mr[:ch].transpose(0, 1, 3, 2)), group_diag(mi[:ch].transpose(0, 1, 3, 2))],
                         axis=-1).astype(BF16)
    w_st = st[rev].transpose(1, 0, 2, 3).reshape(S5_NGB, ch * LANE, 2 * S5_GB * p)
    pr1, pi1 = pr[1:], pi[1:]
    co_re = c_re[None] * pr1[:, :, None, :] - c_im[None] * pi1[:, :, None, :]
    co_im = -(c_re[None] * pi1[:, :, None, :] + c_im[None] * pr1[:, :, None, :])
    co = jnp.stack([group_diag(co_re.transpose(0, 1, 3, 2)), group_diag(co_im.transpose(0, 1, 3, 2))],
                   axis=2).astype(BF16)
    w_out = co.reshape(ch, S5_NGB, 2 * S5_GB * p, LANE).transpose(1, 2, 0, 3).reshape(
        S5_NGB, 2 * S5_GB * p, ch * LANE)
    lam_c = jnp.stack([pr[ch].reshape(S5_NGB, S5_GB * p), pi[ch].reshape(S5_NGB, S5_GB * p)], axis=1)
    d_t = jnp.tile(dvec.reshape(S5_NGB, 1, LANE), (1, 1, ch))
    return w_toe, w_st, w_out, lam_c, d_t


def _block_diag(w):
    h, a, b = w.shape
    eye = jnp.eye(h, dtype=w.dtype)
    return jnp.einsum('hij,hg->higj', w, eye).reshape(h * a, h * b)


def _prep_params(p):
    w_in = p["w_in"]
    nl = w_in.shape[0]
    offs = [0]
    for s in IN_SPLITS:
        offs.append(offs[-1] + s)
    seg = [w_in[:, :, offs[k]:offs[k + 1]] for k in range(len(IN_SPLITS))]
    (w_xr, w_gr, w_ql, w_kvl, w_kpe, w_qd, w_kd, w_vd, w_qi, w_ki, w_wi, w_us, w_gate) = seg
    w_all = jnp.concatenate([w_xr, w_gr, w_ql, w_kvl, _pad_lane(w_kpe), w_qd, _pad_lane(w_kd), _pad_lane(w_vd),
                             w_qi, jnp.tile(w_ki, (1, 1, IDX_HEADS)), _pad_lane(w_wi), w_us],
                            axis=2).astype(BF16)
    assert w_all.shape[2] == _C_END

    wuq = _to_slots(p["mla_w_uq"], MLA_HEADS, MLA_QK).astype(BF16)
    wkv = p["mla_w_ukv"].reshape(nl, MLA_KV_LORA, MLA_HEADS, MLA_NOPE + MLA_V)
    kn = jnp.pad(wkv[..., :MLA_NOPE], ((0, 0), (0, 0), (0, 0), (MLA_ROPE, LANE - MLA_QK)))
    wv = wkv[..., MLA_NOPE:].reshape(nl, MLA_KV_LORA, MLA_HEADS // 2, 2, MLA_V)
    zv = jnp.zeros_like(wv[:, :, :, 0])
    wv = jnp.stack([jnp.concatenate([wv[:, :, :, 0], zv], axis=-1),
                    jnp.concatenate([zv, wv[:, :, :, 1]], axis=-1)], axis=3)
    wukv = jnp.concatenate([kn.reshape(nl, MLA_KV_LORA, MLA_HEADS * LANE),
                            wv.reshape(nl, MLA_KV_LORA, MLA_HEADS * LANE)], axis=2).astype(BF16)

    dq = p["dsa_qk_gain"][:, 0]
    vec = jnp.stack([_pad_lane(v, MLA_Q_LORA) for v in (
        p["mla_q_norm"], p["mla_kv_norm"], p["mla_qk_gain"][:, 0], p["mla_qk_gain"][:, 1],
        jnp.concatenate([dq, dq], axis=-1), p["dsa_qk_gain"][:, 1])], axis=1)

    s5_wt, s5_ws, s5_wo, s5_lam, s5_d = jax.vmap(_s5_weights)(
        p["s5_lambda_re"], p["s5_lambda_im"], p["s5_log_dt"], p["s5_b_re"], p["s5_b_im"],
        p["s5_c_re"], p["s5_c_im"], p["s5_d"])

    return dict(
        w_all=w_all, wuq=wuq, wukv=wukv, vec=vec,
        rot=jnp.stack([_rot_matrix(MLA_ROPE, LANE), _rot_matrix(DSA_ROT, DSA_HEAD_DIM),
                       _rot_matrix(IDX_ROT, IDX_DIM)]),
        conv_w=p["conv_w"],
        rg_vec=jnp.stack([p["conv_b"], p["rg_ba"], p["rg_bx"], p["rg_lambda"]], axis=1),
        rg_wa=jax.vmap(_block_diag)(p["rg_wa"]).astype(BF16),
        rg_wx=jax.vmap(_block_diag)(p["rg_wx"]).astype(BF16),
        s5_wt=s5_wt, s5_ws=s5_ws, s5_wo=s5_wo, s5_lam=s5_lam, s5_d=s5_d,
        w_gate=w_gate.astype(BF16),
        w_branch=p["w_branch"].astype(BF16),
        w_glu=p["s5_w_glu"].astype(BF16),
        b_glu=p["s5_b_glu"][:, None, :],
        w_out=p["w_out"].astype(BF16),
        ffn_w1=p["ffn_w1"].astype(BF16), ffn_w3=p["ffn_w3"].astype(BF16), ffn_w2=p["ffn_w2"].astype(BF16),
        norm_g=p["norm_g"][:, :, None, :],
    )


def _pick_tile(n, pref):
    t = pref
    while n % t:
        t //= 2
    return t


def kernel(x, c, positions, ada_w, ada_b, norm_g, ffn_w1, ffn_w3, ffn_w2, w_in,
           conv_w, conv_b, rg_wa, rg_ba, rg_wx, rg_bx, rg_lambda,
           mla_q_norm, mla_w_uq, mla_kv_norm, mla_w_ukv, mla_qk_gain, dsa_qk_gain,
           s5_lambda_re, s5_lambda_im, s5_log_dt, s5_b_re, s5_b_im, s5_c_re, s5_c_im,
           s5_d, s5_w_glu, s5_b_glu, w_branch, w_out):
    p = dict(norm_g=norm_g, ffn_w1=ffn_w1, ffn_w3=ffn_w3, ffn_w2=ffn_w2, w_in=w_in,
             conv_w=conv_w, conv_b=conv_b, rg_wa=rg_wa, rg_ba=rg_ba, rg_wx=rg_wx, rg_bx=rg_bx,
             rg_lambda=rg_lambda, mla_q_norm=mla_q_norm, mla_w_uq=mla_w_uq, mla_kv_norm=mla_kv_norm,
             mla_w_ukv=mla_w_ukv, mla_qk_gain=mla_qk_gain, dsa_qk_gain=dsa_qk_gain,
             s5_lambda_re=s5_lambda_re, s5_lambda_im=s5_lambda_im, s5_log_dt=s5_log_dt,
             s5_b_re=s5_b_re, s5_b_im=s5_b_im, s5_c_re=s5_c_re, s5_c_im=s5_c_im, s5_d=s5_d,
             s5_w_glu=s5_w_glu, s5_b_glu=s5_b_glu, w_branch=w_branch, w_out=w_out)
    bsz, seq, d = x.shape
    n = bsz * seq
    depth = ada_w.shape[0]
    assert seq % (S5_CHUNK * 8) == 0 and seq % _DSA_QB == 0

    tm = _pick_tile(seq, 512)
    tc = _pick_tile(seq, 256)
    bq = _pick_tile(seq, 512)

    mod = _ada_call(c, ada_w, ada_b)
    pw = _prep_params(p)
    tabs = (_rope_table(positions, MLA_ROPE, LANE) + _rope_table(positions, DSA_ROT, DSA_HEAD_DIM)
            + _rope_table(positions, IDX_ROT, IDX_DIM))
    tri = (jnp.arange(LANE)[:, None] < jnp.arange(LANE)[None, :]).astype(BF16)

    xf = x.reshape(n, d)
    for l in range(depth):
        xf = _ffn_call(xf, mod, pw, l, 0, seq, tm)
        (xr, gr, qm, km, vm, qd, kd, vd, qi, ki, wi, us) = _inproj_call(xf, mod, pw, l, tabs, seq, tm)
        ya = _rglru_call(xr, gr, pw, l, bsz, seq, tc)
        yb = _flash_call(qm, km, vm, bsz, seq, bq)
        yc = _dsa_call(qd, kd, vd, qi, ki, wi, tri, bsz, seq)
        yd = _s5_call(us, pw, l, bsz, seq)
        xf = _merge_call(xf, mod, ya, yb, yc, yd, pw, l, seq, tm)
        xf = _ffn_call(xf, mod, pw, l, 1, seq, tm)
    return xf.reshape(bsz, seq, d)
```

```python
import functools
import math

import jax
import jax.numpy as jnp
from jax import lax
from jax.experimental import pallas as pl
from jax.experimental.pallas import tpu as pltpu

F32 = jnp.float32
BF16 = jnp.bfloat16

D_MODEL = 1024
EPS = 1e-6
ROPE_THETA = 500000.0
D_FF = 2816
N_ADA = 9

D_RNN = 512
RNN_HEADS = 8
RNN_HEAD_DIM = D_RNN // RNN_HEADS
CONV_WIDTH = 4
LRU_C = 8.0

MLA_HEADS = 8
MLA_NOPE = 64
MLA_ROPE = 32
MLA_V = 64
MLA_QK = MLA_ROPE + MLA_NOPE
MLA_Q_LORA = 256
MLA_KV_LORA = 128

DSA_HEADS = 8
DSA_HEAD_DIM = 64
DSA_ROT = DSA_HEAD_DIM // 4
IDX_HEADS = 8
IDX_DIM = 32
IDX_ROT = IDX_DIM // 4
TOPK_MAX = 256

S5_GROUP = 16
S5_GROUPS = 32
D_S5 = S5_GROUP * S5_GROUPS
S5_STATE = 64
S5_CHUNK = 8
S5_GB = 8
S5_NGB = S5_GROUPS // S5_GB

N_BRANCH = 4
BRANCH_W = 512
IN_SPLITS = (D_RNN, D_RNN, MLA_Q_LORA, MLA_KV_LORA, MLA_ROPE,
             DSA_HEADS * DSA_HEAD_DIM, DSA_HEAD_DIM, DSA_HEAD_DIM,
             IDX_HEADS * IDX_DIM, IDX_DIM, IDX_HEADS, D_S5, N_BRANCH * D_MODEL)

LANE = 128
NEG_BIG = -1e30
VMEM_LIMIT = 56 * 1024 * 1024

_C_XR = 0
_C_GR = _C_XR + D_RNN
_C_QL = _C_GR + D_RNN
_C_KVL = _C_QL + MLA_Q_LORA
_C_KPE = _C_KVL + MLA_KV_LORA
_C_QD = _C_KPE + LANE
_C_KD = _C_QD + DSA_HEADS * DSA_HEAD_DIM
_C_VD = _C_KD + LANE
_C_QI = _C_VD + LANE
_C_KI = _C_QI + IDX_HEADS * IDX_DIM
_C_WI = _C_KI + IDX_HEADS * IDX_DIM
_C_US = _C_WI + LANE
_C_END = _C_US + D_S5


def _dot(a, b):
    return jnp.dot(a, b, preferred_element_type=F32)


def _dot_nt(a, b):
    return lax.dot_general(a, b, (((1,), (1,)), ((), ())), preferred_element_type=F32)


def _sigmoid(x):
    return jax.nn.sigmoid(x)


def _gelu_tanh(x):
    return 0.5 * x * (1.0 + jnp.tanh(0.7978845608028654 * (x + 0.044715 * (x * x * x))))


def _rms_mod(x, g, shift, scale):
    ms = jnp.mean(x * x, axis=-1, keepdims=True)
    y = x * lax.rsqrt(ms + EPS) * g
    return y * (1.0 + scale) + shift


def _resident(shape):
    return pl.BlockSpec(shape, lambda *_: (0,) * len(shape), pipeline_mode=pl.Buffered(1))


def _layer(tail, *lead):
    return pl.BlockSpec((None,) * len(lead) + tuple(tail), lambda *_: tuple(lead) + (0,) * len(tail),
                        pipeline_mode=pl.Buffered(1))


def _mod_spec(l, tm, seq, d):
    return pl.BlockSpec((None, 1, N_ADA, d), lambda i: (l, i * tm // seq, 0, 0))


def _params(sem):
    return pltpu.CompilerParams(dimension_semantics=sem, vmem_limit_bytes=VMEM_LIMIT)


def _ada_kernel(c_ref, w_ref, b_ref, o_ref):
    c = c_ref[...]
    a = c * _sigmoid(c)
    w = w_ref[0]
    a_hi = a.astype(BF16)
    a_lo = (a - a_hi.astype(F32)).astype(BF16)
    w_hi = w.astype(BF16)
    w_lo = (w - w_hi.astype(F32)).astype(BF16)
    o_ref[0] = _dot(a_hi, w_hi) + _dot(a_lo, w_hi) + _dot(a_hi, w_lo) + b_ref[0]


def _ada_call(c, ada_w, ada_b):
    nl, d, n9 = ada_w.shape
    b = c.shape[0]
    tn = 1024
    out = pl.pallas_call(
        _ada_kernel,
        grid=(nl, n9 // tn),
        in_specs=[
            pl.BlockSpec((b, d), lambda l, j: (0, 0)),
            pl.BlockSpec((1, d, tn), lambda l, j: (l, 0, j)),
            pl.BlockSpec((1, 1, tn), lambda l, j: (l, 0, j)),
        ],
        out_specs=pl.BlockSpec((1, b, tn), lambda l, j: (l, 0, j)),
        out_shape=jax.ShapeDtypeStruct((nl, b, n9), F32),
        compiler_params=_params(("arbitrary", "arbitrary")),
        name="ada_mod",
    )(c, ada_w, ada_b.reshape(nl, 1, n9))
    return out.reshape(nl, b, N_ADA, d)


_FFN_CHUNK = 256


def _ffn_kernel(row0, x_ref, mod_ref, g_ref, w1_ref, w3_ref, w2_ref, o_ref, h_ref):
    x = x_ref[...]
    m = mod_ref[0]
    u = _rms_mod(x, g_ref[...], m[row0:row0 + 1], m[row0 + 1:row0 + 2]).astype(BF16)
    for c in range(0, w1_ref.shape[1], _FFN_CHUNK):
        h1 = _dot(u, w1_ref[:, c:c + _FFN_CHUNK])
        h3 = _dot(u, w3_ref[:, c:c + _FFN_CHUNK])
        h_ref[:, c:c + _FFN_CHUNK] = (h1 * _sigmoid(h1) * h3).astype(BF16)
    o_ref[...] = x + 0.5 * (1.0 + m[row0 + 2:row0 + 3]) * _dot(h_ref[...], w2_ref[...])


def _ffn_call(x, mod, pw, l, j, seq, tm):
    n, d = x.shape
    f = pw["ffn_w1"].shape[-1]
    return pl.pallas_call(
        functools.partial(_ffn_kernel, 6 * j),
        grid=(n // tm,),
        in_specs=[
            pl.BlockSpec((tm, d), lambda i: (i, 0)),
            _mod_spec(l, tm, seq, d),
            _layer((1, d), l, 2 * j),
            _layer((d, f), l, j), _layer((d, f), l, j), _layer((f, d), l, j),
        ],
        out_specs=pl.BlockSpec((tm, d), lambda i: (i, 0)),
        out_shape=jax.ShapeDtypeStruct((n, d), F32),
        scratch_shapes=[pltpu.VMEM((tm, f), BF16)],
        compiler_params=_params(("arbitrary",)),
        name="ffn",
    )(x, mod, pw["norm_g"], pw["ffn_w1"], pw["ffn_w3"], pw["ffn_w2"])


def _rope_tiles(tiles, cos, sin, rot):
    rows = tiles[0].shape[0]
    partner = _dot(jnp.concatenate(tiles, axis=0).astype(BF16), rot)
    return [t * cos + partner[i * rows:(i + 1) * rows] * sin for i, t in enumerate(tiles)]


def _inproj_kernel(x_ref, mod_ref, g_ref, w_ref, wuq_ref, wukv_ref, vec_ref, rot_ref,
                   cm_ref, sm_ref, cd_ref, sd_ref, ci_ref, si_ref,
                   xr_ref, gr_ref, qm_ref, km_ref, vm_ref, qd_ref, kd_ref, vd_ref,
                   qi_ref, ki_ref, wi_ref, us_ref):
    x = x_ref[...]
    m = mod_ref[0]
    u = _rms_mod(x, g_ref[...], m[3:4], m[4:5]).astype(BF16)
    vec = vec_ref[...]
    lane = lax.broadcasted_iota(jnp.int32, (1, LANE), 1)
    z = _dot(u, w_ref[...])

    xr_ref[...] = z[:, _C_XR:_C_GR].astype(xr_ref.dtype)
    gr_ref[...] = z[:, _C_GR:_C_QL].astype(gr_ref.dtype)
    wi_ref[...] = z[:, _C_WI:_C_US]
    us_ref[...] = z[:, _C_US:_C_END].astype(us_ref.dtype)

    ql = z[:, _C_QL:_C_KVL]
    qn = (ql * lax.rsqrt(jnp.mean(ql * ql, axis=-1, keepdims=True) + EPS)
          * vec[0:1, :MLA_Q_LORA]).astype(BF16)
    q2 = _dot(qn, wuq_ref[...])
    kvl = z[:, _C_KVL:_C_KPE]
    kvn = (kvl * lax.rsqrt(jnp.mean(kvl * kvl, axis=-1, keepdims=True) + EPS)
           * vec[1:2, :MLA_KV_LORA]).astype(BF16)
    kv = _dot(kvn, wukv_ref[...])
    kpe = z[:, _C_KPE:_C_QD]
    gq = vec[2:3, :LANE]
    gk = vec[3:4, :LANE]
    roped = _rope_tiles([kpe * gk] + [q2[:, h * LANE:(h + 1) * LANE] * gq for h in range(MLA_HEADS)],
                        cm_ref[...], sm_ref[...], rot_ref[0])
    kpe_rot = roped[0]
    kpe_ss = jnp.sum(kpe * kpe, axis=-1, keepdims=True)
    q_tiles, k_tiles = [], []
    for h in range(MLA_HEADS):
        hs = slice(h * LANE, (h + 1) * LANE)
        qs = q2[:, hs]
        s = lax.rsqrt(jnp.sum(qs * qs, axis=-1, keepdims=True) * (1.0 / MLA_QK) + EPS) * MLA_QK ** -0.5
        q_tiles.append((s * roped[1 + h]).astype(qm_ref.dtype))
        kn = kv[:, hs]
        s = lax.rsqrt((jnp.sum(kn * kn, axis=-1, keepdims=True) + kpe_ss) * (1.0 / MLA_QK) + EPS)
        k_tiles.append((s * (kn * gk + kpe_rot)).astype(km_ref.dtype))
    qm_ref[...] = jnp.concatenate(q_tiles, axis=1)
    km_ref[...] = jnp.concatenate(k_tiles, axis=1)
    nslot = MLA_HEADS * LANE
    vlane = lax.broadcasted_iota(jnp.int32, (1, nslot), 1)
    v_is_low = (vlane // LANE) % 2 == 0
    ones_half = jnp.where(((vlane % LANE) < MLA_V) == v_is_low, 0.0, 1.0)
    vm_ref[...] = (kv[:, nslot:] + ones_half).astype(vm_ref.dtype)

    gd = vec[4:5, :LANE]
    low = lane < DSA_HEAD_DIM
    kd = z[:, _C_KD:_C_VD]
    xqs = [z[:, _C_QD + t * LANE:_C_QD + (t + 1) * LANE] for t in range(DSA_HEADS // 2)]
    roped = _rope_tiles([xq * gd for xq in xqs] + [kd * vec[5:6, :LANE]], cd_ref[...], sd_ref[...], rot_ref[1])
    d_tiles = []
    for t in range(DSA_HEADS // 2):
        x2 = xqs[t] * xqs[t]
        ss_even = jnp.sum(jnp.where(low, x2, 0.0), axis=-1, keepdims=True)
        ss_odd = jnp.sum(jnp.where(low, 0.0, x2), axis=-1, keepdims=True)
        y = roped[t]
        s_even = lax.rsqrt(ss_even * (1.0 / DSA_HEAD_DIM) + EPS) * DSA_HEAD_DIM ** -0.5
        s_odd = lax.rsqrt(ss_odd * (1.0 / DSA_HEAD_DIM) + EPS) * DSA_HEAD_DIM ** -0.5
        d_tiles.append(jnp.where(low, y * s_even, 0.0).astype(qd_ref.dtype))
        d_tiles.append(jnp.where(low, pltpu.roll(y, DSA_HEAD_DIM, 1) * s_odd, 0.0).astype(qd_ref.dtype))
    qd_ref[...] = jnp.concatenate(d_tiles, axis=1)
    s = lax.rsqrt(jnp.sum(kd * kd, axis=-1, keepdims=True) * (1.0 / DSA_HEAD_DIM) + EPS)
    kd_ref[...] = (s * roped[-1]).astype(kd_ref.dtype)
    vd_ref[...] = (z[:, _C_VD:_C_QI] + jnp.where(low, 0.0, 1.0)).astype(vd_ref.dtype)

    ntile = IDX_HEADS * IDX_DIM // LANE
    roped = _rope_tiles([z[:, c + t * LANE:c + (t + 1) * LANE] for c in (_C_QI, _C_KI) for t in range(ntile)],
                        ci_ref[...], si_ref[...], rot_ref[2])
    qi_ref[...] = jnp.concatenate(roped[:ntile], axis=1).astype(qi_ref.dtype)
    ki_ref[...] = jnp.concatenate(roped[ntile:], axis=1).astype(ki_ref.dtype)


def _inproj_call(x, mod, pw, l, tabs, seq, tm):
    n, d = x.shape
    row = lambda w: pl.BlockSpec((tm, w), lambda i: (i, 0))
    out_widths = [(D_RNN, BF16), (D_RNN, BF16), (MLA_HEADS * LANE, BF16), (MLA_HEADS * LANE, BF16),
                  (MLA_HEADS * LANE, BF16), (DSA_HEADS * LANE, BF16), (LANE, BF16), (LANE, BF16),
                  (IDX_HEADS * IDX_DIM, BF16), (IDX_HEADS * IDX_DIM, BF16), (LANE, F32), (D_S5, BF16)]
    return pl.pallas_call(
        _inproj_kernel,
        grid=(n // tm,),
        in_specs=[
            row(d), _mod_spec(l, tm, seq, d), _layer((1, d), l, 1),
            _layer(pw["w_all"].shape[1:], l), _layer(pw["wuq"].shape[1:], l),
            _layer(pw["wukv"].shape[1:], l), _layer(pw["vec"].shape[1:], l),
            _resident(pw["rot"].shape),
        ] + [row(LANE)] * 6,
        out_specs=[row(w) for w, _ in out_widths],
        out_shape=[jax.ShapeDtypeStruct((n, w), dt) for w, dt in out_widths],
        compiler_params=_params(("arbitrary",)),
        name="in_proj",
    )(x, mod, pw["norm_g"], pw["w_all"], pw["wuq"], pw["wukv"], pw["vec"], pw["rot"], *tabs)


def _rglru_kernel(tc, x_ref, gate_ref, cw_ref, vec_ref, wa_ref, wx_ref, o_ref, xs_ref, h_ref):
    j = pl.program_id(1)

    @pl.when(j == 0)
    def _():
        xs_ref[0:8, :] = jnp.zeros((8, D_RNN), F32)
        h_ref[...] = jnp.zeros(h_ref.shape, F32)

    xs_ref[8:8 + tc, :] = x_ref[...].astype(F32)
    cw = cw_ref[...]
    vec = vec_ref[...]
    xc = vec[0:1]
    for k in range(CONV_WIDTH):
        xc = xc + cw[k:k + 1] * xs_ref[pl.ds(8 - (CONV_WIDTH - 1) + k, tc), :]
    xs_ref[0:8, :] = xs_ref[tc:tc + 8, :]

    xb = xc.astype(BF16)
    r = _sigmoid(_dot(xb, wa_ref[...]) + vec[1:2])
    ig = _sigmoid(_dot(xb, wx_ref[...]) + vec[2:3])
    nl = -vec[3:4]
    softplus = jnp.maximum(nl, 0.0) + jnp.log(1.0 + jnp.exp(-jnp.abs(nl)))
    log_a = (-LRU_C) * r * softplus
    a = jnp.exp(log_a)
    z = 2.0 * log_a
    series = -z * (1.0 + z * (0.5 + z * (1.0 / 6.0 + z * (1.0 / 24.0 + z * (1.0 / 120.0 + z * (1.0 / 720.0))))))
    nem1 = jnp.where(z > -0.25, series, 1.0 - jnp.exp(z))
    b = jnp.sqrt(nem1) * ig * xc

    rowi = lax.broadcasted_iota(jnp.int32, (tc, D_RNN), 0)
    d = 1
    while d < tc:
        keep = rowi >= d
        a_s = jnp.where(keep, pltpu.roll(a, d, 0), 1.0)
        b_s = jnp.where(keep, pltpu.roll(b, d, 0), 0.0)
        b = a * b_s + b
        a = a * a_s
        d *= 2
    h = b + a * h_ref[...]
    h_ref[...] = h[tc - 1:tc, :]
    o_ref[...] = (h * _gelu_tanh(gate_ref[...].astype(F32))).astype(o_ref.dtype)


def _rglru_call(xr, gr, pw, l, bsz, seq, tc):
    n = xr.shape[0]
    nt = seq // tc
    row = pl.BlockSpec((tc, D_RNN), lambda b, j: (b * nt + j, 0))
    return pl.pallas_call(
        functools.partial(_rglru_kernel, tc),
        grid=(bsz, nt),
        in_specs=[row, row,
                  _layer((CONV_WIDTH, D_RNN), l), _layer((4, D_RNN), l),
                  _layer((D_RNN, D_RNN), l), _layer((D_RNN, D_RNN), l)],
        out_specs=row,
        out_shape=jax.ShapeDtypeStruct((n, D_RNN), BF16),
        scratch_shapes=[pltpu.VMEM((tc + 8, D_RNN), F32), pltpu.VMEM((1, D_RNN), F32)],
        compiler_params=_params(("arbitrary", "arbitrary")),
        name="rglru",
    )(xr, gr, pw["conv_w"], pw["rg_vec"], pw["rg_wa"], pw["rg_wx"])


def _flash_kernel(q_ref, k_ref, v_ref, o_ref, m_ref, acc_ref):
    i = pl.program_id(1)
    j = pl.program_id(2)
    bq, bk = q_ref.shape[0], k_ref.shape[0]

    @pl.when(j == 0)
    def _():
        m_ref[...] = jnp.full(m_ref.shape, NEG_BIG, F32)
        acc_ref[...] = jnp.zeros(acc_ref.shape, F32)

    def step(masked):
        if masked:
            causal = (lax.broadcasted_iota(jnp.int32, (bq, bk), 1)
                      <= lax.broadcasted_iota(jnp.int32, (bq, bk), 0))
        m_out, acc_out = [], []
        for h in range(MLA_HEADS):
            hs = slice(h * LANE, (h + 1) * LANE)
            s = _dot_nt(q_ref[:, hs], k_ref[:, hs])
            if masked:
                s = jnp.where(causal, s, NEG_BIG)
            m_prev = m_ref[h]
            m_new = jnp.maximum(m_prev, jnp.max(s, axis=1, keepdims=True))
            p = jnp.exp(s - jnp.concatenate([m_new] * (bk // LANE), axis=1))
            acc_out.append(jnp.exp(m_prev - m_new) * acc_ref[h] + _dot(p.astype(BF16), v_ref[:, hs]))
            m_out.append(m_new)
        m_ref[...] = jnp.stack(m_out)
        acc_ref[...] = jnp.stack(acc_out)

    @pl.when(j < i)
    def _():
        step(False)

    @pl.when(j == i)
    def _():
        step(True)
        lane = lax.broadcasted_iota(jnp.int32, (bq, LANE), 1)
        outs = []
        for pr in range(MLA_HEADS // 2):
            even = acc_ref[2 * pr]
            odd = acc_ref[2 * pr + 1]
            outs.append(jnp.where(lane < MLA_V, even / pltpu.roll(even, MLA_V, 1),
                                  odd / pltpu.roll(odd, MLA_V, 1)))
        o_ref[...] = jnp.concatenate(outs, axis=1).astype(o_ref.dtype)


def _flash_call(q, k, v, bsz, seq, bq):
    n = q.shape[0]
    nq = seq // bq
    kv_row = lambda b, i, j: (b * nq + jnp.minimum(j, i), 0)
    return pl.pallas_call(
        _flash_kernel,
        grid=(bsz, nq, nq),
        in_specs=[
            pl.BlockSpec((bq, MLA_HEADS * LANE), lambda b, i, j: (b * nq + i, 0)),
            pl.BlockSpec((bq, MLA_HEADS * LANE), kv_row),
            pl.BlockSpec((bq, MLA_HEADS * LANE), kv_row),
        ],
        out_specs=pl.BlockSpec((bq, MLA_HEADS * MLA_V), lambda b, i, j: (b * nq + i, 0)),
        out_shape=jax.ShapeDtypeStruct((n, MLA_HEADS * MLA_V), BF16),
        scratch_shapes=[pltpu.VMEM((MLA_HEADS, bq, LANE), F32), pltpu.VMEM((MLA_HEADS, bq, LANE), F32)],
        compiler_params=_params(("arbitrary",) * 3),
        name="mla_flash",
    )(q, k, v)


_DSA_QB = 256
_DSA_HG = 2
_DSA_CHAINS = 8
_INT_MIN = -2 ** 31


def _dsa_body(ns, topk, i, q_ref, k_ref, v_ref, qi_ref, ki_ref, w_ref, tri_ref, o_ref):
    qb = _DSA_QB
    hg = _DSA_HG
    qi = qi_ref[...]
    ki = ki_ref[0:ns, :]
    w = w_ref[...]
    head_of_lane = jnp.right_shift(lax.broadcasted_iota(jnp.int32, qi.shape, 1), int(math.log2(IDX_DIM)))
    zero = jnp.zeros_like(qi)
    score = jnp.zeros((qb, ns), F32)
    for g in range(0, IDX_HEADS, hg):
        lhs = jnp.concatenate([jnp.where(head_of_lane == h, qi, zero) for h in range(g, g + hg)], axis=0)
        rel = jnp.maximum(_dot_nt(lhs, ki), 0.0)
        for t in range(hg):
            score = score + w[:, g + t:g + t + 1] * rel[t * qb:(t + 1) * qb]

    qpos = lax.broadcasted_iota(jnp.int32, (qb, ns), 0) + i * qb
    kpos = lax.broadcasted_iota(jnp.int32, (qb, ns), 1)
    bits = pltpu.bitcast(score + 0.0, jnp.int32)
    key = jnp.where(bits < 0, bits ^ jnp.int32(0x7FFFFFFF), bits)
    key = jnp.where(kpos <= qpos, key, jnp.int32(_INT_MIN))
    kk = jnp.minimum(lax.broadcasted_iota(jnp.int32, (qb, 1), 0) + (i * qb + 1), topk).astype(F32)

    rows = qb // _DSA_CHAINS
    keys = [key[r * rows:(r + 1) * rows] for r in range(_DSA_CHAINS)]
    kks = [kk[r * rows:(r + 1) * rows] for r in range(_DSA_CHAINS)]

    def body(it, thrs):
        bit = lax.shift_left(jnp.int32(1), jnp.int32(31) - it)
        out = []
        for kr, kkr, t in zip(keys, kks, thrs):
            c = t + bit
            cnt = jnp.sum(jnp.where(kr >= c, 1.0, 0.0), axis=1, keepdims=True)
            out.append(jnp.where(cnt >= kkr, c, t))
        return tuple(out)

    thrs = lax.fori_loop(0, 32, body, tuple(jnp.full((rows, 1), _INT_MIN, jnp.int32) for _ in keys), unroll=2)
    thr = jnp.concatenate(thrs, axis=0)

    nchunk = ns // LANE
    need = kk - jnp.sum(jnp.where(key > thr, 1.0, 0.0), axis=1, keepdims=True)
    eqs = [jnp.where(key[:, c * LANE:(c + 1) * LANE] == thr, 1.0, 0.0) for c in range(nchunk)]
    before_all = _dot(jnp.concatenate(eqs, axis=0).astype(BF16), tri_ref[...])
    run = jnp.zeros((qb, 1), F32)
    bias_chunks = []
    for c in range(nchunk):
        before = before_all[c * qb:(c + 1) * qb] + run
        take = jnp.where(key[:, c * LANE:(c + 1) * LANE] > thr, 1.0, jnp.where(before < need, eqs[c], 0.0))
        bias_chunks.append(jnp.where(take > 0.5, 0.0, NEG_BIG))
        run = run + jnp.sum(eqs[c], axis=1, keepdims=True)
    bias = jnp.concatenate(bias_chunks, axis=1)

    k = k_ref[0:ns, :]
    v = v_ref[0:ns, :]
    lane = lax.broadcasted_iota(jnp.int32, (qb, LANE), 1)
    for g in range(0, DSA_HEADS, hg):
        q = jnp.concatenate([q_ref[:, h * LANE:(h + 1) * LANE] for h in range(g, g + hg)], axis=0)
        s = _dot_nt(q, k).reshape(hg, qb, ns) + bias[None]
        p = jnp.exp(s - jnp.max(s, axis=2, keepdims=True))
        o = _dot(p.reshape(hg * qb, ns).astype(BF16), v)
        for t in range(0, hg, 2):
            even = o[t * qb:(t + 1) * qb]
            odd = o[(t + 1) * qb:(t + 2) * qb]
            pr = (g + t) // 2
            o_ref[:, pr * LANE:(pr + 1) * LANE] = jnp.where(
                lane < DSA_HEAD_DIM, even / pltpu.roll(even, DSA_HEAD_DIM, 1),
                pltpu.roll(odd, DSA_HEAD_DIM, 1) / odd).astype(o_ref.dtype)


def _dsa_kernel(seq, bucket, topk, q_ref, k_ref, v_ref, qi_ref, ki_ref, w_ref, tri_ref, o_ref):
    i = pl.program_id(1)
    for bk in range(seq // bucket):
        @pl.when((i * _DSA_QB) // bucket == bk)
        def _(bk=bk):
            _dsa_body((bk + 1) * bucket, topk, i, q_ref, k_ref, v_ref, qi_ref, ki_ref, w_ref,
                      tri_ref, o_ref)


def _dsa_call(qd, kd, vd, qi, ki, wi, tri, bsz, seq):
    n = qd.shape[0]
    qb = _DSA_QB
    nq = seq // qb
    bucket = min(_DSA_QB, seq)
    topk = min(TOPK_MAX, seq // 4)
    qrow = lambda w: pl.BlockSpec((qb, w), lambda b, i: (b * nq + i, 0))
    full = lambda w: pl.BlockSpec((seq, w), lambda b, i: (b, 0))
    return pl.pallas_call(
        functools.partial(_dsa_kernel, seq, bucket, topk),
        grid=(bsz, nq),
        in_specs=[qrow(DSA_HEADS * LANE), full(LANE), full(LANE),
                  qrow(IDX_HEADS * IDX_DIM), full(IDX_HEADS * IDX_DIM), qrow(LANE),
                  _resident((LANE, LANE))],
        out_specs=qrow(DSA_HEADS * DSA_HEAD_DIM),
        out_shape=jax.ShapeDtypeStruct((n, DSA_HEADS * DSA_HEAD_DIM), BF16),
        compiler_params=_params(("arbitrary", "arbitrary")),
        name="dsa",
    )(qd, kd, vd, qi, ki, wi, tri)


def _s5_kernel(rows, u_ref, wt_ref, ws_ref, wo_ref, lam_ref, d_ref, o_ref, uf_ref, yf_ref):
    half = S5_GB * S5_STATE
    rowi = lax.broadcasted_iota(jnp.int32, (rows, half), 0)
    for gb in range(S5_NGB):
        gs = slice(gb * LANE, (gb + 1) * LANE)
        uf_ref[gb] = u_ref[:, gs].astype(F32)
        uf = jnp.concatenate([uf_ref[gb, pl.ds(a, rows, stride=S5_CHUNK), :] for a in range(S5_CHUNK)], axis=1)
        u = uf.astype(BF16)
        y = _dot(u, wt_ref[gb])
        st = _dot(u, ws_ref[gb])
        xr = st[:, :half]
        xi = st[:, half:]
        lam = lam_ref[gb]
        lr = lam[0:1]
        li = lam[1:2]
        d = 1
        while d < rows:
            keep = rowi >= d
            sr = jnp.where(keep, pltpu.roll(xr, d, 0), 0.0)
            si = jnp.where(keep, pltpu.roll(xi, d, 0), 0.0)
            xr, xi = xr + lr * sr - li * si, xi + lr * si + li * sr
            lr, li = lr * lr - li * li, 2.0 * lr * li
            d *= 2
        keep = rowi >= 1
        pr = jnp.where(keep, pltpu.roll(xr, 1, 0), 0.0)
        pi = jnp.where(keep, pltpu.roll(xi, 1, 0), 0.0)
        xp = jnp.concatenate([pr, pi], axis=1).astype(BF16)
        yg = _gelu_tanh(y + _dot(xp, wo_ref[gb]) + d_ref[gb] * uf)
        for a in range(S5_CHUNK):
            yf_ref[gb, pl.ds(a, rows, stride=S5_CHUNK), :] = yg[:, a * LANE:(a + 1) * LANE]
        o_ref[:, gs] = yf_ref[gb].astype(o_ref.dtype)


def _s5_call(us, pw, l, bsz, seq):
    n = us.shape[0]
    blk = pl.BlockSpec((seq, D_S5), lambda b: (b, 0))
    wshape = pw["s5_wt"].shape[1:]
    return pl.pallas_call(
        functools.partial(_s5_kernel, seq // S5_CHUNK),
        grid=(bsz,),
        in_specs=[blk, _layer(wshape, l), _layer(wshape, l), _layer(wshape, l),
                  _layer(pw["s5_lam"].shape[1:], l), _layer(pw["s5_d"].shape[1:], l)],
        out_specs=blk,
        out_shape=jax.ShapeDtypeStruct((n, D_S5), BF16),
        scratch_shapes=[pltpu.VMEM((S5_NGB, seq, LANE), F32), pltpu.VMEM((S5_NGB, seq, LANE), F32)],
        compiler_params=_params(("arbitrary",)),
        name="s5",
    )(us, pw["s5_wt"], pw["s5_ws"], pw["s5_wo"], pw["s5_lam"], pw["s5_d"])


def _merge_kernel(x_ref, mod_ref, g_ref, ya_ref, yb_ref, yc_ref, yd_ref,
                  wg_ref, wb_ref, wglu_ref, bglu_ref, wout_ref, o_ref):
    x = x_ref[...]
    m = mod_ref[0]
    d = x.shape[1]
    u = _rms_mod(x, g_ref[...], m[3:4], m[4:5]).astype(BF16)
    yd = yd_ref[...]
    ydg = (yd.astype(F32) * _sigmoid(_dot(yd, wglu_ref[...]) + bglu_ref[...])).astype(BF16)
    ys = (ya_ref[...], yb_ref[...], yc_ref[...], ydg)
    merged = jnp.zeros(x.shape, F32)
    for nb in range(N_BRANCH):
        gate = _sigmoid(_dot(u, wg_ref[:, nb * d:(nb + 1) * d]))
        merged = merged + gate * _dot(ys[nb], wb_ref[nb])
    o_ref[...] = x + (1.0 + m[5:6]) * _dot(merged.astype(BF16), wout_ref[...])


def _merge_call(x, mod, ya, yb, yc, yd, pw, l, seq, tm):
    n, d = x.shape
    row = lambda w: pl.BlockSpec((tm, w), lambda i: (i, 0))
    return pl.pallas_call(
        _merge_kernel,
        grid=(n // tm,),
        in_specs=[row(d), _mod_spec(l, tm, seq, d), _layer((1, d), l, 1),
                  row(BRANCH_W), row(BRANCH_W), row(BRANCH_W), row(BRANCH_W),
                  _layer((d, N_BRANCH * d), l), _layer((N_BRANCH, BRANCH_W, d), l),
                  _layer((D_S5, D_S5), l), _layer((1, D_S5), l), _layer((d, d), l)],
        out_specs=row(d),
        out_shape=jax.ShapeDtypeStruct((n, d), F32),
        compiler_params=_params(("arbitrary",)),
        name="merge",
    )(x, mod, pw["norm_g"], ya, yb, yc, yd, pw["w_gate"], pw["w_branch"], pw["w_glu"], pw["b_glu"], pw["w_out"])


def _to_slots(w, nheads, hdim):
    lead = w.shape[:-1]
    wh = w.reshape(lead + (nheads, hdim))
    wh = jnp.pad(wh, [(0, 0)] * (len(lead) + 1) + [(0, LANE - hdim)])
    return wh.reshape(lead + (nheads * LANE,))


def _pad_lane(v, width=LANE):
    return jnp.pad(v, [(0, 0)] * (v.ndim - 1) + [(0, width - v.shape[-1])])


def _rope_table(positions, rot, period):
    j = jnp.arange(LANE) % period
    inv = jnp.where(j < rot, ROPE_THETA ** (-(2 * (j % (rot // 2))).astype(F32) / rot), 0.0)
    ang = positions.astype(F32).reshape(-1, 1) * inv[None, :]
    return jnp.cos(ang), jnp.sin(ang)


def _rot_matrix(rot, period):
    src = jnp.arange(LANE)[:, None]
    dst = jnp.arange(LANE)[None, :]
    j = dst % period
    half = rot // 2
    return (jnp.where((j < half) & (src == dst + half), -1.0, 0.0)
            + jnp.where((j >= half) & (j < rot) & (src == dst - half), 1.0, 0.0)).astype(BF16)


def _s5_weights(lam_re, lam_im, log_dt, b_re, b_im, c_re, c_im, dvec):
    hp = lax.Precision.HIGHEST
    g, p = lam_re.shape
    ch = S5_CHUNK
    dt = jnp.exp(log_dt)[:, None]
    mag = jnp.exp(lam_re * dt)
    ar, ai = mag * jnp.cos(lam_im * dt), mag * jnp.sin(lam_im * dt)
    den = lam_re * lam_re + lam_im * lam_im
    nr, ni = ar - 1.0, ai
    f_re = (nr * lam_re + ni * lam_im) / den
    f_im = (ni * lam_re - nr * lam_im) / den
    bb_re = f_re[..., None] * b_re - f_im[..., None] * b_im
    bb_im = f_re[..., None] * b_im + f_im[..., None] * b_re
    pr, pi = [jnp.ones_like(ar)], [jnp.zeros_like(ar)]
    for _ in range(ch):
        pr, pi = pr + [pr[-1] * ar - pi[-1] * ai], pi + [pr[-1] * ai + pi[-1] * ar]
    pr, pi = jnp.stack(pr), jnp.stack(pi)
    mr = pr[..., None] * bb_re - pi[..., None] * bb_im
    mi = pr[..., None] * bb_im + pi[..., None] * bb_re
    kern = (jnp.einsum('gjp,tgpi->tgji', c_re, mr, precision=hp)
            - jnp.einsum('gjp,tgpi->tgji', c_im, mi, precision=hp))

    def group_diag(x):
        r, c = x.shape[-2:]
        x = x.reshape(x.shape[:-3] + (S5_NGB, S5_GB * r, c))
        on_diag = jnp.arange(S5_GB * r)[:, None] // r == jnp.arange(S5_GB * c)[None, :] // c
        return jnp.tile(x, (1,) * (x.ndim - 1) + (S5_GB,)) * on_diag.astype(x.dtype)

    a_in = jnp.arange(ch)[:, None]
    a_out = jnp.arange(ch)[None, :]
    lag = jnp.clip(a_out - a_in, 0, ch)
    bd_k = group_diag(kern.transpose(0, 1, 3, 2)).astype(BF16)
    toe = bd_k[lag] * (a_out >= a_in)[..., None, None, None].astype(BF16)
    w_toe = toe.transpose(2, 0, 3, 1, 4).reshape(S5_NGB, ch * LANE, ch * LANE)
    rev = jnp.arange(ch - 1, -1, -1)
    st = jnp.concatenate([group_diag(mr[:ch].transpose(0, 1, 3, 2)), group_diag(mi[:ch].transpose(0, 1, 3, 2))],
                         axis=-1).astype(BF16)
    w_st = st[rev].transpose(1, 0, 2, 3).reshape(S5_NGB, ch * LANE, 2 * S5_GB * p)
    pr1, pi1 = pr[1:], pi[1:]
    co_re = c_re[None] * pr1[:, :, None, :] - c_im[None] * pi1[:, :, None, :]
    co_im = -(c_re[None] * pi1[:, :, None, :] + c_im[None] * pr1[:, :, None, :])
    co = jnp.stack([group_diag(co_re.transpose(0, 1, 3, 2)), group_diag(co_im.transpose(0, 1, 3, 2))],
                   axis=2).astype(BF16)
    w_out = co.reshape(ch, S5_NGB, 2 * S5_GB * p, LANE).transpose(1, 2, 0, 3).reshape(
        S5_NGB, 2 * S5_GB * p, ch * LANE)
    lam_c = jnp.stack([pr[ch].reshape(S5_NGB, S5_GB * p), pi[ch].reshape(S5_NGB, S5_GB * p)], axis=1)
    d_t = jnp.tile(dvec.reshape(S5_NGB, 1, LANE), (1, 1, ch))
    return w_toe, w_st, w_out, lam_c, d_t


def _block_diag(w):
    h, a, b = w.shape
    eye = jnp.eye(h, dtype=w.dtype)
    return jnp.einsum('hij,hg->higj', w, eye).reshape(h * a, h * b)


def _prep_params(p):
    w_in = p["w_in"]
    nl = w_in.shape[0]
    offs = [0]
    for s in IN_SPLITS:
        offs.append(offs[-1] + s)
    seg = [w_in[:, :, offs[k]:offs[k + 1]] for k in range(len(IN_SPLITS))]
    (w_xr, w_gr, w_ql, w_kvl, w_kpe, w_qd, w_kd, w_vd, w_qi, w_ki, w_wi, w_us, w_gate) = seg
    w_all = jnp.concatenate([w_xr, w_gr, w_ql, w_kvl, _pad_lane(w_kpe), w_qd, _pad_lane(w_kd), _pad_lane(w_vd),
                             w_qi, jnp.tile(w_ki, (1, 1, IDX_HEADS)), _pad_lane(w_wi), w_us],
                            axis=2).astype(BF16)
    assert w_all.shape[2] == _C_END

    wuq = _to_slots(p["mla_w_uq"], MLA_HEADS, MLA_QK).astype(BF16)
    wkv = p["mla_w_ukv"].reshape(nl, MLA_KV_LORA, MLA_HEADS, MLA_NOPE + MLA_V)
    kn = jnp.pad(wkv[..., :MLA_NOPE], ((0, 0), (0, 0), (0, 0), (MLA_ROPE, LANE - MLA_QK)))
    wv = wkv[..., MLA_NOPE:].reshape(nl, MLA_KV_LORA, MLA_HEADS // 2, 2, MLA_V)
    zv = jnp.zeros_like(wv[:, :, :, 0])
    wv = jnp.stack([jnp.concatenate([wv[:, :, :, 0], zv], axis=-1),
                    jnp.concatenate([zv, wv[:, :, :, 1]], axis=-1)], axis=3)
    wukv = jnp.concatenate([kn.reshape(nl, MLA_KV_LORA, MLA_HEADS * LANE),
                            wv.reshape(nl, MLA_KV_LORA, MLA_HEADS * LANE)], axis=2).astype(BF16)

    dq = p["dsa_qk_gain"][:, 0]
    vec = jnp.stack([_pad_lane(v, MLA_Q_LORA) for v in (
        p["mla_q_norm"], p["mla_kv_norm"], p["mla_qk_gain"][:, 0], p["mla_qk_gain"][:, 1],
        jnp.concatenate([dq, dq], axis=-1), p["dsa_qk_gain"][:, 1])], axis=1)

    s5_wt, s5_ws, s5_wo, s5_lam, s5_d = jax.vmap(_s5_weights)(
        p["s5_lambda_re"], p["s5_lambda_im"], p["s5_log_dt"], p["s5_b_re"], p["s5_b_im"],
        p["s5_c_re"], p["s5_c_im"], p["s5_d"])

    return dict(
        w_all=w_all, wuq=wuq, wukv=wukv, vec=vec,
        rot=jnp.stack([_rot_matrix(MLA_ROPE, LANE), _rot_matrix(DSA_ROT, DSA_HEAD_DIM),
                       _rot_matrix(IDX_ROT, IDX_DIM)]),
        conv_w=p["conv_w"],
        rg_vec=jnp.stack([p["conv_b"], p["rg_ba"], p["rg_bx"], p["rg_lambda"]], axis=1),
        rg_wa=jax.vmap(_block_diag)(p["rg_wa"]).astype(BF16),
        rg_wx=jax.vmap(_block_diag)(p["rg_wx"]).astype(BF16),
        s5_wt=s5_wt, s5_ws=s5_ws, s5_wo=s5_wo, s5_lam=s5_lam, s5_d=s5_d,
        w_gate=w_gate.astype(BF16),
        w_branch=p["w_branch"].astype(BF16),
        w_glu=p["s5_w_glu"].astype(BF16),
        b_glu=p["s5_b_glu"][:, None, :],
        w_out=p["w_out"].astype(BF16),
        ffn_w1=p["ffn_w1"].astype(BF16), ffn_w3=p["ffn_w3"].astype(BF16), ffn_w2=p["ffn_w2"].astype(BF16),
        norm_g=p["norm_g"][:, :, None, :],
    )


def _pick_tile(n, pref):
    t = pref
    while n % t:
        t //= 2
    return t


def kernel(x, c, positions, ada_w, ada_b, norm_g, ffn_w1, ffn_w3, ffn_w2, w_in,
           conv_w, conv_b, rg_wa, rg_ba, rg_wx, rg_bx, rg_lambda,
           mla_q_norm, mla_w_uq, mla_kv_norm, mla_w_ukv, mla_qk_gain, dsa_qk_gain,
           s5_lambda_re, s5_lambda_im, s5_log_dt, s5_b_re, s5_b_im, s5_c_re, s5_c_im,
           s5_d, s5_w_glu, s5_b_glu, w_branch, w_out):
    p = dict(norm_g=norm_g, ffn_w1=ffn_w1, ffn_w3=ffn_w3, ffn_w2=ffn_w2, w_in=w_in,
             conv_w=conv_w, conv_b=conv_b, rg_wa=rg_wa, rg_ba=rg_ba, rg_wx=rg_wx, rg_bx=rg_bx,
             rg_lambda=rg_lambda, mla_q_norm=mla_q_norm, mla_w_uq=mla_w_uq, mla_kv_norm=mla_kv_norm,
             mla_w_ukv=mla_w_ukv, mla_qk_gain=mla_qk_gain, dsa_qk_gain=dsa_qk_gain,
             s5_lambda_re=s5_lambda_re, s5_lambda_im=s5_lambda_im, s5_log_dt=s5_log_dt,
             s5_b_re=s5_b_re, s5_b_im=s5_b_im, s5_c_re=s5_c_re, s5_c_im=s5_c_im, s5_d=s5_d,
             s5_w_glu=s5_w_glu, s5_b_glu=s5_b_glu, w_branch=w_branch, w_out=w_out)
    bsz, seq, d = x.shape
    n = bsz * seq
    depth = ada_w.shape[0]
    assert seq % (S5_CHUNK * 8) == 0 and seq % _DSA_QB == 0

    tm = _pick_tile(seq, 512)
    tc = _pick_tile(seq, 256)
    bq = _pick_tile(seq, 512)

    mod = _ada_call(c, ada_w, ada_b)
    pw = _prep_params(p)
    tabs = (_rope_table(positions, MLA_ROPE, LANE) + _rope_table(positions, DSA_ROT, DSA_HEAD_DIM)
            + _rope_table(positions, IDX_ROT, IDX_DIM))
    tri = (jnp.arange(LANE)[:, None] < jnp.arange(LANE)[None, :]).astype(BF16)

    xf = x.reshape(n, d)
    for l in range(depth):
        xf = _ffn_call(xf, mod, pw, l, 0, seq, tm)
        (xr, gr, qm, km, vm, qd, kd, vd, qi, ki, wi, us) = _inproj_call(xf, mod, pw, l, tabs, seq, tm)
        ya = _rglru_call(xr, gr, pw, l, bsz, seq, tc)
        yb = _flash_call(qm, km, vm, bsz, seq, bq)
        yc = _dsa_call(qd, kd, vd, qi, ki, wi, tri, bsz, seq)
        yd = _s5_call(us, pw, l, bsz, seq)
        xf = _merge_call(xf, mod, ya, yb, yc, yd, pw, l, seq, tm)
        xf = _ffn_call(xf, mod, pw, l, 1, seq, tm)
    return xf.reshape(bsz, seq, d)
```

```python
import functools
import math

import jax
import jax.numpy as jnp
from jax import lax
from jax.experimental import pallas as pl
from jax.experimental.pallas import tpu as pltpu

F32 = jnp.float32
BF16 = jnp.bfloat16

D_MODEL = 1024
EPS = 1e-6
ROPE_THETA = 500000.0
D_FF = 2816
N_ADA = 9

D_RNN = 512
RNN_HEADS = 8
RNN_HEAD_DIM = D_RNN // RNN_HEADS
CONV_WIDTH = 4
LRU_C = 8.0

MLA_HEADS = 8
MLA_NOPE = 64
MLA_ROPE = 32
MLA_V = 64
MLA_QK = MLA_ROPE + MLA_NOPE
MLA_Q_LORA = 256
MLA_KV_LORA = 128

DSA_HEADS = 8
DSA_HEAD_DIM = 64
DSA_ROT = DSA_HEAD_DIM // 4
IDX_HEADS = 8
IDX_DIM = 32
IDX_ROT = IDX_DIM // 4
TOPK_MAX = 256

S5_GROUP = 16
S5_GROUPS = 32
D_S5 = S5_GROUP * S5_GROUPS
S5_STATE = 64
S5_CHUNK = 8
S5_GB = 8
S5_NGB = S5_GROUPS // S5_GB

N_BRANCH = 4
BRANCH_W = 512
IN_SPLITS = (D_RNN, D_RNN, MLA_Q_LORA, MLA_KV_LORA, MLA_ROPE,
             DSA_HEADS * DSA_HEAD_DIM, DSA_HEAD_DIM, DSA_HEAD_DIM,
             IDX_HEADS * IDX_DIM, IDX_DIM, IDX_HEADS, D_S5, N_BRANCH * D_MODEL)

LANE = 128
NEG_BIG = -1e30
VMEM_LIMIT = 56 * 1024 * 1024

_C_XR = 0
_C_GR = _C_XR + D_RNN
_C_QL = _C_GR + D_RNN
_C_KVL = _C_QL + MLA_Q_LORA
_C_KPE = _C_KVL + MLA_KV_LORA
_C_QD = _C_KPE + LANE
_C_KD = _C_QD + DSA_HEADS * DSA_HEAD_DIM
_C_VD = _C_KD + LANE
_C_QI = _C_VD + LANE
_C_KI = _C_QI + IDX_HEADS * IDX_DIM
_C_WI = _C_KI + IDX_HEADS * IDX_DIM
_C_US = _C_WI + LANE
_C_END = _C_US + D_S5


def _dot(a, b):
    return jnp.dot(a, b, preferred_element_type=F32)


def _dot_nt(a, b):
    return lax.dot_general(a, b, (((1,), (1,)), ((), ())), preferred_element_type=F32)


def _sigmoid(x):
    return jax.nn.sigmoid(x)


def _gelu_tanh(x):
    return 0.5 * x * (1.0 + jnp.tanh(0.7978845608028654 * (x + 0.044715 * (x * x * x))))


def _rms_mod(x, g, shift, scale):
    ms = jnp.mean(x * x, axis=-1, keepdims=True)
    y = x * lax.rsqrt(ms + EPS) * g
    return y * (1.0 + scale) + shift


def _resident(shape):
    return pl.BlockSpec(shape, lambda *_: (0,) * len(shape), pipeline_mode=pl.Buffered(1))


def _layer(tail, *lead):
    return pl.BlockSpec((None,) * len(lead) + tuple(tail), lambda *_: tuple(lead) + (0,) * len(tail),
                        pipeline_mode=pl.Buffered(1))


def _mod_spec(l, tm, seq, d):
    return pl.BlockSpec((None, 1, N_ADA, d), lambda i: (l, i * tm // seq, 0, 0))


def _params(sem):
    return pltpu.CompilerParams(dimension_semantics=sem, vmem_limit_bytes=VMEM_LIMIT)


def _ada_kernel(c_ref, w_ref, b_ref, o_ref):
    c = c_ref[...]
    a = c * _sigmoid(c)
    w = w_ref[0]
    a_hi = a.astype(BF16)
    a_lo = (a - a_hi.astype(F32)).astype(BF16)
    w_hi = w.astype(BF16)
    w_lo = (w - w_hi.astype(F32)).astype(BF16)
    o_ref[0] = _dot(a_hi, w_hi) + _dot(a_lo, w_hi) + _dot(a_hi, w_lo) + b_ref[0]


def _ada_call(c, ada_w, ada_b):
    nl, d, n9 = ada_w.shape
    b = c.shape[0]
    tn = 1024
    out = pl.pallas_call(
        _ada_kernel,
        grid=(nl, n9 // tn),
        in_specs=[
            pl.BlockSpec((b, d), lambda l, j: (0, 0)),
            pl.BlockSpec((1, d, tn), lambda l, j: (l, 0, j)),
            pl.BlockSpec((1, 1, tn), lambda l, j: (l, 0, j)),
        ],
        out_specs=pl.BlockSpec((1, b, tn), lambda l, j: (l, 0, j)),
        out_shape=jax.ShapeDtypeStruct((nl, b, n9), F32),
        compiler_params=_params(("arbitrary", "arbitrary")),
        name="ada_mod",
    )(c, ada_w, ada_b.reshape(nl, 1, n9))
    return out.reshape(nl, b, N_ADA, d)


_FFN_CHUNK = 256


def _ffn_kernel(row0, x_ref, mod_ref, g_ref, w1_ref, w3_ref, w2_ref, o_ref, h_ref):
    x = x_ref[...]
    m = mod_ref[0]
    u = _rms_mod(x, g_ref[...], m[row0:row0 + 1], m[row0 + 1:row0 + 2]).astype(BF16)
    for c in range(0, w1_ref.shape[1], _FFN_CHUNK):
        h1 = _dot(u, w1_ref[:, c:c + _FFN_CHUNK])
        h3 = _dot(u, w3_ref[:, c:c + _FFN_CHUNK])
        h_ref[:, c:c + _FFN_CHUNK] = (h1 * _sigmoid(h1) * h3).astype(BF16)
    o_ref[...] = x + 0.5 * (1.0 + m[row0 + 2:row0 + 3]) * _dot(h_ref[...], w2_ref[...])


def _ffn_call(x, mod, pw, l, j, seq, tm):
    n, d = x.shape
    f = pw["ffn_w1"].shape[-1]
    return pl.pallas_call(
        functools.partial(_ffn_kernel, 6 * j),
        grid=(n // tm,),
        in_specs=[
            pl.BlockSpec((tm, d), lambda i: (i, 0)),
            _mod_spec(l, tm, seq, d),
            _layer((1, d), l, 2 * j),
            _layer((d, f), l, j), _layer((d, f), l, j), _layer((f, d), l, j),
        ],
        out_specs=pl.BlockSpec((tm, d), lambda i: (i, 0)),
        out_shape=jax.ShapeDtypeStruct((n, d), F32),
        scratch_shapes=[pltpu.VMEM((tm, f), BF16)],
        compiler_params=_params(("arbitrary",)),
        name="ffn",
    )(x, mod, pw["norm_g"], pw["ffn_w1"], pw["ffn_w3"], pw["ffn_w2"])


def _rope_tiles(tiles, cos, sin, rot):
    rows = tiles[0].shape[0]
    partner = _dot(jnp.concatenate(tiles, axis=0).astype(BF16), rot)
    return [t * cos + partner[i * rows:(i + 1) * rows] * sin for i, t in enumerate(tiles)]


def _inproj_kernel(x_ref, mod_ref, g_ref, w_ref, wuq_ref, wukv_ref, vec_ref, rot_ref,
                   cm_ref, sm_ref, cd_ref, sd_ref, ci_ref, si_ref,
                   xr_ref, gr_ref, qm_ref, km_ref, vm_ref, qd_ref, kd_ref, vd_ref,
                   qi_ref, ki_ref, wi_ref, us_ref):
    x = x_ref[...]
    m = mod_ref[0]
    u = _rms_mod(x, g_ref[...], m[3:4], m[4:5]).astype(BF16)
    vec = vec_ref[...]
    lane = lax.broadcasted_iota(jnp.int32, (1, LANE), 1)
    z = _dot(u, w_ref[...])

    xr_ref[...] = z[:, _C_XR:_C_GR].astype(xr_ref.dtype)
    gr_ref[...] = z[:, _C_GR:_C_QL].astype(gr_ref.dtype)
    wi_ref[...] = z[:, _C_WI:_C_US]
    us_ref[...] = z[:, _C_US:_C_END].astype(us_ref.dtype)

    ql = z[:, _C_QL:_C_KVL]
    qn = (ql * lax.rsqrt(jnp.mean(ql * ql, axis=-1, keepdims=True) + EPS)
          * vec[0:1, :MLA_Q_LORA]).astype(BF16)
    q2 = _dot(qn, wuq_ref[...])
    kvl = z[:, _C_KVL:_C_KPE]
    kvn = (kvl * lax.rsqrt(jnp.mean(kvl * kvl, axis=-1, keepdims=True) + EPS)
           * vec[1:2, :MLA_KV_LORA]).astype(BF16)
    kv = _dot(kvn, wukv_ref[...])
    kpe = z[:, _C_KPE:_C_QD]
    gq = vec[2:3, :LANE]
    gk = vec[3:4, :LANE]
    roped = _rope_tiles([kpe * gk] + [q2[:, h * LANE:(h + 1) * LANE] * gq for h in range(MLA_HEADS)],
                        cm_ref[...], sm_ref[...], rot_ref[0])
    kpe_rot = roped[0]
    kpe_ss = jnp.sum(kpe * kpe, axis=-1, keepdims=True)
    q_tiles, k_tiles = [], []
    for h in range(MLA_HEADS):
        hs = slice(h * LANE, (h + 1) * LANE)
        qs = q2[:, hs]
        s = lax.rsqrt(jnp.sum(qs * qs, axis=-1, keepdims=True) * (1.0 / MLA_QK) + EPS) * MLA_QK ** -0.5
        q_tiles.append((s * roped[1 + h]).astype(qm_ref.dtype))
        kn = kv[:, hs]
        s = lax.rsqrt((jnp.sum(kn * kn, axis=-1, keepdims=True) + kpe_ss) * (1.0 / MLA_QK) + EPS)
        k_tiles.append((s * (kn * gk + kpe_rot)).astype(km_ref.dtype))
    qm_ref[...] = jnp.concatenate(q_tiles, axis=1)
    km_ref[...] = jnp.concatenate(k_tiles, axis=1)
    nslot = MLA_HEADS * LANE
    vlane = lax.broadcasted_iota(jnp.int32, (1, nslot), 1)
    v_is_low = (vlane // LANE) % 2 == 0
    ones_half = jnp.where(((vlane % LANE) < MLA_V) == v_is_low, 0.0, 1.0)
    vm_ref[...] = (kv[:, nslot:] + ones_half).astype(vm_ref.dtype)

    gd = vec[4:5, :LANE]
    low = lane < DSA_HEAD_DIM
    kd = z[:, _C_KD:_C_VD]
    xqs = [z[:, _C_QD + t * LANE:_C_QD + (t + 1) * LANE] for t in range(DSA_HEADS // 2)]
    roped = _rope_tiles([xq * gd for xq in xqs] + [kd * vec[5:6, :LANE]], cd_ref[...], sd_ref[...], rot_ref[1])
    d_tiles = []
    for t in range(DSA_HEADS // 2):
        x2 = xqs[t] * xqs[t]
        ss_even = jnp.sum(jnp.where(low, x2, 0.0), axis=-1, keepdims=True)
        ss_odd = jnp.sum(jnp.where(low, 0.0, x2), axis=-1, keepdims=True)
        y = roped[t]
        s_even = lax.rsqrt(ss_even * (1.0 / DSA_HEAD_DIM) + EPS) * DSA_HEAD_DIM ** -0.5
        s_odd = lax.rsqrt(ss_odd * (1.0 / DSA_HEAD_DIM) + EPS) * DSA_HEAD_DIM ** -0.5
        d_tiles.append(jnp.where(low, y * s_even, 0.0).astype(qd_ref.dtype))
        d_tiles.append(jnp.where(low, pltpu.roll(y, DSA_HEAD_DIM, 1) * s_odd, 0.0).astype(qd_ref.dtype))
    qd_ref[...] = jnp.concatenate(d_tiles, axis=1)
    s = lax.rsqrt(jnp.sum(kd * kd, axis=-1, keepdims=True) * (1.0 / DSA_HEAD_DIM) + EPS)
    kd_ref[...] = (s * roped[-1]).astype(kd_ref.dtype)
    vd_ref[...] = (z[:, _C_VD:_C_QI] + jnp.where(low, 0.0, 1.0)).astype(vd_ref.dtype)

    ntile = IDX_HEADS * IDX_DIM // LANE
    roped = _rope_tiles([z[:, c + t * LANE:c + (t + 1) * LANE] for c in (_C_QI, _C_KI) for t in range(ntile)],
                        ci_ref[...], si_ref[...], rot_ref[2])
    qi_ref[...] = jnp.concatenate(roped[:ntile], axis=1).astype(qi_ref.dtype)
    ki_ref[...] = jnp.concatenate(roped[ntile:], axis=1).astype(ki_ref.dtype)


def _inproj_call(x, mod, pw, l, tabs, seq, tm):
    n, d = x.shape
    row = lambda w: pl.BlockSpec((tm, w), lambda i: (i, 0))
    out_widths = [(D_RNN, BF16), (D_RNN, BF16), (MLA_HEADS * LANE, BF16), (MLA_HEADS * LANE, BF16),
                  (MLA_HEADS * LANE, BF16), (DSA_HEADS * LANE, BF16), (LANE, BF16), (LANE, BF16),
                  (IDX_HEADS * IDX_DIM, BF16), (IDX_HEADS * IDX_DIM, BF16), (LANE, F32), (D_S5, BF16)]
    return pl.pallas_call(
        _inproj_kernel,
        grid=(n // tm,),
        in_specs=[
            row(d), _mod_spec(l, tm, seq, d), _layer((1, d), l, 1),
            _layer(pw["w_all"].shape[1:], l), _layer(pw["wuq"].shape[1:], l),
            _layer(pw["wukv"].shape[1:], l), _layer(pw["vec"].shape[1:], l),
            _resident(pw["rot"].shape),
        ] + [row(LANE)] * 6,
        out_specs=[row(w) for w, _ in out_widths],
        out_shape=[jax.ShapeDtypeStruct((n, w), dt) for w, dt in out_widths],
        compiler_params=_params(("arbitrary",)),
        name="in_proj",
    )(x, mod, pw["norm_g"], pw["w_all"], pw["wuq"], pw["wukv"], pw["vec"], pw["rot"], *tabs)


def _rglru_kernel(tc, x_ref, gate_ref, cw_ref, vec_ref, wa_ref, wx_ref, o_ref, xs_ref, h_ref):
    j = pl.program_id(1)

    @pl.when(j == 0)
    def _():
        xs_ref[0:8, :] = jnp.zeros((8, D_RNN), F32)
        h_ref[...] = jnp.zeros(h_ref.shape, F32)

    xs_ref[8:8 + tc, :] = x_ref[...].astype(F32)
    cw = cw_ref[...]
    vec = vec_ref[...]
    xc = vec[0:1]
    for k in range(CONV_WIDTH):
        xc = xc + cw[k:k + 1] * xs_ref[pl.ds(8 - (CONV_WIDTH - 1) + k, tc), :]
    xs_ref[0:8, :] = xs_ref[tc:tc + 8, :]

    xb = xc.astype(BF16)
    r = _sigmoid(_dot(xb, wa_ref[...]) + vec[1:2])
    ig = _sigmoid(_dot(xb, wx_ref[...]) + vec[2:3])
    nl = -vec[3:4]
    softplus = jnp.maximum(nl, 0.0) + jnp.log(1.0 + jnp.exp(-jnp.abs(nl)))
    log_a = (-LRU_C) * r * softplus
    a = jnp.exp(log_a)
    z = 2.0 * log_a
    series = -z * (1.0 + z * (0.5 + z * (1.0 / 6.0 + z * (1.0 / 24.0 + z * (1.0 / 120.0 + z * (1.0 / 720.0))))))
    nem1 = jnp.where(z > -0.25, series, 1.0 - jnp.exp(z))
    b = jnp.sqrt(nem1) * ig * xc

    rowi = lax.broadcasted_iota(jnp.int32, (tc, D_RNN), 0)
    d = 1
    while d < tc:
        keep = rowi >= d
        a_s = jnp.where(keep, pltpu.roll(a, d, 0), 1.0)
        b_s = jnp.where(keep, pltpu.roll(b, d, 0), 0.0)
        b = a * b_s + b
        a = a * a_s
        d *= 2
    h = b + a * h_ref[...]
    h_ref[...] = h[tc - 1:tc, :]
    o_ref[...] = (h * _gelu_tanh(gate_ref[...].astype(F32))).astype(o_ref.dtype)


def _rglru_call(xr, gr, pw, l, bsz, seq, tc):
    n = xr.shape[0]
    nt = seq // tc
    row = pl.BlockSpec((tc, D_RNN), lambda b, j: (b * nt + j, 0))
    return pl.pallas_call(
        functools.partial(_rglru_kernel, tc),
        grid=(bsz, nt),
        in_specs=[row, row,
                  _layer((CONV_WIDTH, D_RNN), l), _layer((4, D_RNN), l),
                  _layer((D_RNN, D_RNN), l), _layer((D_RNN, D_RNN), l)],
        out_specs=row,
        out_shape=jax.ShapeDtypeStruct((n, D_RNN), BF16),
        scratch_shapes=[pltpu.VMEM((tc + 8, D_RNN), F32), pltpu.VMEM((1, D_RNN), F32)],
        compiler_params=_params(("arbitrary", "arbitrary")),
        name="rglru",
    )(xr, gr, pw["conv_w"], pw["rg_vec"], pw["rg_wa"], pw["rg_wx"])


def _flash_kernel(q_ref, k_ref, v_ref, o_ref, m_ref, acc_ref):
    i = pl.program_id(1)
    j = pl.program_id(2)
    bq, bk = q_ref.shape[0], k_ref.shape[0]

    @pl.when(j == 0)
    def _():
        m_ref[...] = jnp.full(m_ref.shape, NEG_BIG, F32)
        acc_ref[...] = jnp.zeros(acc_ref.shape, F32)

    def step(masked):
        if masked:
            causal = (lax.broadcasted_iota(jnp.int32, (bq, bk), 1)
                      <= lax.broadcasted_iota(jnp.int32, (bq, bk), 0))
        scores = [_dot_nt(q_ref[:, h * LANE:(h + 1) * LANE], k_ref[:, h * LANE:(h + 1) * LANE])
                  for h in range(MLA_HEADS)]
        m_out, alphas, probs = [], [], []
        for h in range(MLA_HEADS):
            s = scores[h]
            if masked:
                s = jnp.where(causal, s, NEG_BIG)
            m_prev = m_ref[h]
            m_new = jnp.maximum(m_prev, jnp.max(s, axis=1, keepdims=True))
            probs.append(jnp.exp(s - jnp.concatenate([m_new] * (bk // LANE), axis=1)).astype(BF16))
            alphas.append(jnp.exp(m_prev - m_new))
            m_out.append(m_new)
        m_ref[...] = jnp.stack(m_out)
        acc_ref[...] = jnp.stack([alphas[h] * acc_ref[h] + _dot(probs[h], v_ref[:, h * LANE:(h + 1) * LANE])
                                  for h in range(MLA_HEADS)])

    @pl.when(j < i)
    def _():
        step(False)

    @pl.when(j == i)
    def _():
        step(True)
        lane = lax.broadcasted_iota(jnp.int32, (bq, LANE), 1)
        outs = []
        for pr in range(MLA_HEADS // 2):
            even = acc_ref[2 * pr]
            odd = acc_ref[2 * pr + 1]
            outs.append(jnp.where(lane < MLA_V, even / pltpu.roll(even, MLA_V, 1),
                                  odd / pltpu.roll(odd, MLA_V, 1)))
        o_ref[...] = jnp.concatenate(outs, axis=1).astype(o_ref.dtype)


def _flash_call(q, k, v, bsz, seq, bq):
    n = q.shape[0]
    nq = seq // bq
    kv_row = lambda b, i, j: (b * nq + jnp.minimum(j, i), 0)
    return pl.pallas_call(
        _flash_kernel,
        grid=(bsz, nq, nq),
        in_specs=[
            pl.BlockSpec((bq, MLA_HEADS * LANE), lambda b, i, j: (b * nq + i, 0)),
            pl.BlockSpec((bq, MLA_HEADS * LANE), kv_row),
            pl.BlockSpec((bq, MLA_HEADS * LANE), kv_row),
        ],
        out_specs=pl.BlockSpec((bq, MLA_HEADS * MLA_V), lambda b, i, j: (b * nq + i, 0)),
        out_shape=jax.ShapeDtypeStruct((n, MLA_HEADS * MLA_V), BF16),
        scratch_shapes=[pltpu.VMEM((MLA_HEADS, bq, LANE), F32), pltpu.VMEM((MLA_HEADS, bq, LANE), F32)],
        compiler_params=_params(("arbitrary",) * 3),
        name="mla_flash",
    )(q, k, v)


_DSA_QB = 256
_DSA_HG = 2
_DSA_CHAINS = 8
_INT_MIN = -2 ** 31


def _dsa_body(ns, topk, i, q_ref, k_ref, v_ref, qi_ref, ki_ref, w_ref, tri_ref, o_ref):
    qb = _DSA_QB
    hg = _DSA_HG
    qi = qi_ref[...]
    ki = ki_ref[0:ns, :]
    w = w_ref[...]
    head_of_lane = jnp.right_shift(lax.broadcasted_iota(jnp.int32, qi.shape, 1), int(math.log2(IDX_DIM)))
    zero = jnp.zeros_like(qi)
    rels = [_dot_nt(jnp.concatenate([jnp.where(head_of_lane == h, qi, zero) for h in range(g, g + hg)], axis=0), ki)
            for g in range(0, IDX_HEADS, hg)]
    score = jnp.zeros((qb, ns), F32)
    for g, rel in enumerate(rels):
        for t in range(hg):
            score = score + w[:, g * hg + t:g * hg + t + 1] * jnp.maximum(rel[t * qb:(t + 1) * qb], 0.0)

    qpos = lax.broadcasted_iota(jnp.int32, (qb, ns), 0) + i * qb
    kpos = lax.broadcasted_iota(jnp.int32, (qb, ns), 1)
    bits = pltpu.bitcast(score + 0.0, jnp.int32)
    key = jnp.where(bits < 0, bits ^ jnp.int32(0x7FFFFFFF), bits)
    key = jnp.where(kpos <= qpos, key, jnp.int32(_INT_MIN))
    kk = jnp.minimum(lax.broadcasted_iota(jnp.int32, (qb, 1), 0) + (i * qb + 1), topk).astype(F32)

    rows = qb // _DSA_CHAINS
    keys = [key[r * rows:(r + 1) * rows] for r in range(_DSA_CHAINS)]
    kks = [kk[r * rows:(r + 1) * rows] for r in range(_DSA_CHAINS)]

    def body(it, thrs):
        bit = lax.shift_left(jnp.int32(1), jnp.int32(31) - it)
        out = []
        for kr, kkr, t in zip(keys, kks, thrs):
            c = t + bit
            cnt = jnp.sum(jnp.where(kr >= c, 1.0, 0.0), axis=1, keepdims=True)
            out.append(jnp.where(cnt >= kkr, c, t))
        return tuple(out)

    thrs = lax.fori_loop(0, 32, body, tuple(jnp.full((rows, 1), _INT_MIN, jnp.int32) for _ in keys), unroll=2)
    thr = jnp.concatenate(thrs, axis=0)

    nchunk = ns // LANE
    need = kk - jnp.sum(jnp.where(key > thr, 1.0, 0.0), axis=1, keepdims=True)
    eqs = [jnp.where(key[:, c * LANE:(c + 1) * LANE] == thr, 1.0, 0.0) for c in range(nchunk)]
    before_all = _dot(jnp.concatenate(eqs, axis=0).astype(BF16), tri_ref[...])
    run = jnp.zeros((qb, 1), F32)
    bias_chunks = []
    for c in range(nchunk):
        before = before_all[c * qb:(c + 1) * qb] + run
        take = jnp.where(key[:, c * LANE:(c + 1) * LANE] > thr, 1.0, jnp.where(before < need, eqs[c], 0.0))
        bias_chunks.append(jnp.where(take > 0.5, 0.0, NEG_BIG))
        run = run + jnp.sum(eqs[c], axis=1, keepdims=True)
    bias = jnp.concatenate(bias_chunks, axis=1)

    k = k_ref[0:ns, :]
    v = v_ref[0:ns, :]
    lane = lax.broadcasted_iota(jnp.int32, (qb, LANE), 1)
    scores = [_dot_nt(jnp.concatenate([q_ref[:, h * LANE:(h + 1) * LANE] for h in range(g, g + hg)], axis=0), k)
              for g in range(0, DSA_HEADS, hg)]
    out_tiles = []
    for s in scores:
        s = s.reshape(hg, qb, ns) + bias[None]
        p = jnp.exp(s - jnp.max(s, axis=2, keepdims=True))
        o = _dot(p.reshape(hg * qb, ns).astype(BF16), v)
        for t in range(0, hg, 2):
            even = o[t * qb:(t + 1) * qb]
            odd = o[(t + 1) * qb:(t + 2) * qb]
            out_tiles.append(jnp.where(lane < DSA_HEAD_DIM, even / pltpu.roll(even, DSA_HEAD_DIM, 1),
                                       pltpu.roll(odd, DSA_HEAD_DIM, 1) / odd))
    o_ref[...] = jnp.concatenate(out_tiles, axis=1).astype(o_ref.dtype)


def _dsa_kernel(seq, bucket, topk, q_ref, k_ref, v_ref, qi_ref, ki_ref, w_ref, tri_ref, o_ref):
    i = pl.program_id(1)
    for bk in range(seq // bucket):
        @pl.when((i * _DSA_QB) // bucket == bk)
        def _(bk=bk):
            _dsa_body((bk + 1) * bucket, topk, i, q_ref, k_ref, v_ref, qi_ref, ki_ref, w_ref,
                      tri_ref, o_ref)


def _dsa_call(qd, kd, vd, qi, ki, wi, tri, bsz, seq):
    n = qd.shape[0]
    qb = _DSA_QB
    nq = seq // qb
    bucket = min(512, seq)
    topk = min(TOPK_MAX, seq // 4)
    qrow = lambda w: pl.BlockSpec((qb, w), lambda b, i: (b * nq + i, 0))
    full = lambda w: pl.BlockSpec((seq, w), lambda b, i: (b, 0))
    return pl.pallas_call(
        functools.partial(_dsa_kernel, seq, bucket, topk),
        grid=(bsz, nq),
        in_specs=[qrow(DSA_HEADS * LANE), full(LANE), full(LANE),
                  qrow(IDX_HEADS * IDX_DIM), full(IDX_HEADS * IDX_DIM), qrow(LANE),
                  _resident((LANE, LANE))],
        out_specs=qrow(DSA_HEADS * DSA_HEAD_DIM),
        out_shape=jax.ShapeDtypeStruct((n, DSA_HEADS * DSA_HEAD_DIM), BF16),
        compiler_params=_params(("arbitrary", "arbitrary")),
        name="dsa",
    )(qd, kd, vd, qi, ki, wi, tri)


def _s5_kernel(rows, u_ref, wt_ref, ws_ref, wo_ref, lam_ref, d_ref, o_ref, uf_ref, yf_ref):
    half = S5_GB * S5_STATE
    rowi = lax.broadcasted_iota(jnp.int32, (rows, half), 0)
    for gb in range(S5_NGB):
        gs = slice(gb * LANE, (gb + 1) * LANE)
        uf_ref[gb] = u_ref[:, gs].astype(F32)
        uf = jnp.concatenate([uf_ref[gb, pl.ds(a, rows, stride=S5_CHUNK), :] for a in range(S5_CHUNK)], axis=1)
        u = uf.astype(BF16)
        y = _dot(u, wt_ref[gb])
        st = _dot(u, ws_ref[gb])
        xr = st[:, :half]
        xi = st[:, half:]
        lam = lam_ref[gb]
        lr = lam[0:1]
        li = lam[1:2]
        d = 1
        while d < rows:
            keep = rowi >= d
            sr = jnp.where(keep, pltpu.roll(xr, d, 0), 0.0)
            si = jnp.where(keep, pltpu.roll(xi, d, 0), 0.0)
            xr, xi = xr + lr * sr - li * si, xi + lr * si + li * sr
            lr, li = lr * lr - li * li, 2.0 * lr * li
            d *= 2
        keep = rowi >= 1
        pr = jnp.where(keep, pltpu.roll(xr, 1, 0), 0.0)
        pi = jnp.where(keep, pltpu.roll(xi, 1, 0), 0.0)
        xp = jnp.concatenate([pr, pi], axis=1).astype(BF16)
        yg = _gelu_tanh(y + _dot(xp, wo_ref[gb]) + d_ref[gb] * uf)
        for a in range(S5_CHUNK):
            yf_ref[gb, pl.ds(a, rows, stride=S5_CHUNK), :] = yg[:, a * LANE:(a + 1) * LANE]
        o_ref[:, gs] = yf_ref[gb].astype(o_ref.dtype)


def _s5_call(us, pw, l, bsz, seq):
    n = us.shape[0]
    blk = pl.BlockSpec((seq, D_S5), lambda b: (b, 0))
    wshape = pw["s5_wt"].shape[1:]
    return pl.pallas_call(
        functools.partial(_s5_kernel, seq // S5_CHUNK),
        grid=(bsz,),
        in_specs=[blk, _layer(wshape, l), _layer(wshape, l), _layer(wshape, l),
                  _layer(pw["s5_lam"].shape[1:], l), _layer(pw["s5_d"].shape[1:], l)],
        out_specs=blk,
        out_shape=jax.ShapeDtypeStruct((n, D_S5), BF16),
        scratch_shapes=[pltpu.VMEM((S5_NGB, seq, LANE), F32), pltpu.VMEM((S5_NGB, seq, LANE), F32)],
        compiler_params=_params(("arbitrary",)),
        name="s5",
    )(us, pw["s5_wt"], pw["s5_ws"], pw["s5_wo"], pw["s5_lam"], pw["s5_d"])


def _merge_kernel(x_ref, mod_ref, g_ref, ya_ref, yb_ref, yc_ref, yd_ref,
                  wg_ref, wb_ref, wglu_ref, bglu_ref, wout_ref, o_ref):
    x = x_ref[...]
    m = mod_ref[0]
    d = x.shape[1]
    u = _rms_mod(x, g_ref[...], m[3:4], m[4:5]).astype(BF16)
    yd = yd_ref[...]
    ydg = (yd.astype(F32) * _sigmoid(_dot(yd, wglu_ref[...]) + bglu_ref[...])).astype(BF16)
    ys = (ya_ref[...], yb_ref[...], yc_ref[...], ydg)
    merged = jnp.zeros(x.shape, F32)
    for nb in range(N_BRANCH):
        gate = _sigmoid(_dot(u, wg_ref[:, nb * d:(nb + 1) * d]))
        merged = merged + gate * _dot(ys[nb], wb_ref[nb])
    o_ref[...] = x + (1.0 + m[5:6]) * _dot(merged.astype(BF16), wout_ref[...])


def _merge_call(x, mod, ya, yb, yc, yd, pw, l, seq, tm):
    n, d = x.shape
    row = lambda w: pl.BlockSpec((tm, w), lambda i: (i, 0))
    return pl.pallas_call(
        _merge_kernel,
        grid=(n // tm,),
        in_specs=[row(d), _mod_spec(l, tm, seq, d), _layer((1, d), l, 1),
                  row(BRANCH_W), row(BRANCH_W), row(BRANCH_W), row(BRANCH_W),
                  _layer((d, N_BRANCH * d), l), _layer((N_BRANCH, BRANCH_W, d), l),
                  _layer((D_S5, D_S5), l), _layer((1, D_S5), l), _layer((d, d), l)],
        out_specs=row(d),
        out_shape=jax.ShapeDtypeStruct((n, d), F32),
        compiler_params=_params(("arbitrary",)),
        name="merge",
    )(x, mod, pw["norm_g"], ya, yb, yc, yd, pw["w_gate"], pw["w_branch"], pw["w_glu"], pw["b_glu"], pw["w_out"])


def _to_slots(w, nheads, hdim):
    lead = w.shape[:-1]
    wh = w.reshape(lead + (nheads, hdim))
    wh = jnp.pad(wh, [(0, 0)] * (len(lead) + 1) + [(0, LANE - hdim)])
    return wh.reshape(lead + (nheads * LANE,))


def _pad_lane(v, width=LANE):
    return jnp.pad(v, [(0, 0)] * (v.ndim - 1) + [(0, width - v.shape[-1])])


def _rope_table(positions, rot, period):
    j = jnp.arange(LANE) % period
    inv = jnp.where(j < rot, ROPE_THETA ** (-(2 * (j % (rot // 2))).astype(F32) / rot), 0.0)
    ang = positions.astype(F32).reshape(-1, 1) * inv[None, :]
    return jnp.cos(ang), jnp.sin(ang)


def _rot_matrix(rot, period):
    src = jnp.arange(LANE)[:, None]
    dst = jnp.arange(LANE)[None, :]
    j = dst % period
    half = rot // 2
    return (jnp.where((j < half) & (src == dst + half), -1.0, 0.0)
            + jnp.where((j >= half) & (j < rot) & (src == dst - half), 1.0, 0.0)).astype(BF16)


def _s5_weights(lam_re, lam_im, log_dt, b_re, b_im, c_re, c_im, dvec):
    hp = lax.Precision.HIGHEST
    g, p = lam_re.shape
    ch = S5_CHUNK
    dt = jnp.exp(log_dt)[:, None]
    mag = jnp.exp(lam_re * dt)
    ar, ai = mag * jnp.cos(lam_im * dt), mag * jnp.sin(lam_im * dt)
    den = lam_re * lam_re + lam_im * lam_im
    nr, ni = ar - 1.0, ai
    f_re = (nr * lam_re + ni * lam_im) / den
    f_im = (ni * lam_re - nr * lam_im) / den
    bb_re = f_re[..., None] * b_re - f_im[..., None] * b_im
    bb_im = f_re[..., None] * b_im + f_im[..., None] * b_re
    pr, pi = [jnp.ones_like(ar)], [jnp.zeros_like(ar)]
    for _ in range(ch):
        pr, pi = pr + [pr[-1] * ar - pi[-1] * ai], pi + [pr[-1] * ai + pi[-1] * ar]
    pr, pi = jnp.stack(pr), jnp.stack(pi)
    mr = pr[..., None] * bb_re - pi[..., None] * bb_im
    mi = pr[..., None] * bb_im + pi[..., None] * bb_re
    kern = (jnp.einsum('gjp,tgpi->tgji', c_re, mr, precision=hp)
            - jnp.einsum('gjp,tgpi->tgji', c_im, mi, precision=hp))

    def group_diag(x):
        r, c = x.shape[-2:]
        x = x.reshape(x.shape[:-3] + (S5_NGB, S5_GB * r, c))
        on_diag = jnp.arange(S5_GB * r)[:, None] // r == jnp.arange(S5_GB * c)[None, :] // c
        return jnp.tile(x, (1,) * (x.ndim - 1) + (S5_GB,)) * on_diag.astype(x.dtype)

    a_in = jnp.arange(ch)[:, None]
    a_out = jnp.arange(ch)[None, :]
    lag = jnp.clip(a_out - a_in, 0, ch)
    bd_k = group_diag(kern.transpose(0, 1, 3, 2)).astype(BF16)
    toe = bd_k[lag] * (a_out >= a_in)[..., None, None, None].astype(BF16)
    w_toe = toe.transpose(2, 0, 3, 1, 4).reshape(S5_NGB, ch * LANE, ch * LANE)
    rev = jnp.arange(ch - 1, -1, -1)
    st = jnp.concatenate([group_diag(mr[:ch].transpose(0, 1, 3, 2)), group_diag(mi[:ch].transpose(0, 1, 3, 2))],
                         axis=-1).astype(BF16)
    w_st = st[rev].transpose(1, 0, 2, 3).reshape(S5_NGB, ch * LANE, 2 * S5_GB * p)
    pr1, pi1 = pr[1:], pi[1:]
    co_re = c_re[None] * pr1[:, :, None, :] - c_im[None] * pi1[:, :, None, :]
    co_im = -(c_re[None] * pi1[:, :, None, :] + c_im[None] * pr1[:, :, None, :])
    co = jnp.stack([group_diag(co_re.transpose(0, 1, 3, 2)), group_diag(co_im.transpose(0, 1, 3, 2))],
                   axis=2).astype(BF16)
    w_out = co.reshape(ch, S5_NGB, 2 * S5_GB * p, LANE).transpose(1, 2, 0, 3).reshape(
        S5_NGB, 2 * S5_GB * p, ch * LANE)
    lam_c = jnp.stack([pr[ch].reshape(S5_NGB, S5_GB * p), pi[ch].reshape(S5_NGB, S5_GB * p)], axis=1)
    d_t = jnp.tile(dvec.reshape(S5_NGB, 1, LANE), (1, 1, ch))
    return w_toe, w_st, w_out, lam_c, d_t


def _block_diag(w):
    h, a, b = w.shape
    eye = jnp.eye(h, dtype=w.dtype)
    return jnp.einsum('hij,hg->higj', w, eye).reshape(h * a, h * b)


def _prep_params(p):
    w_in = p["w_in"]
    nl = w_in.shape[0]
    offs = [0]
    for s in IN_SPLITS:
        offs.append(offs[-1] + s)
    seg = [w_in[:, :, offs[k]:offs[k + 1]] for k in range(len(IN_SPLITS))]
    (w_xr, w_gr, w_ql, w_kvl, w_kpe, w_qd, w_kd, w_vd, w_qi, w_ki, w_wi, w_us, w_gate) = seg
    w_all = jnp.concatenate([w_xr, w_gr, w_ql, w_kvl, _pad_lane(w_kpe), w_qd, _pad_lane(w_kd), _pad_lane(w_vd),
                             w_qi, jnp.tile(w_ki, (1, 1, IDX_HEADS)), _pad_lane(w_wi), w_us],
                            axis=2).astype(BF16)
    assert w_all.shape[2] == _C_END

    wuq = _to_slots(p["mla_w_uq"], MLA_HEADS, MLA_QK).astype(BF16)
    wkv = p["mla_w_ukv"].reshape(nl, MLA_KV_LORA, MLA_HEADS, MLA_NOPE + MLA_V)
    kn = jnp.pad(wkv[..., :MLA_NOPE], ((0, 0), (0, 0), (0, 0), (MLA_ROPE, LANE - MLA_QK)))
    wv = wkv[..., MLA_NOPE:].reshape(nl, MLA_KV_LORA, MLA_HEADS // 2, 2, MLA_V)
    zv = jnp.zeros_like(wv[:, :, :, 0])
    wv = jnp.stack([jnp.concatenate([wv[:, :, :, 0], zv], axis=-1),
                    jnp.concatenate([zv, wv[:, :, :, 1]], axis=-1)], axis=3)
    wukv = jnp.concatenate([kn.reshape(nl, MLA_KV_LORA, MLA_HEADS * LANE),
                            wv.reshape(nl, MLA_KV_LORA, MLA_HEADS * LANE)], axis=2).astype(BF16)

    dq = p["dsa_qk_gain"][:, 0]
    vec = jnp.stack([_pad_lane(v, MLA_Q_LORA) for v in (
        p["mla_q_norm"], p["mla_kv_norm"], p["mla_qk_gain"][:, 0], p["mla_qk_gain"][:, 1],
        jnp.concatenate([dq, dq], axis=-1), p["dsa_qk_gain"][:, 1])], axis=1)

    s5_wt, s5_ws, s5_wo, s5_lam, s5_d = jax.vmap(_s5_weights)(
        p["s5_lambda_re"], p["s5_lambda_im"], p["s5_log_dt"], p["s5_b_re"], p["s5_b_im"],
        p["s5_c_re"], p["s5_c_im"], p["s5_d"])

    return dict(
        w_all=w_all, wuq=wuq, wukv=wukv, vec=vec,
        rot=jnp.stack([_rot_matrix(MLA_ROPE, LANE), _rot_matrix(DSA_ROT, DSA_HEAD_DIM),
                       _rot_matrix(IDX_ROT, IDX_DIM)]),
        conv_w=p["conv_w"],
        rg_vec=jnp.stack([p["conv_b"], p["rg_ba"], p["rg_bx"], p["rg_lambda"]], axis=1),
        rg_wa=jax.vmap(_block_diag)(p["rg_wa"]).astype(BF16),
        rg_wx=jax.vmap(_block_diag)(p["rg_wx"]).astype(BF16),
        s5_wt=s5_wt, s5_ws=s5_ws, s5_wo=s5_wo, s5_lam=s5_lam, s5_d=s5_d,
        w_gate=w_gate.astype(BF16),
        w_branch=p["w_branch"].astype(BF16),
        w_glu=p["s5_w_glu"].astype(BF16),
        b_glu=p["s5_b_glu"][:, None, :],
        w_out=p["w_out"].astype(BF16),
        ffn_w1=p["ffn_w1"].astype(BF16), ffn_w3=p["ffn_w3"].astype(BF16), ffn_w2=p["ffn_w2"].astype(BF16),
        norm_g=p["norm_g"][:, :, None, :],
    )


def _pick_tile(n, pref):
    t = pref
    while n % t:
        t //= 2
    return t


def kernel(x, c, positions, ada_w, ada_b, norm_g, ffn_w1, ffn_w3, ffn_w2, w_in,
           conv_w, conv_b, rg_wa, rg_ba, rg_wx, rg_bx, rg_lambda,
           mla_q_norm, mla_w_uq, mla_kv_norm, mla_w_ukv, mla_qk_gain, dsa_qk_gain,
           s5_lambda_re, s5_lambda_im, s5_log_dt, s5_b_re, s5_b_im, s5_c_re, s5_c_im,
           s5_d, s5_w_glu, s5_b_glu, w_branch, w_out):
    p = dict(norm_g=norm_g, ffn_w1=ffn_w1, ffn_w3=ffn_w3, ffn_w2=ffn_w2, w_in=w_in,
             conv_w=conv_w, conv_b=conv_b, rg_wa=rg_wa, rg_ba=rg_ba, rg_wx=rg_wx, rg_bx=rg_bx,
             rg_lambda=rg_lambda, mla_q_norm=mla_q_norm, mla_w_uq=mla_w_uq, mla_kv_norm=mla_kv_norm,
             mla_w_ukv=mla_w_ukv, mla_qk_gain=mla_qk_gain, dsa_qk_gain=dsa_qk_gain,
             s5_lambda_re=s5_lambda_re, s5_lambda_im=s5_lambda_im, s5_log_dt=s5_log_dt,
             s5_b_re=s5_b_re, s5_b_im=s5_b_im, s5_c_re=s5_c_re, s5_c_im=s5_c_im, s5_d=s5_d,
             s5_w_glu=s5_w_glu, s5_b_glu=s5_b_glu, w_branch=w_branch, w_out=w_out)
    bsz, seq, d = x.shape
    n = bsz * seq
    depth = ada_w.shape[0]
    assert seq % (S5_CHUNK * 8) == 0 and seq % _DSA_QB == 0

    tm = _pick_tile(seq, 512)
    tc = _pick_tile(seq, 256)
    bq = _pick_tile(seq, 512)

    mod = _ada_call(c, ada_w, ada_b)
    pw = _prep_params(p)
    tabs = (_rope_table(positions, MLA_ROPE, LANE) + _rope_table(positions, DSA_ROT, DSA_HEAD_DIM)
            + _rope_table(positions, IDX_ROT, IDX_DIM))
    tri = (jnp.arange(LANE)[:, None] < jnp.arange(LANE)[None, :]).astype(BF16)

    xf = x.reshape(n, d)
    for l in range(depth):
        xf = _ffn_call(xf, mod, pw, l, 0, seq, tm)
        (xr, gr, qm, km, vm, qd, kd, vd, qi, ki, wi, us) = _inproj_call(xf, mod, pw, l, tabs, seq, tm)
        ya = _rglru_call(xr, gr, pw, l, bsz, seq, tc)
        yb = _flash_call(qm, km, vm, bsz, seq, bq)
        yc = _dsa_call(qd, kd, vd, qi, ki, wi, tri, bsz, seq)
        yd = _s5_call(us, pw, l, bsz, seq)
        xf = _merge_call(xf, mod, ya, yb, yc, yd, pw, l, seq, tm)
        xf = _ffn_call(xf, mod, pw, l, 1, seq, tm)
    return xf.reshape(bsz, seq, d)
```

```python
import functools
import math

import jax
import jax.numpy as jnp
from jax import lax
from jax.experimental import pallas as pl
from jax.experimental.pallas import tpu as pltpu

F32 = jnp.float32
BF16 = jnp.bfloat16

D_MODEL = 1024
EPS = 1e-6
ROPE_THETA = 500000.0
D_FF = 2816
N_ADA = 9

D_RNN = 512
RNN_HEADS = 8
RNN_HEAD_DIM = D_RNN // RNN_HEADS
CONV_WIDTH = 4
LRU_C = 8.0

MLA_HEADS = 8
MLA_NOPE = 64
MLA_ROPE = 32
MLA_V = 64
MLA_QK = MLA_ROPE + MLA_NOPE
MLA_Q_LORA = 256
MLA_KV_LORA = 128

DSA_HEADS = 8
DSA_HEAD_DIM = 64
DSA_ROT = DSA_HEAD_DIM // 4
IDX_HEADS = 8
IDX_DIM = 32
IDX_ROT = IDX_DIM // 4
TOPK_MAX = 256

S5_GROUP = 16
S5_GROUPS = 32
D_S5 = S5_GROUP * S5_GROUPS
S5_STATE = 64
S5_CHUNK = 8
S5_GB = 8
S5_NGB = S5_GROUPS // S5_GB

N_BRANCH = 4
BRANCH_W = 512
IN_SPLITS = (D_RNN, D_RNN, MLA_Q_LORA, MLA_KV_LORA, MLA_ROPE,
             DSA_HEADS * DSA_HEAD_DIM, DSA_HEAD_DIM, DSA_HEAD_DIM,
             IDX_HEADS * IDX_DIM, IDX_DIM, IDX_HEADS, D_S5, N_BRANCH * D_MODEL)

LANE = 128
SUBLANES = 8
NEG_BIG = -1e30
VMEM_LIMIT = 56 * 1024 * 1024

_C_XR = 0
_C_GR = _C_XR + D_RNN
_C_QL = _C_GR + D_RNN
_C_KVL = _C_QL + MLA_Q_LORA
_C_KPE = _C_KVL + MLA_KV_LORA
_C_QD = _C_KPE + LANE
_C_KD = _C_QD + DSA_HEADS * DSA_HEAD_DIM
_C_VD = _C_KD + LANE
_C_QI = _C_VD + LANE
_C_KI = _C_QI + IDX_HEADS * IDX_DIM
_C_WI = _C_KI + IDX_HEADS * IDX_DIM
_C_US = _C_WI + LANE
_C_END = _C_US + D_S5


def _dot(a, b):
    return jnp.dot(a, b, preferred_element_type=F32)


def _dot_nt(a, b):
    return lax.dot_general(a, b, (((1,), (1,)), ((), ())), preferred_element_type=F32)


def _sigmoid(x):
    return jax.nn.sigmoid(x)


def _gelu_tanh(x):
    return 0.5 * x * (1.0 + jnp.tanh(0.7978845608028654 * (x + 0.044715 * (x * x * x))))


def _rms_mod(x, g, shift, scale):
    ms = jnp.mean(x * x, axis=-1, keepdims=True)
    y = x * lax.rsqrt(ms + EPS) * g
    return y * (1.0 + scale) + shift


def _resident(shape):
    return pl.BlockSpec(shape, lambda *_: (0,) * len(shape), pipeline_mode=pl.Buffered(1))


def _layer(tail, *lead):
    return pl.BlockSpec((None,) * len(lead) + tuple(tail), lambda *_: tuple(lead) + (0,) * len(tail),
                        pipeline_mode=pl.Buffered(1))


def _mod_spec(l, tm, seq, d):
    return pl.BlockSpec((None, 1, N_ADA, d), lambda i: (l, i * tm // seq, 0, 0))


def _params(sem):
    return pltpu.CompilerParams(dimension_semantics=sem, vmem_limit_bytes=VMEM_LIMIT)


def _ada_kernel(c_ref, w_ref, b_ref, o_ref):
    c = c_ref[...]
    a = c * _sigmoid(c)
    w = w_ref[0]
    a_hi = a.astype(BF16)
    a_lo = (a - a_hi.astype(F32)).astype(BF16)
    w_hi = w.astype(BF16)
    w_lo = (w - w_hi.astype(F32)).astype(BF16)
    o_ref[0] = _dot(a_hi, w_hi) + _dot(a_lo, w_hi) + _dot(a_hi, w_lo) + b_ref[0]


def _ada_call(c, ada_w, ada_b):
    nl, d, n9 = ada_w.shape
    b = c.shape[0]
    tn = 1024
    out = pl.pallas_call(
        _ada_kernel,
        grid=(nl, n9 // tn),
        in_specs=[
            pl.BlockSpec((b, d), lambda l, j: (0, 0)),
            pl.BlockSpec((1, d, tn), lambda l, j: (l, 0, j)),
            pl.BlockSpec((1, 1, tn), lambda l, j: (l, 0, j)),
        ],
        out_specs=pl.BlockSpec((1, b, tn), lambda l, j: (l, 0, j)),
        out_shape=jax.ShapeDtypeStruct((nl, b, n9), F32),
        compiler_params=_params(("arbitrary", "arbitrary")),
        name="ada_mod",
    )(c, ada_w, ada_b.reshape(nl, 1, n9))
    return out.reshape(nl, b, N_ADA, d)


_FFN_CHUNK = 256


def _ffn_kernel(row0, x_ref, mod_ref, g_ref, w1_ref, w3_ref, w2_ref, o_ref, h_ref):
    x = x_ref[...]
    m = mod_ref[0]
    u = _rms_mod(x, g_ref[...], m[row0:row0 + 1], m[row0 + 1:row0 + 2]).astype(BF16)
    for c in range(0, w1_ref.shape[1], _FFN_CHUNK):
        h1 = _dot(u, w1_ref[:, c:c + _FFN_CHUNK])
        h3 = _dot(u, w3_ref[:, c:c + _FFN_CHUNK])
        h_ref[:, c:c + _FFN_CHUNK] = (h1 * _sigmoid(h1) * h3).astype(BF16)
    o_ref[...] = x + 0.5 * (1.0 + m[row0 + 2:row0 + 3]) * _dot(h_ref[...], w2_ref[...])


def _ffn_call(x, mod, pw, l, j, seq, tm):
    n, d = x.shape
    f = pw["ffn_w1"].shape[-1]
    return pl.pallas_call(
        functools.partial(_ffn_kernel, 6 * j),
        grid=(n // tm,),
        in_specs=[
            pl.BlockSpec((tm, d), lambda i: (i, 0)),
            _mod_spec(l, tm, seq, d),
            _layer((1, d), l, 2 * j),
            _layer((d, f), l, j), _layer((d, f), l, j), _layer((f, d), l, j),
        ],
        out_specs=pl.BlockSpec((tm, d), lambda i: (i, 0)),
        out_shape=jax.ShapeDtypeStruct((n, d), F32),
        scratch_shapes=[pltpu.VMEM((tm, f), BF16)],
        compiler_params=_params(("arbitrary",)),
        name="ffn",
    )(x, mod, pw["norm_g"], pw["ffn_w1"], pw["ffn_w3"], pw["ffn_w2"])


def _rope_tiles(tiles, cos, sin, rot):
    rows = tiles[0].shape[0]
    partner = _dot(jnp.concatenate(tiles, axis=0).astype(BF16), rot)
    return [t * cos + partner[i * rows:(i + 1) * rows] * sin for i, t in enumerate(tiles)]


def _inproj_kernel(x_ref, mod_ref, g_ref, w_ref, wuq_ref, wukv_ref, vec_ref, rot_ref,
                   cm_ref, sm_ref, cd_ref, sd_ref, ci_ref, si_ref,
                   xr_ref, gr_ref, qm_ref, km_ref, vm_ref, qd_ref, kd_ref, vd_ref,
                   qi_ref, ki_ref, wi_ref, us_ref):
    x = x_ref[...]
    m = mod_ref[0]
    u = _rms_mod(x, g_ref[...], m[3:4], m[4:5]).astype(BF16)
    vec = vec_ref[...]
    lane = lax.broadcasted_iota(jnp.int32, (1, LANE), 1)
    z = _dot(u, w_ref[...])

    xr_ref[...] = z[:, _C_XR:_C_GR].astype(xr_ref.dtype)
    gr_ref[...] = z[:, _C_GR:_C_QL].astype(gr_ref.dtype)
    wi_ref[...] = z[:, _C_WI:_C_US]
    us_ref[...] = z[:, _C_US:_C_END].astype(us_ref.dtype)

    ql = z[:, _C_QL:_C_KVL]
    qn = (ql * lax.rsqrt(jnp.mean(ql * ql, axis=-1, keepdims=True) + EPS)
          * vec[0:1, :MLA_Q_LORA]).astype(BF16)
    q2 = _dot(qn, wuq_ref[...])
    kvl = z[:, _C_KVL:_C_KPE]
    kvn = (kvl * lax.rsqrt(jnp.mean(kvl * kvl, axis=-1, keepdims=True) + EPS)
           * vec[1:2, :MLA_KV_LORA]).astype(BF16)
    kv = _dot(kvn, wukv_ref[...])
    kpe = z[:, _C_KPE:_C_QD]
    gq = vec[2:3, :LANE]
    gk = vec[3:4, :LANE]
    roped = _rope_tiles([kpe * gk] + [q2[:, h * LANE:(h + 1) * LANE] * gq for h in range(MLA_HEADS)],
                        cm_ref[...], sm_ref[...], rot_ref[0])
    kpe_rot = roped[0]
    kpe_ss = jnp.sum(kpe * kpe, axis=-1, keepdims=True)
    q_tiles, k_tiles = [], []
    for h in range(MLA_HEADS):
        hs = slice(h * LANE, (h + 1) * LANE)
        qs = q2[:, hs]
        s = lax.rsqrt(jnp.sum(qs * qs, axis=-1, keepdims=True) * (1.0 / MLA_QK) + EPS) * MLA_QK ** -0.5
        q_tiles.append((s * roped[1 + h]).astype(qm_ref.dtype))
        kn = kv[:, hs]
        s = lax.rsqrt((jnp.sum(kn * kn, axis=-1, keepdims=True) + kpe_ss) * (1.0 / MLA_QK) + EPS)
        k_tiles.append((s * (kn * gk + kpe_rot)).astype(km_ref.dtype))
    qm_ref[...] = jnp.concatenate(q_tiles, axis=1)
    km_ref[...] = jnp.concatenate(k_tiles, axis=1)
    nslot = MLA_HEADS * LANE
    vlane = lax.broadcasted_iota(jnp.int32, (1, nslot), 1)
    v_is_low = (vlane // LANE) % 2 == 0
    ones_half = jnp.where(((vlane % LANE) < MLA_V) == v_is_low, 0.0, 1.0)
    vm_ref[...] = (kv[:, nslot:] + ones_half).astype(vm_ref.dtype)

    gd = vec[4:5, :LANE]
    low = lane < DSA_HEAD_DIM
    kd = z[:, _C_KD:_C_VD]
    xqs = [z[:, _C_QD + t * LANE:_C_QD + (t + 1) * LANE] for t in range(DSA_HEADS // 2)]
    roped = _rope_tiles([xq * gd for xq in xqs] + [kd * vec[5:6, :LANE]], cd_ref[...], sd_ref[...], rot_ref[1])
    d_tiles = []
    for t in range(DSA_HEADS // 2):
        x2 = xqs[t] * xqs[t]
        ss_even = jnp.sum(jnp.where(low, x2, 0.0), axis=-1, keepdims=True)
        ss_odd = jnp.sum(jnp.where(low, 0.0, x2), axis=-1, keepdims=True)
        y = roped[t]
        s_even = lax.rsqrt(ss_even * (1.0 / DSA_HEAD_DIM) + EPS) * DSA_HEAD_DIM ** -0.5
        s_odd = lax.rsqrt(ss_odd * (1.0 / DSA_HEAD_DIM) + EPS) * DSA_HEAD_DIM ** -0.5
        d_tiles.append(jnp.where(low, y * s_even, 0.0).astype(qd_ref.dtype))
        d_tiles.append(jnp.where(low, pltpu.roll(y, DSA_HEAD_DIM, 1) * s_odd, 0.0).astype(qd_ref.dtype))
    qd_ref[...] = jnp.concatenate(d_tiles, axis=1)
    s = lax.rsqrt(jnp.sum(kd * kd, axis=-1, keepdims=True) * (1.0 / DSA_HEAD_DIM) + EPS)
    kd_ref[...] = (s * roped[-1]).astype(kd_ref.dtype)
    vd_ref[...] = (z[:, _C_VD:_C_QI] + jnp.where(low, 0.0, 1.0)).astype(vd_ref.dtype)

    ntile = IDX_HEADS * IDX_DIM // LANE
    roped = _rope_tiles([z[:, c + t * LANE:c + (t + 1) * LANE] for c in (_C_QI, _C_KI) for t in range(ntile)],
                        ci_ref[...], si_ref[...], rot_ref[2])
    qi_ref[...] = jnp.concatenate(roped[:ntile], axis=1).astype(qi_ref.dtype)
    ki_ref[...] = jnp.concatenate(roped[ntile:], axis=1).astype(ki_ref.dtype)


def _inproj_call(x, mod, pw, l, tabs, seq, tm):
    n, d = x.shape
    row = lambda w: pl.BlockSpec((tm, w), lambda i: (i, 0))
    out_widths = [(D_RNN, BF16), (D_RNN, BF16), (MLA_HEADS * LANE, BF16), (MLA_HEADS * LANE, BF16),
                  (MLA_HEADS * LANE, BF16), (DSA_HEADS * LANE, BF16), (LANE, BF16), (LANE, BF16),
                  (IDX_HEADS * IDX_DIM, BF16), (IDX_HEADS * IDX_DIM, BF16), (LANE, F32), (D_S5, BF16)]
    return pl.pallas_call(
        _inproj_kernel,
        grid=(n // tm,),
        in_specs=[
            row(d), _mod_spec(l, tm, seq, d), _layer((1, d), l, 1),
            _layer(pw["w_all"].shape[1:], l), _layer(pw["wuq"].shape[1:], l),
            _layer(pw["wukv"].shape[1:], l), _layer(pw["vec"].shape[1:], l),
            _resident(pw["rot"].shape),
        ] + [row(LANE)] * 6,
        out_specs=[row(w) for w, _ in out_widths],
        out_shape=[jax.ShapeDtypeStruct((n, w), dt) for w, dt in out_widths],
        compiler_params=_params(("arbitrary",)),
        name="in_proj",
    )(x, mod, pw["norm_g"], pw["w_all"], pw["wuq"], pw["wukv"], pw["vec"], pw["rot"], *tabs)


def _rglru_kernel(tc, x_ref, gate_ref, cw_ref, vec_ref, wa_ref, wx_ref, o_ref, xs_ref, h_ref):
    j = pl.program_id(1)

    @pl.when(j == 0)
    def _():
        xs_ref[0:8, :] = jnp.zeros((8, D_RNN), F32)
        h_ref[...] = jnp.zeros(h_ref.shape, F32)

    xs_ref[8:8 + tc, :] = x_ref[...].astype(F32)
    cw = cw_ref[...]
    vec = vec_ref[...]
    xc = vec[0:1]
    for k in range(CONV_WIDTH):
        xc = xc + cw[k:k + 1] * xs_ref[pl.ds(8 - (CONV_WIDTH - 1) + k, tc), :]
    xs_ref[0:8, :] = xs_ref[tc:tc + 8, :]

    xb = xc.astype(BF16)
    r = _sigmoid(_dot(xb, wa_ref[...]) + vec[1:2])
    ig = _sigmoid(_dot(xb, wx_ref[...]) + vec[2:3])
    nl = -vec[3:4]
    softplus = jnp.maximum(nl, 0.0) + jnp.log(1.0 + jnp.exp(-jnp.abs(nl)))
    log_a = (-LRU_C) * r * softplus
    a = jnp.exp(log_a)
    z = 2.0 * log_a
    series = -z * (1.0 + z * (0.5 + z * (1.0 / 6.0 + z * (1.0 / 24.0 + z * (1.0 / 120.0 + z * (1.0 / 720.0))))))
    nem1 = jnp.where(z > -0.25, series, 1.0 - jnp.exp(z))
    b = jnp.sqrt(nem1) * ig * xc

    sub = lax.broadcasted_iota(jnp.int32, (tc, D_RNN), 0) & (SUBLANES - 1)
    d = 1
    while d < SUBLANES:
        keep = sub >= d
        a_s = jnp.where(keep, pltpu.roll(a, d, 0), 1.0)
        b_s = jnp.where(keep, pltpu.roll(b, d, 0), 0.0)
        b = a * b_s + b
        a = a * a_s
        d *= 2
    carry = h_ref[...]
    groups = []
    for g in range(tc // SUBLANES):
        rows = slice(g * SUBLANES, (g + 1) * SUBLANES)
        hg = b[rows] + a[rows] * carry
        groups.append(hg)
        carry = hg[SUBLANES - 1:SUBLANES]
    h = jnp.concatenate(groups, axis=0)
    h_ref[...] = carry
    o_ref[...] = (h * _gelu_tanh(gate_ref[...].astype(F32))).astype(o_ref.dtype)


def _rglru_call(xr, gr, pw, l, bsz, seq, tc):
    n = xr.shape[0]
    nt = seq // tc
    row = pl.BlockSpec((tc, D_RNN), lambda b, j: (b * nt + j, 0))
    return pl.pallas_call(
        functools.partial(_rglru_kernel, tc),
        grid=(bsz, nt),
        in_specs=[row, row,
                  _layer((CONV_WIDTH, D_RNN), l), _layer((4, D_RNN), l),
                  _layer((D_RNN, D_RNN), l), _layer((D_RNN, D_RNN), l)],
        out_specs=row,
        out_shape=jax.ShapeDtypeStruct((n, D_RNN), BF16),
        scratch_shapes=[pltpu.VMEM((tc + 8, D_RNN), F32), pltpu.VMEM((1, D_RNN), F32)],
        compiler_params=_params(("arbitrary", "arbitrary")),
        name="rglru",
    )(xr, gr, pw["conv_w"], pw["rg_vec"], pw["rg_wa"], pw["rg_wx"])


def _flash_kernel(q_ref, k_ref, v_ref, o_ref, m_ref, acc_ref):
    i = pl.program_id(1)
    j = pl.program_id(2)
    bq, bk = q_ref.shape[0], k_ref.shape[0]

    @pl.when(j == 0)
    def _():
        m_ref[...] = jnp.full(m_ref.shape, NEG_BIG, F32)
        acc_ref[...] = jnp.zeros(acc_ref.shape, F32)

    def step(masked):
        if masked:
            causal = (lax.broadcasted_iota(jnp.int32, (bq, bk), 1)
                      <= lax.broadcasted_iota(jnp.int32, (bq, bk), 0))
        scores = [_dot_nt(q_ref[:, h * LANE:(h + 1) * LANE], k_ref[:, h * LANE:(h + 1) * LANE])
                  for h in range(MLA_HEADS)]
        m_out, alphas, probs = [], [], []
        for h in range(MLA_HEADS):
            s = scores[h]
            if masked:
                s = jnp.where(causal, s, NEG_BIG)
            m_prev = m_ref[h]
            m_new = jnp.maximum(m_prev, jnp.max(s, axis=1, keepdims=True))
            probs.append(jnp.exp(s - jnp.concatenate([m_new] * (bk // LANE), axis=1)).astype(BF16))
            alphas.append(jnp.exp(m_prev - m_new))
            m_out.append(m_new)
        m_ref[...] = jnp.stack(m_out)
        acc_ref[...] = jnp.stack([alphas[h] * acc_ref[h] + _dot(probs[h], v_ref[:, h * LANE:(h + 1) * LANE])
                                  for h in range(MLA_HEADS)])

    @pl.when(j < i)
    def _():
        step(False)

    @pl.when(j == i)
    def _():
        step(True)
        lane = lax.broadcasted_iota(jnp.int32, (bq, LANE), 1)
        outs = []
        for pr in range(MLA_HEADS // 2):
            even = acc_ref[2 * pr]
            odd = acc_ref[2 * pr + 1]
            outs.append(jnp.where(lane < MLA_V, even / pltpu.roll(even, MLA_V, 1),
                                  odd / pltpu.roll(odd, MLA_V, 1)))
        o_ref[...] = jnp.concatenate(outs, axis=1).astype(o_ref.dtype)


def _flash_call(q, k, v, bsz, seq, bq):
    n = q.shape[0]
    nq = seq // bq
    kv_row = lambda b, i, j: (b * nq + jnp.minimum(j, i), 0)
    return pl.pallas_call(
        _flash_kernel,
        grid=(bsz, nq, nq),
        in_specs=[
            pl.BlockSpec((bq, MLA_HEADS * LANE), lambda b, i, j: (b * nq + i, 0)),
            pl.BlockSpec((bq, MLA_HEADS * LANE), kv_row),
            pl.BlockSpec((bq, MLA_HEADS * LANE), kv_row),
        ],
        out_specs=pl.BlockSpec((bq, MLA_HEADS * MLA_V), lambda b, i, j: (b * nq + i, 0)),
        out_shape=jax.ShapeDtypeStruct((n, MLA_HEADS * MLA_V), BF16),
        scratch_shapes=[pltpu.VMEM((MLA_HEADS, bq, LANE), F32), pltpu.VMEM((MLA_HEADS, bq, LANE), F32)],
        compiler_params=_params(("arbitrary",) * 3),
        name="mla_flash",
    )(q, k, v)


_DSA_QB = 256
_DSA_HG = 2
_DSA_CHAINS = 8
_INT_MIN = -2 ** 31


def _dsa_body(ns, topk, i, q_ref, k_ref, v_ref, qi_ref, ki_ref, w_ref, tri_ref, o_ref):
    qb = _DSA_QB
    hg = _DSA_HG
    qi = qi_ref[...]
    ki = ki_ref[0:ns, :]
    w = w_ref[...]
    head_of_lane = jnp.right_shift(lax.broadcasted_iota(jnp.int32, qi.shape, 1), int(math.log2(IDX_DIM)))
    zero = jnp.zeros_like(qi)
    rels = [_dot_nt(jnp.concatenate([jnp.where(head_of_lane == h, qi, zero) for h in range(g, g + hg)], axis=0), ki)
            for g in range(0, IDX_HEADS, hg)]
    score = jnp.zeros((qb, ns), F32)
    for g, rel in enumerate(rels):
        for t in range(hg):
            score = score + w[:, g * hg + t:g * hg + t + 1] * jnp.maximum(rel[t * qb:(t + 1) * qb], 0.0)

    qpos = lax.broadcasted_iota(jnp.int32, (qb, ns), 0) + i * qb
    kpos = lax.broadcasted_iota(jnp.int32, (qb, ns), 1)
    bits = pltpu.bitcast(score + 0.0, jnp.int32)
    key = jnp.where(bits < 0, bits ^ jnp.int32(0x7FFFFFFF), bits)
    key = jnp.where(kpos <= qpos, key, jnp.int32(_INT_MIN))
    kk = jnp.minimum(lax.broadcasted_iota(jnp.int32, (qb, 1), 0) + (i * qb + 1), topk).astype(F32)

    rows = qb // _DSA_CHAINS
    keys = [key[r * rows:(r + 1) * rows] for r in range(_DSA_CHAINS)]
    kks = [kk[r * rows:(r + 1) * rows] for r in range(_DSA_CHAINS)]

    def body(it, thrs):
        bit = lax.shift_left(jnp.int32(1), jnp.int32(31) - it)
        out = []
        for kr, kkr, t in zip(keys, kks, thrs):
            c = t + bit
            cnt = jnp.sum(jnp.where(kr >= c, 1.0, 0.0), axis=1, keepdims=True)
            out.append(jnp.where(cnt >= kkr, c, t))
        return tuple(out)

    thrs = lax.fori_loop(0, 32, body, tuple(jnp.full((rows, 1), _INT_MIN, jnp.int32) for _ in keys), unroll=2)
    thr = jnp.concatenate(thrs, axis=0)

    nchunk = ns // LANE
    need = kk - jnp.sum(jnp.where(key > thr, 1.0, 0.0), axis=1, keepdims=True)
    eqs = [jnp.where(key[:, c * LANE:(c + 1) * LANE] == thr, 1.0, 0.0) for c in range(nchunk)]
    before_all = _dot(jnp.concatenate(eqs, axis=0).astype(BF16), tri_ref[...])
    run = jnp.zeros((qb, 1), F32)
    bias_chunks = []
    for c in range(nchunk):
        before = before_all[c * qb:(c + 1) * qb] + run
        take = jnp.where(key[:, c * LANE:(c + 1) * LANE] > thr, 1.0, jnp.where(before < need, eqs[c], 0.0))
        bias_chunks.append(jnp.where(take > 0.5, 0.0, NEG_BIG))
        run = run + jnp.sum(eqs[c], axis=1, keepdims=True)
    bias = jnp.concatenate(bias_chunks, axis=1)

    k = k_ref[0:ns, :]
    v = v_ref[0:ns, :]
    lane = lax.broadcasted_iota(jnp.int32, (qb, LANE), 1)
    scores = [_dot_nt(jnp.concatenate([q_ref[:, h * LANE:(h + 1) * LANE] for h in range(g, g + hg)], axis=0), k)
              for g in range(0, DSA_HEADS, hg)]
    out_tiles = []
    for s in scores:
        s = s.reshape(hg, qb, ns) + bias[None]
        p = jnp.exp(s - jnp.max(s, axis=2, keepdims=True))
        o = _dot(p.reshape(hg * qb, ns).astype(BF16), v)
        for t in range(0, hg, 2):
            even = o[t * qb:(t + 1) * qb]
            odd = o[(t + 1) * qb:(t + 2) * qb]
            out_tiles.append(jnp.where(lane < DSA_HEAD_DIM, even / pltpu.roll(even, DSA_HEAD_DIM, 1),
                                       pltpu.roll(odd, DSA_HEAD_DIM, 1) / odd))
    o_ref[...] = jnp.concatenate(out_tiles, axis=1).astype(o_ref.dtype)


def _dsa_kernel(seq, bucket, topk, q_ref, k_ref, v_ref, qi_ref, ki_ref, w_ref, tri_ref, o_ref):
    i = pl.program_id(1)
    for bk in range(seq // bucket):
        @pl.when((i * _DSA_QB) // bucket == bk)
        def _(bk=bk):
            _dsa_body((bk + 1) * bucket, topk, i, q_ref, k_ref, v_ref, qi_ref, ki_ref, w_ref,
                      tri_ref, o_ref)


def _dsa_call(qd, kd, vd, qi, ki, wi, tri, bsz, seq):
    n = qd.shape[0]
    qb = _DSA_QB
    nq = seq // qb
    bucket = min(512, seq)
    topk = min(TOPK_MAX, seq // 4)
    qrow = lambda w: pl.BlockSpec((qb, w), lambda b, i: (b * nq + i, 0))
    full = lambda w: pl.BlockSpec((seq, w), lambda b, i: (b, 0))
    return pl.pallas_call(
        functools.partial(_dsa_kernel, seq, bucket, topk),
        grid=(bsz, nq),
        in_specs=[qrow(DSA_HEADS * LANE), full(LANE), full(LANE),
                  qrow(IDX_HEADS * IDX_DIM), full(IDX_HEADS * IDX_DIM), qrow(LANE),
                  _resident((LANE, LANE))],
        out_specs=qrow(DSA_HEADS * DSA_HEAD_DIM),
        out_shape=jax.ShapeDtypeStruct((n, DSA_HEADS * DSA_HEAD_DIM), BF16),
        compiler_params=_params(("arbitrary", "arbitrary")),
        name="dsa",
    )(qd, kd, vd, qi, ki, wi, tri)


def _s5_kernel(rows, u_ref, wt_ref, ws_ref, wo_ref, lam_ref, d_ref, o_ref, uf_ref, yf_ref):
    half = S5_GB * S5_STATE
    rowi = lax.broadcasted_iota(jnp.int32, (rows, half), 0)
    sub = rowi & (SUBLANES - 1)
    for gb in range(S5_NGB):
        gs = slice(gb * LANE, (gb + 1) * LANE)
        uf_ref[gb] = u_ref[:, gs].astype(F32)
        uf = jnp.concatenate([uf_ref[gb, pl.ds(a, rows, stride=S5_CHUNK), :] for a in range(S5_CHUNK)], axis=1)
        u = uf.astype(BF16)
        y = _dot(u, wt_ref[gb])
        st = _dot(u, ws_ref[gb])
        xr = st[:, :half]
        xi = st[:, half:]
        lam = lam_ref[gb]
        lr = lam[0:1]
        li = lam[1:2]
        pw_r, pw_i = [lr], [li]
        for _ in range(SUBLANES - 1):
            pw_r, pw_i = pw_r + [pw_r[-1] * lr - pw_i[-1] * li], pw_i + [pw_r[-1] * li + pw_i[-1] * lr]
        pw_r = jnp.concatenate(pw_r, axis=0)
        pw_i = jnp.concatenate(pw_i, axis=0)
        d = 1
        while d < SUBLANES:
            keep = sub >= d
            sr = jnp.where(keep, pltpu.roll(xr, d, 0), 0.0)
            si = jnp.where(keep, pltpu.roll(xi, d, 0), 0.0)
            xr, xi = xr + lr * sr - li * si, xi + lr * si + li * sr
            lr, li = lr * lr - li * li, 2.0 * lr * li
            d *= 2
        cr = jnp.zeros((1, half), F32)
        ci = jnp.zeros((1, half), F32)
        gr, gi = [], []
        for g in range(rows // SUBLANES):
            rs = slice(g * SUBLANES, (g + 1) * SUBLANES)
            hr = xr[rs] + pw_r * cr - pw_i * ci
            hi = xi[rs] + pw_r * ci + pw_i * cr
            gr.append(hr)
            gi.append(hi)
            cr = hr[SUBLANES - 1:SUBLANES]
            ci = hi[SUBLANES - 1:SUBLANES]
        xr = jnp.concatenate(gr, axis=0)
        xi = jnp.concatenate(gi, axis=0)
        keep = rowi >= 1
        pr = jnp.where(keep, pltpu.roll(xr, 1, 0), 0.0)
        pi = jnp.where(keep, pltpu.roll(xi, 1, 0), 0.0)
        xp = jnp.concatenate([pr, pi], axis=1).astype(BF16)
        yg = _gelu_tanh(y + _dot(xp, wo_ref[gb]) + d_ref[gb] * uf)
        for a in range(S5_CHUNK):
            yf_ref[gb, pl.ds(a, rows, stride=S5_CHUNK), :] = yg[:, a * LANE:(a + 1) * LANE]
        o_ref[:, gs] = yf_ref[gb].astype(o_ref.dtype)


def _s5_call(us, pw, l, bsz, seq):
    n = us.shape[0]
    blk = pl.BlockSpec((seq, D_S5), lambda b: (b, 0))
    wshape = pw["s5_wt"].shape[1:]
    return pl.pallas_call(
        functools.partial(_s5_kernel, seq // S5_CHUNK),
        grid=(bsz,),
        in_specs=[blk, _layer(wshape, l), _layer(wshape, l), _layer(wshape, l),
                  _layer(pw["s5_lam"].shape[1:], l), _layer(pw["s5_d"].shape[1:], l)],
        out_specs=blk,
        out_shape=jax.ShapeDtypeStruct((n, D_S5), BF16),
        scratch_shapes=[pltpu.VMEM((S5_NGB, seq, LANE), F32), pltpu.VMEM((S5_NGB, seq, LANE), F32)],
        compiler_params=_params(("arbitrary",)),
        name="s5",
    )(us, pw["s5_wt"], pw["s5_ws"], pw["s5_wo"], pw["s5_lam"], pw["s5_d"])


def _merge_kernel(x_ref, mod_ref, g_ref, ya_ref, yb_ref, yc_ref, yd_ref,
                  wg_ref, wb_ref, wglu_ref, bglu_ref, wout_ref, o_ref):
    x = x_ref[...]
    m = mod_ref[0]
    d = x.shape[1]
    u = _rms_mod(x, g_ref[...], m[3:4], m[4:5]).astype(BF16)
    yd = yd_ref[...]
    ydg = (yd.astype(F32) * _sigmoid(_dot(yd, wglu_ref[...]) + bglu_ref[...])).astype(BF16)
    ys = (ya_ref[...], yb_ref[...], yc_ref[...], ydg)
    merged = jnp.zeros(x.shape, F32)
    for nb in range(N_BRANCH):
        gate = _sigmoid(_dot(u, wg_ref[:, nb * d:(nb + 1) * d]))
        merged = merged + gate * _dot(ys[nb], wb_ref[nb])
    o_ref[...] = x + (1.0 + m[5:6]) * _dot(merged.astype(BF16), wout_ref[...])


def _merge_call(x, mod, ya, yb, yc, yd, pw, l, seq, tm):
    n, d = x.shape
    row = lambda w: pl.BlockSpec((tm, w), lambda i: (i, 0))
    return pl.pallas_call(
        _merge_kernel,
        grid=(n // tm,),
        in_specs=[row(d), _mod_spec(l, tm, seq, d), _layer((1, d), l, 1),
                  row(BRANCH_W), row(BRANCH_W), row(BRANCH_W), row(BRANCH_W),
                  _layer((d, N_BRANCH * d), l), _layer((N_BRANCH, BRANCH_W, d), l),
                  _layer((D_S5, D_S5), l), _layer((1, D_S5), l), _layer((d, d), l)],
        out_specs=row(d),
        out_shape=jax.ShapeDtypeStruct((n, d), F32),
        compiler_params=_params(("arbitrary",)),
        name="merge",
    )(x, mod, pw["norm_g"], ya, yb, yc, yd, pw["w_gate"], pw["w_branch"], pw["w_glu"], pw["b_glu"], pw["w_out"])


def _to_slots(w, nheads, hdim):
    lead = w.shape[:-1]
    wh = w.reshape(lead + (nheads, hdim))
    wh = jnp.pad(wh, [(0, 0)] * (len(lead) + 1) + [(0, LANE - hdim)])
    return wh.reshape(lead + (nheads * LANE,))


def _pad_lane(v, width=LANE):
    return jnp.pad(v, [(0, 0)] * (v.ndim - 1) + [(0, width - v.shape[-1])])


def _rope_table(positions, rot, period):
    j = jnp.arange(LANE) % period
    inv = jnp.where(j < rot, ROPE_THETA ** (-(2 * (j % (rot // 2))).astype(F32) / rot), 0.0)
    ang = positions.astype(F32).reshape(-1, 1) * inv[None, :]
    return jnp.cos(ang), jnp.sin(ang)


def _rot_matrix(rot, period):
    src = jnp.arange(LANE)[:, None]
    dst = jnp.arange(LANE)[None, :]
    j = dst % period
    half = rot // 2
    return (jnp.where((j < half) & (src == dst + half), -1.0, 0.0)
            + jnp.where((j >= half) & (j < rot) & (src == dst - half), 1.0, 0.0)).astype(BF16)


def _s5_weights(lam_re, lam_im, log_dt, b_re, b_im, c_re, c_im, dvec):
    hp = lax.Precision.HIGHEST
    g, p = lam_re.shape
    ch = S5_CHUNK
    dt = jnp.exp(log_dt)[:, None]
    mag = jnp.exp(lam_re * dt)
    ar, ai = mag * jnp.cos(lam_im * dt), mag * jnp.sin(lam_im * dt)
    den = lam_re * lam_re + lam_im * lam_im
    nr, ni = ar - 1.0, ai
    f_re = (nr * lam_re + ni * lam_im) / den
    f_im = (ni * lam_re - nr * lam_im) / den
    bb_re = f_re[..., None] * b_re - f_im[..., None] * b_im
    bb_im = f_re[..., None] * b_im + f_im[..., None] * b_re
    pr, pi = [jnp.ones_like(ar)], [jnp.zeros_like(ar)]
    for _ in range(ch):
        pr, pi = pr + [pr[-1] * ar - pi[-1] * ai], pi + [pr[-1] * ai + pi[-1] * ar]
    pr, pi = jnp.stack(pr), jnp.stack(pi)
    mr = pr[..., None] * bb_re - pi[..., None] * bb_im
    mi = pr[..., None] * bb_im + pi[..., None] * bb_re
    kern = (jnp.einsum('gjp,tgpi->tgji', c_re, mr, precision=hp)
            - jnp.einsum('gjp,tgpi->tgji', c_im, mi, precision=hp))

    def group_diag(x):
        r, c = x.shape[-2:]
        x = x.reshape(x.shape[:-3] + (S5_NGB, S5_GB * r, c))
        on_diag = jnp.arange(S5_GB * r)[:, None] // r == jnp.arange(S5_GB * c)[None, :] // c
        return jnp.tile(x, (1,) * (x.ndim - 1) + (S5_GB,)) * on_diag.astype(x.dtype)

    a_in = jnp.arange(ch)[:, None]
    a_out = jnp.arange(ch)[None, :]
    lag = jnp.clip(a_out - a_in, 0, ch)
    bd_k = group_diag(kern.transpose(0, 1, 3, 2)).astype(BF16)
    toe = bd_k[lag] * (a_out >= a_in)[..., None, None, None].astype(BF16)
    w_toe = toe.transpose(2, 0, 3, 1, 4).reshape(S5_NGB, ch * LANE, ch * LANE)
    rev = jnp.arange(ch - 1, -1, -1)
    st = jnp.concatenate([group_diag(mr[:ch].transpose(0, 1, 3, 2)), group_diag(mi[:ch].transpose(0, 1, 3, 2))],
                         axis=-1).astype(BF16)
    w_st = st[rev].transpose(1, 0, 2, 3).reshape(S5_NGB, ch * LANE, 2 * S5_GB * p)
    pr1, pi1 = pr[1:], pi[1:]
    co_re = c_re[None] * pr1[:, :, None, :] - c_im[None] * pi1[:, :, None, :]
    co_im = -(c_re[None] * pi1[:, :, None, :] + c_im[None] * pr1[:, :, None, :])
    co = jnp.stack([group_diag(co_re.transpose(0, 1, 3, 2)), group_diag(co_im.transpose(0, 1, 3, 2))],
                   axis=2).astype(BF16)
    w_out = co.reshape(ch, S5_NGB, 2 * S5_GB * p, LANE).transpose(1, 2, 0, 3).reshape(
        S5_NGB, 2 * S5_GB * p, ch * LANE)
    lam_c = jnp.stack([pr[ch].reshape(S5_NGB, S5_GB * p), pi[ch].reshape(S5_NGB, S5_GB * p)], axis=1)
    d_t = jnp.tile(dvec.reshape(S5_NGB, 1, LANE), (1, 1, ch))
    return w_toe, w_st, w_out, lam_c, d_t


def _block_diag(w):
    h, a, b = w.shape
    eye = jnp.eye(h, dtype=w.dtype)
    return jnp.einsum('hij,hg->higj', w, eye).reshape(h * a, h * b)


def _prep_params(p):
    w_in = p["w_in"]
    nl = w_in.shape[0]
    offs = [0]
    for s in IN_SPLITS:
        offs.append(offs[-1] + s)
    seg = [w_in[:, :, offs[k]:offs[k + 1]] for k in range(len(IN_SPLITS))]
    (w_xr, w_gr, w_ql, w_kvl, w_kpe, w_qd, w_kd, w_vd, w_qi, w_ki, w_wi, w_us, w_gate) = seg
    w_all = jnp.concatenate([w_xr, w_gr, w_ql, w_kvl, _pad_lane(w_kpe), w_qd, _pad_lane(w_kd), _pad_lane(w_vd),
                             w_qi, jnp.tile(w_ki, (1, 1, IDX_HEADS)), _pad_lane(w_wi), w_us],
                            axis=2).astype(BF16)
    assert w_all.shape[2] == _C_END

    wuq = _to_slots(p["mla_w_uq"], MLA_HEADS, MLA_QK).astype(BF16)
    wkv = p["mla_w_ukv"].reshape(nl, MLA_KV_LORA, MLA_HEADS, MLA_NOPE + MLA_V)
    kn = jnp.pad(wkv[..., :MLA_NOPE], ((0, 0), (0, 0), (0, 0), (MLA_ROPE, LANE - MLA_QK)))
    wv = wkv[..., MLA_NOPE:].reshape(nl, MLA_KV_LORA, MLA_HEADS // 2, 2, MLA_V)
    zv = jnp.zeros_like(wv[:, :, :, 0])
    wv = jnp.stack([jnp.concatenate([wv[:, :, :, 0], zv], axis=-1),
                    jnp.concatenate([zv, wv[:, :, :, 1]], axis=-1)], axis=3)
    wukv = jnp.concatenate([kn.reshape(nl, MLA_KV_LORA, MLA_HEADS * LANE),
                            wv.reshape(nl, MLA_KV_LORA, MLA_HEADS * LANE)], axis=2).astype(BF16)

    dq = p["dsa_qk_gain"][:, 0]
    vec = jnp.stack([_pad_lane(v, MLA_Q_LORA) for v in (
        p["mla_q_norm"], p["mla_kv_norm"], p["mla_qk_gain"][:, 0], p["mla_qk_gain"][:, 1],
        jnp.concatenate([dq, dq], axis=-1), p["dsa_qk_gain"][:, 1])], axis=1)

    s5_wt, s5_ws, s5_wo, s5_lam, s5_d = jax.vmap(_s5_weights)(
        p["s5_lambda_re"], p["s5_lambda_im"], p["s5_log_dt"], p["s5_b_re"], p["s5_b_im"],
        p["s5_c_re"], p["s5_c_im"], p["s5_d"])

    return dict(
        w_all=w_all, wuq=wuq, wukv=wukv, vec=vec,
        rot=jnp.stack([_rot_matrix(MLA_ROPE, LANE), _rot_matrix(DSA_ROT, DSA_HEAD_DIM),
                       _rot_matrix(IDX_ROT, IDX_DIM)]),
        conv_w=p["conv_w"],
        rg_vec=jnp.stack([p["conv_b"], p["rg_ba"], p["rg_bx"], p["rg_lambda"]], axis=1),
        rg_wa=jax.vmap(_block_diag)(p["rg_wa"]).astype(BF16),
        rg_wx=jax.vmap(_block_diag)(p["rg_wx"]).astype(BF16),
        s5_wt=s5_wt, s5_ws=s5_ws, s5_wo=s5_wo, s5_lam=s5_lam, s5_d=s5_d,
        w_gate=w_gate.astype(BF16),
        w_branch=p["w_branch"].astype(BF16),
        w_glu=p["s5_w_glu"].astype(BF16),
        b_glu=p["s5_b_glu"][:, None, :],
        w_out=p["w_out"].astype(BF16),
        ffn_w1=p["ffn_w1"].astype(BF16), ffn_w3=p["ffn_w3"].astype(BF16), ffn_w2=p["ffn_w2"].astype(BF16),
        norm_g=p["norm_g"][:, :, None, :],
    )


def _pick_tile(n, pref):
    t = pref
    while n % t:
        t //= 2
    return t


def kernel(x, c, positions, ada_w, ada_b, norm_g, ffn_w1, ffn_w3, ffn_w2, w_in,
           conv_w, conv_b, rg_wa, rg_ba, rg_wx, rg_bx, rg_lambda,
           mla_q_norm, mla_w_uq, mla_kv_norm, mla_w_ukv, mla_qk_gain, dsa_qk_gain,
           s5_lambda_re, s5_lambda_im, s5_log_dt, s5_b_re, s5_b_im, s5_c_re, s5_c_im,
           s5_d, s5_w_glu, s5_b_glu, w_branch, w_out):
    p = dict(norm_g=norm_g, ffn_w1=ffn_w1, ffn_w3=ffn_w3, ffn_w2=ffn_w2, w_in=w_in,
             conv_w=conv_w, conv_b=conv_b, rg_wa=rg_wa, rg_ba=rg_ba, rg_wx=rg_wx, rg_bx=rg_bx,
             rg_lambda=rg_lambda, mla_q_norm=mla_q_norm, mla_w_uq=mla_w_uq, mla_kv_norm=mla_kv_norm,
             mla_w_ukv=mla_w_ukv, mla_qk_gain=mla_qk_gain, dsa_qk_gain=dsa_qk_gain,
             s5_lambda_re=s5_lambda_re, s5_lambda_im=s5_lambda_im, s5_log_dt=s5_log_dt,
             s5_b_re=s5_b_re, s5_b_im=s5_b_im, s5_c_re=s5_c_re, s5_c_im=s5_c_im, s5_d=s5_d,
             s5_w_glu=s5_w_glu, s5_b_glu=s5_b_glu, w_branch=w_branch, w_out=w_out)
    bsz, seq, d = x.shape
    n = bsz * seq
    depth = ada_w.shape[0]
    assert seq % (S5_CHUNK * 8) == 0 and seq % _DSA_QB == 0

    tm = _pick_tile(seq, 512)
    tw = _pick_tile(seq, 1024)
    tc = _pick_tile(seq, 256)
    bq = _pick_tile(seq, 512)

    mod = _ada_call(c, ada_w, ada_b)
    pw = _prep_params(p)
    tabs = (_rope_table(positions, MLA_ROPE, LANE) + _rope_table(positions, DSA_ROT, DSA_HEAD_DIM)
            + _rope_table(positions, IDX_ROT, IDX_DIM))
    tri = (jnp.arange(LANE)[:, None] < jnp.arange(LANE)[None, :]).astype(BF16)

    xf = x.reshape(n, d)
    for l in range(depth):
        xf = _ffn_call(xf, mod, pw, l, 0, seq, tw)
        (xr, gr, qm, km, vm, qd, kd, vd, qi, ki, wi, us) = _inproj_call(xf, mod, pw, l, tabs, seq, tm)
        ya = _rglru_call(xr, gr, pw, l, bsz, seq, tc)
        yb = _flash_call(qm, km, vm, bsz, seq, bq)
        yc = _dsa_call(qd, kd, vd, qi, ki, wi, tri, bsz, seq)
        yd = _s5_call(us, pw, l, bsz, seq)
        xf = _merge_call(xf, mod, ya, yb, yc, yd, pw, l, seq, tw)
        xf = _ffn_call(xf, mod, pw, l, 1, seq, tw)
    return xf.reshape(bsz, seq, d)
```

```python
import functools
import math

import jax
import jax.numpy as jnp
from jax import lax
from jax.experimental import pallas as pl
from jax.experimental.pallas import tpu as pltpu

F32 = jnp.float32
BF16 = jnp.bfloat16

D_MODEL = 1024
EPS = 1e-6
ROPE_THETA = 500000.0
D_FF = 2816
N_ADA = 9

D_RNN = 512
RNN_HEADS = 8
RNN_HEAD_DIM = D_RNN // RNN_HEADS
CONV_WIDTH = 4
LRU_C = 8.0

MLA_HEADS = 8
MLA_NOPE = 64
MLA_ROPE = 32
MLA_V = 64
MLA_QK = MLA_ROPE + MLA_NOPE
MLA_Q_LORA = 256
MLA_KV_LORA = 128

DSA_HEADS = 8
DSA_HEAD_DIM = 64
DSA_ROT = DSA_HEAD_DIM // 4
IDX_HEADS = 8
IDX_DIM = 32
IDX_ROT = IDX_DIM // 4
TOPK_MAX = 256

S5_GROUP = 16
S5_GROUPS = 32
D_S5 = S5_GROUP * S5_GROUPS
S5_STATE = 64
S5_CHUNK = 8
S5_GB = 8
S5_NGB = S5_GROUPS // S5_GB

N_BRANCH = 4
BRANCH_W = 512
IN_SPLITS = (D_RNN, D_RNN, MLA_Q_LORA, MLA_KV_LORA, MLA_ROPE,
             DSA_HEADS * DSA_HEAD_DIM, DSA_HEAD_DIM, DSA_HEAD_DIM,
             IDX_HEADS * IDX_DIM, IDX_DIM, IDX_HEADS, D_S5, N_BRANCH * D_MODEL)

LANE = 128
SUBLANES = 8
NEG_BIG = -1e30
VMEM_LIMIT = 56 * 1024 * 1024

_C_XR = 0
_C_GR = _C_XR + D_RNN
_C_QL = _C_GR + D_RNN
_C_KVL = _C_QL + MLA_Q_LORA
_C_KPE = _C_KVL + MLA_KV_LORA
_C_QD = _C_KPE + LANE
_C_KD = _C_QD + DSA_HEADS * DSA_HEAD_DIM
_C_VD = _C_KD + LANE
_C_QI = _C_VD + LANE
_C_KI = _C_QI + IDX_HEADS * IDX_DIM
_C_WI = _C_KI + IDX_HEADS * IDX_DIM
_C_US = _C_WI + LANE
_C_END = _C_US + D_S5


def _dot(a, b):
    return jnp.dot(a, b, preferred_element_type=F32)


def _dot_nt(a, b):
    return lax.dot_general(a, b, (((1,), (1,)), ((), ())), preferred_element_type=F32)


def _sigmoid(x):
    return jax.nn.sigmoid(x)


def _gelu_tanh(x):
    return 0.5 * x * (1.0 + jnp.tanh(0.7978845608028654 * (x + 0.044715 * (x * x * x))))


def _rms_mod(x, g, shift, scale):
    ms = jnp.mean(x * x, axis=-1, keepdims=True)
    y = x * lax.rsqrt(ms + EPS) * g
    return y * (1.0 + scale) + shift


def _resident(shape):
    return pl.BlockSpec(shape, lambda *_: (0,) * len(shape), pipeline_mode=pl.Buffered(1))


def _layer(tail, *lead):
    return pl.BlockSpec((None,) * len(lead) + tuple(tail), lambda *_: tuple(lead) + (0,) * len(tail),
                        pipeline_mode=pl.Buffered(1))


def _mod_spec(l, tm, seq, d):
    return pl.BlockSpec((None, 1, N_ADA, d), lambda i: (l, i * tm // seq, 0, 0))


def _params(sem):
    return pltpu.CompilerParams(dimension_semantics=sem, vmem_limit_bytes=VMEM_LIMIT)


def _ada_kernel(c_ref, w_ref, b_ref, o_ref):
    c = c_ref[...]
    a = c * _sigmoid(c)
    w = w_ref[0]
    a_hi = a.astype(BF16)
    a_lo = (a - a_hi.astype(F32)).astype(BF16)
    w_hi = w.astype(BF16)
    w_lo = (w - w_hi.astype(F32)).astype(BF16)
    o_ref[0] = _dot(a_hi, w_hi) + _dot(a_lo, w_hi) + _dot(a_hi, w_lo) + b_ref[0]


def _ada_call(c, ada_w, ada_b):
    nl, d, n9 = ada_w.shape
    b = c.shape[0]
    tn = 1024
    out = pl.pallas_call(
        _ada_kernel,
        grid=(nl, n9 // tn),
        in_specs=[
            pl.BlockSpec((b, d), lambda l, j: (0, 0)),
            pl.BlockSpec((1, d, tn), lambda l, j: (l, 0, j)),
            pl.BlockSpec((1, 1, tn), lambda l, j: (l, 0, j)),
        ],
        out_specs=pl.BlockSpec((1, b, tn), lambda l, j: (l, 0, j)),
        out_shape=jax.ShapeDtypeStruct((nl, b, n9), F32),
        compiler_params=_params(("arbitrary", "arbitrary")),
        name="ada_mod",
    )(c, ada_w, ada_b.reshape(nl, 1, n9))
    return out.reshape(nl, b, N_ADA, d)


_FFN_CHUNK = 256


def _ffn_kernel(row0, x_ref, mod_ref, g_ref, w1_ref, w3_ref, w2_ref, o_ref, h_ref):
    x = x_ref[...]
    m = mod_ref[0]
    u = _rms_mod(x, g_ref[...], m[row0:row0 + 1], m[row0 + 1:row0 + 2]).astype(BF16)
    for c in range(0, w1_ref.shape[1], _FFN_CHUNK):
        h1 = _dot(u, w1_ref[:, c:c + _FFN_CHUNK])
        h3 = _dot(u, w3_ref[:, c:c + _FFN_CHUNK])
        h_ref[:, c:c + _FFN_CHUNK] = (h1 * _sigmoid(h1) * h3).astype(BF16)
    o_ref[...] = x + 0.5 * (1.0 + m[row0 + 2:row0 + 3]) * _dot(h_ref[...], w2_ref[...])


def _ffn_call(x, mod, pw, l, j, seq, tm):
    n, d = x.shape
    f = pw["ffn_w1"].shape[-1]
    return pl.pallas_call(
        functools.partial(_ffn_kernel, 6 * j),
        grid=(n // tm,),
        in_specs=[
            pl.BlockSpec((tm, d), lambda i: (i, 0)),
            _mod_spec(l, tm, seq, d),
            _layer((1, d), l, 2 * j),
            _layer((d, f), l, j), _layer((d, f), l, j), _layer((f, d), l, j),
        ],
        out_specs=pl.BlockSpec((tm, d), lambda i: (i, 0)),
        out_shape=jax.ShapeDtypeStruct((n, d), F32),
        scratch_shapes=[pltpu.VMEM((tm, f), BF16)],
        compiler_params=_params(("arbitrary",)),
        name="ffn",
    )(x, mod, pw["norm_g"], pw["ffn_w1"], pw["ffn_w3"], pw["ffn_w2"])


def _rope_tiles(tiles, cos, sin, rot):
    rows = tiles[0].shape[0]
    partner = _dot(jnp.concatenate(tiles, axis=0).astype(BF16), rot)
    return [t * cos + partner[i * rows:(i + 1) * rows] * sin for i, t in enumerate(tiles)]


def _inproj_kernel(x_ref, mod_ref, g_ref, w_ref, wuq_ref, wukv_ref, vec_ref, rot_ref,
                   cm_ref, sm_ref, cd_ref, sd_ref, ci_ref, si_ref,
                   xr_ref, gr_ref, qm_ref, km_ref, vm_ref, qd_ref, kd_ref, vd_ref,
                   qi_ref, ki_ref, wi_ref, us_ref):
    x = x_ref[...]
    m = mod_ref[0]
    u = _rms_mod(x, g_ref[...], m[3:4], m[4:5]).astype(BF16)
    vec = vec_ref[...]
    lane = lax.broadcasted_iota(jnp.int32, (1, LANE), 1)
    z = _dot(u, w_ref[...])

    xr_ref[...] = z[:, _C_XR:_C_GR].astype(xr_ref.dtype)
    gr_ref[...] = z[:, _C_GR:_C_QL].astype(gr_ref.dtype)
    wi_ref[...] = z[:, _C_WI:_C_US]
    us_ref[...] = z[:, _C_US:_C_END].astype(us_ref.dtype)

    ql = z[:, _C_QL:_C_KVL]
    qn = (ql * lax.rsqrt(jnp.mean(ql * ql, axis=-1, keepdims=True) + EPS)
          * vec[0:1, :MLA_Q_LORA]).astype(BF16)
    q2 = _dot(qn, wuq_ref[...])
    kvl = z[:, _C_KVL:_C_KPE]
    kvn = (kvl * lax.rsqrt(jnp.mean(kvl * kvl, axis=-1, keepdims=True) + EPS)
           * vec[1:2, :MLA_KV_LORA]).astype(BF16)
    kv = _dot(kvn, wukv_ref[...])
    kpe = z[:, _C_KPE:_C_QD]
    gq = vec[2:3, :LANE]
    gk = vec[3:4, :LANE]
    roped = _rope_tiles([kpe * gk] + [q2[:, h * LANE:(h + 1) * LANE] * gq for h in range(MLA_HEADS)],
                        cm_ref[...], sm_ref[...], rot_ref[0])
    kpe_rot = roped[0]
    kpe_ss = jnp.sum(kpe * kpe, axis=-1, keepdims=True)
    q_tiles, k_tiles = [], []
    for h in range(MLA_HEADS):
        hs = slice(h * LANE, (h + 1) * LANE)
        qs = q2[:, hs]
        s = lax.rsqrt(jnp.sum(qs * qs, axis=-1, keepdims=True) * (1.0 / MLA_QK) + EPS) * MLA_QK ** -0.5
        q_tiles.append((s * roped[1 + h]).astype(qm_ref.dtype))
        kn = kv[:, hs]
        s = lax.rsqrt((jnp.sum(kn * kn, axis=-1, keepdims=True) + kpe_ss) * (1.0 / MLA_QK) + EPS)
        k_tiles.append((s * (kn * gk + kpe_rot)).astype(km_ref.dtype))
    qm_ref[...] = jnp.concatenate(q_tiles, axis=1)
    km_ref[...] = jnp.concatenate(k_tiles, axis=1)
    nslot = MLA_HEADS * LANE
    vlane = lax.broadcasted_iota(jnp.int32, (1, nslot), 1)
    v_is_low = (vlane // LANE) % 2 == 0
    ones_half = jnp.where(((vlane % LANE) < MLA_V) == v_is_low, 0.0, 1.0)
    vm_ref[...] = (kv[:, nslot:] + ones_half).astype(vm_ref.dtype)

    gd = vec[4:5, :LANE]
    low = lane < DSA_HEAD_DIM
    kd = z[:, _C_KD:_C_VD]
    xqs = [z[:, _C_QD + t * LANE:_C_QD + (t + 1) * LANE] for t in range(DSA_HEADS // 2)]
    roped = _rope_tiles([xq * gd for xq in xqs] + [kd * vec[5:6, :LANE]], cd_ref[...], sd_ref[...], rot_ref[1])
    d_tiles = []
    for t in range(DSA_HEADS // 2):
        x2 = xqs[t] * xqs[t]
        ss_even = jnp.sum(jnp.where(low, x2, 0.0), axis=-1, keepdims=True)
        ss_odd = jnp.sum(jnp.where(low, 0.0, x2), axis=-1, keepdims=True)
        y = roped[t]
        s_even = lax.rsqrt(ss_even * (1.0 / DSA_HEAD_DIM) + EPS) * DSA_HEAD_DIM ** -0.5
        s_odd = lax.rsqrt(ss_odd * (1.0 / DSA_HEAD_DIM) + EPS) * DSA_HEAD_DIM ** -0.5
        d_tiles.append(jnp.where(low, y * s_even, 0.0).astype(qd_ref.dtype))
        d_tiles.append(jnp.where(low, pltpu.roll(y, DSA_HEAD_DIM, 1) * s_odd, 0.0).astype(qd_ref.dtype))
    qd_ref[...] = jnp.concatenate(d_tiles, axis=1)
    s = lax.rsqrt(jnp.sum(kd * kd, axis=-1, keepdims=True) * (1.0 / DSA_HEAD_DIM) + EPS)
    kd_ref[...] = (s * roped[-1]).astype(kd_ref.dtype)
    vd_ref[...] = (z[:, _C_VD:_C_QI] + jnp.where(low, 0.0, 1.0)).astype(vd_ref.dtype)

    ntile = IDX_HEADS * IDX_DIM // LANE
    roped = _rope_tiles([z[:, c + t * LANE:c + (t + 1) * LANE] for c in (_C_QI, _C_KI) for t in range(ntile)],
                        ci_ref[...], si_ref[...], rot_ref[2])
    qi_ref[...] = jnp.concatenate(roped[:ntile], axis=1).astype(qi_ref.dtype)
    ki_ref[...] = jnp.concatenate(roped[ntile:], axis=1).astype(ki_ref.dtype)


def _inproj_call(x, mod, pw, l, tabs, seq, tm):
    n, d = x.shape
    row = lambda w: pl.BlockSpec((tm, w), lambda i: (i, 0))
    out_widths = [(D_RNN, BF16), (D_RNN, BF16), (MLA_HEADS * LANE, BF16), (MLA_HEADS * LANE, BF16),
                  (MLA_HEADS * LANE, BF16), (DSA_HEADS * LANE, BF16), (LANE, BF16), (LANE, BF16),
                  (IDX_HEADS * IDX_DIM, BF16), (IDX_HEADS * IDX_DIM, BF16), (LANE, F32), (D_S5, BF16)]
    return pl.pallas_call(
        _inproj_kernel,
        grid=(n // tm,),
        in_specs=[
            row(d), _mod_spec(l, tm, seq, d), _layer((1, d), l, 1),
            _layer(pw["w_all"].shape[1:], l), _layer(pw["wuq"].shape[1:], l),
            _layer(pw["wukv"].shape[1:], l), _layer(pw["vec"].shape[1:], l),
            _resident(pw["rot"].shape),
        ] + [row(LANE)] * 6,
        out_specs=[row(w) for w, _ in out_widths],
        out_shape=[jax.ShapeDtypeStruct((n, w), dt) for w, dt in out_widths],
        compiler_params=_params(("arbitrary",)),
        name="in_proj",
    )(x, mod, pw["norm_g"], pw["w_all"], pw["wuq"], pw["wukv"], pw["vec"], pw["rot"], *tabs)


def _rglru_kernel(tc, x_ref, gate_ref, cw_ref, vec_ref, wa_ref, wx_ref, o_ref, xs_ref, h_ref):
    j = pl.program_id(1)

    @pl.when(j == 0)
    def _():
        xs_ref[0:8, :] = jnp.zeros((8, D_RNN), F32)
        h_ref[...] = jnp.zeros(h_ref.shape, F32)

    xs_ref[8:8 + tc, :] = x_ref[...].astype(F32)
    cw = cw_ref[...]
    vec = vec_ref[...]
    xc = vec[0:1]
    for k in range(CONV_WIDTH):
        xc = xc + cw[k:k + 1] * xs_ref[pl.ds(8 - (CONV_WIDTH - 1) + k, tc), :]
    xs_ref[0:8, :] = xs_ref[tc:tc + 8, :]

    xb = xc.astype(BF16)
    r = _sigmoid(_dot(xb, wa_ref[...]) + vec[1:2])
    ig = _sigmoid(_dot(xb, wx_ref[...]) + vec[2:3])
    nl = -vec[3:4]
    softplus = jnp.maximum(nl, 0.0) + jnp.log(1.0 + jnp.exp(-jnp.abs(nl)))
    log_a = (-LRU_C) * r * softplus
    a = jnp.exp(log_a)
    z = 2.0 * log_a
    series = -z * (1.0 + z * (0.5 + z * (1.0 / 6.0 + z * (1.0 / 24.0 + z * (1.0 / 120.0 + z * (1.0 / 720.0))))))
    nem1 = jnp.where(z > -0.25, series, 1.0 - jnp.exp(z))
    b = jnp.sqrt(nem1) * ig * xc

    sub = lax.broadcasted_iota(jnp.int32, (tc, D_RNN), 0) & (SUBLANES - 1)
    d = 1
    while d < SUBLANES:
        keep = sub >= d
        a_s = jnp.where(keep, pltpu.roll(a, d, 0), 1.0)
        b_s = jnp.where(keep, pltpu.roll(b, d, 0), 0.0)
        b = a * b_s + b
        a = a * a_s
        d *= 2
    carry = h_ref[...]
    groups = []
    for g in range(tc // SUBLANES):
        rows = slice(g * SUBLANES, (g + 1) * SUBLANES)
        hg = b[rows] + a[rows] * carry
        groups.append(hg)
        carry = hg[SUBLANES - 1:SUBLANES]
    h = jnp.concatenate(groups, axis=0)
    h_ref[...] = carry
    o_ref[...] = (h * _gelu_tanh(gate_ref[...].astype(F32))).astype(o_ref.dtype)


def _rglru_call(xr, gr, pw, l, bsz, seq, tc):
    n = xr.shape[0]
    nt = seq // tc
    row = pl.BlockSpec((tc, D_RNN), lambda b, j: (b * nt + j, 0))
    return pl.pallas_call(
        functools.partial(_rglru_kernel, tc),
        grid=(bsz, nt),
        in_specs=[row, row,
                  _layer((CONV_WIDTH, D_RNN), l), _layer((4, D_RNN), l),
                  _layer((D_RNN, D_RNN), l), _layer((D_RNN, D_RNN), l)],
        out_specs=row,
        out_shape=jax.ShapeDtypeStruct((n, D_RNN), BF16),
        scratch_shapes=[pltpu.VMEM((tc + 8, D_RNN), F32), pltpu.VMEM((1, D_RNN), F32)],
        compiler_params=_params(("arbitrary", "arbitrary")),
        name="rglru",
    )(xr, gr, pw["conv_w"], pw["rg_vec"], pw["rg_wa"], pw["rg_wx"])


def _flash_kernel(q_ref, k_ref, v_ref, o_ref, m_ref, acc_ref):
    i = pl.program_id(1)
    bq = q_ref.shape[0]
    bk = bq

    m_ref[...] = jnp.full(m_ref.shape, NEG_BIG, F32)
    acc_ref[...] = jnp.zeros(acc_ref.shape, F32)

    def step(j, masked):
        rows = pl.ds(pl.multiple_of(j * bk, bk), bk)
        if masked:
            causal = (lax.broadcasted_iota(jnp.int32, (bq, bk), 1)
                      <= lax.broadcasted_iota(jnp.int32, (bq, bk), 0))
        scores = [_dot_nt(q_ref[:, h * LANE:(h + 1) * LANE], k_ref[rows, h * LANE:(h + 1) * LANE])
                  for h in range(MLA_HEADS)]
        m_out, alphas, probs = [], [], []
        for h in range(MLA_HEADS):
            s = scores[h]
            if masked:
                s = jnp.where(causal, s, NEG_BIG)
            m_prev = m_ref[h]
            m_new = jnp.maximum(m_prev, jnp.max(s, axis=1, keepdims=True))
            probs.append(jnp.exp(s - jnp.concatenate([m_new] * (bk // LANE), axis=1)).astype(BF16))
            alphas.append(jnp.exp(m_prev - m_new))
            m_out.append(m_new)
        m_ref[...] = jnp.stack(m_out)
        acc_ref[...] = jnp.stack([alphas[h] * acc_ref[h] + _dot(probs[h], v_ref[rows, h * LANE:(h + 1) * LANE])
                                  for h in range(MLA_HEADS)])

    def below_diagonal(j, carry):
        step(j, False)
        return carry

    lax.fori_loop(0, i, below_diagonal, 0)
    step(i, True)
    lane = lax.broadcasted_iota(jnp.int32, (bq, LANE), 1)
    outs = []
    for pr in range(MLA_HEADS // 2):
        even = acc_ref[2 * pr]
        odd = acc_ref[2 * pr + 1]
        outs.append(jnp.where(lane < MLA_V, even / pltpu.roll(even, MLA_V, 1),
                              odd / pltpu.roll(odd, MLA_V, 1)))
    o_ref[...] = jnp.concatenate(outs, axis=1).astype(o_ref.dtype)


def _flash_call(q, k, v, bsz, seq, bq):
    n = q.shape[0]
    nq = seq // bq
    whole = pl.BlockSpec((seq, MLA_HEADS * LANE), lambda b, i: (b, 0))
    return pl.pallas_call(
        _flash_kernel,
        grid=(bsz, nq),
        in_specs=[pl.BlockSpec((bq, MLA_HEADS * LANE), lambda b, i: (b * nq + i, 0)), whole, whole],
        out_specs=pl.BlockSpec((bq, MLA_HEADS * MLA_V), lambda b, i: (b * nq + i, 0)),
        out_shape=jax.ShapeDtypeStruct((n, MLA_HEADS * MLA_V), BF16),
        scratch_shapes=[pltpu.VMEM((MLA_HEADS, bq, LANE), F32), pltpu.VMEM((MLA_HEADS, bq, LANE), F32)],
        compiler_params=_params(("arbitrary",) * 2),
        name="mla_flash",
    )(q, k, v)


_DSA_QB = 256
_DSA_HG = 2
_DSA_CHAINS = 8
_INT_MIN = -2 ** 31


def _dsa_body(ns, topk, i, q_ref, k_ref, v_ref, qi_ref, ki_ref, w_ref, tri_ref, o_ref):
    qb = _DSA_QB
    hg = _DSA_HG
    qi = qi_ref[...]
    ki = ki_ref[0:ns, :]
    w = w_ref[...]
    head_of_lane = jnp.right_shift(lax.broadcasted_iota(jnp.int32, qi.shape, 1), int(math.log2(IDX_DIM)))
    zero = jnp.zeros_like(qi)
    rels = [_dot_nt(jnp.concatenate([jnp.where(head_of_lane == h, qi, zero) for h in range(g, g + hg)], axis=0), ki)
            for g in range(0, IDX_HEADS, hg)]
    score = jnp.zeros((qb, ns), F32)
    for g, rel in enumerate(rels):
        for t in range(hg):
            score = score + w[:, g * hg + t:g * hg + t + 1] * jnp.maximum(rel[t * qb:(t + 1) * qb], 0.0)

    qpos = lax.broadcasted_iota(jnp.int32, (qb, ns), 0) + i * qb
    kpos = lax.broadcasted_iota(jnp.int32, (qb, ns), 1)
    bits = pltpu.bitcast(score + 0.0, jnp.int32)
    key = jnp.where(bits < 0, bits ^ jnp.int32(0x7FFFFFFF), bits)
    key = jnp.where(kpos <= qpos, key, jnp.int32(_INT_MIN))
    kk = jnp.minimum(lax.broadcasted_iota(jnp.int32, (qb, 1), 0) + (i * qb + 1), topk).astype(F32)

    rows = qb // _DSA_CHAINS
    keys = [key[r * rows:(r + 1) * rows] for r in range(_DSA_CHAINS)]
    kks = [kk[r * rows:(r + 1) * rows] for r in range(_DSA_CHAINS)]

    def body(it, thrs):
        bit = lax.shift_left(jnp.int32(1), jnp.int32(31) - it)
        out = []
        for kr, kkr, t in zip(keys, kks, thrs):
            c = t + bit
            cnt = jnp.sum(jnp.where(kr >= c, 1.0, 0.0), axis=1, keepdims=True)
            out.append(jnp.where(cnt >= kkr, c, t))
        return tuple(out)

    thrs = lax.fori_loop(0, 32, body, tuple(jnp.full((rows, 1), _INT_MIN, jnp.int32) for _ in keys), unroll=2)
    thr = jnp.concatenate(thrs, axis=0)

    nchunk = ns // LANE
    need = kk - jnp.sum(jnp.where(key > thr, 1.0, 0.0), axis=1, keepdims=True)
    eqs = [jnp.where(key[:, c * LANE:(c + 1) * LANE] == thr, 1.0, 0.0) for c in range(nchunk)]
    before_all = _dot(jnp.concatenate(eqs, axis=0).astype(BF16), tri_ref[...])
    run = jnp.zeros((qb, 1), F32)
    bias_chunks = []
    for c in range(nchunk):
        before = before_all[c * qb:(c + 1) * qb] + run
        take = jnp.where(key[:, c * LANE:(c + 1) * LANE] > thr, 1.0, jnp.where(before < need, eqs[c], 0.0))
        bias_chunks.append(jnp.where(take > 0.5, 0.0, NEG_BIG))
        run = run + jnp.sum(eqs[c], axis=1, keepdims=True)
    bias = jnp.concatenate(bias_chunks, axis=1)

    k = k_ref[0:ns, :]
    v = v_ref[0:ns, :]
    lane = lax.broadcasted_iota(jnp.int32, (qb, LANE), 1)
    scores = [_dot_nt(jnp.concatenate([q_ref[:, h * LANE:(h + 1) * LANE] for h in range(g, g + hg)], axis=0), k)
              for g in range(0, DSA_HEADS, hg)]
    out_tiles = []
    for s in scores:
        s = s.reshape(hg, qb, ns) + bias[None]
        p = jnp.exp(s - jnp.max(s, axis=2, keepdims=True))
        o = _dot(p.reshape(hg * qb, ns).astype(BF16), v)
        for t in range(0, hg, 2):
            even = o[t * qb:(t + 1) * qb]
            odd = o[(t + 1) * qb:(t + 2) * qb]
            out_tiles.append(jnp.where(lane < DSA_HEAD_DIM, even / pltpu.roll(even, DSA_HEAD_DIM, 1),
                                       pltpu.roll(odd, DSA_HEAD_DIM, 1) / odd))
    o_ref[...] = jnp.concatenate(out_tiles, axis=1).astype(o_ref.dtype)


def _dsa_kernel(seq, bucket, topk, q_ref, k_ref, v_ref, qi_ref, ki_ref, w_ref, tri_ref, o_ref):
    i = pl.program_id(1)
    for bk in range(seq // bucket):
        @pl.when((i * _DSA_QB) // bucket == bk)
        def _(bk=bk):
            _dsa_body((bk + 1) * bucket, topk, i, q_ref, k_ref, v_ref, qi_ref, ki_ref, w_ref,
                      tri_ref, o_ref)


def _dsa_call(qd, kd, vd, qi, ki, wi, tri, bsz, seq):
    n = qd.shape[0]
    qb = _DSA_QB
    nq = seq // qb
    bucket = min(512, seq)
    topk = min(TOPK_MAX, seq // 4)
    qrow = lambda w: pl.BlockSpec((qb, w), lambda b, i: (b * nq + i, 0))
    full = lambda w: pl.BlockSpec((seq, w), lambda b, i: (b, 0))
    return pl.pallas_call(
        functools.partial(_dsa_kernel, seq, bucket, topk),
        grid=(bsz, nq),
        in_specs=[qrow(DSA_HEADS * LANE), full(LANE), full(LANE),
                  qrow(IDX_HEADS * IDX_DIM), full(IDX_HEADS * IDX_DIM), qrow(LANE),
                  _resident((LANE, LANE))],
        out_specs=qrow(DSA_HEADS * DSA_HEAD_DIM),
        out_shape=jax.ShapeDtypeStruct((n, DSA_HEADS * DSA_HEAD_DIM), BF16),
        compiler_params=_params(("arbitrary", "arbitrary")),
        name="dsa",
    )(qd, kd, vd, qi, ki, wi, tri)


def _s5_kernel(rows, u_ref, wt_ref, ws_ref, wo_ref, lam_ref, d_ref, o_ref, uf_ref, yf_ref):
    half = S5_GB * S5_STATE
    rowi = lax.broadcasted_iota(jnp.int32, (rows, half), 0)
    sub = rowi & (SUBLANES - 1)
    for gb in range(S5_NGB):
        gs = slice(gb * LANE, (gb + 1) * LANE)
        uf_ref[gb] = u_ref[:, gs].astype(F32)
        uf = jnp.concatenate([uf_ref[gb, pl.ds(a, rows, stride=S5_CHUNK), :] for a in range(S5_CHUNK)], axis=1)
        u = uf.astype(BF16)
        y = _dot(u, wt_ref[gb])
        st = _dot(u, ws_ref[gb])
        xr = st[:, :half]
        xi = st[:, half:]
        lam = lam_ref[gb]
        lr = lam[0:1]
        li = lam[1:2]
        pw_r, pw_i = [lr], [li]
        for _ in range(SUBLANES - 1):
            pw_r, pw_i = pw_r + [pw_r[-1] * lr - pw_i[-1] * li], pw_i + [pw_r[-1] * li + pw_i[-1] * lr]
        pw_r = jnp.concatenate(pw_r, axis=0)
        pw_i = jnp.concatenate(pw_i, axis=0)
        d = 1
        while d < SUBLANES:
            keep = sub >= d
            sr = jnp.where(keep, pltpu.roll(xr, d, 0), 0.0)
            si = jnp.where(keep, pltpu.roll(xi, d, 0), 0.0)
            xr, xi = xr + lr * sr - li * si, xi + lr * si + li * sr
            lr, li = lr * lr - li * li, 2.0 * lr * li
            d *= 2
        cr = jnp.zeros((1, half), F32)
        ci = jnp.zeros((1, half), F32)
        gr, gi = [], []
        for g in range(rows // SUBLANES):
            rs = slice(g * SUBLANES, (g + 1) * SUBLANES)
            hr = xr[rs] + pw_r * cr - pw_i * ci
            hi = xi[rs] + pw_r * ci + pw_i * cr
            gr.append(hr)
            gi.append(hi)
            cr = hr[SUBLANES - 1:SUBLANES]
            ci = hi[SUBLANES - 1:SUBLANES]
        xr = jnp.concatenate(gr, axis=0)
        xi = jnp.concatenate(gi, axis=0)
        keep = rowi >= 1
        pr = jnp.where(keep, pltpu.roll(xr, 1, 0), 0.0)
        pi = jnp.where(keep, pltpu.roll(xi, 1, 0), 0.0)
        xp = jnp.concatenate([pr, pi], axis=1).astype(BF16)
        yg = _gelu_tanh(y + _dot(xp, wo_ref[gb]) + d_ref[gb] * uf)
        for a in range(S5_CHUNK):
            yf_ref[gb, pl.ds(a, rows, stride=S5_CHUNK), :] = yg[:, a * LANE:(a + 1) * LANE]
        o_ref[:, gs] = yf_ref[gb].astype(o_ref.dtype)


def _s5_call(us, pw, l, bsz, seq):
    n = us.shape[0]
    blk = pl.BlockSpec((seq, D_S5), lambda b: (b, 0))
    wshape = pw["s5_wt"].shape[1:]
    return pl.pallas_call(
        functools.partial(_s5_kernel, seq // S5_CHUNK),
        grid=(bsz,),
        in_specs=[blk, _layer(wshape, l), _layer(wshape, l), _layer(wshape, l),
                  _layer(pw["s5_lam"].shape[1:], l), _layer(pw["s5_d"].shape[1:], l)],
        out_specs=blk,
        out_shape=jax.ShapeDtypeStruct((n, D_S5), BF16),
        scratch_shapes=[pltpu.VMEM((S5_NGB, seq, LANE), F32), pltpu.VMEM((S5_NGB, seq, LANE), F32)],
        compiler_params=_params(("arbitrary",)),
        name="s5",
    )(us, pw["s5_wt"], pw["s5_ws"], pw["s5_wo"], pw["s5_lam"], pw["s5_d"])


def _merge_kernel(x_ref, mod_ref, g_ref, ya_ref, yb_ref, yc_ref, yd_ref,
                  wg_ref, wb_ref, wglu_ref, bglu_ref, wout_ref, o_ref):
    x = x_ref[...]
    m = mod_ref[0]
    d = x.shape[1]
    u = _rms_mod(x, g_ref[...], m[3:4], m[4:5]).astype(BF16)
    yd = yd_ref[...]
    ydg = (yd.astype(F32) * _sigmoid(_dot(yd, wglu_ref[...]) + bglu_ref[...])).astype(BF16)
    ys = (ya_ref[...], yb_ref[...], yc_ref[...], ydg)
    merged = jnp.zeros(x.shape, F32)
    for nb in range(N_BRANCH):
        gate = _sigmoid(_dot(u, wg_ref[:, nb * d:(nb + 1) * d]))
        merged = merged + gate * _dot(ys[nb], wb_ref[nb])
    o_ref[...] = x + (1.0 + m[5:6]) * _dot(merged.astype(BF16), wout_ref[...])


def _merge_call(x, mod, ya, yb, yc, yd, pw, l, seq, tm):
    n, d = x.shape
    row = lambda w: pl.BlockSpec((tm, w), lambda i: (i, 0))
    return pl.pallas_call(
        _merge_kernel,
        grid=(n // tm,),
        in_specs=[row(d), _mod_spec(l, tm, seq, d), _layer((1, d), l, 1),
                  row(BRANCH_W), row(BRANCH_W), row(BRANCH_W), row(BRANCH_W),
                  _layer((d, N_BRANCH * d), l), _layer((N_BRANCH, BRANCH_W, d), l),
                  _layer((D_S5, D_S5), l), _layer((1, D_S5), l), _layer((d, d), l)],
        out_specs=row(d),
        out_shape=jax.ShapeDtypeStruct((n, d), F32),
        compiler_params=_params(("arbitrary",)),
        name="merge",
    )(x, mod, pw["norm_g"], ya, yb, yc, yd, pw["w_gate"], pw["w_branch"], pw["w_glu"], pw["b_glu"], pw["w_out"])


def _to_slots(w, nheads, hdim):
    lead = w.shape[:-1]
    wh = w.reshape(lead + (nheads, hdim))
    wh = jnp.pad(wh, [(0, 0)] * (len(lead) + 1) + [(0, LANE - hdim)])
    return wh.reshape(lead + (nheads * LANE,))


def _pad_lane(v, width=LANE):
    return jnp.pad(v, [(0, 0)] * (v.ndim - 1) + [(0, width - v.shape[-1])])


def _rope_table(positions, rot, period):
    j = jnp.arange(LANE) % period
    inv = jnp.where(j < rot, ROPE_THETA ** (-(2 * (j % (rot // 2))).astype(F32) / rot), 0.0)
    ang = positions.astype(F32).reshape(-1, 1) * inv[None, :]
    return jnp.cos(ang), jnp.sin(ang)


def _rot_matrix(rot, period):
    src = jnp.arange(LANE)[:, None]
    dst = jnp.arange(LANE)[None, :]
    j = dst % period
    half = rot // 2
    return (jnp.where((j < half) & (src == dst + half), -1.0, 0.0)
            + jnp.where((j >= half) & (j < rot) & (src == dst - half), 1.0, 0.0)).astype(BF16)


def _s5_weights(lam_re, lam_im, log_dt, b_re, b_im, c_re, c_im, dvec):
    hp = lax.Precision.HIGHEST
    g, p = lam_re.shape
    ch = S5_CHUNK
    dt = jnp.exp(log_dt)[:, None]
    mag = jnp.exp(lam_re * dt)
    ar, ai = mag * jnp.cos(lam_im * dt), mag * jnp.sin(lam_im * dt)
    den = lam_re * lam_re + lam_im * lam_im
    nr, ni = ar - 1.0, ai
    f_re = (nr * lam_re + ni * lam_im) / den
    f_im = (ni * lam_re - nr * lam_im) / den
    bb_re = f_re[..., None] * b_re - f_im[..., None] * b_im
    bb_im = f_re[..., None] * b_im + f_im[..., None] * b_re
    pr, pi = [jnp.ones_like(ar)], [jnp.zeros_like(ar)]
    for _ in range(ch):
        pr, pi = pr + [pr[-1] * ar - pi[-1] * ai], pi + [pr[-1] * ai + pi[-1] * ar]
    pr, pi = jnp.stack(pr), jnp.stack(pi)
    mr = pr[..., None] * bb_re - pi[..., None] * bb_im
    mi = pr[..., None] * bb_im + pi[..., None] * bb_re
    kern = (jnp.einsum('gjp,tgpi->tgji', c_re, mr, precision=hp)
            - jnp.einsum('gjp,tgpi->tgji', c_im, mi, precision=hp))

    def group_diag(x):
        r, c = x.shape[-2:]
        x = x.reshape(x.shape[:-3] + (S5_NGB, S5_GB * r, c))
        on_diag = jnp.arange(S5_GB * r)[:, None] // r == jnp.arange(S5_GB * c)[None, :] // c
        return jnp.tile(x, (1,) * (x.ndim - 1) + (S5_GB,)) * on_diag.astype(x.dtype)

    a_in = jnp.arange(ch)[:, None]
    a_out = jnp.arange(ch)[None, :]
    lag = jnp.clip(a_out - a_in, 0, ch)
    bd_k = group_diag(kern.transpose(0, 1, 3, 2)).astype(BF16)
    toe = bd_k[lag] * (a_out >= a_in)[..., None, None, None].astype(BF16)
    w_toe = toe.transpose(2, 0, 3, 1, 4).reshape(S5_NGB, ch * LANE, ch * LANE)
    rev = jnp.arange(ch - 1, -1, -1)
    st = jnp.concatenate([group_diag(mr[:ch].transpose(0, 1, 3, 2)), group_diag(mi[:ch].transpose(0, 1, 3, 2))],
                         axis=-1).astype(BF16)
    w_st = st[rev].transpose(1, 0, 2, 3).reshape(S5_NGB, ch * LANE, 2 * S5_GB * p)
    pr1, pi1 = pr[1:], pi[1:]
    co_re = c_re[None] * pr1[:, :, None, :] - c_im[None] * pi1[:, :, None, :]
    co_im = -(c_re[None] * pi1[:, :, None, :] + c_im[None] * pr1[:, :, None, :])
    co = jnp.stack([group_diag(co_re.transpose(0, 1, 3, 2)), group_diag(co_im.transpose(0, 1, 3, 2))],
                   axis=2).astype(BF16)
    w_out = co.reshape(ch, S5_NGB, 2 * S5_GB * p, LANE).transpose(1, 2, 0, 3).reshape(
        S5_NGB, 2 * S5_GB * p, ch * LANE)
    lam_c = jnp.stack([pr[ch].reshape(S5_NGB, S5_GB * p), pi[ch].reshape(S5_NGB, S5_GB * p)], axis=1)
    d_t = jnp.tile(dvec.reshape(S5_NGB, 1, LANE), (1, 1, ch))
    return w_toe, w_st, w_out, lam_c, d_t


def _block_diag(w):
    h, a, b = w.shape
    eye = jnp.eye(h, dtype=w.dtype)
    return jnp.einsum('hij,hg->higj', w, eye).reshape(h * a, h * b)


def _prep_params(p):
    w_in = p["w_in"]
    nl = w_in.shape[0]
    offs = [0]
    for s in IN_SPLITS:
        offs.append(offs[-1] + s)
    seg = [w_in[:, :, offs[k]:offs[k + 1]] for k in range(len(IN_SPLITS))]
    (w_xr, w_gr, w_ql, w_kvl, w_kpe, w_qd, w_kd, w_vd, w_qi, w_ki, w_wi, w_us, w_gate) = seg
    w_all = jnp.concatenate([w_xr, w_gr, w_ql, w_kvl, _pad_lane(w_kpe), w_qd, _pad_lane(w_kd), _pad_lane(w_vd),
                             w_qi, jnp.tile(w_ki, (1, 1, IDX_HEADS)), _pad_lane(w_wi), w_us],
                            axis=2).astype(BF16)
    assert w_all.shape[2] == _C_END

    wuq = _to_slots(p["mla_w_uq"], MLA_HEADS, MLA_QK).astype(BF16)
    wkv = p["mla_w_ukv"].reshape(nl, MLA_KV_LORA, MLA_HEADS, MLA_NOPE + MLA_V)
    kn = jnp.pad(wkv[..., :MLA_NOPE], ((0, 0), (0, 0), (0, 0), (MLA_ROPE, LANE - MLA_QK)))
    wv = wkv[..., MLA_NOPE:].reshape(nl, MLA_KV_LORA, MLA_HEADS // 2, 2, MLA_V)
    zv = jnp.zeros_like(wv[:, :, :, 0])
    wv = jnp.stack([jnp.concatenate([wv[:, :, :, 0], zv], axis=-1),
                    jnp.concatenate([zv, wv[:, :, :, 1]], axis=-1)], axis=3)
    wukv = jnp.concatenate([kn.reshape(nl, MLA_KV_LORA, MLA_HEADS * LANE),
                            wv.reshape(nl, MLA_KV_LORA, MLA_HEADS * LANE)], axis=2).astype(BF16)

    dq = p["dsa_qk_gain"][:, 0]
    vec = jnp.stack([_pad_lane(v, MLA_Q_LORA) for v in (
        p["mla_q_norm"], p["mla_kv_norm"], p["mla_qk_gain"][:, 0], p["mla_qk_gain"][:, 1],
        jnp.concatenate([dq, dq], axis=-1), p["dsa_qk_gain"][:, 1])], axis=1)

    s5_wt, s5_ws, s5_wo, s5_lam, s5_d = jax.vmap(_s5_weights)(
        p["s5_lambda_re"], p["s5_lambda_im"], p["s5_log_dt"], p["s5_b_re"], p["s5_b_im"],
        p["s5_c_re"], p["s5_c_im"], p["s5_d"])

    return dict(
        w_all=w_all, wuq=wuq, wukv=wukv, vec=vec,
        rot=jnp.stack([_rot_matrix(MLA_ROPE, LANE), _rot_matrix(DSA_ROT, DSA_HEAD_DIM),
                       _rot_matrix(IDX_ROT, IDX_DIM)]),
        conv_w=p["conv_w"],
        rg_vec=jnp.stack([p["conv_b"], p["rg_ba"], p["rg_bx"], p["rg_lambda"]], axis=1),
        rg_wa=jax.vmap(_block_diag)(p["rg_wa"]).astype(BF16),
        rg_wx=jax.vmap(_block_diag)(p["rg_wx"]).astype(BF16),
        s5_wt=s5_wt, s5_ws=s5_ws, s5_wo=s5_wo, s5_lam=s5_lam, s5_d=s5_d,
        w_gate=w_gate.astype(BF16),
        w_branch=p["w_branch"].astype(BF16),
        w_glu=p["s5_w_glu"].astype(BF16),
        b_glu=p["s5_b_glu"][:, None, :],
        w_out=p["w_out"].astype(BF16),
        ffn_w1=p["ffn_w1"].astype(BF16), ffn_w3=p["ffn_w3"].astype(BF16), ffn_w2=p["ffn_w2"].astype(BF16),
        norm_g=p["norm_g"][:, :, None, :],
    )


def _pick_tile(n, pref):
    t = pref
    while n % t:
        t //= 2
    return t


def kernel(x, c, positions, ada_w, ada_b, norm_g, ffn_w1, ffn_w3, ffn_w2, w_in,
           conv_w, conv_b, rg_wa, rg_ba, rg_wx, rg_bx, rg_lambda,
           mla_q_norm, mla_w_uq, mla_kv_norm, mla_w_ukv, mla_qk_gain, dsa_qk_gain,
           s5_lambda_re, s5_lambda_im, s5_log_dt, s5_b_re, s5_b_im, s5_c_re, s5_c_im,
           s5_d, s5_w_glu, s5_b_glu, w_branch, w_out):
    p = dict(norm_g=norm_g, ffn_w1=ffn_w1, ffn_w3=ffn_w3, ffn_w2=ffn_w2, w_in=w_in,
             conv_w=conv_w, conv_b=conv_b, rg_wa=rg_wa, rg_ba=rg_ba, rg_wx=rg_wx, rg_bx=rg_bx,
             rg_lambda=rg_lambda, mla_q_norm=mla_q_norm, mla_w_uq=mla_w_uq, mla_kv_norm=mla_kv_norm,
             mla_w_ukv=mla_w_ukv, mla_qk_gain=mla_qk_gain, dsa_qk_gain=dsa_qk_gain,
             s5_lambda_re=s5_lambda_re, s5_lambda_im=s5_lambda_im, s5_log_dt=s5_log_dt,
             s5_b_re=s5_b_re, s5_b_im=s5_b_im, s5_c_re=s5_c_re, s5_c_im=s5_c_im, s5_d=s5_d,
             s5_w_glu=s5_w_glu, s5_b_glu=s5_b_glu, w_branch=w_branch, w_out=w_out)
    bsz, seq, d = x.shape
    n = bsz * seq
    depth = ada_w.shape[0]
    assert seq % (S5_CHUNK * 8) == 0 and seq % _DSA_QB == 0

    tm = _pick_tile(seq, 512)
    tw = _pick_tile(seq, 1024)
    tc = _pick_tile(seq, 256)
    bq = _pick_tile(seq, 512)

    mod = _ada_call(c, ada_w, ada_b)
    pw = _prep_params(p)
    tabs = (_rope_table(positions, MLA_ROPE, LANE) + _rope_table(positions, DSA_ROT, DSA_HEAD_DIM)
            + _rope_table(positions, IDX_ROT, IDX_DIM))
    tri = (jnp.arange(LANE)[:, None] < jnp.arange(LANE)[None, :]).astype(BF16)

    xf = x.reshape(n, d)
    for l in range(depth):
        xf = _ffn_call(xf, mod, pw, l, 0, seq, tw)
        (xr, gr, qm, km, vm, qd, kd, vd, qi, ki, wi, us) = _inproj_call(xf, mod, pw, l, tabs, seq, tm)
        ya = _rglru_call(xr, gr, pw, l, bsz, seq, tc)
        yb = _flash_call(qm, km, vm, bsz, seq, bq)
        yc = _dsa_call(qd, kd, vd, qi, ki, wi, tri, bsz, seq)
        yd = _s5_call(us, pw, l, bsz, seq)
        xf = _merge_call(xf, mod, ya, yb, yc, yd, pw, l, seq, tw)
        xf = _ffn_call(xf, mod, pw, l, 1, seq, tw)
    return xf.reshape(bsz, seq, d)
```

```python
import functools
import math

import jax
import jax.numpy as jnp
from jax import lax
from jax.experimental import pallas as pl
from jax.experimental.pallas import tpu as pltpu

F32 = jnp.float32
BF16 = jnp.bfloat16

D_MODEL = 1024
EPS = 1e-6
ROPE_THETA = 500000.0
D_FF = 2816
N_ADA = 9

D_RNN = 512
RNN_HEADS = 8
RNN_HEAD_DIM = D_RNN // RNN_HEADS
CONV_WIDTH = 4
LRU_C = 8.0

MLA_HEADS = 8
MLA_NOPE = 64
MLA_ROPE = 32
MLA_V = 64
MLA_QK = MLA_ROPE + MLA_NOPE
MLA_Q_LORA = 256
MLA_KV_LORA = 128

DSA_HEADS = 8
DSA_HEAD_DIM = 64
DSA_ROT = DSA_HEAD_DIM // 4
IDX_HEADS = 8
IDX_DIM = 32
IDX_ROT = IDX_DIM // 4
TOPK_MAX = 256

S5_GROUP = 16
S5_GROUPS = 32
D_S5 = S5_GROUP * S5_GROUPS
S5_STATE = 64
S5_CHUNK = 8
S5_GB = 8
S5_NGB = S5_GROUPS // S5_GB

N_BRANCH = 4
BRANCH_W = 512
IN_SPLITS = (D_RNN, D_RNN, MLA_Q_LORA, MLA_KV_LORA, MLA_ROPE,
             DSA_HEADS * DSA_HEAD_DIM, DSA_HEAD_DIM, DSA_HEAD_DIM,
             IDX_HEADS * IDX_DIM, IDX_DIM, IDX_HEADS, D_S5, N_BRANCH * D_MODEL)

LANE = 128
SUBLANES = 8
NEG_BIG = -1e30
VMEM_LIMIT = 56 * 1024 * 1024

_C_XR = 0
_C_GR = _C_XR + D_RNN
_C_QL = _C_GR + D_RNN
_C_KVL = _C_QL + MLA_Q_LORA
_C_KPE = _C_KVL + MLA_KV_LORA
_C_QD = _C_KPE + LANE
_C_KD = _C_QD + DSA_HEADS * DSA_HEAD_DIM
_C_VD = _C_KD + LANE
_C_QI = _C_VD + LANE
_C_KI = _C_QI + IDX_HEADS * IDX_DIM
_C_WI = _C_KI + IDX_HEADS * IDX_DIM
_C_US = _C_WI + LANE
_C_END = _C_US + D_S5


def _dot(a, b):
    return jnp.dot(a, b, preferred_element_type=F32)


def _dot_nt(a, b):
    return lax.dot_general(a, b, (((1,), (1,)), ((), ())), preferred_element_type=F32)


def _sigmoid(x):
    return jax.nn.sigmoid(x)


def _gelu_tanh(x):
    return 0.5 * x * (1.0 + jnp.tanh(0.7978845608028654 * (x + 0.044715 * (x * x * x))))


def _rms_mod(x, g, shift, scale):
    ms = jnp.mean(x * x, axis=-1, keepdims=True)
    y = x * lax.rsqrt(ms + EPS) * g
    return y * (1.0 + scale) + shift


def _resident(shape):
    return pl.BlockSpec(shape, lambda *_: (0,) * len(shape), pipeline_mode=pl.Buffered(1))


def _layer(tail, *lead):
    return pl.BlockSpec((None,) * len(lead) + tuple(tail), lambda *_: tuple(lead) + (0,) * len(tail),
                        pipeline_mode=pl.Buffered(1))


def _mod_spec(l, tm, seq, d):
    return pl.BlockSpec((None, 1, N_ADA, d), lambda i: (l, i * tm // seq, 0, 0))


def _params(sem):
    return pltpu.CompilerParams(dimension_semantics=sem, vmem_limit_bytes=VMEM_LIMIT)


def _ada_kernel(c_ref, w_ref, b_ref, o_ref):
    c = c_ref[...]
    a = c * _sigmoid(c)
    w = w_ref[0]
    a_hi = a.astype(BF16)
    a_lo = (a - a_hi.astype(F32)).astype(BF16)
    w_hi = w.astype(BF16)
    w_lo = (w - w_hi.astype(F32)).astype(BF16)
    o_ref[0] = _dot(a_hi, w_hi) + _dot(a_lo, w_hi) + _dot(a_hi, w_lo) + b_ref[0]


def _ada_call(c, ada_w, ada_b):
    nl, d, n9 = ada_w.shape
    b = c.shape[0]
    tn = 1024
    out = pl.pallas_call(
        _ada_kernel,
        grid=(nl, n9 // tn),
        in_specs=[
            pl.BlockSpec((b, d), lambda l, j: (0, 0)),
            pl.BlockSpec((1, d, tn), lambda l, j: (l, 0, j)),
            pl.BlockSpec((1, 1, tn), lambda l, j: (l, 0, j)),
        ],
        out_specs=pl.BlockSpec((1, b, tn), lambda l, j: (l, 0, j)),
        out_shape=jax.ShapeDtypeStruct((nl, b, n9), F32),
        compiler_params=_params(("arbitrary", "arbitrary")),
        name="ada_mod",
    )(c, ada_w, ada_b.reshape(nl, 1, n9))
    return out.reshape(nl, b, N_ADA, d)


_FFN_CHUNK = 256


def _ffn_kernel(row0, x_ref, mod_ref, g_ref, w1_ref, w3_ref, w2_ref, o_ref, h_ref):
    x = x_ref[...]
    m = mod_ref[0]
    u = _rms_mod(x, g_ref[...], m[row0:row0 + 1], m[row0 + 1:row0 + 2]).astype(BF16)
    for c in range(0, w1_ref.shape[1], _FFN_CHUNK):
        h1 = _dot(u, w1_ref[:, c:c + _FFN_CHUNK])
        h3 = _dot(u, w3_ref[:, c:c + _FFN_CHUNK])
        h_ref[:, c:c + _FFN_CHUNK] = (h1 * _sigmoid(h1) * h3).astype(BF16)
    o_ref[...] = x + 0.5 * (1.0 + m[row0 + 2:row0 + 3]) * _dot(h_ref[...], w2_ref[...])


def _ffn_call(x, mod, pw, l, j, seq, tm):
    n, d = x.shape
    f = pw["ffn_w1"].shape[-1]
    return pl.pallas_call(
        functools.partial(_ffn_kernel, 6 * j),
        grid=(n // tm,),
        in_specs=[
            pl.BlockSpec((tm, d), lambda i: (i, 0)),
            _mod_spec(l, tm, seq, d),
            _layer((1, d), l, 2 * j),
            _layer((d, f), l, j), _layer((d, f), l, j), _layer((f, d), l, j),
        ],
        out_specs=pl.BlockSpec((tm, d), lambda i: (i, 0)),
        out_shape=jax.ShapeDtypeStruct((n, d), F32),
        scratch_shapes=[pltpu.VMEM((tm, f), BF16)],
        compiler_params=_params(("arbitrary",)),
        name="ffn",
    )(x, mod, pw["norm_g"], pw["ffn_w1"], pw["ffn_w3"], pw["ffn_w2"])


def _rope_tiles(tiles, cos, sin, rot):
    rows = tiles[0].shape[0]
    partner = _dot(jnp.concatenate(tiles, axis=0).astype(BF16), rot)
    return [t * cos + partner[i * rows:(i + 1) * rows] * sin for i, t in enumerate(tiles)]


def _inproj_kernel(x_ref, mod_ref, g_ref, w_ref, wuq_ref, wukv_ref, vec_ref, rot_ref,
                   cm_ref, sm_ref, cd_ref, sd_ref, ci_ref, si_ref,
                   xr_ref, gr_ref, qm_ref, km_ref, vm_ref, qd_ref, kd_ref, vd_ref,
                   qi_ref, ki_ref, wi_ref, us_ref):
    x = x_ref[...]
    m = mod_ref[0]
    u = _rms_mod(x, g_ref[...], m[3:4], m[4:5]).astype(BF16)
    vec = vec_ref[...]
    lane = lax.broadcasted_iota(jnp.int32, (1, LANE), 1)
    z = _dot(u, w_ref[...])

    xr_ref[...] = z[:, _C_XR:_C_GR].astype(xr_ref.dtype)
    gr_ref[...] = z[:, _C_GR:_C_QL].astype(gr_ref.dtype)
    wi_ref[...] = z[:, _C_WI:_C_US]
    us_ref[...] = z[:, _C_US:_C_END].astype(us_ref.dtype)

    ql = z[:, _C_QL:_C_KVL]
    qn = (ql * lax.rsqrt(jnp.mean(ql * ql, axis=-1, keepdims=True) + EPS)
          * vec[0:1, :MLA_Q_LORA]).astype(BF16)
    q2 = _dot(qn, wuq_ref[...])
    kvl = z[:, _C_KVL:_C_KPE]
    kvn = (kvl * lax.rsqrt(jnp.mean(kvl * kvl, axis=-1, keepdims=True) + EPS)
           * vec[1:2, :MLA_KV_LORA]).astype(BF16)
    kv = _dot(kvn, wukv_ref[...])
    kpe = z[:, _C_KPE:_C_QD]
    gq = vec[2:3, :LANE]
    gk = vec[3:4, :LANE]
    roped = _rope_tiles([kpe * gk] + [q2[:, h * LANE:(h + 1) * LANE] * gq for h in range(MLA_HEADS)],
                        cm_ref[...], sm_ref[...], rot_ref[0])
    kpe_rot = roped[0]
    kpe_ss = jnp.sum(kpe * kpe, axis=-1, keepdims=True)
    q_tiles, k_tiles = [], []
    for h in range(MLA_HEADS):
        hs = slice(h * LANE, (h + 1) * LANE)
        qs = q2[:, hs]
        s = lax.rsqrt(jnp.sum(qs * qs, axis=-1, keepdims=True) * (1.0 / MLA_QK) + EPS) * MLA_QK ** -0.5
        q_tiles.append((s * roped[1 + h]).astype(qm_ref.dtype))
        kn = kv[:, hs]
        s = lax.rsqrt((jnp.sum(kn * kn, axis=-1, keepdims=True) + kpe_ss) * (1.0 / MLA_QK) + EPS)
        k_tiles.append((s * (kn * gk + kpe_rot)).astype(km_ref.dtype))
    qm_ref[...] = jnp.concatenate(q_tiles, axis=1)
    km_ref[...] = jnp.concatenate(k_tiles, axis=1)
    nslot = MLA_HEADS * LANE
    vlane = lax.broadcasted_iota(jnp.int32, (1, nslot), 1)
    v_is_low = (vlane // LANE) % 2 == 0
    ones_half = jnp.where(((vlane % LANE) < MLA_V) == v_is_low, 0.0, 1.0)
    vm_ref[...] = (kv[:, nslot:] + ones_half).astype(vm_ref.dtype)

    gd = vec[4:5, :LANE]
    low = lane < DSA_HEAD_DIM
    kd = z[:, _C_KD:_C_VD]
    xqs = [z[:, _C_QD + t * LANE:_C_QD + (t + 1) * LANE] for t in range(DSA_HEADS // 2)]
    roped = _rope_tiles([xq * gd for xq in xqs] + [kd * vec[5:6, :LANE]], cd_ref[...], sd_ref[...], rot_ref[1])
    d_tiles = []
    for t in range(DSA_HEADS // 2):
        x2 = xqs[t] * xqs[t]
        ss_even = jnp.sum(jnp.where(low, x2, 0.0), axis=-1, keepdims=True)
        ss_odd = jnp.sum(jnp.where(low, 0.0, x2), axis=-1, keepdims=True)
        y = roped[t]
        s_even = lax.rsqrt(ss_even * (1.0 / DSA_HEAD_DIM) + EPS) * DSA_HEAD_DIM ** -0.5
        s_odd = lax.rsqrt(ss_odd * (1.0 / DSA_HEAD_DIM) + EPS) * DSA_HEAD_DIM ** -0.5
        d_tiles.append(jnp.where(low, y * s_even, 0.0).astype(qd_ref.dtype))
        d_tiles.append(jnp.where(low, pltpu.roll(y, DSA_HEAD_DIM, 1) * s_odd, 0.0).astype(qd_ref.dtype))
    qd_ref[...] = jnp.concatenate(d_tiles, axis=1)
    s = lax.rsqrt(jnp.sum(kd * kd, axis=-1, keepdims=True) * (1.0 / DSA_HEAD_DIM) + EPS)
    kd_ref[...] = (s * roped[-1]).astype(kd_ref.dtype)
    vd_ref[...] = (z[:, _C_VD:_C_QI] + jnp.where(low, 0.0, 1.0)).astype(vd_ref.dtype)

    ntile = IDX_HEADS * IDX_DIM // LANE
    roped = _rope_tiles([z[:, c + t * LANE:c + (t + 1) * LANE] for c in (_C_QI, _C_KI) for t in range(ntile)],
                        ci_ref[...], si_ref[...], rot_ref[2])
    qi_ref[...] = jnp.concatenate(roped[:ntile], axis=1).astype(qi_ref.dtype)
    ki_ref[...] = jnp.concatenate(roped[ntile:], axis=1).astype(ki_ref.dtype)


def _inproj_call(x, mod, pw, l, tabs, seq, tm):
    n, d = x.shape
    row = lambda w: pl.BlockSpec((tm, w), lambda i: (i, 0))
    out_widths = [(D_RNN, BF16), (D_RNN, BF16), (MLA_HEADS * LANE, BF16), (MLA_HEADS * LANE, BF16),
                  (MLA_HEADS * LANE, BF16), (DSA_HEADS * LANE, BF16), (LANE, BF16), (LANE, BF16),
                  (IDX_HEADS * IDX_DIM, BF16), (IDX_HEADS * IDX_DIM, BF16), (LANE, F32), (D_S5, BF16)]
    return pl.pallas_call(
        _inproj_kernel,
        grid=(n // tm,),
        in_specs=[
            row(d), _mod_spec(l, tm, seq, d), _layer((1, d), l, 1),
            _layer(pw["w_all"].shape[1:], l), _layer(pw["wuq"].shape[1:], l),
            _layer(pw["wukv"].shape[1:], l), _layer(pw["vec"].shape[1:], l),
            _resident(pw["rot"].shape),
        ] + [row(LANE)] * 6,
        out_specs=[row(w) for w, _ in out_widths],
        out_shape=[jax.ShapeDtypeStruct((n, w), dt) for w, dt in out_widths],
        compiler_params=_params(("arbitrary",)),
        name="in_proj",
    )(x, mod, pw["norm_g"], pw["w_all"], pw["wuq"], pw["wukv"], pw["vec"], pw["rot"], *tabs)


def _rglru_kernel(tc, x_ref, gate_ref, cw_ref, vec_ref, wa_ref, wx_ref, o_ref, xs_ref, h_ref):
    j = pl.program_id(1)

    @pl.when(j == 0)
    def _():
        xs_ref[0:8, :] = jnp.zeros((8, D_RNN), F32)
        h_ref[...] = jnp.zeros(h_ref.shape, F32)

    xs_ref[8:8 + tc, :] = x_ref[...].astype(F32)
    cw = cw_ref[...]
    vec = vec_ref[...]
    xc = vec[0:1]
    for k in range(CONV_WIDTH):
        xc = xc + cw[k:k + 1] * xs_ref[pl.ds(8 - (CONV_WIDTH - 1) + k, tc), :]
    xs_ref[0:8, :] = xs_ref[tc:tc + 8, :]

    xb = xc.astype(BF16)
    r = _sigmoid(_dot(xb, wa_ref[...]) + vec[1:2])
    ig = _sigmoid(_dot(xb, wx_ref[...]) + vec[2:3])
    nl = -vec[3:4]
    softplus = jnp.maximum(nl, 0.0) + jnp.log(1.0 + jnp.exp(-jnp.abs(nl)))
    log_a = (-LRU_C) * r * softplus
    a = jnp.exp(log_a)
    z = 2.0 * log_a
    series = -z * (1.0 + z * (0.5 + z * (1.0 / 6.0 + z * (1.0 / 24.0 + z * (1.0 / 120.0 + z * (1.0 / 720.0))))))
    nem1 = jnp.where(z > -0.25, series, 1.0 - jnp.exp(z))
    b = jnp.sqrt(nem1) * ig * xc

    sub = lax.broadcasted_iota(jnp.int32, (tc, D_RNN), 0) & (SUBLANES - 1)
    d = 1
    while d < SUBLANES:
        keep = sub >= d
        a_s = jnp.where(keep, pltpu.roll(a, d, 0), 1.0)
        b_s = jnp.where(keep, pltpu.roll(b, d, 0), 0.0)
        b = a * b_s + b
        a = a * a_s
        d *= 2
    carry = h_ref[...]
    groups = []
    for g in range(tc // SUBLANES):
        rows = slice(g * SUBLANES, (g + 1) * SUBLANES)
        hg = b[rows] + a[rows] * carry
        groups.append(hg)
        carry = hg[SUBLANES - 1:SUBLANES]
    h = jnp.concatenate(groups, axis=0)
    h_ref[...] = carry
    o_ref[...] = (h * _gelu_tanh(gate_ref[...].astype(F32))).astype(o_ref.dtype)


def _rglru_call(xr, gr, pw, l, bsz, seq, tc):
    n = xr.shape[0]
    nt = seq // tc
    row = pl.BlockSpec((tc, D_RNN), lambda b, j: (b * nt + j, 0))
    return pl.pallas_call(
        functools.partial(_rglru_kernel, tc),
        grid=(bsz, nt),
        in_specs=[row, row,
                  _layer((CONV_WIDTH, D_RNN), l), _layer((4, D_RNN), l),
                  _layer((D_RNN, D_RNN), l), _layer((D_RNN, D_RNN), l)],
        out_specs=row,
        out_shape=jax.ShapeDtypeStruct((n, D_RNN), BF16),
        scratch_shapes=[pltpu.VMEM((tc + 8, D_RNN), F32), pltpu.VMEM((1, D_RNN), F32)],
        compiler_params=_params(("arbitrary", "arbitrary")),
        name="rglru",
    )(xr, gr, pw["conv_w"], pw["rg_vec"], pw["rg_wa"], pw["rg_wx"])


def _flash_kernel(q_ref, k_ref, v_ref, o_ref, m_ref, acc_ref):
    i = pl.program_id(1)
    bq = q_ref.shape[0]
    bk = bq

    m_ref[...] = jnp.full(m_ref.shape, NEG_BIG, F32)
    acc_ref[...] = jnp.zeros(acc_ref.shape, F32)

    def step(j, masked):
        rows = pl.ds(pl.multiple_of(j * bk, bk), bk)
        if masked:
            causal = (lax.broadcasted_iota(jnp.int32, (bq, bk), 1)
                      <= lax.broadcasted_iota(jnp.int32, (bq, bk), 0))
        scores = [_dot_nt(q_ref[:, h * LANE:(h + 1) * LANE], k_ref[rows, h * LANE:(h + 1) * LANE])
                  for h in range(MLA_HEADS)]
        m_out, alphas, probs = [], [], []
        for h in range(MLA_HEADS):
            s = scores[h]
            if masked:
                s = jnp.where(causal, s, NEG_BIG)
            m_prev = m_ref[h]
            m_new = jnp.maximum(m_prev, jnp.max(s, axis=1, keepdims=True))
            probs.append(jnp.exp(s - jnp.concatenate([m_new] * (bk // LANE), axis=1)).astype(BF16))
            alphas.append(jnp.exp(m_prev - m_new))
            m_out.append(m_new)
        m_ref[...] = jnp.stack(m_out)
        acc_ref[...] = jnp.stack([alphas[h] * acc_ref[h] + _dot(probs[h], v_ref[rows, h * LANE:(h + 1) * LANE])
                                  for h in range(MLA_HEADS)])

    def below_diagonal(j, carry):
        step(j, False)
        return carry

    lax.fori_loop(0, i, below_diagonal, 0)
    step(i, True)
    lane = lax.broadcasted_iota(jnp.int32, (bq, LANE), 1)
    outs = []
    for pr in range(MLA_HEADS // 2):
        even = acc_ref[2 * pr]
        odd = acc_ref[2 * pr + 1]
        outs.append(jnp.where(lane < MLA_V, even / pltpu.roll(even, MLA_V, 1),
                              odd / pltpu.roll(odd, MLA_V, 1)))
    o_ref[...] = jnp.concatenate(outs, axis=1).astype(o_ref.dtype)


def _flash_call(q, k, v, bsz, seq, bq):
    n = q.shape[0]
    nq = seq // bq
    whole = pl.BlockSpec((seq, MLA_HEADS * LANE), lambda b, i: (b, 0))
    return pl.pallas_call(
        _flash_kernel,
        grid=(bsz, nq),
        in_specs=[pl.BlockSpec((bq, MLA_HEADS * LANE), lambda b, i: (b * nq + i, 0)), whole, whole],
        out_specs=pl.BlockSpec((bq, MLA_HEADS * MLA_V), lambda b, i: (b * nq + i, 0)),
        out_shape=jax.ShapeDtypeStruct((n, MLA_HEADS * MLA_V), BF16),
        scratch_shapes=[pltpu.VMEM((MLA_HEADS, bq, LANE), F32), pltpu.VMEM((MLA_HEADS, bq, LANE), F32)],
        compiler_params=_params(("arbitrary",) * 2),
        name="mla_flash",
    )(q, k, v)


_DSA_QB = 256
_DSA_HG = 2
_DSA_CHAINS = 8
_INT_MIN = -2 ** 31


def _dsa_body(ns, topk, i, q_ref, k_ref, v_ref, qi_ref, ki_ref, w_ref, tri_ref, o_ref):
    qb = _DSA_QB
    hg = _DSA_HG
    qi = qi_ref[...]
    ki = ki_ref[0:ns, :]
    w = w_ref[...]
    head_of_lane = jnp.right_shift(lax.broadcasted_iota(jnp.int32, qi.shape, 1), int(math.log2(IDX_DIM)))
    zero = jnp.zeros_like(qi)
    rels = [_dot_nt(jnp.concatenate([jnp.where(head_of_lane == h, qi, zero) for h in range(g, g + hg)], axis=0), ki)
            for g in range(0, IDX_HEADS, hg)]
    score = jnp.zeros((qb, ns), F32)
    for g, rel in enumerate(rels):
        for t in range(hg):
            score = score + w[:, g * hg + t:g * hg + t + 1] * jnp.maximum(rel[t * qb:(t + 1) * qb], 0.0)

    qpos = lax.broadcasted_iota(jnp.int32, (qb, ns), 0) + i * qb
    kpos = lax.broadcasted_iota(jnp.int32, (qb, ns), 1)
    bits = pltpu.bitcast(score + 0.0, jnp.int32)
    key = jnp.where(bits < 0, bits ^ jnp.int32(0x7FFFFFFF), bits)
    key = jnp.where(kpos <= qpos, key, jnp.int32(_INT_MIN))
    kk = jnp.minimum(lax.broadcasted_iota(jnp.int32, (qb, 1), 0) + (i * qb + 1), topk).astype(F32)

    rows = qb // _DSA_CHAINS
    keys = [key[r * rows:(r + 1) * rows] for r in range(_DSA_CHAINS)]
    kks = [kk[r * rows:(r + 1) * rows] for r in range(_DSA_CHAINS)]

    def body(it, thrs):
        bit = lax.shift_left(jnp.int32(1), jnp.int32(31) - it)
        out = []
        for kr, kkr, t in zip(keys, kks, thrs):
            c = t + bit
            cnt = jnp.sum(jnp.where(kr >= c, 1.0, 0.0), axis=1, keepdims=True)
            out.append(jnp.where(cnt >= kkr, c, t))
        return tuple(out)

    thrs = lax.fori_loop(0, 32, body, tuple(jnp.full((rows, 1), _INT_MIN, jnp.int32) for _ in keys), unroll=4)
    thr = jnp.concatenate(thrs, axis=0)

    nchunk = ns // LANE
    need = kk - jnp.sum(jnp.where(key > thr, 1.0, 0.0), axis=1, keepdims=True)
    eqs = [jnp.where(key[:, c * LANE:(c + 1) * LANE] == thr, 1.0, 0.0) for c in range(nchunk)]
    before_all = _dot(jnp.concatenate(eqs, axis=0).astype(BF16), tri_ref[...])
    run = jnp.zeros((qb, 1), F32)
    bias_chunks = []
    for c in range(nchunk):
        before = before_all[c * qb:(c + 1) * qb] + run
        take = jnp.where(key[:, c * LANE:(c + 1) * LANE] > thr, 1.0, jnp.where(before < need, eqs[c], 0.0))
        bias_chunks.append(jnp.where(take > 0.5, 0.0, NEG_BIG))
        run = run + jnp.sum(eqs[c], axis=1, keepdims=True)
    bias = jnp.concatenate(bias_chunks, axis=1)

    k = k_ref[0:ns, :]
    v = v_ref[0:ns, :]
    lane = lax.broadcasted_iota(jnp.int32, (qb, LANE), 1)
    scores = [_dot_nt(jnp.concatenate([q_ref[:, h * LANE:(h + 1) * LANE] for h in range(g, g + hg)], axis=0), k)
              for g in range(0, DSA_HEADS, hg)]
    out_tiles = []
    for s in scores:
        s = s.reshape(hg, qb, ns) + bias[None]
        p = jnp.exp(s - jnp.max(s, axis=2, keepdims=True))
        o = _dot(p.reshape(hg * qb, ns).astype(BF16), v)
        for t in range(0, hg, 2):
            even = o[t * qb:(t + 1) * qb]
            odd = o[(t + 1) * qb:(t + 2) * qb]
            out_tiles.append(jnp.where(lane < DSA_HEAD_DIM, even / pltpu.roll(even, DSA_HEAD_DIM, 1),
                                       pltpu.roll(odd, DSA_HEAD_DIM, 1) / odd))
    o_ref[...] = jnp.concatenate(out_tiles, axis=1).astype(o_ref.dtype)


def _dsa_kernel(seq, bucket, topk, q_ref, k_ref, v_ref, qi_ref, ki_ref, w_ref, tri_ref, o_ref):
    i = pl.program_id(1)
    for bk in range(seq // bucket):
        @pl.when((i * _DSA_QB) // bucket == bk)
        def _(bk=bk):
            _dsa_body((bk + 1) * bucket, topk, i, q_ref, k_ref, v_ref, qi_ref, ki_ref, w_ref,
                      tri_ref, o_ref)


def _dsa_call(qd, kd, vd, qi, ki, wi, tri, bsz, seq):
    n = qd.shape[0]
    qb = _DSA_QB
    nq = seq // qb
    bucket = min(512, seq)
    topk = min(TOPK_MAX, seq // 4)
    qrow = lambda w: pl.BlockSpec((qb, w), lambda b, i: (b * nq + i, 0))
    full = lambda w: pl.BlockSpec((seq, w), lambda b, i: (b, 0))
    return pl.pallas_call(
        functools.partial(_dsa_kernel, seq, bucket, topk),
        grid=(bsz, nq),
        in_specs=[qrow(DSA_HEADS * LANE), full(LANE), full(LANE),
                  qrow(IDX_HEADS * IDX_DIM), full(IDX_HEADS * IDX_DIM), qrow(LANE),
                  _resident((LANE, LANE))],
        out_specs=qrow(DSA_HEADS * DSA_HEAD_DIM),
        out_shape=jax.ShapeDtypeStruct((n, DSA_HEADS * DSA_HEAD_DIM), BF16),
        compiler_params=_params(("arbitrary", "arbitrary")),
        name="dsa",
    )(qd, kd, vd, qi, ki, wi, tri)


def _s5_kernel(rows, u_ref, wt_ref, ws_ref, wo_ref, lam_ref, d_ref, o_ref, uf_ref, yf_ref):
    half = S5_GB * S5_STATE
    rowi = lax.broadcasted_iota(jnp.int32, (rows, half), 0)
    sub = rowi & (SUBLANES - 1)
    for gb in range(S5_NGB):
        gs = slice(gb * LANE, (gb + 1) * LANE)
        uf_ref[gb] = u_ref[:, gs].astype(F32)
        uf = jnp.concatenate([uf_ref[gb, pl.ds(a, rows, stride=S5_CHUNK), :] for a in range(S5_CHUNK)], axis=1)
        u = uf.astype(BF16)
        y = _dot(u, wt_ref[gb])
        st = _dot(u, ws_ref[gb])
        xr = st[:, :half]
        xi = st[:, half:]
        lam = lam_ref[gb]
        lr = lam[0:1]
        li = lam[1:2]
        pw_r, pw_i = [lr], [li]
        for _ in range(SUBLANES - 1):
            pw_r, pw_i = pw_r + [pw_r[-1] * lr - pw_i[-1] * li], pw_i + [pw_r[-1] * li + pw_i[-1] * lr]
        pw_r = jnp.concatenate(pw_r, axis=0)
        pw_i = jnp.concatenate(pw_i, axis=0)
        d = 1
        while d < SUBLANES:
            keep = sub >= d
            sr = jnp.where(keep, pltpu.roll(xr, d, 0), 0.0)
            si = jnp.where(keep, pltpu.roll(xi, d, 0), 0.0)
            xr, xi = xr + lr * sr - li * si, xi + lr * si + li * sr
            lr, li = lr * lr - li * li, 2.0 * lr * li
            d *= 2
        cr = jnp.zeros((1, half), F32)
        ci = jnp.zeros((1, half), F32)
        gr, gi = [], []
        for g in range(rows // SUBLANES):
            rs = slice(g * SUBLANES, (g + 1) * SUBLANES)
            hr = xr[rs] + pw_r * cr - pw_i * ci
            hi = xi[rs] + pw_r * ci + pw_i * cr
            gr.append(hr)
            gi.append(hi)
            cr = hr[SUBLANES - 1:SUBLANES]
            ci = hi[SUBLANES - 1:SUBLANES]
        xr = jnp.concatenate(gr, axis=0)
        xi = jnp.concatenate(gi, axis=0)
        keep = rowi >= 1
        pr = jnp.where(keep, pltpu.roll(xr, 1, 0), 0.0)
        pi = jnp.where(keep, pltpu.roll(xi, 1, 0), 0.0)
        xp = jnp.concatenate([pr, pi], axis=1).astype(BF16)
        yg = _gelu_tanh(y + _dot(xp, wo_ref[gb]) + d_ref[gb] * uf)
        for a in range(S5_CHUNK):
            yf_ref[gb, pl.ds(a, rows, stride=S5_CHUNK), :] = yg[:, a * LANE:(a + 1) * LANE]
        o_ref[:, gs] = yf_ref[gb].astype(o_ref.dtype)


def _s5_call(us, pw, l, bsz, seq):
    n = us.shape[0]
    blk = pl.BlockSpec((seq, D_S5), lambda b: (b, 0))
    wshape = pw["s5_wt"].shape[1:]
    return pl.pallas_call(
        functools.partial(_s5_kernel, seq // S5_CHUNK),
        grid=(bsz,),
        in_specs=[blk, _layer(wshape, l), _layer(wshape, l), _layer(wshape, l),
                  _layer(pw["s5_lam"].shape[1:], l), _layer(pw["s5_d"].shape[1:], l)],
        out_specs=blk,
        out_shape=jax.ShapeDtypeStruct((n, D_S5), BF16),
        scratch_shapes=[pltpu.VMEM((S5_NGB, seq, LANE), F32), pltpu.VMEM((S5_NGB, seq, LANE), F32)],
        compiler_params=_params(("arbitrary",)),
        name="s5",
    )(us, pw["s5_wt"], pw["s5_ws"], pw["s5_wo"], pw["s5_lam"], pw["s5_d"])


def _merge_kernel(x_ref, mod_ref, g_ref, ya_ref, yb_ref, yc_ref, yd_ref,
                  wg_ref, wb_ref, wglu_ref, bglu_ref, wout_ref, o_ref):
    x = x_ref[...]
    m = mod_ref[0]
    d = x.shape[1]
    u = _rms_mod(x, g_ref[...], m[3:4], m[4:5]).astype(BF16)
    yd = yd_ref[...]
    ydg = (yd.astype(F32) * _sigmoid(_dot(yd, wglu_ref[...]) + bglu_ref[...])).astype(BF16)
    ys = (ya_ref[...], yb_ref[...], yc_ref[...], ydg)
    merged = jnp.zeros(x.shape, F32)
    for nb in range(N_BRANCH):
        gate = _sigmoid(_dot(u, wg_ref[:, nb * d:(nb + 1) * d]))
        merged = merged + gate * _dot(ys[nb], wb_ref[nb])
    o_ref[...] = x + (1.0 + m[5:6]) * _dot(merged.astype(BF16), wout_ref[...])


def _merge_call(x, mod, ya, yb, yc, yd, pw, l, seq, tm):
    n, d = x.shape
    row = lambda w: pl.BlockSpec((tm, w), lambda i: (i, 0))
    return pl.pallas_call(
        _merge_kernel,
        grid=(n // tm,),
        in_specs=[row(d), _mod_spec(l, tm, seq, d), _layer((1, d), l, 1),
                  row(BRANCH_W), row(BRANCH_W), row(BRANCH_W), row(BRANCH_W),
                  _layer((d, N_BRANCH * d), l), _layer((N_BRANCH, BRANCH_W, d), l),
                  _layer((D_S5, D_S5), l), _layer((1, D_S5), l), _layer((d, d), l)],
        out_specs=row(d),
        out_shape=jax.ShapeDtypeStruct((n, d), F32),
        compiler_params=_params(("arbitrary",)),
        name="merge",
    )(x, mod, pw["norm_g"], ya, yb, yc, yd, pw["w_gate"], pw["w_branch"], pw["w_glu"], pw["b_glu"], pw["w_out"])


def _to_slots(w, nheads, hdim):
    lead = w.shape[:-1]
    wh = w.reshape(lead + (nheads, hdim))
    wh = jnp.pad(wh, [(0, 0)] * (len(lead) + 1) + [(0, LANE - hdim)])
    return wh.reshape(lead + (nheads * LANE,))


def _pad_lane(v, width=LANE):
    return jnp.pad(v, [(0, 0)] * (v.ndim - 1) + [(0, width - v.shape[-1])])


def _rope_table(positions, rot, period):
    j = jnp.arange(LANE) % period
    inv = jnp.where(j < rot, ROPE_THETA ** (-(2 * (j % (rot // 2))).astype(F32) / rot), 0.0)
    ang = positions.astype(F32).reshape(-1, 1) * inv[None, :]
    return jnp.cos(ang), jnp.sin(ang)


def _rot_matrix(rot, period):
    src = jnp.arange(LANE)[:, None]
    dst = jnp.arange(LANE)[None, :]
    j = dst % period
    half = rot // 2
    return (jnp.where((j < half) & (src == dst + half), -1.0, 0.0)
            + jnp.where((j >= half) & (j < rot) & (src == dst - half), 1.0, 0.0)).astype(BF16)


def _s5_weights(lam_re, lam_im, log_dt, b_re, b_im, c_re, c_im, dvec):
    hp = lax.Precision.HIGHEST
    g, p = lam_re.shape
    ch = S5_CHUNK
    dt = jnp.exp(log_dt)[:, None]
    mag = jnp.exp(lam_re * dt)
    ar, ai = mag * jnp.cos(lam_im * dt), mag * jnp.sin(lam_im * dt)
    den = lam_re * lam_re + lam_im * lam_im
    nr, ni = ar - 1.0, ai
    f_re = (nr * lam_re + ni * lam_im) / den
    f_im = (ni * lam_re - nr * lam_im) / den
    bb_re = f_re[..., None] * b_re - f_im[..., None] * b_im
    bb_im = f_re[..., None] * b_im + f_im[..., None] * b_re
    pr, pi = [jnp.ones_like(ar)], [jnp.zeros_like(ar)]
    for _ in range(ch):
        pr, pi = pr + [pr[-1] * ar - pi[-1] * ai], pi + [pr[-1] * ai + pi[-1] * ar]
    pr, pi = jnp.stack(pr), jnp.stack(pi)
    mr = pr[..., None] * bb_re - pi[..., None] * bb_im
    mi = pr[..., None] * bb_im + pi[..., None] * bb_re
    kern = (jnp.einsum('gjp,tgpi->tgji', c_re, mr, precision=hp)
            - jnp.einsum('gjp,tgpi->tgji', c_im, mi, precision=hp))

    def group_diag(x):
        r, c = x.shape[-2:]
        x = x.reshape(x.shape[:-3] + (S5_NGB, S5_GB * r, c))
        on_diag = jnp.arange(S5_GB * r)[:, None] // r == jnp.arange(S5_GB * c)[None, :] // c
        return jnp.tile(x, (1,) * (x.ndim - 1) + (S5_GB,)) * on_diag.astype(x.dtype)

    a_in = jnp.arange(ch)[:, None]
    a_out = jnp.arange(ch)[None, :]
    lag = jnp.clip(a_out - a_in, 0, ch)
    bd_k = group_diag(kern.transpose(0, 1, 3, 2)).astype(BF16)
    toe = bd_k[lag] * (a_out >= a_in)[..., None, None, None].astype(BF16)
    w_toe = toe.transpose(2, 0, 3, 1, 4).reshape(S5_NGB, ch * LANE, ch * LANE)
    rev = jnp.arange(ch - 1, -1, -1)
    st = jnp.concatenate([group_diag(mr[:ch].transpose(0, 1, 3, 2)), group_diag(mi[:ch].transpose(0, 1, 3, 2))],
                         axis=-1).astype(BF16)
    w_st = st[rev].transpose(1, 0, 2, 3).reshape(S5_NGB, ch * LANE, 2 * S5_GB * p)
    pr1, pi1 = pr[1:], pi[1:]
    co_re = c_re[None] * pr1[:, :, None, :] - c_im[None] * pi1[:, :, None, :]
    co_im = -(c_re[None] * pi1[:, :, None, :] + c_im[None] * pr1[:, :, None, :])
    co = jnp.stack([group_diag(co_re.transpose(0, 1, 3, 2)), group_diag(co_im.transpose(0, 1, 3, 2))],
                   axis=2).astype(BF16)
    w_out = co.reshape(ch, S5_NGB, 2 * S5_GB * p, LANE).transpose(1, 2, 0, 3).reshape(
        S5_NGB, 2 * S5_GB * p, ch * LANE)
    lam_c = jnp.stack([pr[ch].reshape(S5_NGB, S5_GB * p), pi[ch].reshape(S5_NGB, S5_GB * p)], axis=1)
    d_t = jnp.tile(dvec.reshape(S5_NGB, 1, LANE), (1, 1, ch))
    return w_toe, w_st, w_out, lam_c, d_t


def _block_diag(w):
    h, a, b = w.shape
    eye = jnp.eye(h, dtype=w.dtype)
    return jnp.einsum('hij,hg->higj', w, eye).reshape(h * a, h * b)


def _prep_params(p):
    w_in = p["w_in"]
    nl = w_in.shape[0]
    offs = [0]
    for s in IN_SPLITS:
        offs.append(offs[-1] + s)
    seg = [w_in[:, :, offs[k]:offs[k + 1]] for k in range(len(IN_SPLITS))]
    (w_xr, w_gr, w_ql, w_kvl, w_kpe, w_qd, w_kd, w_vd, w_qi, w_ki, w_wi, w_us, w_gate) = seg
    w_all = jnp.concatenate([w_xr, w_gr, w_ql, w_kvl, _pad_lane(w_kpe), w_qd, _pad_lane(w_kd), _pad_lane(w_vd),
                             w_qi, jnp.tile(w_ki, (1, 1, IDX_HEADS)), _pad_lane(w_wi), w_us],
                            axis=2).astype(BF16)
    assert w_all.shape[2] == _C_END

    wuq = _to_slots(p["mla_w_uq"], MLA_HEADS, MLA_QK).astype(BF16)
    wkv = p["mla_w_ukv"].reshape(nl, MLA_KV_LORA, MLA_HEADS, MLA_NOPE + MLA_V)
    kn = jnp.pad(wkv[..., :MLA_NOPE], ((0, 0), (0, 0), (0, 0), (MLA_ROPE, LANE - MLA_QK)))
    wv = wkv[..., MLA_NOPE:].reshape(nl, MLA_KV_LORA, MLA_HEADS // 2, 2, MLA_V)
    zv = jnp.zeros_like(wv[:, :, :, 0])
    wv = jnp.stack([jnp.concatenate([wv[:, :, :, 0], zv], axis=-1),
                    jnp.concatenate([zv, wv[:, :, :, 1]], axis=-1)], axis=3)
    wukv = jnp.concatenate([kn.reshape(nl, MLA_KV_LORA, MLA_HEADS * LANE),
                            wv.reshape(nl, MLA_KV_LORA, MLA_HEADS * LANE)], axis=2).astype(BF16)

    dq = p["dsa_qk_gain"][:, 0]
    vec = jnp.stack([_pad_lane(v, MLA_Q_LORA) for v in (
        p["mla_q_norm"], p["mla_kv_norm"], p["mla_qk_gain"][:, 0], p["mla_qk_gain"][:, 1],
        jnp.concatenate([dq, dq], axis=-1), p["dsa_qk_gain"][:, 1])], axis=1)

    s5_wt, s5_ws, s5_wo, s5_lam, s5_d = jax.vmap(_s5_weights)(
        p["s5_lambda_re"], p["s5_lambda_im"], p["s5_log_dt"], p["s5_b_re"], p["s5_b_im"],
        p["s5_c_re"], p["s5_c_im"], p["s5_d"])

    return dict(
        w_all=w_all, wuq=wuq, wukv=wukv, vec=vec,
        rot=jnp.stack([_rot_matrix(MLA_ROPE, LANE), _rot_matrix(DSA_ROT, DSA_HEAD_DIM),
                       _rot_matrix(IDX_ROT, IDX_DIM)]),
        conv_w=p["conv_w"],
        rg_vec=jnp.stack([p["conv_b"], p["rg_ba"], p["rg_bx"], p["rg_lambda"]], axis=1),
        rg_wa=jax.vmap(_block_diag)(p["rg_wa"]).astype(BF16),
        rg_wx=jax.vmap(_block_diag)(p["rg_wx"]).astype(BF16),
        s5_wt=s5_wt, s5_ws=s5_ws, s5_wo=s5_wo, s5_lam=s5_lam, s5_d=s5_d,
        w_gate=w_gate.astype(BF16),
        w_branch=p["w_branch"].astype(BF16),
        w_glu=p["s5_w_glu"].astype(BF16),
        b_glu=p["s5_b_glu"][:, None, :],
        w_out=p["w_out"].astype(BF16),
        ffn_w1=p["ffn_w1"].astype(BF16), ffn_w3=p["ffn_w3"].astype(BF16), ffn_w2=p["ffn_w2"].astype(BF16),
        norm_g=p["norm_g"][:, :, None, :],
    )


def _pick_tile(n, pref):
    t = pref
    while n % t:
        t //= 2
    return t


def kernel(x, c, positions, ada_w, ada_b, norm_g, ffn_w1, ffn_w3, ffn_w2, w_in,
           conv_w, conv_b, rg_wa, rg_ba, rg_wx, rg_bx, rg_lambda,
           mla_q_norm, mla_w_uq, mla_kv_norm, mla_w_ukv, mla_qk_gain, dsa_qk_gain,
           s5_lambda_re, s5_lambda_im, s5_log_dt, s5_b_re, s5_b_im, s5_c_re, s5_c_im,
           s5_d, s5_w_glu, s5_b_glu, w_branch, w_out):
    p = dict(norm_g=norm_g, ffn_w1=ffn_w1, ffn_w3=ffn_w3, ffn_w2=ffn_w2, w_in=w_in,
             conv_w=conv_w, conv_b=conv_b, rg_wa=rg_wa, rg_ba=rg_ba, rg_wx=rg_wx, rg_bx=rg_bx,
             rg_lambda=rg_lambda, mla_q_norm=mla_q_norm, mla_w_uq=mla_w_uq, mla_kv_norm=mla_kv_norm,
             mla_w_ukv=mla_w_ukv, mla_qk_gain=mla_qk_gain, dsa_qk_gain=dsa_qk_gain,
             s5_lambda_re=s5_lambda_re, s5_lambda_im=s5_lambda_im, s5_log_dt=s5_log_dt,
             s5_b_re=s5_b_re, s5_b_im=s5_b_im, s5_c_re=s5_c_re, s5_c_im=s5_c_im, s5_d=s5_d,
             s5_w_glu=s5_w_glu, s5_b_glu=s5_b_glu, w_branch=w_branch, w_out=w_out)
    bsz, seq, d = x.shape
    n = bsz * seq
    depth = ada_w.shape[0]
    assert seq % (S5_CHUNK * 8) == 0 and seq % _DSA_QB == 0

    tm = _pick_tile(seq, 512)
    tw = _pick_tile(seq, 1024)
    tc = _pick_tile(seq, 256)
    bq = _pick_tile(seq, 512)

    mod = _ada_call(c, ada_w, ada_b)
    pw = _prep_params(p)
    tabs = (_rope_table(positions, MLA_ROPE, LANE) + _rope_table(positions, DSA_ROT, DSA_HEAD_DIM)
            + _rope_table(positions, IDX_ROT, IDX_DIM))
    tri = (jnp.arange(LANE)[:, None] < jnp.arange(LANE)[None, :]).astype(BF16)

    xf = x.reshape(n, d)
    for l in range(depth):
        xf = _ffn_call(xf, mod, pw, l, 0, seq, tw)
        (xr, gr, qm, km, vm, qd, kd, vd, qi, ki, wi, us) = _inproj_call(xf, mod, pw, l, tabs, seq, tm)
        ya = _rglru_call(xr, gr, pw, l, bsz, seq, tc)
        yb = _flash_call(qm, km, vm, bsz, seq, bq)
        yc = _dsa_call(qd, kd, vd, qi, ki, wi, tri, bsz, seq)
        yd = _s5_call(us, pw, l, bsz, seq)
        xf = _merge_call(xf, mod, ya, yb, yc, yd, pw, l, seq, tw)
        xf = _ffn_call(xf, mod, pw, l, 1, seq, tw)
    return xf.reshape(bsz, seq, d)
```

```python
import functools
import math

import jax
import jax.numpy as jnp
from jax import lax
from jax.experimental import pallas as pl
from jax.experimental.pallas import tpu as pltpu

F32 = jnp.float32
BF16 = jnp.bfloat16

D_MODEL = 1024
EPS = 1e-6
ROPE_THETA = 500000.0
D_FF = 2816
N_ADA = 9

D_RNN = 512
RNN_HEADS = 8
RNN_HEAD_DIM = D_RNN // RNN_HEADS
CONV_WIDTH = 4
LRU_C = 8.0

MLA_HEADS = 8
MLA_NOPE = 64
MLA_ROPE = 32
MLA_V = 64
MLA_QK = MLA_ROPE + MLA_NOPE
MLA_Q_LORA = 256
MLA_KV_LORA = 128

DSA_HEADS = 8
DSA_HEAD_DIM = 64
DSA_ROT = DSA_HEAD_DIM // 4
IDX_HEADS = 8
IDX_DIM = 32
IDX_ROT = IDX_DIM // 4
TOPK_MAX = 256

S5_GROUP = 16
S5_GROUPS = 32
D_S5 = S5_GROUP * S5_GROUPS
S5_STATE = 64
S5_CHUNK = 8
S5_GB = 8
S5_NGB = S5_GROUPS // S5_GB

N_BRANCH = 4
BRANCH_W = 512
IN_SPLITS = (D_RNN, D_RNN, MLA_Q_LORA, MLA_KV_LORA, MLA_ROPE,
             DSA_HEADS * DSA_HEAD_DIM, DSA_HEAD_DIM, DSA_HEAD_DIM,
             IDX_HEADS * IDX_DIM, IDX_DIM, IDX_HEADS, D_S5, N_BRANCH * D_MODEL)

LANE = 128
SUBLANES = 8
NEG_BIG = -1e30
VMEM_LIMIT = 56 * 1024 * 1024

PROJ_ROWS = 512
WIDE_ROWS = 1024
SCAN_ROWS = 256
ATTN_ROWS = 512
ADA_COLS = 1024
DSA_KEY_STEP = 512

_C_XR = 0
_C_GR = _C_XR + D_RNN
_C_QL = _C_GR + D_RNN
_C_KVL = _C_QL + MLA_Q_LORA
_C_KPE = _C_KVL + MLA_KV_LORA
_C_QD = _C_KPE + LANE
_C_KD = _C_QD + DSA_HEADS * DSA_HEAD_DIM
_C_VD = _C_KD + LANE
_C_QI = _C_VD + LANE
_C_KI = _C_QI + IDX_HEADS * IDX_DIM
_C_WI = _C_KI + IDX_HEADS * IDX_DIM
_C_US = _C_WI + LANE
_C_END = _C_US + D_S5


def _dot(a, b):
    return jnp.dot(a, b, preferred_element_type=F32)


def _dot_nt(a, b):
    return lax.dot_general(a, b, (((1,), (1,)), ((), ())), preferred_element_type=F32)


def _sigmoid(x):
    return jax.nn.sigmoid(x)


def _gelu_tanh(x):
    return 0.5 * x * (1.0 + jnp.tanh(0.7978845608028654 * (x + 0.044715 * (x * x * x))))


def _rms_mod(x, g, shift, scale):
    ms = jnp.mean(x * x, axis=-1, keepdims=True)
    y = x * lax.rsqrt(ms + EPS) * g
    return y * (1.0 + scale) + shift


def _resident(shape):
    return pl.BlockSpec(shape, lambda *_: (0,) * len(shape), pipeline_mode=pl.Buffered(1))


def _layer(tail, *lead):
    return pl.BlockSpec((None,) * len(lead) + tuple(tail), lambda *_: tuple(lead) + (0,) * len(tail),
                        pipeline_mode=pl.Buffered(1))


def _mod_spec(l, tm, seq, d):
    return pl.BlockSpec((None, 1, N_ADA, d), lambda i: (l, i * tm // seq, 0, 0))


def _params(sem):
    return pltpu.CompilerParams(dimension_semantics=sem, vmem_limit_bytes=VMEM_LIMIT)


def _ada_kernel(c_ref, w_ref, b_ref, o_ref):
    c = c_ref[...]
    a = c * _sigmoid(c)
    w = w_ref[0]
    a_hi = a.astype(BF16)
    a_lo = (a - a_hi.astype(F32)).astype(BF16)
    w_hi = w.astype(BF16)
    w_lo = (w - w_hi.astype(F32)).astype(BF16)
    o_ref[0] = _dot(a_hi, w_hi) + _dot(a_lo, w_hi) + _dot(a_hi, w_lo) + b_ref[0]


def _ada_call(c, ada_w, ada_b):
    nl, d, n9 = ada_w.shape
    b = c.shape[0]
    tn = ADA_COLS
    out = pl.pallas_call(
        _ada_kernel,
        grid=(nl, n9 // tn),
        in_specs=[
            pl.BlockSpec((b, d), lambda l, j: (0, 0)),
            pl.BlockSpec((1, d, tn), lambda l, j: (l, 0, j)),
            pl.BlockSpec((1, 1, tn), lambda l, j: (l, 0, j)),
        ],
        out_specs=pl.BlockSpec((1, b, tn), lambda l, j: (l, 0, j)),
        out_shape=jax.ShapeDtypeStruct((nl, b, n9), F32),
        compiler_params=_params(("arbitrary", "arbitrary")),
        name="ada_mod",
    )(c, ada_w, ada_b.reshape(nl, 1, n9))
    return out.reshape(nl, b, N_ADA, d)


_FFN_CHUNK = 256


def _ffn_kernel(row0, x_ref, mod_ref, g_ref, w1_ref, w3_ref, w2_ref, o_ref, h_ref):
    x = x_ref[...]
    m = mod_ref[0]
    u = _rms_mod(x, g_ref[...], m[row0:row0 + 1], m[row0 + 1:row0 + 2]).astype(BF16)
    for c in range(0, w1_ref.shape[1], _FFN_CHUNK):
        h1 = _dot(u, w1_ref[:, c:c + _FFN_CHUNK])
        h3 = _dot(u, w3_ref[:, c:c + _FFN_CHUNK])
        h_ref[:, c:c + _FFN_CHUNK] = (h1 * _sigmoid(h1) * h3).astype(BF16)
    o_ref[...] = x + 0.5 * (1.0 + m[row0 + 2:row0 + 3]) * _dot(h_ref[...], w2_ref[...])


def _ffn_call(x, mod, pw, l, j, seq, tm):
    n, d = x.shape
    f = pw["ffn_w1"].shape[-1]
    return pl.pallas_call(
        functools.partial(_ffn_kernel, 6 * j),
        grid=(n // tm,),
        in_specs=[
            pl.BlockSpec((tm, d), lambda i: (i, 0)),
            _mod_spec(l, tm, seq, d),
            _layer((1, d), l, 2 * j),
            _layer((d, f), l, j), _layer((d, f), l, j), _layer((f, d), l, j),
        ],
        out_specs=pl.BlockSpec((tm, d), lambda i: (i, 0)),
        out_shape=jax.ShapeDtypeStruct((n, d), F32),
        scratch_shapes=[pltpu.VMEM((tm, f), BF16)],
        compiler_params=_params(("arbitrary",)),
        name="ffn",
    )(x, mod, pw["norm_g"], pw["ffn_w1"], pw["ffn_w3"], pw["ffn_w2"])


def _rope_tiles(tiles, cos, sin, rot):
    rows = tiles[0].shape[0]
    partner = _dot(jnp.concatenate(tiles, axis=0).astype(BF16), rot)
    return [t * cos + partner[i * rows:(i + 1) * rows] * sin for i, t in enumerate(tiles)]


def _inproj_kernel(x_ref, mod_ref, g_ref, w_ref, wuq_ref, wukv_ref, vec_ref, rot_ref,
                   cm_ref, sm_ref, cd_ref, sd_ref, ci_ref, si_ref,
                   xr_ref, gr_ref, qm_ref, km_ref, vm_ref, qd_ref, kd_ref, vd_ref,
                   qi_ref, ki_ref, wi_ref, us_ref):
    x = x_ref[...]
    m = mod_ref[0]
    u = _rms_mod(x, g_ref[...], m[3:4], m[4:5]).astype(BF16)
    vec = vec_ref[...]
    lane = lax.broadcasted_iota(jnp.int32, (1, LANE), 1)
    z = _dot(u, w_ref[...])

    xr_ref[...] = z[:, _C_XR:_C_GR].astype(xr_ref.dtype)
    gr_ref[...] = z[:, _C_GR:_C_QL].astype(gr_ref.dtype)
    wi_ref[...] = z[:, _C_WI:_C_US]
    us_ref[...] = z[:, _C_US:_C_END].astype(us_ref.dtype)

    ql = z[:, _C_QL:_C_KVL]
    qn = (ql * lax.rsqrt(jnp.mean(ql * ql, axis=-1, keepdims=True) + EPS)
          * vec[0:1, :MLA_Q_LORA]).astype(BF16)
    q2 = _dot(qn, wuq_ref[...])
    kvl = z[:, _C_KVL:_C_KPE]
    kvn = (kvl * lax.rsqrt(jnp.mean(kvl * kvl, axis=-1, keepdims=True) + EPS)
           * vec[1:2, :MLA_KV_LORA]).astype(BF16)
    kv = _dot(kvn, wukv_ref[...])
    kpe = z[:, _C_KPE:_C_QD]
    gq = vec[2:3, :LANE]
    gk = vec[3:4, :LANE]
    roped = _rope_tiles([kpe * gk] + [q2[:, h * LANE:(h + 1) * LANE] * gq for h in range(MLA_HEADS)],
                        cm_ref[...], sm_ref[...], rot_ref[0])
    kpe_rot = roped[0]
    kpe_ss = jnp.sum(kpe * kpe, axis=-1, keepdims=True)
    q_tiles, k_tiles = [], []
    for h in range(MLA_HEADS):
        hs = slice(h * LANE, (h + 1) * LANE)
        qs = q2[:, hs]
        s = lax.rsqrt(jnp.sum(qs * qs, axis=-1, keepdims=True) * (1.0 / MLA_QK) + EPS) * MLA_QK ** -0.5
        q_tiles.append((s * roped[1 + h]).astype(qm_ref.dtype))
        kn = kv[:, hs]
        s = lax.rsqrt((jnp.sum(kn * kn, axis=-1, keepdims=True) + kpe_ss) * (1.0 / MLA_QK) + EPS)
        k_tiles.append((s * (kn * gk + kpe_rot)).astype(km_ref.dtype))
    qm_ref[...] = jnp.concatenate(q_tiles, axis=1)
    km_ref[...] = jnp.concatenate(k_tiles, axis=1)
    nslot = MLA_HEADS * LANE
    vlane = lax.broadcasted_iota(jnp.int32, (1, nslot), 1)
    v_is_low = (vlane // LANE) % 2 == 0
    ones_half = jnp.where(((vlane % LANE) < MLA_V) == v_is_low, 0.0, 1.0)
    vm_ref[...] = (kv[:, nslot:] + ones_half).astype(vm_ref.dtype)

    gd = vec[4:5, :LANE]
    low = lane < DSA_HEAD_DIM
    kd = z[:, _C_KD:_C_VD]
    xqs = [z[:, _C_QD + t * LANE:_C_QD + (t + 1) * LANE] for t in range(DSA_HEADS // 2)]
    roped = _rope_tiles([xq * gd for xq in xqs] + [kd * vec[5:6, :LANE]], cd_ref[...], sd_ref[...], rot_ref[1])
    d_tiles = []
    for t in range(DSA_HEADS // 2):
        x2 = xqs[t] * xqs[t]
        ss_even = jnp.sum(jnp.where(low, x2, 0.0), axis=-1, keepdims=True)
        ss_odd = jnp.sum(jnp.where(low, 0.0, x2), axis=-1, keepdims=True)
        y = roped[t]
        s_even = lax.rsqrt(ss_even * (1.0 / DSA_HEAD_DIM) + EPS) * DSA_HEAD_DIM ** -0.5
        s_odd = lax.rsqrt(ss_odd * (1.0 / DSA_HEAD_DIM) + EPS) * DSA_HEAD_DIM ** -0.5
        d_tiles.append(jnp.where(low, y * s_even, 0.0).astype(qd_ref.dtype))
        d_tiles.append(jnp.where(low, pltpu.roll(y, DSA_HEAD_DIM, 1) * s_odd, 0.0).astype(qd_ref.dtype))
    qd_ref[...] = jnp.concatenate(d_tiles, axis=1)
    s = lax.rsqrt(jnp.sum(kd * kd, axis=-1, keepdims=True) * (1.0 / DSA_HEAD_DIM) + EPS)
    kd_ref[...] = (s * roped[-1]).astype(kd_ref.dtype)
    vd_ref[...] = (z[:, _C_VD:_C_QI] + jnp.where(low, 0.0, 1.0)).astype(vd_ref.dtype)

    ntile = IDX_HEADS * IDX_DIM // LANE
    roped = _rope_tiles([z[:, c + t * LANE:c + (t + 1) * LANE] for c in (_C_QI, _C_KI) for t in range(ntile)],
                        ci_ref[...], si_ref[...], rot_ref[2])
    qi_ref[...] = jnp.concatenate(roped[:ntile], axis=1).astype(qi_ref.dtype)
    ki_ref[...] = jnp.concatenate(roped[ntile:], axis=1).astype(ki_ref.dtype)


def _inproj_call(x, mod, pw, l, tabs, seq, tm):
    n, d = x.shape
    row = lambda w: pl.BlockSpec((tm, w), lambda i: (i, 0))
    out_widths = [(D_RNN, BF16), (D_RNN, BF16), (MLA_HEADS * LANE, BF16), (MLA_HEADS * LANE, BF16),
                  (MLA_HEADS * LANE, BF16), (DSA_HEADS * LANE, BF16), (LANE, BF16), (LANE, BF16),
                  (IDX_HEADS * IDX_DIM, BF16), (IDX_HEADS * IDX_DIM, BF16), (LANE, F32), (D_S5, BF16)]
    return pl.pallas_call(
        _inproj_kernel,
        grid=(n // tm,),
        in_specs=[
            row(d), _mod_spec(l, tm, seq, d), _layer((1, d), l, 1),
            _layer(pw["w_all"].shape[1:], l), _layer(pw["wuq"].shape[1:], l),
            _layer(pw["wukv"].shape[1:], l), _layer(pw["vec"].shape[1:], l),
            _resident(pw["rot"].shape),
        ] + [row(LANE)] * 6,
        out_specs=[row(w) for w, _ in out_widths],
        out_shape=[jax.ShapeDtypeStruct((n, w), dt) for w, dt in out_widths],
        compiler_params=_params(("arbitrary",)),
        name="in_proj",
    )(x, mod, pw["norm_g"], pw["w_all"], pw["wuq"], pw["wukv"], pw["vec"], pw["rot"], *tabs)


def _rglru_kernel(tc, x_ref, gate_ref, cw_ref, vec_ref, wa_ref, wx_ref, o_ref, xs_ref, h_ref):
    j = pl.program_id(1)

    @pl.when(j == 0)
    def _():
        xs_ref[0:8, :] = jnp.zeros((8, D_RNN), F32)
        h_ref[...] = jnp.zeros(h_ref.shape, F32)

    xs_ref[8:8 + tc, :] = x_ref[...].astype(F32)
    cw = cw_ref[...]
    vec = vec_ref[...]
    xc = vec[0:1]
    for k in range(CONV_WIDTH):
        xc = xc + cw[k:k + 1] * xs_ref[pl.ds(8 - (CONV_WIDTH - 1) + k, tc), :]
    xs_ref[0:8, :] = xs_ref[tc:tc + 8, :]

    xb = xc.astype(BF16)
    r = _sigmoid(_dot(xb, wa_ref[...]) + vec[1:2])
    ig = _sigmoid(_dot(xb, wx_ref[...]) + vec[2:3])
    nl = -vec[3:4]
    softplus = jnp.maximum(nl, 0.0) + jnp.log(1.0 + jnp.exp(-jnp.abs(nl)))
    log_a = (-LRU_C) * r * softplus
    a = jnp.exp(log_a)
    z = 2.0 * log_a
    series = -z * (1.0 + z * (0.5 + z * (1.0 / 6.0 + z * (1.0 / 24.0 + z * (1.0 / 120.0 + z * (1.0 / 720.0))))))
    nem1 = jnp.where(z > -0.25, series, 1.0 - jnp.exp(z))
    b = jnp.sqrt(nem1) * ig * xc

    sub = lax.broadcasted_iota(jnp.int32, (tc, D_RNN), 0) & (SUBLANES - 1)
    d = 1
    while d < SUBLANES:
        keep = sub >= d
        a_s = jnp.where(keep, pltpu.roll(a, d, 0), 1.0)
        b_s = jnp.where(keep, pltpu.roll(b, d, 0), 0.0)
        b = a * b_s + b
        a = a * a_s
        d *= 2
    carry = h_ref[...]
    groups = []
    for g in range(tc // SUBLANES):
        rows = slice(g * SUBLANES, (g + 1) * SUBLANES)
        hg = b[rows] + a[rows] * carry
        groups.append(hg)
        carry = hg[SUBLANES - 1:SUBLANES]
    h = jnp.concatenate(groups, axis=0)
    h_ref[...] = carry
    o_ref[...] = (h * _gelu_tanh(gate_ref[...].astype(F32))).astype(o_ref.dtype)


def _rglru_call(xr, gr, pw, l, bsz, seq, tc):
    n = xr.shape[0]
    nt = seq // tc
    row = pl.BlockSpec((tc, D_RNN), lambda b, j: (b * nt + j, 0))
    return pl.pallas_call(
        functools.partial(_rglru_kernel, tc),
        grid=(bsz, nt),
        in_specs=[row, row,
                  _layer((CONV_WIDTH, D_RNN), l), _layer((4, D_RNN), l),
                  _layer((D_RNN, D_RNN), l), _layer((D_RNN, D_RNN), l)],
        out_specs=row,
        out_shape=jax.ShapeDtypeStruct((n, D_RNN), BF16),
        scratch_shapes=[pltpu.VMEM((tc + 8, D_RNN), F32), pltpu.VMEM((1, D_RNN), F32)],
        compiler_params=_params(("arbitrary", "arbitrary")),
        name="rglru",
    )(xr, gr, pw["conv_w"], pw["rg_vec"], pw["rg_wa"], pw["rg_wx"])


def _flash_kernel(q_ref, k_ref, v_ref, o_ref, m_ref, acc_ref):
    i = pl.program_id(1)
    bq = q_ref.shape[0]
    bk = bq

    m_ref[...] = jnp.full(m_ref.shape, NEG_BIG, F32)
    acc_ref[...] = jnp.zeros(acc_ref.shape, F32)

    def step(j, masked):
        rows = pl.ds(pl.multiple_of(j * bk, bk), bk)
        if masked:
            causal = (lax.broadcasted_iota(jnp.int32, (bq, bk), 1)
                      <= lax.broadcasted_iota(jnp.int32, (bq, bk), 0))
        scores = [_dot_nt(q_ref[:, h * LANE:(h + 1) * LANE], k_ref[rows, h * LANE:(h + 1) * LANE])
                  for h in range(MLA_HEADS)]
        m_out, alphas, probs = [], [], []
        for h in range(MLA_HEADS):
            s = scores[h]
            if masked:
                s = jnp.where(causal, s, NEG_BIG)
            m_prev = m_ref[h]
            m_new = jnp.maximum(m_prev, jnp.max(s, axis=1, keepdims=True))
            probs.append(jnp.exp(s - jnp.concatenate([m_new] * (bk // LANE), axis=1)).astype(BF16))
            alphas.append(jnp.exp(m_prev - m_new))
            m_out.append(m_new)
        m_ref[...] = jnp.stack(m_out)
        acc_ref[...] = jnp.stack([alphas[h] * acc_ref[h] + _dot(probs[h], v_ref[rows, h * LANE:(h + 1) * LANE])
                                  for h in range(MLA_HEADS)])

    def below_diagonal(j, carry):
        step(j, False)
        return carry

    lax.fori_loop(0, i, below_diagonal, 0)
    step(i, True)
    lane = lax.broadcasted_iota(jnp.int32, (bq, LANE), 1)
    outs = []
    for pr in range(MLA_HEADS // 2):
        even = acc_ref[2 * pr]
        odd = acc_ref[2 * pr + 1]
        outs.append(jnp.where(lane < MLA_V, even / pltpu.roll(even, MLA_V, 1),
                              odd / pltpu.roll(odd, MLA_V, 1)))
    o_ref[...] = jnp.concatenate(outs, axis=1).astype(o_ref.dtype)


def _flash_call(q, k, v, bsz, seq, bq):
    n = q.shape[0]
    nq = seq // bq
    whole = pl.BlockSpec((seq, MLA_HEADS * LANE), lambda b, i: (b, 0))
    return pl.pallas_call(
        _flash_kernel,
        grid=(bsz, nq),
        in_specs=[pl.BlockSpec((bq, MLA_HEADS * LANE), lambda b, i: (b * nq + i, 0)), whole, whole],
        out_specs=pl.BlockSpec((bq, MLA_HEADS * MLA_V), lambda b, i: (b * nq + i, 0)),
        out_shape=jax.ShapeDtypeStruct((n, MLA_HEADS * MLA_V), BF16),
        scratch_shapes=[pltpu.VMEM((MLA_HEADS, bq, LANE), F32), pltpu.VMEM((MLA_HEADS, bq, LANE), F32)],
        compiler_params=_params(("arbitrary",) * 2),
        name="mla_flash",
    )(q, k, v)


_DSA_QB = 256
_DSA_HG = 2
_DSA_CHAINS = 8
_INT_MIN = -2 ** 31


def _dsa_body(ns, topk, i, q_ref, k_ref, v_ref, qi_ref, ki_ref, w_ref, tri_ref, o_ref):
    qb = _DSA_QB
    hg = _DSA_HG
    qi = qi_ref[...]
    ki = ki_ref[0:ns, :]
    w = w_ref[...]
    head_of_lane = jnp.right_shift(lax.broadcasted_iota(jnp.int32, qi.shape, 1), int(math.log2(IDX_DIM)))
    zero = jnp.zeros_like(qi)
    rels = [_dot_nt(jnp.concatenate([jnp.where(head_of_lane == h, qi, zero) for h in range(g, g + hg)], axis=0), ki)
            for g in range(0, IDX_HEADS, hg)]
    score = jnp.zeros((qb, ns), F32)
    for g, rel in enumerate(rels):
        for t in range(hg):
            score = score + w[:, g * hg + t:g * hg + t + 1] * jnp.maximum(rel[t * qb:(t + 1) * qb], 0.0)

    qpos = lax.broadcasted_iota(jnp.int32, (qb, ns), 0) + i * qb
    kpos = lax.broadcasted_iota(jnp.int32, (qb, ns), 1)
    bits = pltpu.bitcast(score + 0.0, jnp.int32)
    key = jnp.where(bits < 0, bits ^ jnp.int32(0x7FFFFFFF), bits)
    key = jnp.where(kpos <= qpos, key, jnp.int32(_INT_MIN))
    kk = jnp.minimum(lax.broadcasted_iota(jnp.int32, (qb, 1), 0) + (i * qb + 1), topk).astype(F32)

    rows = qb // _DSA_CHAINS
    keys = [key[r * rows:(r + 1) * rows] for r in range(_DSA_CHAINS)]
    kks = [kk[r * rows:(r + 1) * rows] for r in range(_DSA_CHAINS)]

    def body(it, thrs):
        bit = lax.shift_left(jnp.int32(1), jnp.int32(31) - it)
        out = []
        for kr, kkr, t in zip(keys, kks, thrs):
            c = t + bit
            cnt = jnp.sum(jnp.where(kr >= c, 1.0, 0.0), axis=1, keepdims=True)
            out.append(jnp.where(cnt >= kkr, c, t))
        return tuple(out)

    thrs = lax.fori_loop(0, 32, body, tuple(jnp.full((rows, 1), _INT_MIN, jnp.int32) for _ in keys), unroll=8)
    thr = jnp.concatenate(thrs, axis=0)

    nchunk = ns // LANE
    need = kk - jnp.sum(jnp.where(key > thr, 1.0, 0.0), axis=1, keepdims=True)
    eqs = [jnp.where(key[:, c * LANE:(c + 1) * LANE] == thr, 1.0, 0.0) for c in range(nchunk)]
    before_all = _dot(jnp.concatenate(eqs, axis=0).astype(BF16), tri_ref[...])
    run = jnp.zeros((qb, 1), F32)
    bias_chunks = []
    for c in range(nchunk):
        before = before_all[c * qb:(c + 1) * qb] + run
        take = jnp.where(key[:, c * LANE:(c + 1) * LANE] > thr, 1.0, jnp.where(before < need, eqs[c], 0.0))
        bias_chunks.append(jnp.where(take > 0.5, 0.0, NEG_BIG))
        run = run + jnp.sum(eqs[c], axis=1, keepdims=True)
    bias = jnp.concatenate(bias_chunks, axis=1)

    k = k_ref[0:ns, :]
    v = v_ref[0:ns, :]
    lane = lax.broadcasted_iota(jnp.int32, (qb, LANE), 1)
    scores = [_dot_nt(jnp.concatenate([q_ref[:, h * LANE:(h + 1) * LANE] for h in range(g, g + hg)], axis=0), k)
              for g in range(0, DSA_HEADS, hg)]
    out_tiles = []
    for s in scores:
        s = s.reshape(hg, qb, ns) + bias[None]
        p = jnp.exp(s - jnp.max(s, axis=2, keepdims=True))
        o = _dot(p.reshape(hg * qb, ns).astype(BF16), v)
        for t in range(0, hg, 2):
            even = o[t * qb:(t + 1) * qb]
            odd = o[(t + 1) * qb:(t + 2) * qb]
            out_tiles.append(jnp.where(lane < DSA_HEAD_DIM, even / pltpu.roll(even, DSA_HEAD_DIM, 1),
                                       pltpu.roll(odd, DSA_HEAD_DIM, 1) / odd))
    o_ref[...] = jnp.concatenate(out_tiles, axis=1).astype(o_ref.dtype)


def _dsa_kernel(seq, bucket, topk, q_ref, k_ref, v_ref, qi_ref, ki_ref, w_ref, tri_ref, o_ref):
    i = pl.program_id(1)
    for bk in range(seq // bucket):
        @pl.when((i * _DSA_QB) // bucket == bk)
        def _(bk=bk):
            _dsa_body((bk + 1) * bucket, topk, i, q_ref, k_ref, v_ref, qi_ref, ki_ref, w_ref,
                      tri_ref, o_ref)


def _dsa_call(qd, kd, vd, qi, ki, wi, tri, bsz, seq):
    n = qd.shape[0]
    qb = _DSA_QB
    nq = seq // qb
    bucket = min(DSA_KEY_STEP, seq)
    topk = min(TOPK_MAX, seq // 4)
    qrow = lambda w: pl.BlockSpec((qb, w), lambda b, i: (b * nq + i, 0))
    full = lambda w: pl.BlockSpec((seq, w), lambda b, i: (b, 0))
    return pl.pallas_call(
        functools.partial(_dsa_kernel, seq, bucket, topk),
        grid=(bsz, nq),
        in_specs=[qrow(DSA_HEADS * LANE), full(LANE), full(LANE),
                  qrow(IDX_HEADS * IDX_DIM), full(IDX_HEADS * IDX_DIM), qrow(LANE),
                  _resident((LANE, LANE))],
        out_specs=qrow(DSA_HEADS * DSA_HEAD_DIM),
        out_shape=jax.ShapeDtypeStruct((n, DSA_HEADS * DSA_HEAD_DIM), BF16),
        compiler_params=_params(("arbitrary", "arbitrary")),
        name="dsa",
    )(qd, kd, vd, qi, ki, wi, tri)


def _s5_kernel(rows, u_ref, wt_ref, ws_ref, wo_ref, lam_ref, d_ref, o_ref, uf_ref, yf_ref):
    half = S5_GB * S5_STATE
    rowi = lax.broadcasted_iota(jnp.int32, (rows, half), 0)
    sub = rowi & (SUBLANES - 1)
    for gb in range(S5_NGB):
        gs = slice(gb * LANE, (gb + 1) * LANE)
        uf_ref[gb] = u_ref[:, gs].astype(F32)
        uf = jnp.concatenate([uf_ref[gb, pl.ds(a, rows, stride=S5_CHUNK), :] for a in range(S5_CHUNK)], axis=1)
        u = uf.astype(BF16)
        y = _dot(u, wt_ref[gb])
        st = _dot(u, ws_ref[gb])
        xr = st[:, :half]
        xi = st[:, half:]
        lam = lam_ref[gb]
        lr = lam[0:1]
        li = lam[1:2]
        pw_r, pw_i = [lr], [li]
        for _ in range(SUBLANES - 1):
            pw_r, pw_i = pw_r + [pw_r[-1] * lr - pw_i[-1] * li], pw_i + [pw_r[-1] * li + pw_i[-1] * lr]
        pw_r = jnp.concatenate(pw_r, axis=0)
        pw_i = jnp.concatenate(pw_i, axis=0)
        d = 1
        while d < SUBLANES:
            keep = sub >= d
            sr = jnp.where(keep, pltpu.roll(xr, d, 0), 0.0)
            si = jnp.where(keep, pltpu.roll(xi, d, 0), 0.0)
            xr, xi = xr + lr * sr - li * si, xi + lr * si + li * sr
            lr, li = lr * lr - li * li, 2.0 * lr * li
            d *= 2
        cr = jnp.zeros((1, half), F32)
        ci = jnp.zeros((1, half), F32)
        gr, gi = [], []
        for g in range(rows // SUBLANES):
            rs = slice(g * SUBLANES, (g + 1) * SUBLANES)
            hr = xr[rs] + pw_r * cr - pw_i * ci
            hi = xi[rs] + pw_r * ci + pw_i * cr
            gr.append(hr)
            gi.append(hi)
            cr = hr[SUBLANES - 1:SUBLANES]
            ci = hi[SUBLANES - 1:SUBLANES]
        xr = jnp.concatenate(gr, axis=0)
        xi = jnp.concatenate(gi, axis=0)
        keep = rowi >= 1
        pr = jnp.where(keep, pltpu.roll(xr, 1, 0), 0.0)
        pi = jnp.where(keep, pltpu.roll(xi, 1, 0), 0.0)
        xp = jnp.concatenate([pr, pi], axis=1).astype(BF16)
        yg = _gelu_tanh(y + _dot(xp, wo_ref[gb]) + d_ref[gb] * uf)
        for a in range(S5_CHUNK):
            yf_ref[gb, pl.ds(a, rows, stride=S5_CHUNK), :] = yg[:, a * LANE:(a + 1) * LANE]
        o_ref[:, gs] = yf_ref[gb].astype(o_ref.dtype)


def _s5_call(us, pw, l, bsz, seq):
    n = us.shape[0]
    blk = pl.BlockSpec((seq, D_S5), lambda b: (b, 0))
    wshape = pw["s5_wt"].shape[1:]
    return pl.pallas_call(
        functools.partial(_s5_kernel, seq // S5_CHUNK),
        grid=(bsz,),
        in_specs=[blk, _layer(wshape, l), _layer(wshape, l), _layer(wshape, l),
                  _layer(pw["s5_lam"].shape[1:], l), _layer(pw["s5_d"].shape[1:], l)],
        out_specs=blk,
        out_shape=jax.ShapeDtypeStruct((n, D_S5), BF16),
        scratch_shapes=[pltpu.VMEM((S5_NGB, seq, LANE), F32), pltpu.VMEM((S5_NGB, seq, LANE), F32)],
        compiler_params=_params(("arbitrary",)),
        name="s5",
    )(us, pw["s5_wt"], pw["s5_ws"], pw["s5_wo"], pw["s5_lam"], pw["s5_d"])


def _merge_kernel(x_ref, mod_ref, g_ref, ya_ref, yb_ref, yc_ref, yd_ref,
                  wg_ref, wb_ref, wglu_ref, bglu_ref, wout_ref, o_ref):
    x = x_ref[...]
    m = mod_ref[0]
    d = x.shape[1]
    u = _rms_mod(x, g_ref[...], m[3:4], m[4:5]).astype(BF16)
    yd = yd_ref[...]
    ydg = (yd.astype(F32) * _sigmoid(_dot(yd, wglu_ref[...]) + bglu_ref[...])).astype(BF16)
    ys = (ya_ref[...], yb_ref[...], yc_ref[...], ydg)
    merged = jnp.zeros(x.shape, F32)
    for nb in range(N_BRANCH):
        gate = _sigmoid(_dot(u, wg_ref[:, nb * d:(nb + 1) * d]))
        merged = merged + gate * _dot(ys[nb], wb_ref[nb])
    o_ref[...] = x + (1.0 + m[5:6]) * _dot(merged.astype(BF16), wout_ref[...])


def _merge_call(x, mod, ya, yb, yc, yd, pw, l, seq, tm):
    n, d = x.shape
    row = lambda w: pl.BlockSpec((tm, w), lambda i: (i, 0))
    return pl.pallas_call(
        _merge_kernel,
        grid=(n // tm,),
        in_specs=[row(d), _mod_spec(l, tm, seq, d), _layer((1, d), l, 1),
                  row(BRANCH_W), row(BRANCH_W), row(BRANCH_W), row(BRANCH_W),
                  _layer((d, N_BRANCH * d), l), _layer((N_BRANCH, BRANCH_W, d), l),
                  _layer((D_S5, D_S5), l), _layer((1, D_S5), l), _layer((d, d), l)],
        out_specs=row(d),
        out_shape=jax.ShapeDtypeStruct((n, d), F32),
        compiler_params=_params(("arbitrary",)),
        name="merge",
    )(x, mod, pw["norm_g"], ya, yb, yc, yd, pw["w_gate"], pw["w_branch"], pw["w_glu"], pw["b_glu"], pw["w_out"])


def _to_slots(w, nheads, hdim):
    lead = w.shape[:-1]
    wh = w.reshape(lead + (nheads, hdim))
    wh = jnp.pad(wh, [(0, 0)] * (len(lead) + 1) + [(0, LANE - hdim)])
    return wh.reshape(lead + (nheads * LANE,))


def _pad_lane(v, width=LANE):
    return jnp.pad(v, [(0, 0)] * (v.ndim - 1) + [(0, width - v.shape[-1])])


def _rope_table(positions, rot, period):
    j = jnp.arange(LANE) % period
    inv = jnp.where(j < rot, ROPE_THETA ** (-(2 * (j % (rot // 2))).astype(F32) / rot), 0.0)
    ang = positions.astype(F32).reshape(-1, 1) * inv[None, :]
    return jnp.cos(ang), jnp.sin(ang)


def _rot_matrix(rot, period):
    src = jnp.arange(LANE)[:, None]
    dst = jnp.arange(LANE)[None, :]
    j = dst % period
    half = rot // 2
    return (jnp.where((j < half) & (src == dst + half), -1.0, 0.0)
            + jnp.where((j >= half) & (j < rot) & (src == dst - half), 1.0, 0.0)).astype(BF16)


def _s5_weights(lam_re, lam_im, log_dt, b_re, b_im, c_re, c_im, dvec):
    hp = lax.Precision.HIGHEST
    g, p = lam_re.shape
    ch = S5_CHUNK
    dt = jnp.exp(log_dt)[:, None]
    mag = jnp.exp(lam_re * dt)
    ar, ai = mag * jnp.cos(lam_im * dt), mag * jnp.sin(lam_im * dt)
    den = lam_re * lam_re + lam_im * lam_im
    nr, ni = ar - 1.0, ai
    f_re = (nr * lam_re + ni * lam_im) / den
    f_im = (ni * lam_re - nr * lam_im) / den
    bb_re = f_re[..., None] * b_re - f_im[..., None] * b_im
    bb_im = f_re[..., None] * b_im + f_im[..., None] * b_re
    pr, pi = [jnp.ones_like(ar)], [jnp.zeros_like(ar)]
    for _ in range(ch):
        pr, pi = pr + [pr[-1] * ar - pi[-1] * ai], pi + [pr[-1] * ai + pi[-1] * ar]
    pr, pi = jnp.stack(pr), jnp.stack(pi)
    mr = pr[..., None] * bb_re - pi[..., None] * bb_im
    mi = pr[..., None] * bb_im + pi[..., None] * bb_re
    kern = (jnp.einsum('gjp,tgpi->tgji', c_re, mr, precision=hp)
            - jnp.einsum('gjp,tgpi->tgji', c_im, mi, precision=hp))

    def group_diag(x):
        r, c = x.shape[-2:]
        x = x.reshape(x.shape[:-3] + (S5_NGB, S5_GB * r, c))
        on_diag = jnp.arange(S5_GB * r)[:, None] // r == jnp.arange(S5_GB * c)[None, :] // c
        return jnp.tile(x, (1,) * (x.ndim - 1) + (S5_GB,)) * on_diag.astype(x.dtype)

    a_in = jnp.arange(ch)[:, None]
    a_out = jnp.arange(ch)[None, :]
    lag = jnp.clip(a_out - a_in, 0, ch)
    bd_k = group_diag(kern.transpose(0, 1, 3, 2)).astype(BF16)
    toe = bd_k[lag] * (a_out >= a_in)[..., None, None, None].astype(BF16)
    w_toe = toe.transpose(2, 0, 3, 1, 4).reshape(S5_NGB, ch * LANE, ch * LANE)
    rev = jnp.arange(ch - 1, -1, -1)
    st = jnp.concatenate([group_diag(mr[:ch].transpose(0, 1, 3, 2)), group_diag(mi[:ch].transpose(0, 1, 3, 2))],
                         axis=-1).astype(BF16)
    w_st = st[rev].transpose(1, 0, 2, 3).reshape(S5_NGB, ch * LANE, 2 * S5_GB * p)
    pr1, pi1 = pr[1:], pi[1:]
    co_re = c_re[None] * pr1[:, :, None, :] - c_im[None] * pi1[:, :, None, :]
    co_im = -(c_re[None] * pi1[:, :, None, :] + c_im[None] * pr1[:, :, None, :])
    co = jnp.stack([group_diag(co_re.transpose(0, 1, 3, 2)), group_diag(co_im.transpose(0, 1, 3, 2))],
                   axis=2).astype(BF16)
    w_out = co.reshape(ch, S5_NGB, 2 * S5_GB * p, LANE).transpose(1, 2, 0, 3).reshape(
        S5_NGB, 2 * S5_GB * p, ch * LANE)
    lam_c = jnp.stack([pr[ch].reshape(S5_NGB, S5_GB * p), pi[ch].reshape(S5_NGB, S5_GB * p)], axis=1)
    d_t = jnp.tile(dvec.reshape(S5_NGB, 1, LANE), (1, 1, ch))
    return w_toe, w_st, w_out, lam_c, d_t


def _block_diag(w):
    h, a, b = w.shape
    eye = jnp.eye(h, dtype=w.dtype)
    return jnp.einsum('hij,hg->higj', w, eye).reshape(h * a, h * b)


def _prep_params(p):
    w_in = p["w_in"]
    nl = w_in.shape[0]
    offs = [0]
    for s in IN_SPLITS:
        offs.append(offs[-1] + s)
    seg = [w_in[:, :, offs[k]:offs[k + 1]] for k in range(len(IN_SPLITS))]
    (w_xr, w_gr, w_ql, w_kvl, w_kpe, w_qd, w_kd, w_vd, w_qi, w_ki, w_wi, w_us, w_gate) = seg
    w_all = jnp.concatenate([w_xr, w_gr, w_ql, w_kvl, _pad_lane(w_kpe), w_qd, _pad_lane(w_kd), _pad_lane(w_vd),
                             w_qi, jnp.tile(w_ki, (1, 1, IDX_HEADS)), _pad_lane(w_wi), w_us],
                            axis=2).astype(BF16)
    assert w_all.shape[2] == _C_END

    wuq = _to_slots(p["mla_w_uq"], MLA_HEADS, MLA_QK).astype(BF16)
    wkv = p["mla_w_ukv"].reshape(nl, MLA_KV_LORA, MLA_HEADS, MLA_NOPE + MLA_V)
    kn = jnp.pad(wkv[..., :MLA_NOPE], ((0, 0), (0, 0), (0, 0), (MLA_ROPE, LANE - MLA_QK)))
    wv = wkv[..., MLA_NOPE:].reshape(nl, MLA_KV_LORA, MLA_HEADS // 2, 2, MLA_V)
    zv = jnp.zeros_like(wv[:, :, :, 0])
    wv = jnp.stack([jnp.concatenate([wv[:, :, :, 0], zv], axis=-1),
                    jnp.concatenate([zv, wv[:, :, :, 1]], axis=-1)], axis=3)
    wukv = jnp.concatenate([kn.reshape(nl, MLA_KV_LORA, MLA_HEADS * LANE),
                            wv.reshape(nl, MLA_KV_LORA, MLA_HEADS * LANE)], axis=2).astype(BF16)

    dq = p["dsa_qk_gain"][:, 0]
    vec = jnp.stack([_pad_lane(v, MLA_Q_LORA) for v in (
        p["mla_q_norm"], p["mla_kv_norm"], p["mla_qk_gain"][:, 0], p["mla_qk_gain"][:, 1],
        jnp.concatenate([dq, dq], axis=-1), p["dsa_qk_gain"][:, 1])], axis=1)

    s5_wt, s5_ws, s5_wo, s5_lam, s5_d = jax.vmap(_s5_weights)(
        p["s5_lambda_re"], p["s5_lambda_im"], p["s5_log_dt"], p["s5_b_re"], p["s5_b_im"],
        p["s5_c_re"], p["s5_c_im"], p["s5_d"])

    return dict(
        w_all=w_all, wuq=wuq, wukv=wukv, vec=vec,
        rot=jnp.stack([_rot_matrix(MLA_ROPE, LANE), _rot_matrix(DSA_ROT, DSA_HEAD_DIM),
                       _rot_matrix(IDX_ROT, IDX_DIM)]),
        conv_w=p["conv_w"],
        rg_vec=jnp.stack([p["conv_b"], p["rg_ba"], p["rg_bx"], p["rg_lambda"]], axis=1),
        rg_wa=jax.vmap(_block_diag)(p["rg_wa"]).astype(BF16),
        rg_wx=jax.vmap(_block_diag)(p["rg_wx"]).astype(BF16),
        s5_wt=s5_wt, s5_ws=s5_ws, s5_wo=s5_wo, s5_lam=s5_lam, s5_d=s5_d,
        w_gate=w_gate.astype(BF16),
        w_branch=p["w_branch"].astype(BF16),
        w_glu=p["s5_w_glu"].astype(BF16),
        b_glu=p["s5_b_glu"][:, None, :],
        w_out=p["w_out"].astype(BF16),
        ffn_w1=p["ffn_w1"].astype(BF16), ffn_w3=p["ffn_w3"].astype(BF16), ffn_w2=p["ffn_w2"].astype(BF16),
        norm_g=p["norm_g"][:, :, None, :],
    )


def _pick_tile(n, pref):
    t = pref
    while n % t:
        t //= 2
    return t


def kernel(x, c, positions, ada_w, ada_b, norm_g, ffn_w1, ffn_w3, ffn_w2, w_in,
           conv_w, conv_b, rg_wa, rg_ba, rg_wx, rg_bx, rg_lambda,
           mla_q_norm, mla_w_uq, mla_kv_norm, mla_w_ukv, mla_qk_gain, dsa_qk_gain,
           s5_lambda_re, s5_lambda_im, s5_log_dt, s5_b_re, s5_b_im, s5_c_re, s5_c_im,
           s5_d, s5_w_glu, s5_b_glu, w_branch, w_out):
    p = dict(norm_g=norm_g, ffn_w1=ffn_w1, ffn_w3=ffn_w3, ffn_w2=ffn_w2, w_in=w_in,
             conv_w=conv_w, conv_b=conv_b, rg_wa=rg_wa, rg_ba=rg_ba, rg_wx=rg_wx, rg_bx=rg_bx,
             rg_lambda=rg_lambda, mla_q_norm=mla_q_norm, mla_w_uq=mla_w_uq, mla_kv_norm=mla_kv_norm,
             mla_w_ukv=mla_w_ukv, mla_qk_gain=mla_qk_gain, dsa_qk_gain=dsa_qk_gain,
             s5_lambda_re=s5_lambda_re, s5_lambda_im=s5_lambda_im, s5_log_dt=s5_log_dt,
             s5_b_re=s5_b_re, s5_b_im=s5_b_im, s5_c_re=s5_c_re, s5_c_im=s5_c_im, s5_d=s5_d,
             s5_w_glu=s5_w_glu, s5_b_glu=s5_b_glu, w_branch=w_branch, w_out=w_out)
    bsz, seq, d = x.shape
    n = bsz * seq
    depth = ada_w.shape[0]
    assert seq % (S5_CHUNK * 8) == 0 and seq % _DSA_QB == 0

    tm = _pick_tile(seq, PROJ_ROWS)
    tw = _pick_tile(seq, WIDE_ROWS)
    tc = _pick_tile(seq, SCAN_ROWS)
    bq = _pick_tile(seq, ATTN_ROWS)

    mod = _ada_call(c, ada_w, ada_b)
    pw = _prep_params(p)
    tabs = (_rope_table(positions, MLA_ROPE, LANE) + _rope_table(positions, DSA_ROT, DSA_HEAD_DIM)
            + _rope_table(positions, IDX_ROT, IDX_DIM))
    tri = (jnp.arange(LANE)[:, None] < jnp.arange(LANE)[None, :]).astype(BF16)

    xf = x.reshape(n, d)
    for l in range(depth):
        xf = _ffn_call(xf, mod, pw, l, 0, seq, tw)
        (xr, gr, qm, km, vm, qd, kd, vd, qi, ki, wi, us) = _inproj_call(xf, mod, pw, l, tabs, seq, tm)
        ya = _rglru_call(xr, gr, pw, l, bsz, seq, tc)
        yb = _flash_call(qm, km, vm, bsz, seq, bq)
        yc = _dsa_call(qd, kd, vd, qi, ki, wi, tri, bsz, seq)
        yd = _s5_call(us, pw, l, bsz, seq)
        xf = _merge_call(xf, mod, ya, yb, yc, yd, pw, l, seq, tw)
        xf = _ffn_call(xf, mod, pw, l, 1, seq, tw)
    return xf.reshape(bsz, seq, d)
```

```python
import functools
import math

import jax
import jax.numpy as jnp
from jax import lax
from jax.experimental import pallas as pl
from jax.experimental.pallas import tpu as pltpu

F32 = jnp.float32
BF16 = jnp.bfloat16

D_MODEL = 1024
EPS = 1e-6
ROPE_THETA = 500000.0
D_FF = 2816
N_ADA = 9

D_RNN = 512
RNN_HEADS = 8
RNN_HEAD_DIM = D_RNN // RNN_HEADS
CONV_WIDTH = 4
LRU_C = 8.0

MLA_HEADS = 8
MLA_NOPE = 64
MLA_ROPE = 32
MLA_V = 64
MLA_QK = MLA_ROPE + MLA_NOPE
MLA_Q_LORA = 256
MLA_KV_LORA = 128

DSA_HEADS = 8
DSA_HEAD_DIM = 64
DSA_ROT = DSA_HEAD_DIM // 4
IDX_HEADS = 8
IDX_DIM = 32
IDX_ROT = IDX_DIM // 4
TOPK_MAX = 256

S5_GROUP = 16
S5_GROUPS = 32
D_S5 = S5_GROUP * S5_GROUPS
S5_STATE = 64
S5_CHUNK = 8
S5_GB = 8
S5_NGB = S5_GROUPS // S5_GB

N_BRANCH = 4
BRANCH_W = 512
IN_SPLITS = (D_RNN, D_RNN, MLA_Q_LORA, MLA_KV_LORA, MLA_ROPE,
             DSA_HEADS * DSA_HEAD_DIM, DSA_HEAD_DIM, DSA_HEAD_DIM,
             IDX_HEADS * IDX_DIM, IDX_DIM, IDX_HEADS, D_S5, N_BRANCH * D_MODEL)

LANE = 128
SUBLANES = 8
NEG_BIG = -1e30
VMEM_LIMIT = 56 * 1024 * 1024

PROJ_ROWS = 512
WIDE_ROWS = 1024
SCAN_ROWS = 512
ATTN_ROWS = 512
ADA_COLS = 2304
DSA_KEY_STEP = 512

_C_XR = 0
_C_GR = _C_XR + D_RNN
_C_QL = _C_GR + D_RNN
_C_KVL = _C_QL + MLA_Q_LORA
_C_KPE = _C_KVL + MLA_KV_LORA
_C_QD = _C_KPE + LANE
_C_KD = _C_QD + DSA_HEADS * DSA_HEAD_DIM
_C_VD = _C_KD + LANE
_C_QI = _C_VD + LANE
_C_KI = _C_QI + IDX_HEADS * IDX_DIM
_C_WI = _C_KI + IDX_HEADS * IDX_DIM
_C_US = _C_WI + LANE
_C_END = _C_US + D_S5


def _dot(a, b):
    return jnp.dot(a, b, preferred_element_type=F32)


def _dot_nt(a, b):
    return lax.dot_general(a, b, (((1,), (1,)), ((), ())), preferred_element_type=F32)


def _sigmoid(x):
    return jax.nn.sigmoid(x)


def _gelu_tanh(x):
    return 0.5 * x * (1.0 + jnp.tanh(0.7978845608028654 * (x + 0.044715 * (x * x * x))))


def _rms_mod(x, g, shift, scale):
    ms = jnp.mean(x * x, axis=-1, keepdims=True)
    y = x * lax.rsqrt(ms + EPS) * g
    return y * (1.0 + scale) + shift


def _resident(shape):
    return pl.BlockSpec(shape, lambda *_: (0,) * len(shape), pipeline_mode=pl.Buffered(1))


def _layer(tail, *lead):
    return pl.BlockSpec((None,) * len(lead) + tuple(tail), lambda *_: tuple(lead) + (0,) * len(tail),
                        pipeline_mode=pl.Buffered(1))


def _mod_spec(l, tm, seq, d):
    return pl.BlockSpec((None, 1, N_ADA, d), lambda i: (l, i * tm // seq, 0, 0))


def _params(sem):
    return pltpu.CompilerParams(dimension_semantics=sem, vmem_limit_bytes=VMEM_LIMIT)


def _ada_kernel(c_ref, w_ref, b_ref, o_ref):
    c = c_ref[...]
    a = c * _sigmoid(c)
    w = w_ref[0]
    a_hi = a.astype(BF16)
    a_lo = (a - a_hi.astype(F32)).astype(BF16)
    w_hi = w.astype(BF16)
    w_lo = (w - w_hi.astype(F32)).astype(BF16)
    o_ref[0] = _dot(a_hi, w_hi) + _dot(a_lo, w_hi) + _dot(a_hi, w_lo) + b_ref[0]


def _ada_call(c, ada_w, ada_b):
    nl, d, n9 = ada_w.shape
    b = c.shape[0]
    tn = ADA_COLS
    out = pl.pallas_call(
        _ada_kernel,
        grid=(nl, n9 // tn),
        in_specs=[
            pl.BlockSpec((b, d), lambda l, j: (0, 0)),
            pl.BlockSpec((1, d, tn), lambda l, j: (l, 0, j)),
            pl.BlockSpec((1, 1, tn), lambda l, j: (l, 0, j)),
        ],
        out_specs=pl.BlockSpec((1, b, tn), lambda l, j: (l, 0, j)),
        out_shape=jax.ShapeDtypeStruct((nl, b, n9), F32),
        compiler_params=_params(("arbitrary", "arbitrary")),
        name="ada_mod",
    )(c, ada_w, ada_b.reshape(nl, 1, n9))
    return out.reshape(nl, b, N_ADA, d)


_FFN_CHUNK = 256


def _ffn_kernel(row0, x_ref, mod_ref, g_ref, w1_ref, w3_ref, w2_ref, o_ref, h_ref):
    x = x_ref[...]
    m = mod_ref[0]
    u = _rms_mod(x, g_ref[...], m[row0:row0 + 1], m[row0 + 1:row0 + 2]).astype(BF16)
    for c in range(0, w1_ref.shape[1], _FFN_CHUNK):
        h1 = _dot(u, w1_ref[:, c:c + _FFN_CHUNK])
        h3 = _dot(u, w3_ref[:, c:c + _FFN_CHUNK])
        h_ref[:, c:c + _FFN_CHUNK] = (h1 * _sigmoid(h1) * h3).astype(BF16)
    o_ref[...] = x + 0.5 * (1.0 + m[row0 + 2:row0 + 3]) * _dot(h_ref[...], w2_ref[...])


def _ffn_call(x, mod, pw, l, j, seq, tm):
    n, d = x.shape
    f = pw["ffn_w1"].shape[-1]
    return pl.pallas_call(
        functools.partial(_ffn_kernel, 6 * j),
        grid=(n // tm,),
        in_specs=[
            pl.BlockSpec((tm, d), lambda i: (i, 0)),
            _mod_spec(l, tm, seq, d),
            _layer((1, d), l, 2 * j),
            _layer((d, f), l, j), _layer((d, f), l, j), _layer((f, d), l, j),
        ],
        out_specs=pl.BlockSpec((tm, d), lambda i: (i, 0)),
        out_shape=jax.ShapeDtypeStruct((n, d), F32),
        scratch_shapes=[pltpu.VMEM((tm, f), BF16)],
        compiler_params=_params(("arbitrary",)),
        name="ffn",
    )(x, mod, pw["norm_g"], pw["ffn_w1"], pw["ffn_w3"], pw["ffn_w2"])


def _rope_tiles(tiles, cos, sin, rot):
    rows = tiles[0].shape[0]
    partner = _dot(jnp.concatenate(tiles, axis=0).astype(BF16), rot)
    return [t * cos + partner[i * rows:(i + 1) * rows] * sin for i, t in enumerate(tiles)]


def _inproj_kernel(x_ref, mod_ref, g_ref, w_ref, wuq_ref, wukv_ref, vec_ref, rot_ref,
                   cm_ref, sm_ref, cd_ref, sd_ref, ci_ref, si_ref,
                   xr_ref, gr_ref, qm_ref, km_ref, vm_ref, qd_ref, kd_ref, vd_ref,
                   qi_ref, ki_ref, wi_ref, us_ref):
    x = x_ref[...]
    m = mod_ref[0]
    u = _rms_mod(x, g_ref[...], m[3:4], m[4:5]).astype(BF16)
    vec = vec_ref[...]
    lane = lax.broadcasted_iota(jnp.int32, (1, LANE), 1)
    z = _dot(u, w_ref[...])

    xr_ref[...] = z[:, _C_XR:_C_GR].astype(xr_ref.dtype)
    gr_ref[...] = z[:, _C_GR:_C_QL].astype(gr_ref.dtype)
    wi_ref[...] = z[:, _C_WI:_C_US]
    us_ref[...] = z[:, _C_US:_C_END].astype(us_ref.dtype)

    ql = z[:, _C_QL:_C_KVL]
    qn = (ql * lax.rsqrt(jnp.mean(ql * ql, axis=-1, keepdims=True) + EPS)
          * vec[0:1, :MLA_Q_LORA]).astype(BF16)
    q2 = _dot(qn, wuq_ref[...])
    kvl = z[:, _C_KVL:_C_KPE]
    kvn = (kvl * lax.rsqrt(jnp.mean(kvl * kvl, axis=-1, keepdims=True) + EPS)
           * vec[1:2, :MLA_KV_LORA]).astype(BF16)
    kv = _dot(kvn, wukv_ref[...])
    kpe = z[:, _C_KPE:_C_QD]
    gq = vec[2:3, :LANE]
    gk = vec[3:4, :LANE]
    roped = _rope_tiles([kpe * gk] + [q2[:, h * LANE:(h + 1) * LANE] * gq for h in range(MLA_HEADS)],
                        cm_ref[...], sm_ref[...], rot_ref[0])
    kpe_rot = roped[0]
    kpe_ss = jnp.sum(kpe * kpe, axis=-1, keepdims=True)
    q_tiles, k_tiles = [], []
    for h in range(MLA_HEADS):
        hs = slice(h * LANE, (h + 1) * LANE)
        qs = q2[:, hs]
        s = lax.rsqrt(jnp.sum(qs * qs, axis=-1, keepdims=True) * (1.0 / MLA_QK) + EPS) * MLA_QK ** -0.5
        q_tiles.append((s * roped[1 + h]).astype(qm_ref.dtype))
        kn = kv[:, hs]
        s = lax.rsqrt((jnp.sum(kn * kn, axis=-1, keepdims=True) + kpe_ss) * (1.0 / MLA_QK) + EPS)
        k_tiles.append((s * (kn * gk + kpe_rot)).astype(km_ref.dtype))
    qm_ref[...] = jnp.concatenate(q_tiles, axis=1)
    km_ref[...] = jnp.concatenate(k_tiles, axis=1)
    nslot = MLA_HEADS * LANE
    vlane = lax.broadcasted_iota(jnp.int32, (1, nslot), 1)
    v_is_low = (vlane // LANE) % 2 == 0
    ones_half = jnp.where(((vlane % LANE) < MLA_V) == v_is_low, 0.0, 1.0)
    vm_ref[...] = (kv[:, nslot:] + ones_half).astype(vm_ref.dtype)

    gd = vec[4:5, :LANE]
    low = lane < DSA_HEAD_DIM
    kd = z[:, _C_KD:_C_VD]
    xqs = [z[:, _C_QD + t * LANE:_C_QD + (t + 1) * LANE] for t in range(DSA_HEADS // 2)]
    roped = _rope_tiles([xq * gd for xq in xqs] + [kd * vec[5:6, :LANE]], cd_ref[...], sd_ref[...], rot_ref[1])
    d_tiles = []
    for t in range(DSA_HEADS // 2):
        x2 = xqs[t] * xqs[t]
        ss_even = jnp.sum(jnp.where(low, x2, 0.0), axis=-1, keepdims=True)
        ss_odd = jnp.sum(jnp.where(low, 0.0, x2), axis=-1, keepdims=True)
        y = roped[t]
        s_even = lax.rsqrt(ss_even * (1.0 / DSA_HEAD_DIM) + EPS) * DSA_HEAD_DIM ** -0.5
        s_odd = lax.rsqrt(ss_odd * (1.0 / DSA_HEAD_DIM) + EPS) * DSA_HEAD_DIM ** -0.5
        d_tiles.append(jnp.where(low, y * s_even, 0.0).astype(qd_ref.dtype))
        d_tiles.append(jnp.where(low, pltpu.roll(y, DSA_HEAD_DIM, 1) * s_odd, 0.0).astype(qd_ref.dtype))
    qd_ref[...] = jnp.concatenate(d_tiles, axis=1)
    s = lax.rsqrt(jnp.sum(kd * kd, axis=-1, keepdims=True) * (1.0 / DSA_HEAD_DIM) + EPS)
    kd_ref[...] = (s * roped[-1]).astype(kd_ref.dtype)
    vd_ref[...] = (z[:, _C_VD:_C_QI] + jnp.where(low, 0.0, 1.0)).astype(vd_ref.dtype)

    ntile = IDX_HEADS * IDX_DIM // LANE
    roped = _rope_tiles([z[:, c + t * LANE:c + (t + 1) * LANE] for c in (_C_QI, _C_KI) for t in range(ntile)],
                        ci_ref[...], si_ref[...], rot_ref[2])
    qi_ref[...] = jnp.concatenate(roped[:ntile], axis=1).astype(qi_ref.dtype)
    ki_ref[...] = jnp.concatenate(roped[ntile:], axis=1).astype(ki_ref.dtype)


def _inproj_call(x, mod, pw, l, tabs, seq, tm):
    n, d = x.shape
    row = lambda w: pl.BlockSpec((tm, w), lambda i: (i, 0))
    out_widths = [(D_RNN, BF16), (D_RNN, BF16), (MLA_HEADS * LANE, BF16), (MLA_HEADS * LANE, BF16),
                  (MLA_HEADS * LANE, BF16), (DSA_HEADS * LANE, BF16), (LANE, BF16), (LANE, BF16),
                  (IDX_HEADS * IDX_DIM, BF16), (IDX_HEADS * IDX_DIM, BF16), (LANE, F32), (D_S5, BF16)]
    return pl.pallas_call(
        _inproj_kernel,
        grid=(n // tm,),
        in_specs=[
            row(d), _mod_spec(l, tm, seq, d), _layer((1, d), l, 1),
            _layer(pw["w_all"].shape[1:], l), _layer(pw["wuq"].shape[1:], l),
            _layer(pw["wukv"].shape[1:], l), _layer(pw["vec"].shape[1:], l),
            _resident(pw["rot"].shape),
        ] + [row(LANE)] * 6,
        out_specs=[row(w) for w, _ in out_widths],
        out_shape=[jax.ShapeDtypeStruct((n, w), dt) for w, dt in out_widths],
        compiler_params=_params(("arbitrary",)),
        name="in_proj",
    )(x, mod, pw["norm_g"], pw["w_all"], pw["wuq"], pw["wukv"], pw["vec"], pw["rot"], *tabs)


def _rglru_kernel(tc, x_ref, gate_ref, cw_ref, vec_ref, wa_ref, wx_ref, o_ref, xs_ref, h_ref):
    j = pl.program_id(1)

    @pl.when(j == 0)
    def _():
        xs_ref[0:8, :] = jnp.zeros((8, D_RNN), F32)
        h_ref[...] = jnp.zeros(h_ref.shape, F32)

    xs_ref[8:8 + tc, :] = x_ref[...].astype(F32)
    cw = cw_ref[...]
    vec = vec_ref[...]
    xc = vec[0:1]
    for k in range(CONV_WIDTH):
        xc = xc + cw[k:k + 1] * xs_ref[pl.ds(8 - (CONV_WIDTH - 1) + k, tc), :]
    xs_ref[0:8, :] = xs_ref[tc:tc + 8, :]

    xb = xc.astype(BF16)
    r = _sigmoid(_dot(xb, wa_ref[...]) + vec[1:2])
    ig = _sigmoid(_dot(xb, wx_ref[...]) + vec[2:3])
    nl = -vec[3:4]
    softplus = jnp.maximum(nl, 0.0) + jnp.log(1.0 + jnp.exp(-jnp.abs(nl)))
    log_a = (-LRU_C) * r * softplus
    a = jnp.exp(log_a)
    z = 2.0 * log_a
    series = -z * (1.0 + z * (0.5 + z * (1.0 / 6.0 + z * (1.0 / 24.0 + z * (1.0 / 120.0 + z * (1.0 / 720.0))))))
    nem1 = jnp.where(z > -0.25, series, 1.0 - jnp.exp(z))
    b = jnp.sqrt(nem1) * ig * xc

    sub = lax.broadcasted_iota(jnp.int32, (tc, D_RNN), 0) & (SUBLANES - 1)
    d = 1
    while d < SUBLANES:
        keep = sub >= d
        a_s = jnp.where(keep, pltpu.roll(a, d, 0), 1.0)
        b_s = jnp.where(keep, pltpu.roll(b, d, 0), 0.0)
        b = a * b_s + b
        a = a * a_s
        d *= 2
    carry = h_ref[...]
    groups = []
    for g in range(tc // SUBLANES):
        rows = slice(g * SUBLANES, (g + 1) * SUBLANES)
        hg = b[rows] + a[rows] * carry
        groups.append(hg)
        carry = hg[SUBLANES - 1:SUBLANES]
    h = jnp.concatenate(groups, axis=0)
    h_ref[...] = carry
    o_ref[...] = (h * _gelu_tanh(gate_ref[...].astype(F32))).astype(o_ref.dtype)


def _rglru_call(xr, gr, pw, l, bsz, seq, tc):
    n = xr.shape[0]
    nt = seq // tc
    row = pl.BlockSpec((tc, D_RNN), lambda b, j: (b * nt + j, 0))
    return pl.pallas_call(
        functools.partial(_rglru_kernel, tc),
        grid=(bsz, nt),
        in_specs=[row, row,
                  _layer((CONV_WIDTH, D_RNN), l), _layer((4, D_RNN), l),
                  _layer((D_RNN, D_RNN), l), _layer((D_RNN, D_RNN), l)],
        out_specs=row,
        out_shape=jax.ShapeDtypeStruct((n, D_RNN), BF16),
        scratch_shapes=[pltpu.VMEM((tc + 8, D_RNN), F32), pltpu.VMEM((1, D_RNN), F32)],
        compiler_params=_params(("arbitrary", "arbitrary")),
        name="rglru",
    )(xr, gr, pw["conv_w"], pw["rg_vec"], pw["rg_wa"], pw["rg_wx"])


def _flash_kernel(q_ref, k_ref, v_ref, o_ref, m_ref, acc_ref):
    i = pl.program_id(1)
    bq = q_ref.shape[0]
    bk = bq

    m_ref[...] = jnp.full(m_ref.shape, NEG_BIG, F32)
    acc_ref[...] = jnp.zeros(acc_ref.shape, F32)

    def step(j, masked):
        rows = pl.ds(pl.multiple_of(j * bk, bk), bk)
        if masked:
            causal = (lax.broadcasted_iota(jnp.int32, (bq, bk), 1)
                      <= lax.broadcasted_iota(jnp.int32, (bq, bk), 0))
        scores = [_dot_nt(q_ref[:, h * LANE:(h + 1) * LANE], k_ref[rows, h * LANE:(h + 1) * LANE])
                  for h in range(MLA_HEADS)]
        m_out, alphas, probs = [], [], []
        for h in range(MLA_HEADS):
            s = scores[h]
            if masked:
                s = jnp.where(causal, s, NEG_BIG)
            m_prev = m_ref[h]
            m_new = jnp.maximum(m_prev, jnp.max(s, axis=1, keepdims=True))
            probs.append(jnp.exp(s - jnp.concatenate([m_new] * (bk // LANE), axis=1)).astype(BF16))
            alphas.append(jnp.exp(m_prev - m_new))
            m_out.append(m_new)
        m_ref[...] = jnp.stack(m_out)
        acc_ref[...] = jnp.stack([alphas[h] * acc_ref[h] + _dot(probs[h], v_ref[rows, h * LANE:(h + 1) * LANE])
                                  for h in range(MLA_HEADS)])

    def below_diagonal(j, carry):
        step(j, False)
        return carry

    lax.fori_loop(0, i, below_diagonal, 0)
    step(i, True)
    lane = lax.broadcasted_iota(jnp.int32, (bq, LANE), 1)
    outs = []
    for pr in range(MLA_HEADS // 2):
        even = acc_ref[2 * pr]
        odd = acc_ref[2 * pr + 1]
        outs.append(jnp.where(lane < MLA_V, even / pltpu.roll(even, MLA_V, 1),
                              odd / pltpu.roll(odd, MLA_V, 1)))
    o_ref[...] = jnp.concatenate(outs, axis=1).astype(o_ref.dtype)


def _flash_call(q, k, v, bsz, seq, bq):
    n = q.shape[0]
    nq = seq // bq
    whole = pl.BlockSpec((seq, MLA_HEADS * LANE), lambda b, i: (b, 0))
    return pl.pallas_call(
        _flash_kernel,
        grid=(bsz, nq),
        in_specs=[pl.BlockSpec((bq, MLA_HEADS * LANE), lambda b, i: (b * nq + i, 0)), whole, whole],
        out_specs=pl.BlockSpec((bq, MLA_HEADS * MLA_V), lambda b, i: (b * nq + i, 0)),
        out_shape=jax.ShapeDtypeStruct((n, MLA_HEADS * MLA_V), BF16),
        scratch_shapes=[pltpu.VMEM((MLA_HEADS, bq, LANE), F32), pltpu.VMEM((MLA_HEADS, bq, LANE), F32)],
        compiler_params=_params(("arbitrary",) * 2),
        name="mla_flash",
    )(q, k, v)


_DSA_QB = 256
_DSA_HG = 2
_DSA_CHAINS = 8
_INT_MIN = -2 ** 31


def _dsa_body(ns, topk, i, q_ref, k_ref, v_ref, qi_ref, ki_ref, w_ref, tri_ref, o_ref):
    qb = _DSA_QB
    hg = _DSA_HG
    qi = qi_ref[...]
    ki = ki_ref[0:ns, :]
    w = w_ref[...]
    head_of_lane = jnp.right_shift(lax.broadcasted_iota(jnp.int32, qi.shape, 1), int(math.log2(IDX_DIM)))
    zero = jnp.zeros_like(qi)
    rels = [_dot_nt(jnp.concatenate([jnp.where(head_of_lane == h, qi, zero) for h in range(g, g + hg)], axis=0), ki)
            for g in range(0, IDX_HEADS, hg)]
    score = jnp.zeros((qb, ns), F32)
    for g, rel in enumerate(rels):
        for t in range(hg):
            score = score + w[:, g * hg + t:g * hg + t + 1] * jnp.maximum(rel[t * qb:(t + 1) * qb], 0.0)

    qpos = lax.broadcasted_iota(jnp.int32, (qb, ns), 0) + i * qb
    kpos = lax.broadcasted_iota(jnp.int32, (qb, ns), 1)
    bits = pltpu.bitcast(score + 0.0, jnp.int32)
    key = jnp.where(bits < 0, bits ^ jnp.int32(0x7FFFFFFF), bits)
    key = jnp.where(kpos <= qpos, key, jnp.int32(_INT_MIN))
    kk = jnp.minimum(lax.broadcasted_iota(jnp.int32, (qb, 1), 0) + (i * qb + 1), topk).astype(F32)

    rows = qb // _DSA_CHAINS
    keys = [key[r * rows:(r + 1) * rows] for r in range(_DSA_CHAINS)]
    kks = [kk[r * rows:(r + 1) * rows] for r in range(_DSA_CHAINS)]

    def body(it, thrs):
        bit = lax.shift_left(jnp.int32(1), jnp.int32(31) - it)
        out = []
        for kr, kkr, t in zip(keys, kks, thrs):
            c = t + bit
            cnt = jnp.sum(jnp.where(kr >= c, 1.0, 0.0), axis=1, keepdims=True)
            out.append(jnp.where(cnt >= kkr, c, t))
        return tuple(out)

    thrs = lax.fori_loop(0, 32, body, tuple(jnp.full((rows, 1), _INT_MIN, jnp.int32) for _ in keys), unroll=16)
    thr = jnp.concatenate(thrs, axis=0)

    nchunk = ns // LANE
    need = kk - jnp.sum(jnp.where(key > thr, 1.0, 0.0), axis=1, keepdims=True)
    eqs = [jnp.where(key[:, c * LANE:(c + 1) * LANE] == thr, 1.0, 0.0) for c in range(nchunk)]
    before_all = _dot(jnp.concatenate(eqs, axis=0).astype(BF16), tri_ref[...])
    run = jnp.zeros((qb, 1), F32)
    bias_chunks = []
    for c in range(nchunk):
        before = before_all[c * qb:(c + 1) * qb] + run
        take = jnp.where(key[:, c * LANE:(c + 1) * LANE] > thr, 1.0, jnp.where(before < need, eqs[c], 0.0))
        bias_chunks.append(jnp.where(take > 0.5, 0.0, NEG_BIG))
        run = run + jnp.sum(eqs[c], axis=1, keepdims=True)
    bias = jnp.concatenate(bias_chunks, axis=1)

    k = k_ref[0:ns, :]
    v = v_ref[0:ns, :]
    lane = lax.broadcasted_iota(jnp.int32, (qb, LANE), 1)
    scores = [_dot_nt(jnp.concatenate([q_ref[:, h * LANE:(h + 1) * LANE] for h in range(g, g + hg)], axis=0), k)
              for g in range(0, DSA_HEADS, hg)]
    out_tiles = []
    for s in scores:
        s = s.reshape(hg, qb, ns) + bias[None]
        p = jnp.exp(s - jnp.max(s, axis=2, keepdims=True))
        o = _dot(p.reshape(hg * qb, ns).astype(BF16), v)
        for t in range(0, hg, 2):
            even = o[t * qb:(t + 1) * qb]
            odd = o[(t + 1) * qb:(t + 2) * qb]
            out_tiles.append(jnp.where(lane < DSA_HEAD_DIM, even / pltpu.roll(even, DSA_HEAD_DIM, 1),
                                       pltpu.roll(odd, DSA_HEAD_DIM, 1) / odd))
    o_ref[...] = jnp.concatenate(out_tiles, axis=1).astype(o_ref.dtype)


def _dsa_kernel(seq, bucket, topk, q_ref, k_ref, v_ref, qi_ref, ki_ref, w_ref, tri_ref, o_ref):
    i = pl.program_id(1)
    for bk in range(seq // bucket):
        @pl.when((i * _DSA_QB) // bucket == bk)
        def _(bk=bk):
            _dsa_body((bk + 1) * bucket, topk, i, q_ref, k_ref, v_ref, qi_ref, ki_ref, w_ref,
                      tri_ref, o_ref)


def _dsa_call(qd, kd, vd, qi, ki, wi, tri, bsz, seq):
    n = qd.shape[0]
    qb = _DSA_QB
    nq = seq // qb
    bucket = min(DSA_KEY_STEP, seq)
    topk = min(TOPK_MAX, seq // 4)
    qrow = lambda w: pl.BlockSpec((qb, w), lambda b, i: (b * nq + i, 0))
    full = lambda w: pl.BlockSpec((seq, w), lambda b, i: (b, 0))
    return pl.pallas_call(
        functools.partial(_dsa_kernel, seq, bucket, topk),
        grid=(bsz, nq),
        in_specs=[qrow(DSA_HEADS * LANE), full(LANE), full(LANE),
                  qrow(IDX_HEADS * IDX_DIM), full(IDX_HEADS * IDX_DIM), qrow(LANE),
                  _resident((LANE, LANE))],
        out_specs=qrow(DSA_HEADS * DSA_HEAD_DIM),
        out_shape=jax.ShapeDtypeStruct((n, DSA_HEADS * DSA_HEAD_DIM), BF16),
        compiler_params=_params(("arbitrary", "arbitrary")),
        name="dsa",
    )(qd, kd, vd, qi, ki, wi, tri)


def _s5_kernel(rows, u_ref, wt_ref, ws_ref, wo_ref, lam_ref, d_ref, o_ref, uf_ref, yf_ref):
    half = S5_GB * S5_STATE
    rowi = lax.broadcasted_iota(jnp.int32, (rows, half), 0)
    sub = rowi & (SUBLANES - 1)
    for gb in range(S5_NGB):
        gs = slice(gb * LANE, (gb + 1) * LANE)
        uf_ref[gb] = u_ref[:, gs].astype(F32)
        uf = jnp.concatenate([uf_ref[gb, pl.ds(a, rows, stride=S5_CHUNK), :] for a in range(S5_CHUNK)], axis=1)
        u = uf.astype(BF16)
        y = _dot(u, wt_ref[gb])
        st = _dot(u, ws_ref[gb])
        xr = st[:, :half]
        xi = st[:, half:]
        lam = lam_ref[gb]
        lr = lam[0:1]
        li = lam[1:2]
        pw_r, pw_i = [lr], [li]
        for _ in range(SUBLANES - 1):
            pw_r, pw_i = pw_r + [pw_r[-1] * lr - pw_i[-1] * li], pw_i + [pw_r[-1] * li + pw_i[-1] * lr]
        pw_r = jnp.concatenate(pw_r, axis=0)
        pw_i = jnp.concatenate(pw_i, axis=0)
        d = 1
        while d < SUBLANES:
            keep = sub >= d
            sr = jnp.where(keep, pltpu.roll(xr, d, 0), 0.0)
            si = jnp.where(keep, pltpu.roll(xi, d, 0), 0.0)
            xr, xi = xr + lr * sr - li * si, xi + lr * si + li * sr
            lr, li = lr * lr - li * li, 2.0 * lr * li
            d *= 2
        cr = jnp.zeros((1, half), F32)
        ci = jnp.zeros((1, half), F32)
        gr, gi = [], []
        for g in range(rows // SUBLANES):
            rs = slice(g * SUBLANES, (g + 1) * SUBLANES)
            hr = xr[rs] + pw_r * cr - pw_i * ci
            hi = xi[rs] + pw_r * ci + pw_i * cr
            gr.append(hr)
            gi.append(hi)
            cr = hr[SUBLANES - 1:SUBLANES]
            ci = hi[SUBLANES - 1:SUBLANES]
        xr = jnp.concatenate(gr, axis=0)
        xi = jnp.concatenate(gi, axis=0)
        keep = rowi >= 1
        pr = jnp.where(keep, pltpu.roll(xr, 1, 0), 0.0)
        pi = jnp.where(keep, pltpu.roll(xi, 1, 0), 0.0)
        xp = jnp.concatenate([pr, pi], axis=1).astype(BF16)
        yg = _gelu_tanh(y + _dot(xp, wo_ref[gb]) + d_ref[gb] * uf)
        for a in range(S5_CHUNK):
            yf_ref[gb, pl.ds(a, rows, stride=S5_CHUNK), :] = yg[:, a * LANE:(a + 1) * LANE]
        o_ref[:, gs] = yf_ref[gb].astype(o_ref.dtype)


def _s5_call(us, pw, l, bsz, seq):
    n = us.shape[0]
    blk = pl.BlockSpec((seq, D_S5), lambda b: (b, 0))
    wshape = pw["s5_wt"].shape[1:]
    return pl.pallas_call(
        functools.partial(_s5_kernel, seq // S5_CHUNK),
        grid=(bsz,),
        in_specs=[blk, _layer(wshape, l), _layer(wshape, l), _layer(wshape, l),
                  _layer(pw["s5_lam"].shape[1:], l), _layer(pw["s5_d"].shape[1:], l)],
        out_specs=blk,
        out_shape=jax.ShapeDtypeStruct((n, D_S5), BF16),
        scratch_shapes=[pltpu.VMEM((S5_NGB, seq, LANE), F32), pltpu.VMEM((S5_NGB, seq, LANE), F32)],
        compiler_params=_params(("arbitrary",)),
        name="s5",
    )(us, pw["s5_wt"], pw["s5_ws"], pw["s5_wo"], pw["s5_lam"], pw["s5_d"])


def _merge_kernel(x_ref, mod_ref, g_ref, ya_ref, yb_ref, yc_ref, yd_ref,
                  wg_ref, wb_ref, wglu_ref, bglu_ref, wout_ref, o_ref):
    x = x_ref[...]
    m = mod_ref[0]
    d = x.shape[1]
    u = _rms_mod(x, g_ref[...], m[3:4], m[4:5]).astype(BF16)
    yd = yd_ref[...]
    ydg = (yd.astype(F32) * _sigmoid(_dot(yd, wglu_ref[...]) + bglu_ref[...])).astype(BF16)
    ys = (ya_ref[...], yb_ref[...], yc_ref[...], ydg)
    merged = jnp.zeros(x.shape, F32)
    for nb in range(N_BRANCH):
        gate = _sigmoid(_dot(u, wg_ref[:, nb * d:(nb + 1) * d]))
        merged = merged + gate * _dot(ys[nb], wb_ref[nb])
    o_ref[...] = x + (1.0 + m[5:6]) * _dot(merged.astype(BF16), wout_ref[...])


def _merge_call(x, mod, ya, yb, yc, yd, pw, l, seq, tm):
    n, d = x.shape
    row = lambda w: pl.BlockSpec((tm, w), lambda i: (i, 0))
    return pl.pallas_call(
        _merge_kernel,
        grid=(n // tm,),
        in_specs=[row(d), _mod_spec(l, tm, seq, d), _layer((1, d), l, 1),
                  row(BRANCH_W), row(BRANCH_W), row(BRANCH_W), row(BRANCH_W),
                  _layer((d, N_BRANCH * d), l), _layer((N_BRANCH, BRANCH_W, d), l),
                  _layer((D_S5, D_S5), l), _layer((1, D_S5), l), _layer((d, d), l)],
        out_specs=row(d),
        out_shape=jax.ShapeDtypeStruct((n, d), F32),
        compiler_params=_params(("arbitrary",)),
        name="merge",
    )(x, mod, pw["norm_g"], ya, yb, yc, yd, pw["w_gate"], pw["w_branch"], pw["w_glu"], pw["b_glu"], pw["w_out"])


def _to_slots(w, nheads, hdim):
    lead = w.shape[:-1]
    wh = w.reshape(lead + (nheads, hdim))
    wh = jnp.pad(wh, [(0, 0)] * (len(lead) + 1) + [(0, LANE - hdim)])
    return wh.reshape(lead + (nheads * LANE,))


def _pad_lane(v, width=LANE):
    return jnp.pad(v, [(0, 0)] * (v.ndim - 1) + [(0, width - v.shape[-1])])


def _rope_table(positions, rot, period):
    j = jnp.arange(LANE) % period
    inv = jnp.where(j < rot, ROPE_THETA ** (-(2 * (j % (rot // 2))).astype(F32) / rot), 0.0)
    ang = positions.astype(F32).reshape(-1, 1) * inv[None, :]
    return jnp.cos(ang), jnp.sin(ang)


def _rot_matrix(rot, period):
    src = jnp.arange(LANE)[:, None]
    dst = jnp.arange(LANE)[None, :]
    j = dst % period
    half = rot // 2
    return (jnp.where((j < half) & (src == dst + half), -1.0, 0.0)
            + jnp.where((j >= half) & (j < rot) & (src == dst - half), 1.0, 0.0)).astype(BF16)


def _s5_weights(lam_re, lam_im, log_dt, b_re, b_im, c_re, c_im, dvec):
    hp = lax.Precision.HIGHEST
    g, p = lam_re.shape
    ch = S5_CHUNK
    dt = jnp.exp(log_dt)[:, None]
    mag = jnp.exp(lam_re * dt)
    ar, ai = mag * jnp.cos(lam_im * dt), mag * jnp.sin(lam_im * dt)
    den = lam_re * lam_re + lam_im * lam_im
    nr, ni = ar - 1.0, ai
    f_re = (nr * lam_re + ni * lam_im) / den
    f_im = (ni * lam_re - nr * lam_im) / den
    bb_re = f_re[..., None] * b_re - f_im[..., None] * b_im
    bb_im = f_re[..., None] * b_im + f_im[..., None] * b_re
    pr, pi = [jnp.ones_like(ar)], [jnp.zeros_like(ar)]
    for _ in range(ch):
        pr, pi = pr + [pr[-1] * ar - pi[-1] * ai], pi + [pr[-1] * ai + pi[-1] * ar]
    pr, pi = jnp.stack(pr), jnp.stack(pi)
    mr = pr[..., None] * bb_re - pi[..., None] * bb_im
    mi = pr[..., None] * bb_im + pi[..., None] * bb_re
    kern = (jnp.einsum('gjp,tgpi->tgji', c_re, mr, precision=hp)
            - jnp.einsum('gjp,tgpi->tgji', c_im, mi, precision=hp))

    def group_diag(x):
        r, c = x.shape[-2:]
        x = x.reshape(x.shape[:-3] + (S5_NGB, S5_GB * r, c))
        on_diag = jnp.arange(S5_GB * r)[:, None] // r == jnp.arange(S5_GB * c)[None, :] // c
        return jnp.tile(x, (1,) * (x.ndim - 1) + (S5_GB,)) * on_diag.astype(x.dtype)

    a_in = jnp.arange(ch)[:, None]
    a_out = jnp.arange(ch)[None, :]
    lag = jnp.clip(a_out - a_in, 0, ch)
    bd_k = group_diag(kern.transpose(0, 1, 3, 2)).astype(BF16)
    toe = bd_k[lag] * (a_out >= a_in)[..., None, None, None].astype(BF16)
    w_toe = toe.transpose(2, 0, 3, 1, 4).reshape(S5_NGB, ch * LANE, ch * LANE)
    rev = jnp.arange(ch - 1, -1, -1)
    st = jnp.concatenate([group_diag(mr[:ch].transpose(0, 1, 3, 2)), group_diag(mi[:ch].transpose(0, 1, 3, 2))],
                         axis=-1).astype(BF16)
    w_st = st[rev].transpose(1, 0, 2, 3).reshape(S5_NGB, ch * LANE, 2 * S5_GB * p)
    pr1, pi1 = pr[1:], pi[1:]
    co_re = c_re[None] * pr1[:, :, None, :] - c_im[None] * pi1[:, :, None, :]
    co_im = -(c_re[None] * pi1[:, :, None, :] + c_im[None] * pr1[:, :, None, :])
    co = jnp.stack([group_diag(co_re.transpose(0, 1, 3, 2)), group_diag(co_im.transpose(0, 1, 3, 2))],
                   axis=2).astype(BF16)
    w_out = co.reshape(ch, S5_NGB, 2 * S5_GB * p, LANE).transpose(1, 2, 0, 3).reshape(
        S5_NGB, 2 * S5_GB * p, ch * LANE)
    lam_c = jnp.stack([pr[ch].reshape(S5_NGB, S5_GB * p), pi[ch].reshape(S5_NGB, S5_GB * p)], axis=1)
    d_t = jnp.tile(dvec.reshape(S5_NGB, 1, LANE), (1, 1, ch))
    return w_toe, w_st, w_out, lam_c, d_t


def _block_diag(w):
    h, a, b = w.shape
    eye = jnp.eye(h, dtype=w.dtype)
    return jnp.einsum('hij,hg->higj', w, eye).reshape(h * a, h * b)


def _prep_params(p):
    w_in = p["w_in"]
    nl = w_in.shape[0]
    offs = [0]
    for s in IN_SPLITS:
        offs.append(offs[-1] + s)
    seg = [w_in[:, :, offs[k]:offs[k + 1]] for k in range(len(IN_SPLITS))]
    (w_xr, w_gr, w_ql, w_kvl, w_kpe, w_qd, w_kd, w_vd, w_qi, w_ki, w_wi, w_us, w_gate) = seg
    w_all = jnp.concatenate([w_xr, w_gr, w_ql, w_kvl, _pad_lane(w_kpe), w_qd, _pad_lane(w_kd), _pad_lane(w_vd),
                             w_qi, jnp.tile(w_ki, (1, 1, IDX_HEADS)), _pad_lane(w_wi), w_us],
                            axis=2).astype(BF16)
    assert w_all.shape[2] == _C_END

    wuq = _to_slots(p["mla_w_uq"], MLA_HEADS, MLA_QK).astype(BF16)
    wkv = p["mla_w_ukv"].reshape(nl, MLA_KV_LORA, MLA_HEADS, MLA_NOPE + MLA_V)
    kn = jnp.pad(wkv[..., :MLA_NOPE], ((0, 0), (0, 0), (0, 0), (MLA_ROPE, LANE - MLA_QK)))
    wv = wkv[..., MLA_NOPE:].reshape(nl, MLA_KV_LORA, MLA_HEADS // 2, 2, MLA_V)
    zv = jnp.zeros_like(wv[:, :, :, 0])
    wv = jnp.stack([jnp.concatenate([wv[:, :, :, 0], zv], axis=-1),
                    jnp.concatenate([zv, wv[:, :, :, 1]], axis=-1)], axis=3)
    wukv = jnp.concatenate([kn.reshape(nl, MLA_KV_LORA, MLA_HEADS * LANE),
                            wv.reshape(nl, MLA_KV_LORA, MLA_HEADS * LANE)], axis=2).astype(BF16)

    dq = p["dsa_qk_gain"][:, 0]
    vec = jnp.stack([_pad_lane(v, MLA_Q_LORA) for v in (
        p["mla_q_norm"], p["mla_kv_norm"], p["mla_qk_gain"][:, 0], p["mla_qk_gain"][:, 1],
        jnp.concatenate([dq, dq], axis=-1), p["dsa_qk_gain"][:, 1])], axis=1)

    s5_wt, s5_ws, s5_wo, s5_lam, s5_d = jax.vmap(_s5_weights)(
        p["s5_lambda_re"], p["s5_lambda_im"], p["s5_log_dt"], p["s5_b_re"], p["s5_b_im"],
        p["s5_c_re"], p["s5_c_im"], p["s5_d"])

    return dict(
        w_all=w_all, wuq=wuq, wukv=wukv, vec=vec,
        rot=jnp.stack([_rot_matrix(MLA_ROPE, LANE), _rot_matrix(DSA_ROT, DSA_HEAD_DIM),
                       _rot_matrix(IDX_ROT, IDX_DIM)]),
        conv_w=p["conv_w"],
        rg_vec=jnp.stack([p["conv_b"], p["rg_ba"], p["rg_bx"], p["rg_lambda"]], axis=1),
        rg_wa=jax.vmap(_block_diag)(p["rg_wa"]).astype(BF16),
        rg_wx=jax.vmap(_block_diag)(p["rg_wx"]).astype(BF16),
        s5_wt=s5_wt, s5_ws=s5_ws, s5_wo=s5_wo, s5_lam=s5_lam, s5_d=s5_d,
        w_gate=w_gate.astype(BF16),
        w_branch=p["w_branch"].astype(BF16),
        w_glu=p["s5_w_glu"].astype(BF16),
        b_glu=p["s5_b_glu"][:, None, :],
        w_out=p["w_out"].astype(BF16),
        ffn_w1=p["ffn_w1"].astype(BF16), ffn_w3=p["ffn_w3"].astype(BF16), ffn_w2=p["ffn_w2"].astype(BF16),
        norm_g=p["norm_g"][:, :, None, :],
    )


def _pick_tile(n, pref):
    t = pref
    while n % t:
        t //= 2
    return t


def kernel(x, c, positions, ada_w, ada_b, norm_g, ffn_w1, ffn_w3, ffn_w2, w_in,
           conv_w, conv_b, rg_wa, rg_ba, rg_wx, rg_bx, rg_lambda,
           mla_q_norm, mla_w_uq, mla_kv_norm, mla_w_ukv, mla_qk_gain, dsa_qk_gain,
           s5_lambda_re, s5_lambda_im, s5_log_dt, s5_b_re, s5_b_im, s5_c_re, s5_c_im,
           s5_d, s5_w_glu, s5_b_glu, w_branch, w_out):
    p = dict(norm_g=norm_g, ffn_w1=ffn_w1, ffn_w3=ffn_w3, ffn_w2=ffn_w2, w_in=w_in,
             conv_w=conv_w, conv_b=conv_b, rg_wa=rg_wa, rg_ba=rg_ba, rg_wx=rg_wx, rg_bx=rg_bx,
             rg_lambda=rg_lambda, mla_q_norm=mla_q_norm, mla_w_uq=mla_w_uq, mla_kv_norm=mla_kv_norm,
             mla_w_ukv=mla_w_ukv, mla_qk_gain=mla_qk_gain, dsa_qk_gain=dsa_qk_gain,
             s5_lambda_re=s5_lambda_re, s5_lambda_im=s5_lambda_im, s5_log_dt=s5_log_dt,
             s5_b_re=s5_b_re, s5_b_im=s5_b_im, s5_c_re=s5_c_re, s5_c_im=s5_c_im, s5_d=s5_d,
             s5_w_glu=s5_w_glu, s5_b_glu=s5_b_glu, w_branch=w_branch, w_out=w_out)
    bsz, seq, d = x.shape
    n = bsz * seq
    depth = ada_w.shape[0]
    assert seq % (S5_CHUNK * 8) == 0 and seq % _DSA_QB == 0

    tm = _pick_tile(seq, PROJ_ROWS)
    tw = _pick_tile(seq, WIDE_ROWS)
    tc = _pick_tile(seq, SCAN_ROWS)
    bq = _pick_tile(seq, ATTN_ROWS)

    mod = _ada_call(c, ada_w, ada_b)
    pw = _prep_params(p)
    tabs = (_rope_table(positions, MLA_ROPE, LANE) + _rope_table(positions, DSA_ROT, DSA_HEAD_DIM)
            + _rope_table(positions, IDX_ROT, IDX_DIM))
    tri = (jnp.arange(LANE)[:, None] < jnp.arange(LANE)[None, :]).astype(BF16)

    xf = x.reshape(n, d)
    for l in range(depth):
        xf = _ffn_call(xf, mod, pw, l, 0, seq, tw)
        (xr, gr, qm, km, vm, qd, kd, vd, qi, ki, wi, us) = _inproj_call(xf, mod, pw, l, tabs, seq, tm)
        ya = _rglru_call(xr, gr, pw, l, bsz, seq, tc)
        yb = _flash_call(qm, km, vm, bsz, seq, bq)
        yc = _dsa_call(qd, kd, vd, qi, ki, wi, tri, bsz, seq)
        yd = _s5_call(us, pw, l, bsz, seq)
        xf = _merge_call(xf, mod, ya, yb, yc, yd, pw, l, seq, tw)
        xf = _ffn_call(xf, mod, pw, l, 1, seq, tw)
    return xf.reshape(bsz, seq, d)
```

```python
import functools
import math

import jax
import jax.numpy as jnp
from jax import lax
from jax.experimental import pallas as pl
from jax.experimental.pallas import tpu as pltpu

F32 = jnp.float32
BF16 = jnp.bfloat16

D_MODEL = 1024
EPS = 1e-6
ROPE_THETA = 500000.0
D_FF = 2816
N_ADA = 9

D_RNN = 512
RNN_HEADS = 8
RNN_HEAD_DIM = D_RNN // RNN_HEADS
CONV_WIDTH = 4
LRU_C = 8.0

MLA_HEADS = 8
MLA_NOPE = 64
MLA_ROPE = 32
MLA_V = 64
MLA_QK = MLA_ROPE + MLA_NOPE
MLA_Q_LORA = 256
MLA_KV_LORA = 128

DSA_HEADS = 8
DSA_HEAD_DIM = 64
DSA_ROT = DSA_HEAD_DIM // 4
IDX_HEADS = 8
IDX_DIM = 32
IDX_ROT = IDX_DIM // 4
TOPK_MAX = 256

S5_GROUP = 16
S5_GROUPS = 32
D_S5 = S5_GROUP * S5_GROUPS
S5_STATE = 64
S5_CHUNK = 8
S5_GB = 8
S5_NGB = S5_GROUPS // S5_GB

N_BRANCH = 4
BRANCH_W = 512
IN_SPLITS = (D_RNN, D_RNN, MLA_Q_LORA, MLA_KV_LORA, MLA_ROPE,
             DSA_HEADS * DSA_HEAD_DIM, DSA_HEAD_DIM, DSA_HEAD_DIM,
             IDX_HEADS * IDX_DIM, IDX_DIM, IDX_HEADS, D_S5, N_BRANCH * D_MODEL)

LANE = 128
SUBLANES = 8
NEG_BIG = -1e30
VMEM_LIMIT = 56 * 1024 * 1024

PROJ_ROWS = 512
WIDE_ROWS = 1024
SCAN_ROWS = 256
ATTN_ROWS = 512
ADA_COLS = 2304
DSA_KEY_STEP = 512

_C_XR = 0
_C_GR = _C_XR + D_RNN
_C_QL = _C_GR + D_RNN
_C_KVL = _C_QL + MLA_Q_LORA
_C_KPE = _C_KVL + MLA_KV_LORA
_C_QD = _C_KPE + LANE
_C_KD = _C_QD + DSA_HEADS * DSA_HEAD_DIM
_C_VD = _C_KD + LANE
_C_QI = _C_VD + LANE
_C_KI = _C_QI + IDX_HEADS * IDX_DIM
_C_WI = _C_KI + IDX_HEADS * IDX_DIM
_C_US = _C_WI + LANE
_C_END = _C_US + D_S5


def _dot(a, b):
    return jnp.dot(a, b, preferred_element_type=F32)


def _dot_nt(a, b):
    return lax.dot_general(a, b, (((1,), (1,)), ((), ())), preferred_element_type=F32)


def _sigmoid(x):
    return jax.nn.sigmoid(x)


def _gelu_tanh(x):
    return 0.5 * x * (1.0 + jnp.tanh(0.7978845608028654 * (x + 0.044715 * (x * x * x))))


def _rms_mod(x, g, shift, scale):
    ms = jnp.mean(x * x, axis=-1, keepdims=True)
    y = x * lax.rsqrt(ms + EPS) * g
    return y * (1.0 + scale) + shift


def _resident(shape):
    return pl.BlockSpec(shape, lambda *_: (0,) * len(shape), pipeline_mode=pl.Buffered(1))


def _layer(tail, *lead):
    return pl.BlockSpec((None,) * len(lead) + tuple(tail), lambda *_: tuple(lead) + (0,) * len(tail),
                        pipeline_mode=pl.Buffered(1))


def _mod_spec(l, tm, seq, d):
    return pl.BlockSpec((None, 1, N_ADA, d), lambda i: (l, i * tm // seq, 0, 0))


def _params(sem):
    return pltpu.CompilerParams(dimension_semantics=sem, vmem_limit_bytes=VMEM_LIMIT)


def _ada_kernel(c_ref, w_ref, b_ref, o_ref):
    c = c_ref[...]
    a = c * _sigmoid(c)
    w = w_ref[0]
    a_hi = a.astype(BF16)
    a_lo = (a - a_hi.astype(F32)).astype(BF16)
    w_hi = w.astype(BF16)
    w_lo = (w - w_hi.astype(F32)).astype(BF16)
    o_ref[0] = _dot(a_hi, w_hi) + _dot(a_lo, w_hi) + _dot(a_hi, w_lo) + b_ref[0]


def _ada_call(c, ada_w, ada_b):
    nl, d, n9 = ada_w.shape
    b = c.shape[0]
    tn = ADA_COLS
    out = pl.pallas_call(
        _ada_kernel,
        grid=(nl, n9 // tn),
        in_specs=[
            pl.BlockSpec((b, d), lambda l, j: (0, 0)),
            pl.BlockSpec((1, d, tn), lambda l, j: (l, 0, j)),
            pl.BlockSpec((1, 1, tn), lambda l, j: (l, 0, j)),
        ],
        out_specs=pl.BlockSpec((1, b, tn), lambda l, j: (l, 0, j)),
        out_shape=jax.ShapeDtypeStruct((nl, b, n9), F32),
        compiler_params=_params(("arbitrary", "arbitrary")),
        name="ada_mod",
    )(c, ada_w, ada_b.reshape(nl, 1, n9))
    return out.reshape(nl, b, N_ADA, d)


_FFN_CHUNK = 256


def _ffn_kernel(row0, x_ref, mod_ref, g_ref, w1_ref, w3_ref, w2_ref, o_ref, h_ref):
    x = x_ref[...]
    m = mod_ref[0]
    u = _rms_mod(x, g_ref[...], m[row0:row0 + 1], m[row0 + 1:row0 + 2]).astype(BF16)
    for c in range(0, w1_ref.shape[1], _FFN_CHUNK):
        h1 = _dot(u, w1_ref[:, c:c + _FFN_CHUNK])
        h3 = _dot(u, w3_ref[:, c:c + _FFN_CHUNK])
        h_ref[:, c:c + _FFN_CHUNK] = (h1 * _sigmoid(h1) * h3).astype(BF16)
    o_ref[...] = x + 0.5 * (1.0 + m[row0 + 2:row0 + 3]) * _dot(h_ref[...], w2_ref[...])


def _ffn_call(x, mod, pw, l, j, seq, tm):
    n, d = x.shape
    f = pw["ffn_w1"].shape[-1]
    return pl.pallas_call(
        functools.partial(_ffn_kernel, 6 * j),
        grid=(n // tm,),
        in_specs=[
            pl.BlockSpec((tm, d), lambda i: (i, 0)),
            _mod_spec(l, tm, seq, d),
            _layer((1, d), l, 2 * j),
            _layer((d, f), l, j), _layer((d, f), l, j), _layer((f, d), l, j),
        ],
        out_specs=pl.BlockSpec((tm, d), lambda i: (i, 0)),
        out_shape=jax.ShapeDtypeStruct((n, d), F32),
        scratch_shapes=[pltpu.VMEM((tm, f), BF16)],
        compiler_params=_params(("arbitrary",)),
        name="ffn",
    )(x, mod, pw["norm_g"], pw["ffn_w1"], pw["ffn_w3"], pw["ffn_w2"])


def _rope_tiles(tiles, cos, sin, rot):
    rows = tiles[0].shape[0]
    partner = _dot(jnp.concatenate(tiles, axis=0).astype(BF16), rot)
    return [t * cos + partner[i * rows:(i + 1) * rows] * sin for i, t in enumerate(tiles)]


def _inproj_kernel(x_ref, mod_ref, g_ref, w_ref, wuq_ref, wukv_ref, vec_ref, rot_ref,
                   cm_ref, sm_ref, cd_ref, sd_ref, ci_ref, si_ref,
                   xr_ref, gr_ref, qm_ref, km_ref, vm_ref, qd_ref, kd_ref, vd_ref,
                   qi_ref, ki_ref, wi_ref, us_ref):
    x = x_ref[...]
    m = mod_ref[0]
    u = _rms_mod(x, g_ref[...], m[3:4], m[4:5]).astype(BF16)
    vec = vec_ref[...]
    lane = lax.broadcasted_iota(jnp.int32, (1, LANE), 1)
    z = _dot(u, w_ref[...])

    xr_ref[...] = z[:, _C_XR:_C_GR].astype(xr_ref.dtype)
    gr_ref[...] = z[:, _C_GR:_C_QL].astype(gr_ref.dtype)
    wi_ref[...] = z[:, _C_WI:_C_US]
    us_ref[...] = z[:, _C_US:_C_END].astype(us_ref.dtype)

    ql = z[:, _C_QL:_C_KVL]
    qn = (ql * lax.rsqrt(jnp.mean(ql * ql, axis=-1, keepdims=True) + EPS)
          * vec[0:1, :MLA_Q_LORA]).astype(BF16)
    q2 = _dot(qn, wuq_ref[...])
    kvl = z[:, _C_KVL:_C_KPE]
    kvn = (kvl * lax.rsqrt(jnp.mean(kvl * kvl, axis=-1, keepdims=True) + EPS)
           * vec[1:2, :MLA_KV_LORA]).astype(BF16)
    kv = _dot(kvn, wukv_ref[...])
    kpe = z[:, _C_KPE:_C_QD]
    gq = vec[2:3, :LANE]
    gk = vec[3:4, :LANE]
    roped = _rope_tiles([kpe * gk] + [q2[:, h * LANE:(h + 1) * LANE] * gq for h in range(MLA_HEADS)],
                        cm_ref[...], sm_ref[...], rot_ref[0])
    kpe_rot = roped[0]
    kpe_ss = jnp.sum(kpe * kpe, axis=-1, keepdims=True)
    q_tiles, k_tiles = [], []
    for h in range(MLA_HEADS):
        hs = slice(h * LANE, (h + 1) * LANE)
        qs = q2[:, hs]
        s = lax.rsqrt(jnp.sum(qs * qs, axis=-1, keepdims=True) * (1.0 / MLA_QK) + EPS) * MLA_QK ** -0.5
        q_tiles.append((s * roped[1 + h]).astype(qm_ref.dtype))
        kn = kv[:, hs]
        s = lax.rsqrt((jnp.sum(kn * kn, axis=-1, keepdims=True) + kpe_ss) * (1.0 / MLA_QK) + EPS)
        k_tiles.append((s * (kn * gk + kpe_rot)).astype(km_ref.dtype))
    qm_ref[...] = jnp.concatenate(q_tiles, axis=1)
    km_ref[...] = jnp.concatenate(k_tiles, axis=1)
    nslot = MLA_HEADS * LANE
    vlane = lax.broadcasted_iota(jnp.int32, (1, nslot), 1)
    v_is_low = (vlane // LANE) % 2 == 0
    ones_half = jnp.where(((vlane % LANE) < MLA_V) == v_is_low, 0.0, 1.0)
    vm_ref[...] = (kv[:, nslot:] + ones_half).astype(vm_ref.dtype)

    gd = vec[4:5, :LANE]
    low = lane < DSA_HEAD_DIM
    kd = z[:, _C_KD:_C_VD]
    xqs = [z[:, _C_QD + t * LANE:_C_QD + (t + 1) * LANE] for t in range(DSA_HEADS // 2)]
    roped = _rope_tiles([xq * gd for xq in xqs] + [kd * vec[5:6, :LANE]], cd_ref[...], sd_ref[...], rot_ref[1])
    d_tiles = []
    for t in range(DSA_HEADS // 2):
        x2 = xqs[t] * xqs[t]
        ss_even = jnp.sum(jnp.where(low, x2, 0.0), axis=-1, keepdims=True)
        ss_odd = jnp.sum(jnp.where(low, 0.0, x2), axis=-1, keepdims=True)
        y = roped[t]
        s_even = lax.rsqrt(ss_even * (1.0 / DSA_HEAD_DIM) + EPS) * DSA_HEAD_DIM ** -0.5
        s_odd = lax.rsqrt(ss_odd * (1.0 / DSA_HEAD_DIM) + EPS) * DSA_HEAD_DIM ** -0.5
        d_tiles.append(jnp.where(low, y * s_even, 0.0).astype(qd_ref.dtype))
        d_tiles.append(jnp.where(low, pltpu.roll(y, DSA_HEAD_DIM, 1) * s_odd, 0.0).astype(qd_ref.dtype))
    qd_ref[...] = jnp.concatenate(d_tiles, axis=1)
    s = lax.rsqrt(jnp.sum(kd * kd, axis=-1, keepdims=True) * (1.0 / DSA_HEAD_DIM) + EPS)
    kd_ref[...] = (s * roped[-1]).astype(kd_ref.dtype)
    vd_ref[...] = (z[:, _C_VD:_C_QI] + jnp.where(low, 0.0, 1.0)).astype(vd_ref.dtype)

    ntile = IDX_HEADS * IDX_DIM // LANE
    roped = _rope_tiles([z[:, c + t * LANE:c + (t + 1) * LANE] for c in (_C_QI, _C_KI) for t in range(ntile)],
                        ci_ref[...], si_ref[...], rot_ref[2])
    qi_ref[...] = jnp.concatenate(roped[:ntile], axis=1).astype(qi_ref.dtype)
    ki_ref[...] = jnp.concatenate(roped[ntile:], axis=1).astype(ki_ref.dtype)


def _inproj_call(x, mod, pw, l, tabs, seq, tm):
    n, d = x.shape
    row = lambda w: pl.BlockSpec((tm, w), lambda i: (i, 0))
    out_widths = [(D_RNN, BF16), (D_RNN, BF16), (MLA_HEADS * LANE, BF16), (MLA_HEADS * LANE, BF16),
                  (MLA_HEADS * LANE, BF16), (DSA_HEADS * LANE, BF16), (LANE, BF16), (LANE, BF16),
                  (IDX_HEADS * IDX_DIM, BF16), (IDX_HEADS * IDX_DIM, BF16), (LANE, F32), (D_S5, BF16)]
    return pl.pallas_call(
        _inproj_kernel,
        grid=(n // tm,),
        in_specs=[
            row(d), _mod_spec(l, tm, seq, d), _layer((1, d), l, 1),
            _layer(pw["w_all"].shape[1:], l), _layer(pw["wuq"].shape[1:], l),
            _layer(pw["wukv"].shape[1:], l), _layer(pw["vec"].shape[1:], l),
            _resident(pw["rot"].shape),
        ] + [row(LANE)] * 6,
        out_specs=[row(w) for w, _ in out_widths],
        out_shape=[jax.ShapeDtypeStruct((n, w), dt) for w, dt in out_widths],
        compiler_params=_params(("arbitrary",)),
        name="in_proj",
    )(x, mod, pw["norm_g"], pw["w_all"], pw["wuq"], pw["wukv"], pw["vec"], pw["rot"], *tabs)


def _rglru_kernel(tc, x_ref, gate_ref, cw_ref, vec_ref, wa_ref, wx_ref, o_ref, xs_ref, h_ref):
    j = pl.program_id(1)

    @pl.when(j == 0)
    def _():
        xs_ref[0:8, :] = jnp.zeros((8, D_RNN), F32)
        h_ref[...] = jnp.zeros(h_ref.shape, F32)

    xs_ref[8:8 + tc, :] = x_ref[...].astype(F32)
    cw = cw_ref[...]
    vec = vec_ref[...]
    xc = vec[0:1]
    for k in range(CONV_WIDTH):
        xc = xc + cw[k:k + 1] * xs_ref[pl.ds(8 - (CONV_WIDTH - 1) + k, tc), :]
    xs_ref[0:8, :] = xs_ref[tc:tc + 8, :]

    xb = xc.astype(BF16)
    r = _sigmoid(_dot(xb, wa_ref[...]) + vec[1:2])
    ig = _sigmoid(_dot(xb, wx_ref[...]) + vec[2:3])
    nl = -vec[3:4]
    softplus = jnp.maximum(nl, 0.0) + jnp.log(1.0 + jnp.exp(-jnp.abs(nl)))
    log_a = (-LRU_C) * r * softplus
    a = jnp.exp(log_a)
    z = 2.0 * log_a
    series = -z * (1.0 + z * (0.5 + z * (1.0 / 6.0 + z * (1.0 / 24.0 + z * (1.0 / 120.0 + z * (1.0 / 720.0))))))
    nem1 = jnp.where(z > -0.25, series, 1.0 - jnp.exp(z))
    b = jnp.sqrt(nem1) * ig * xc

    sub = lax.broadcasted_iota(jnp.int32, (tc, D_RNN), 0) & (SUBLANES - 1)
    d = 1
    while d < SUBLANES:
        keep = sub >= d
        a_s = jnp.where(keep, pltpu.roll(a, d, 0), 1.0)
        b_s = jnp.where(keep, pltpu.roll(b, d, 0), 0.0)
        b = a * b_s + b
        a = a * a_s
        d *= 2
    carry = h_ref[...]
    groups = []
    for g in range(tc // SUBLANES):
        rows = slice(g * SUBLANES, (g + 1) * SUBLANES)
        hg = b[rows] + a[rows] * carry
        groups.append(hg)
        carry = hg[SUBLANES - 1:SUBLANES]
    h = jnp.concatenate(groups, axis=0)
    h_ref[...] = carry
    o_ref[...] = (h * _gelu_tanh(gate_ref[...].astype(F32))).astype(o_ref.dtype)


def _rglru_call(xr, gr, pw, l, bsz, seq, tc):
    n = xr.shape[0]
    nt = seq // tc
    row = pl.BlockSpec((tc, D_RNN), lambda b, j: (b * nt + j, 0))
    return pl.pallas_call(
        functools.partial(_rglru_kernel, tc),
        grid=(bsz, nt),
        in_specs=[row, row,
                  _layer((CONV_WIDTH, D_RNN), l), _layer((4, D_RNN), l),
                  _layer((D_RNN, D_RNN), l), _layer((D_RNN, D_RNN), l)],
        out_specs=row,
        out_shape=jax.ShapeDtypeStruct((n, D_RNN), BF16),
        scratch_shapes=[pltpu.VMEM((tc + 8, D_RNN), F32), pltpu.VMEM((1, D_RNN), F32)],
        compiler_params=_params(("arbitrary", "arbitrary")),
        name="rglru",
    )(xr, gr, pw["conv_w"], pw["rg_vec"], pw["rg_wa"], pw["rg_wx"])


def _flash_kernel(q_ref, k_ref, v_ref, o_ref, m_ref, acc_ref):
    i = pl.program_id(1)
    bq = q_ref.shape[0]
    bk = bq

    m_ref[...] = jnp.full(m_ref.shape, NEG_BIG, F32)
    acc_ref[...] = jnp.zeros(acc_ref.shape, F32)

    def step(j, masked):
        rows = pl.ds(pl.multiple_of(j * bk, bk), bk)
        if masked:
            causal = (lax.broadcasted_iota(jnp.int32, (bq, bk), 1)
                      <= lax.broadcasted_iota(jnp.int32, (bq, bk), 0))
        scores = [_dot_nt(q_ref[:, h * LANE:(h + 1) * LANE], k_ref[rows, h * LANE:(h + 1) * LANE])
                  for h in range(MLA_HEADS)]
        m_out, alphas, probs = [], [], []
        for h in range(MLA_HEADS):
            s = scores[h]
            if masked:
                s = jnp.where(causal, s, NEG_BIG)
            m_prev = m_ref[h]
            m_new = jnp.maximum(m_prev, jnp.max(s, axis=1, keepdims=True))
            probs.append(jnp.exp(s - jnp.concatenate([m_new] * (bk // LANE), axis=1)).astype(BF16))
            alphas.append(jnp.exp(m_prev - m_new))
            m_out.append(m_new)
        m_ref[...] = jnp.stack(m_out)
        acc_ref[...] = jnp.stack([alphas[h] * acc_ref[h] + _dot(probs[h], v_ref[rows, h * LANE:(h + 1) * LANE])
                                  for h in range(MLA_HEADS)])

    def below_diagonal(j, carry):
        step(j, False)
        return carry

    lax.fori_loop(0, i, below_diagonal, 0)
    step(i, True)
    lane = lax.broadcasted_iota(jnp.int32, (bq, LANE), 1)
    outs = []
    for pr in range(MLA_HEADS // 2):
        even = acc_ref[2 * pr]
        odd = acc_ref[2 * pr + 1]
        outs.append(jnp.where(lane < MLA_V, even / pltpu.roll(even, MLA_V, 1),
                              odd / pltpu.roll(odd, MLA_V, 1)))
    o_ref[...] = jnp.concatenate(outs, axis=1).astype(o_ref.dtype)


def _flash_call(q, k, v, bsz, seq, bq):
    n = q.shape[0]
    nq = seq // bq
    whole = pl.BlockSpec((seq, MLA_HEADS * LANE), lambda b, i: (b, 0))
    return pl.pallas_call(
        _flash_kernel,
        grid=(bsz, nq),
        in_specs=[pl.BlockSpec((bq, MLA_HEADS * LANE), lambda b, i: (b * nq + i, 0)), whole, whole],
        out_specs=pl.BlockSpec((bq, MLA_HEADS * MLA_V), lambda b, i: (b * nq + i, 0)),
        out_shape=jax.ShapeDtypeStruct((n, MLA_HEADS * MLA_V), BF16),
        scratch_shapes=[pltpu.VMEM((MLA_HEADS, bq, LANE), F32), pltpu.VMEM((MLA_HEADS, bq, LANE), F32)],
        compiler_params=_params(("arbitrary",) * 2),
        name="mla_flash",
    )(q, k, v)


_DSA_QB = 256
_DSA_HG = 2
_DSA_CHAINS = 8
_INT_MIN = -2 ** 31


def _dsa_body(ns, topk, i, q_ref, k_ref, v_ref, qi_ref, ki_ref, w_ref, tri_ref, o_ref):
    qb = _DSA_QB
    hg = _DSA_HG
    qi = qi_ref[...]
    ki = ki_ref[0:ns, :]
    w = w_ref[...]
    head_of_lane = jnp.right_shift(lax.broadcasted_iota(jnp.int32, qi.shape, 1), int(math.log2(IDX_DIM)))
    zero = jnp.zeros_like(qi)
    rels = [_dot_nt(jnp.concatenate([jnp.where(head_of_lane == h, qi, zero) for h in range(g, g + hg)], axis=0), ki)
            for g in range(0, IDX_HEADS, hg)]
    score = jnp.zeros((qb, ns), F32)
    for g, rel in enumerate(rels):
        for t in range(hg):
            score = score + w[:, g * hg + t:g * hg + t + 1] * jnp.maximum(rel[t * qb:(t + 1) * qb], 0.0)

    qpos = lax.broadcasted_iota(jnp.int32, (qb, ns), 0) + i * qb
    kpos = lax.broadcasted_iota(jnp.int32, (qb, ns), 1)
    bits = pltpu.bitcast(score + 0.0, jnp.int32)
    key = jnp.where(bits < 0, bits ^ jnp.int32(0x7FFFFFFF), bits)
    key = jnp.where(kpos <= qpos, key, jnp.int32(_INT_MIN))
    kk = jnp.minimum(lax.broadcasted_iota(jnp.int32, (qb, 1), 0) + (i * qb + 1), topk).astype(F32)

    rows = qb // _DSA_CHAINS
    keys = [key[r * rows:(r + 1) * rows] for r in range(_DSA_CHAINS)]
    kks = [kk[r * rows:(r + 1) * rows] for r in range(_DSA_CHAINS)]

    def body(it, thrs):
        bit = lax.shift_left(jnp.int32(1), jnp.int32(31) - it)
        out = []
        for kr, kkr, t in zip(keys, kks, thrs):
            c = t + bit
            cnt = jnp.sum(jnp.where(kr >= c, 1.0, 0.0), axis=1, keepdims=True)
            out.append(jnp.where(cnt >= kkr, c, t))
        return tuple(out)

    thrs = lax.fori_loop(0, 32, body, tuple(jnp.full((rows, 1), _INT_MIN, jnp.int32) for _ in keys), unroll=16)
    thr = jnp.concatenate(thrs, axis=0)

    nchunk = ns // LANE
    need = kk - jnp.sum(jnp.where(key > thr, 1.0, 0.0), axis=1, keepdims=True)
    eqs = [jnp.where(key[:, c * LANE:(c + 1) * LANE] == thr, 1.0, 0.0) for c in range(nchunk)]
    before_all = _dot(jnp.concatenate(eqs, axis=0).astype(BF16), tri_ref[...])
    run = jnp.zeros((qb, 1), F32)
    bias_chunks = []
    for c in range(nchunk):
        before = before_all[c * qb:(c + 1) * qb] + run
        take = jnp.where(key[:, c * LANE:(c + 1) * LANE] > thr, 1.0, jnp.where(before < need, eqs[c], 0.0))
        bias_chunks.append(jnp.where(take > 0.5, 0.0, NEG_BIG))
        run = run + jnp.sum(eqs[c], axis=1, keepdims=True)
    bias = jnp.concatenate(bias_chunks, axis=1)

    k = k_ref[0:ns, :]
    v = v_ref[0:ns, :]
    lane = lax.broadcasted_iota(jnp.int32, (qb, LANE), 1)
    scores = [_dot_nt(jnp.concatenate([q_ref[:, h * LANE:(h + 1) * LANE] for h in range(g, g + hg)], axis=0), k)
              for g in range(0, DSA_HEADS, hg)]
    out_tiles = []
    for s in scores:
        s = s.reshape(hg, qb, ns) + bias[None]
        p = jnp.exp(s - jnp.max(s, axis=2, keepdims=True))
        o = _dot(p.reshape(hg * qb, ns).astype(BF16), v)
        for t in range(0, hg, 2):
            even = o[t * qb:(t + 1) * qb]
            odd = o[(t + 1) * qb:(t + 2) * qb]
            out_tiles.append(jnp.where(lane < DSA_HEAD_DIM, even / pltpu.roll(even, DSA_HEAD_DIM, 1),
                                       pltpu.roll(odd, DSA_HEAD_DIM, 1) / odd))
    o_ref[...] = jnp.concatenate(out_tiles, axis=1).astype(o_ref.dtype)


def _dsa_kernel(seq, bucket, topk, q_ref, k_ref, v_ref, qi_ref, ki_ref, w_ref, tri_ref, o_ref):
    i = pl.program_id(1)
    for bk in range(seq // bucket):
        @pl.when((i * _DSA_QB) // bucket == bk)
        def _(bk=bk):
            _dsa_body((bk + 1) * bucket, topk, i, q_ref, k_ref, v_ref, qi_ref, ki_ref, w_ref,
                      tri_ref, o_ref)


def _dsa_call(qd, kd, vd, qi, ki, wi, tri, bsz, seq):
    n = qd.shape[0]
    qb = _DSA_QB
    nq = seq // qb
    bucket = min(DSA_KEY_STEP, seq)
    topk = min(TOPK_MAX, seq // 4)
    qrow = lambda w: pl.BlockSpec((qb, w), lambda b, i: (b * nq + i, 0))
    full = lambda w: pl.BlockSpec((seq, w), lambda b, i: (b, 0))
    return pl.pallas_call(
        functools.partial(_dsa_kernel, seq, bucket, topk),
        grid=(bsz, nq),
        in_specs=[qrow(DSA_HEADS * LANE), full(LANE), full(LANE),
                  qrow(IDX_HEADS * IDX_DIM), full(IDX_HEADS * IDX_DIM), qrow(LANE),
                  _resident((LANE, LANE))],
        out_specs=qrow(DSA_HEADS * DSA_HEAD_DIM),
        out_shape=jax.ShapeDtypeStruct((n, DSA_HEADS * DSA_HEAD_DIM), BF16),
        compiler_params=_params(("arbitrary", "arbitrary")),
        name="dsa",
    )(qd, kd, vd, qi, ki, wi, tri)


def _s5_kernel(rows, u_ref, wt_ref, ws_ref, wo_ref, lam_ref, d_ref, o_ref, uf_ref, yf_ref):
    half = S5_GB * S5_STATE
    rowi = lax.broadcasted_iota(jnp.int32, (rows, half), 0)
    sub = rowi & (SUBLANES - 1)
    for gb in range(S5_NGB):
        gs = slice(gb * LANE, (gb + 1) * LANE)
        uf_ref[gb] = u_ref[:, gs].astype(F32)
        uf = jnp.concatenate([uf_ref[gb, pl.ds(a, rows, stride=S5_CHUNK), :] for a in range(S5_CHUNK)], axis=1)
        u = uf.astype(BF16)
        y = _dot(u, wt_ref[gb])
        st = _dot(u, ws_ref[gb])
        xr = st[:, :half]
        xi = st[:, half:]
        lam = lam_ref[gb]
        lr = lam[0:1]
        li = lam[1:2]
        pw_r, pw_i = [lr], [li]
        for _ in range(SUBLANES - 1):
            pw_r, pw_i = pw_r + [pw_r[-1] * lr - pw_i[-1] * li], pw_i + [pw_r[-1] * li + pw_i[-1] * lr]
        pw_r = jnp.concatenate(pw_r, axis=0)
        pw_i = jnp.concatenate(pw_i, axis=0)
        d = 1
        while d < SUBLANES:
            keep = sub >= d
            sr = jnp.where(keep, pltpu.roll(xr, d, 0), 0.0)
            si = jnp.where(keep, pltpu.roll(xi, d, 0), 0.0)
            xr, xi = xr + lr * sr - li * si, xi + lr * si + li * sr
            lr, li = lr * lr - li * li, 2.0 * lr * li
            d *= 2
        cr = jnp.zeros((1, half), F32)
        ci = jnp.zeros((1, half), F32)
        gr, gi = [], []
        for g in range(rows // SUBLANES):
            rs = slice(g * SUBLANES, (g + 1) * SUBLANES)
            hr = xr[rs] + pw_r * cr - pw_i * ci
            hi = xi[rs] + pw_r * ci + pw_i * cr
            gr.append(hr)
            gi.append(hi)
            cr = hr[SUBLANES - 1:SUBLANES]
            ci = hi[SUBLANES - 1:SUBLANES]
        xr = jnp.concatenate(gr, axis=0)
        xi = jnp.concatenate(gi, axis=0)
        keep = rowi >= 1
        pr = jnp.where(keep, pltpu.roll(xr, 1, 0), 0.0)
        pi = jnp.where(keep, pltpu.roll(xi, 1, 0), 0.0)
        xp = jnp.concatenate([pr, pi], axis=1).astype(BF16)
        yg = _gelu_tanh(y + _dot(xp, wo_ref[gb]) + d_ref[gb] * uf)
        for a in range(S5_CHUNK):
            yf_ref[gb, pl.ds(a, rows, stride=S5_CHUNK), :] = yg[:, a * LANE:(a + 1) * LANE]
        o_ref[:, gs] = yf_ref[gb].astype(o_ref.dtype)


def _s5_call(us, pw, l, bsz, seq):
    n = us.shape[0]
    blk = pl.BlockSpec((seq, D_S5), lambda b: (b, 0))
    wshape = pw["s5_wt"].shape[1:]
    return pl.pallas_call(
        functools.partial(_s5_kernel, seq // S5_CHUNK),
        grid=(bsz,),
        in_specs=[blk, _layer(wshape, l), _layer(wshape, l), _layer(wshape, l),
                  _layer(pw["s5_lam"].shape[1:], l), _layer(pw["s5_d"].shape[1:], l)],
        out_specs=blk,
        out_shape=jax.ShapeDtypeStruct((n, D_S5), BF16),
        scratch_shapes=[pltpu.VMEM((S5_NGB, seq, LANE), F32), pltpu.VMEM((S5_NGB, seq, LANE), F32)],
        compiler_params=_params(("arbitrary",)),
        name="s5",
    )(us, pw["s5_wt"], pw["s5_ws"], pw["s5_wo"], pw["s5_lam"], pw["s5_d"])


def _merge_kernel(x_ref, mod_ref, g_ref, ya_ref, yb_ref, yc_ref, yd_ref,
                  wg_ref, wb_ref, wglu_ref, bglu_ref, wout_ref, o_ref):
    x = x_ref[...]
    m = mod_ref[0]
    d = x.shape[1]
    u = _rms_mod(x, g_ref[...], m[3:4], m[4:5]).astype(BF16)
    yd = yd_ref[...]
    ydg = (yd.astype(F32) * _sigmoid(_dot(yd, wglu_ref[...]) + bglu_ref[...])).astype(BF16)
    ys = (ya_ref[...], yb_ref[...], yc_ref[...], ydg)
    merged = jnp.zeros(x.shape, F32)
    for nb in range(N_BRANCH):
        gate = _sigmoid(_dot(u, wg_ref[:, nb * d:(nb + 1) * d]))
        merged = merged + gate * _dot(ys[nb], wb_ref[nb])
    o_ref[...] = x + (1.0 + m[5:6]) * _dot(merged.astype(BF16), wout_ref[...])


def _merge_call(x, mod, ya, yb, yc, yd, pw, l, seq, tm):
    n, d = x.shape
    row = lambda w: pl.BlockSpec((tm, w), lambda i: (i, 0))
    return pl.pallas_call(
        _merge_kernel,
        grid=(n // tm,),
        in_specs=[row(d), _mod_spec(l, tm, seq, d), _layer((1, d), l, 1),
                  row(BRANCH_W), row(BRANCH_W), row(BRANCH_W), row(BRANCH_W),
                  _layer((d, N_BRANCH * d), l), _layer((N_BRANCH, BRANCH_W, d), l),
                  _layer((D_S5, D_S5), l), _layer((1, D_S5), l), _layer((d, d), l)],
        out_specs=row(d),
        out_shape=jax.ShapeDtypeStruct((n, d), F32),
        compiler_params=_params(("arbitrary",)),
        name="merge",
    )(x, mod, pw["norm_g"], ya, yb, yc, yd, pw["w_gate"], pw["w_branch"], pw["w_glu"], pw["b_glu"], pw["w_out"])


def _to_slots(w, nheads, hdim):
    lead = w.shape[:-1]
    wh = w.reshape(lead + (nheads, hdim))
    wh = jnp.pad(wh, [(0, 0)] * (len(lead) + 1) + [(0, LANE - hdim)])
    return wh.reshape(lead + (nheads * LANE,))


def _pad_lane(v, width=LANE):
    return jnp.pad(v, [(0, 0)] * (v.ndim - 1) + [(0, width - v.shape[-1])])


def _rope_table(positions, rot, period):
    j = jnp.arange(LANE) % period
    inv = jnp.where(j < rot, ROPE_THETA ** (-(2 * (j % (rot // 2))).astype(F32) / rot), 0.0)
    ang = positions.astype(F32).reshape(-1, 1) * inv[None, :]
    return jnp.cos(ang), jnp.sin(ang)


def _rot_matrix(rot, period):
    src = jnp.arange(LANE)[:, None]
    dst = jnp.arange(LANE)[None, :]
    j = dst % period
    half = rot // 2
    return (jnp.where((j < half) & (src == dst + half), -1.0, 0.0)
            + jnp.where((j >= half) & (j < rot) & (src == dst - half), 1.0, 0.0)).astype(BF16)


def _s5_weights(lam_re, lam_im, log_dt, b_re, b_im, c_re, c_im, dvec):
    hp = lax.Precision.HIGHEST
    g, p = lam_re.shape
    ch = S5_CHUNK
    dt = jnp.exp(log_dt)[:, None]
    mag = jnp.exp(lam_re * dt)
    ar, ai = mag * jnp.cos(lam_im * dt), mag * jnp.sin(lam_im * dt)
    den = lam_re * lam_re + lam_im * lam_im
    nr, ni = ar - 1.0, ai
    f_re = (nr * lam_re + ni * lam_im) / den
    f_im = (ni * lam_re - nr * lam_im) / den
    bb_re = f_re[..., None] * b_re - f_im[..., None] * b_im
    bb_im = f_re[..., None] * b_im + f_im[..., None] * b_re
    pr, pi = [jnp.ones_like(ar)], [jnp.zeros_like(ar)]
    for _ in range(ch):
        pr, pi = pr + [pr[-1] * ar - pi[-1] * ai], pi + [pr[-1] * ai + pi[-1] * ar]
    pr, pi = jnp.stack(pr), jnp.stack(pi)
    mr = pr[..., None] * bb_re - pi[..., None] * bb_im
    mi = pr[..., None] * bb_im + pi[..., None] * bb_re
    kern = (jnp.einsum('gjp,tgpi->tgji', c_re, mr, precision=hp)
            - jnp.einsum('gjp,tgpi->tgji', c_im, mi, precision=hp))

    def group_diag(x):
        r, c = x.shape[-2:]
        x = x.reshape(x.shape[:-3] + (S5_NGB, S5_GB * r, c))
        on_diag = jnp.arange(S5_GB * r)[:, None] // r == jnp.arange(S5_GB * c)[None, :] // c
        return jnp.tile(x, (1,) * (x.ndim - 1) + (S5_GB,)) * on_diag.astype(x.dtype)

    a_in = jnp.arange(ch)[:, None]
    a_out = jnp.arange(ch)[None, :]
    lag = jnp.clip(a_out - a_in, 0, ch)
    bd_k = group_diag(kern.transpose(0, 1, 3, 2)).astype(BF16)
    toe = bd_k[lag] * (a_out >= a_in)[..., None, None, None].astype(BF16)
    w_toe = toe.transpose(2, 0, 3, 1, 4).reshape(S5_NGB, ch * LANE, ch * LANE)
    rev = jnp.arange(ch - 1, -1, -1)
    st = jnp.concatenate([group_diag(mr[:ch].transpose(0, 1, 3, 2)), group_diag(mi[:ch].transpose(0, 1, 3, 2))],
                         axis=-1).astype(BF16)
    w_st = st[rev].transpose(1, 0, 2, 3).reshape(S5_NGB, ch * LANE, 2 * S5_GB * p)
    pr1, pi1 = pr[1:], pi[1:]
    co_re = c_re[None] * pr1[:, :, None, :] - c_im[None] * pi1[:, :, None, :]
    co_im = -(c_re[None] * pi1[:, :, None, :] + c_im[None] * pr1[:, :, None, :])
    co = jnp.stack([group_diag(co_re.transpose(0, 1, 3, 2)), group_diag(co_im.transpose(0, 1, 3, 2))],
                   axis=2).astype(BF16)
    w_out = co.reshape(ch, S5_NGB, 2 * S5_GB * p, LANE).transpose(1, 2, 0, 3).reshape(
        S5_NGB, 2 * S5_GB * p, ch * LANE)
    lam_c = jnp.stack([pr[ch].reshape(S5_NGB, S5_GB * p), pi[ch].reshape(S5_NGB, S5_GB * p)], axis=1)
    d_t = jnp.tile(dvec.reshape(S5_NGB, 1, LANE), (1, 1, ch))
    return w_toe, w_st, w_out, lam_c, d_t


def _block_diag(w):
    h, a, b = w.shape
    eye = jnp.eye(h, dtype=w.dtype)
    return jnp.einsum('hij,hg->higj', w, eye).reshape(h * a, h * b)


def _prep_params(p):
    w_in = p["w_in"]
    nl = w_in.shape[0]
    offs = [0]
    for s in IN_SPLITS:
        offs.append(offs[-1] + s)
    seg = [w_in[:, :, offs[k]:offs[k + 1]] for k in range(len(IN_SPLITS))]
    (w_xr, w_gr, w_ql, w_kvl, w_kpe, w_qd, w_kd, w_vd, w_qi, w_ki, w_wi, w_us, w_gate) = seg
    w_all = jnp.concatenate([w_xr, w_gr, w_ql, w_kvl, _pad_lane(w_kpe), w_qd, _pad_lane(w_kd), _pad_lane(w_vd),
                             w_qi, jnp.tile(w_ki, (1, 1, IDX_HEADS)), _pad_lane(w_wi), w_us],
                            axis=2).astype(BF16)
    assert w_all.shape[2] == _C_END

    wuq = _to_slots(p["mla_w_uq"], MLA_HEADS, MLA_QK).astype(BF16)
    wkv = p["mla_w_ukv"].reshape(nl, MLA_KV_LORA, MLA_HEADS, MLA_NOPE + MLA_V)
    kn = jnp.pad(wkv[..., :MLA_NOPE], ((0, 0), (0, 0), (0, 0), (MLA_ROPE, LANE - MLA_QK)))
    wv = wkv[..., MLA_NOPE:].reshape(nl, MLA_KV_LORA, MLA_HEADS // 2, 2, MLA_V)
    zv = jnp.zeros_like(wv[:, :, :, 0])
    wv = jnp.stack([jnp.concatenate([wv[:, :, :, 0], zv], axis=-1),
                    jnp.concatenate([zv, wv[:, :, :, 1]], axis=-1)], axis=3)
    wukv = jnp.concatenate([kn.reshape(nl, MLA_KV_LORA, MLA_HEADS * LANE),
                            wv.reshape(nl, MLA_KV_LORA, MLA_HEADS * LANE)], axis=2).astype(BF16)

    dq = p["dsa_qk_gain"][:, 0]
    vec = jnp.stack([_pad_lane(v, MLA_Q_LORA) for v in (
        p["mla_q_norm"], p["mla_kv_norm"], p["mla_qk_gain"][:, 0], p["mla_qk_gain"][:, 1],
        jnp.concatenate([dq, dq], axis=-1), p["dsa_qk_gain"][:, 1])], axis=1)

    s5_wt, s5_ws, s5_wo, s5_lam, s5_d = jax.vmap(_s5_weights)(
        p["s5_lambda_re"], p["s5_lambda_im"], p["s5_log_dt"], p["s5_b_re"], p["s5_b_im"],
        p["s5_c_re"], p["s5_c_im"], p["s5_d"])

    return dict(
        w_all=w_all, wuq=wuq, wukv=wukv, vec=vec,
        rot=jnp.stack([_rot_matrix(MLA_ROPE, LANE), _rot_matrix(DSA_ROT, DSA_HEAD_DIM),
                       _rot_matrix(IDX_ROT, IDX_DIM)]),
        conv_w=p["conv_w"],
        rg_vec=jnp.stack([p["conv_b"], p["rg_ba"], p["rg_bx"], p["rg_lambda"]], axis=1),
        rg_wa=jax.vmap(_block_diag)(p["rg_wa"]).astype(BF16),
        rg_wx=jax.vmap(_block_diag)(p["rg_wx"]).astype(BF16),
        s5_wt=s5_wt, s5_ws=s5_ws, s5_wo=s5_wo, s5_lam=s5_lam, s5_d=s5_d,
        w_gate=w_gate.astype(BF16),
        w_branch=p["w_branch"].astype(BF16),
        w_glu=p["s5_w_glu"].astype(BF16),
        b_glu=p["s5_b_glu"][:, None, :],
        w_out=p["w_out"].astype(BF16),
        ffn_w1=p["ffn_w1"].astype(BF16), ffn_w3=p["ffn_w3"].astype(BF16), ffn_w2=p["ffn_w2"].astype(BF16),
        norm_g=p["norm_g"][:, :, None, :],
    )


def _pick_tile(n, pref):
    t = pref
    while n % t:
        t //= 2
    return t


def kernel(x, c, positions, ada_w, ada_b, norm_g, ffn_w1, ffn_w3, ffn_w2, w_in,
           conv_w, conv_b, rg_wa, rg_ba, rg_wx, rg_bx, rg_lambda,
           mla_q_norm, mla_w_uq, mla_kv_norm, mla_w_ukv, mla_qk_gain, dsa_qk_gain,
           s5_lambda_re, s5_lambda_im, s5_log_dt, s5_b_re, s5_b_im, s5_c_re, s5_c_im,
           s5_d, s5_w_glu, s5_b_glu, w_branch, w_out):
    p = dict(norm_g=norm_g, ffn_w1=ffn_w1, ffn_w3=ffn_w3, ffn_w2=ffn_w2, w_in=w_in,
             conv_w=conv_w, conv_b=conv_b, rg_wa=rg_wa, rg_ba=rg_ba, rg_wx=rg_wx, rg_bx=rg_bx,
             rg_lambda=rg_lambda, mla_q_norm=mla_q_norm, mla_w_uq=mla_w_uq, mla_kv_norm=mla_kv_norm,
             mla_w_ukv=mla_w_ukv, mla_qk_gain=mla_qk_gain, dsa_qk_gain=dsa_qk_gain,
             s5_lambda_re=s5_lambda_re, s5_lambda_im=s5_lambda_im, s5_log_dt=s5_log_dt,
             s5_b_re=s5_b_re, s5_b_im=s5_b_im, s5_c_re=s5_c_re, s5_c_im=s5_c_im, s5_d=s5_d,
             s5_w_glu=s5_w_glu, s5_b_glu=s5_b_glu, w_branch=w_branch, w_out=w_out)
    bsz, seq, d = x.shape
    n = bsz * seq
    depth = ada_w.shape[0]
    assert seq % (S5_CHUNK * 8) == 0 and seq % _DSA_QB == 0

    tm = _pick_tile(seq, PROJ_ROWS)
    tw = _pick_tile(seq, WIDE_ROWS)
    tc = _pick_tile(seq, SCAN_ROWS)
    bq = _pick_tile(seq, ATTN_ROWS)

    mod = _ada_call(c, ada_w, ada_b)
    pw = _prep_params(p)
    tabs = (_rope_table(positions, MLA_ROPE, LANE) + _rope_table(positions, DSA_ROT, DSA_HEAD_DIM)
            + _rope_table(positions, IDX_ROT, IDX_DIM))
    tri = (jnp.arange(LANE)[:, None] < jnp.arange(LANE)[None, :]).astype(BF16)

    xf = x.reshape(n, d)
    for l in range(depth):
        xf = _ffn_call(xf, mod, pw, l, 0, seq, tw)
        (xr, gr, qm, km, vm, qd, kd, vd, qi, ki, wi, us) = _inproj_call(xf, mod, pw, l, tabs, seq, tm)
        ya = _rglru_call(xr, gr, pw, l, bsz, seq, tc)
        yb = _flash_call(qm, km, vm, bsz, seq, bq)
        yc = _dsa_call(qd, kd, vd, qi, ki, wi, tri, bsz, seq)
        yd = _s5_call(us, pw, l, bsz, seq)
        xf = _merge_call(xf, mod, ya, yb, yc, yd, pw, l, seq, tw)
        xf = _ffn_call(xf, mod, pw, l, 1, seq, tw)
    return xf.reshape(bsz, seq, d)
```

```python
import functools
import math

import jax
import jax.numpy as jnp
from jax import lax
from jax.experimental import pallas as pl
from jax.experimental.pallas import tpu as pltpu

F32 = jnp.float32
BF16 = jnp.bfloat16

D_MODEL = 1024
EPS = 1e-6
ROPE_THETA = 500000.0
D_FF = 2816
N_ADA = 9

D_RNN = 512
RNN_HEADS = 8
RNN_HEAD_DIM = D_RNN // RNN_HEADS
CONV_WIDTH = 4
LRU_C = 8.0

MLA_HEADS = 8
MLA_NOPE = 64
MLA_ROPE = 32
MLA_V = 64
MLA_QK = MLA_ROPE + MLA_NOPE
MLA_Q_LORA = 256
MLA_KV_LORA = 128

DSA_HEADS = 8
DSA_HEAD_DIM = 64
DSA_ROT = DSA_HEAD_DIM // 4
IDX_HEADS = 8
IDX_DIM = 32
IDX_ROT = IDX_DIM // 4
TOPK_MAX = 256

S5_GROUP = 16
S5_GROUPS = 32
D_S5 = S5_GROUP * S5_GROUPS
S5_STATE = 64
S5_CHUNK = 8
S5_GB = 8
S5_NGB = S5_GROUPS // S5_GB

N_BRANCH = 4
BRANCH_W = 512
IN_SPLITS = (D_RNN, D_RNN, MLA_Q_LORA, MLA_KV_LORA, MLA_ROPE,
             DSA_HEADS * DSA_HEAD_DIM, DSA_HEAD_DIM, DSA_HEAD_DIM,
             IDX_HEADS * IDX_DIM, IDX_DIM, IDX_HEADS, D_S5, N_BRANCH * D_MODEL)

LANE = 128
SUBLANES = 8
NEG_BIG = -1e30
VMEM_LIMIT = 56 * 1024 * 1024

PROJ_ROWS = 512
WIDE_ROWS = 1024
SCAN_ROWS = 256
ATTN_ROWS = 512
ADA_COLS = 2304
DSA_KEY_STEP = 512

_C_XR = 0
_C_GR = _C_XR + D_RNN
_C_QL = _C_GR + D_RNN
_C_KVL = _C_QL + MLA_Q_LORA
_C_KPE = _C_KVL + MLA_KV_LORA
_C_QD = _C_KPE + LANE
_C_KD = _C_QD + DSA_HEADS * DSA_HEAD_DIM
_C_VD = _C_KD + LANE
_C_QI = _C_VD + LANE
_C_KI = _C_QI + IDX_HEADS * IDX_DIM
_C_WI = _C_KI + IDX_HEADS * IDX_DIM
_C_US = _C_WI + LANE
_C_END = _C_US + D_S5


def _dot(a, b):
    return jnp.dot(a, b, preferred_element_type=F32)


def _dot_nt(a, b):
    return lax.dot_general(a, b, (((1,), (1,)), ((), ())), preferred_element_type=F32)


def _sigmoid(x):
    return jax.nn.sigmoid(x)


def _gelu_tanh(x):
    return 0.5 * x * (1.0 + jnp.tanh(0.7978845608028654 * (x + 0.044715 * (x * x * x))))


def _rms_mod(x, g, shift, scale):
    ms = jnp.mean(x * x, axis=-1, keepdims=True)
    y = x * lax.rsqrt(ms + EPS) * g
    return y * (1.0 + scale) + shift


def _resident(shape):
    return pl.BlockSpec(shape, lambda *_: (0,) * len(shape), pipeline_mode=pl.Buffered(1))


def _layer(tail, *lead):
    return pl.BlockSpec((None,) * len(lead) + tuple(tail), lambda *_: tuple(lead) + (0,) * len(tail),
                        pipeline_mode=pl.Buffered(1))


def _mod_spec(l, tm, seq, d):
    return pl.BlockSpec((None, 1, N_ADA, d), lambda i: (l, i * tm // seq, 0, 0))


def _params(sem):
    return pltpu.CompilerParams(dimension_semantics=sem, vmem_limit_bytes=VMEM_LIMIT)


def _ada_kernel(c_ref, w_ref, b_ref, o_ref):
    c = c_ref[...]
    a = c * _sigmoid(c)
    w = w_ref[0]
    a_hi = a.astype(BF16)
    a_lo = (a - a_hi.astype(F32)).astype(BF16)
    w_hi = w.astype(BF16)
    w_lo = (w - w_hi.astype(F32)).astype(BF16)
    o_ref[0] = _dot(a_hi, w_hi) + _dot(a_lo, w_hi) + _dot(a_hi, w_lo) + b_ref[0]


def _ada_call(c, ada_w, ada_b):
    nl, d, n9 = ada_w.shape
    b = c.shape[0]
    tn = ADA_COLS
    out = pl.pallas_call(
        _ada_kernel,
        grid=(nl, n9 // tn),
        in_specs=[
            pl.BlockSpec((b, d), lambda l, j: (0, 0)),
            pl.BlockSpec((1, d, tn), lambda l, j: (l, 0, j)),
            pl.BlockSpec((1, 1, tn), lambda l, j: (l, 0, j)),
        ],
        out_specs=pl.BlockSpec((1, b, tn), lambda l, j: (l, 0, j)),
        out_shape=jax.ShapeDtypeStruct((nl, b, n9), F32),
        compiler_params=_params(("arbitrary", "arbitrary")),
        name="ada_mod",
    )(c, ada_w, ada_b.reshape(nl, 1, n9))
    return out.reshape(nl, b, N_ADA, d)


_FFN_CHUNK = 256


def _ffn_kernel(row0, x_ref, mod_ref, g_ref, w1_ref, w3_ref, w2_ref, o_ref, h_ref):
    x = x_ref[...]
    m = mod_ref[0]
    u = _rms_mod(x, g_ref[...], m[row0:row0 + 1], m[row0 + 1:row0 + 2]).astype(BF16)
    for c in range(0, w1_ref.shape[1], _FFN_CHUNK):
        h1 = _dot(u, w1_ref[:, c:c + _FFN_CHUNK])
        h3 = _dot(u, w3_ref[:, c:c + _FFN_CHUNK])
        h_ref[:, c:c + _FFN_CHUNK] = (h1 * _sigmoid(h1) * h3).astype(BF16)
    o_ref[...] = x + 0.5 * (1.0 + m[row0 + 2:row0 + 3]) * _dot(h_ref[...], w2_ref[...])


def _ffn_call(x, mod, pw, l, j, seq, tm):
    n, d = x.shape
    f = pw["ffn_w1"].shape[-1]
    return pl.pallas_call(
        functools.partial(_ffn_kernel, 6 * j),
        grid=(n // tm,),
        in_specs=[
            pl.BlockSpec((tm, d), lambda i: (i, 0)),
            _mod_spec(l, tm, seq, d),
            _layer((1, d), l, 2 * j),
            _layer((d, f), l, j), _layer((d, f), l, j), _layer((f, d), l, j),
        ],
        out_specs=pl.BlockSpec((tm, d), lambda i: (i, 0)),
        out_shape=jax.ShapeDtypeStruct((n, d), F32),
        scratch_shapes=[pltpu.VMEM((tm, f), BF16)],
        compiler_params=_params(("arbitrary",)),
        name="ffn",
    )(x, mod, pw["norm_g"], pw["ffn_w1"], pw["ffn_w3"], pw["ffn_w2"])


def _rope_tiles(tiles, cos, sin, rot):
    rows = tiles[0].shape[0]
    partner = _dot(jnp.concatenate(tiles, axis=0).astype(BF16), rot)
    return [t * cos + partner[i * rows:(i + 1) * rows] * sin for i, t in enumerate(tiles)]


def _inproj_kernel(x_ref, mod_ref, g_ref, w_ref, wuq_ref, wukv_ref, vec_ref, rot_ref,
                   cm_ref, sm_ref, cd_ref, sd_ref, ci_ref, si_ref,
                   xr_ref, gr_ref, qm_ref, km_ref, vm_ref, qd_ref, kd_ref, vd_ref,
                   qi_ref, ki_ref, wi_ref, us_ref):
    x = x_ref[...]
    m = mod_ref[0]
    u = _rms_mod(x, g_ref[...], m[3:4], m[4:5]).astype(BF16)
    vec = vec_ref[...]
    lane = lax.broadcasted_iota(jnp.int32, (1, LANE), 1)
    z = _dot(u, w_ref[...])

    xr_ref[...] = z[:, _C_XR:_C_GR].astype(xr_ref.dtype)
    gr_ref[...] = z[:, _C_GR:_C_QL].astype(gr_ref.dtype)
    wi_ref[...] = z[:, _C_WI:_C_US]
    us_ref[...] = z[:, _C_US:_C_END].astype(us_ref.dtype)

    ql = z[:, _C_QL:_C_KVL]
    qn = (ql * lax.rsqrt(jnp.mean(ql * ql, axis=-1, keepdims=True) + EPS)
          * vec[0:1, :MLA_Q_LORA]).astype(BF16)
    q2 = _dot(qn, wuq_ref[...])
    kvl = z[:, _C_KVL:_C_KPE]
    kvn = (kvl * lax.rsqrt(jnp.mean(kvl * kvl, axis=-1, keepdims=True) + EPS)
           * vec[1:2, :MLA_KV_LORA]).astype(BF16)
    kv = _dot(kvn, wukv_ref[...])
    kpe = z[:, _C_KPE:_C_QD]
    gq = vec[2:3, :LANE]
    gk = vec[3:4, :LANE]
    roped = _rope_tiles([kpe * gk] + [q2[:, h * LANE:(h + 1) * LANE] * gq for h in range(MLA_HEADS)],
                        cm_ref[...], sm_ref[...], rot_ref[0])
    kpe_rot = roped[0]
    kpe_ss = jnp.sum(kpe * kpe, axis=-1, keepdims=True)
    q_tiles, k_tiles = [], []
    for h in range(MLA_HEADS):
        hs = slice(h * LANE, (h + 1) * LANE)
        qs = q2[:, hs]
        s = lax.rsqrt(jnp.sum(qs * qs, axis=-1, keepdims=True) * (1.0 / MLA_QK) + EPS) * MLA_QK ** -0.5
        q_tiles.append((s * roped[1 + h]).astype(qm_ref.dtype))
        kn = kv[:, hs]
        s = lax.rsqrt((jnp.sum(kn * kn, axis=-1, keepdims=True) + kpe_ss) * (1.0 / MLA_QK) + EPS)
        k_tiles.append((s * (kn * gk + kpe_rot)).astype(km_ref.dtype))
    qm_ref[...] = jnp.concatenate(q_tiles, axis=1)
    km_ref[...] = jnp.concatenate(k_tiles, axis=1)
    nslot = MLA_HEADS * LANE
    vlane = lax.broadcasted_iota(jnp.int32, (1, nslot), 1)
    v_is_low = (vlane // LANE) % 2 == 0
    ones_half = jnp.where(((vlane % LANE) < MLA_V) == v_is_low, 0.0, 1.0)
    vm_ref[...] = (kv[:, nslot:] + ones_half).astype(vm_ref.dtype)

    gd = vec[4:5, :LANE]
    low = lane < DSA_HEAD_DIM
    kd = z[:, _C_KD:_C_VD]
    xqs = [z[:, _C_QD + t * LANE:_C_QD + (t + 1) * LANE] for t in range(DSA_HEADS // 2)]
    roped = _rope_tiles([xq * gd for xq in xqs] + [kd * vec[5:6, :LANE]], cd_ref[...], sd_ref[...], rot_ref[1])
    d_tiles = []
    for t in range(DSA_HEADS // 2):
        x2 = xqs[t] * xqs[t]
        ss_even = jnp.sum(jnp.where(low, x2, 0.0), axis=-1, keepdims=True)
        ss_odd = jnp.sum(jnp.where(low, 0.0, x2), axis=-1, keepdims=True)
        y = roped[t]
        s_even = lax.rsqrt(ss_even * (1.0 / DSA_HEAD_DIM) + EPS) * DSA_HEAD_DIM ** -0.5
        s_odd = lax.rsqrt(ss_odd * (1.0 / DSA_HEAD_DIM) + EPS) * DSA_HEAD_DIM ** -0.5
        d_tiles.append(jnp.where(low, y * s_even, 0.0).astype(qd_ref.dtype))
        d_tiles.append(jnp.where(low, pltpu.roll(y, DSA_HEAD_DIM, 1) * s_odd, 0.0).astype(qd_ref.dtype))
    qd_ref[...] = jnp.concatenate(d_tiles, axis=1)
    s = lax.rsqrt(jnp.sum(kd * kd, axis=-1, keepdims=True) * (1.0 / DSA_HEAD_DIM) + EPS)
    kd_ref[...] = (s * roped[-1]).astype(kd_ref.dtype)
    vd_ref[...] = (z[:, _C_VD:_C_QI] + jnp.where(low, 0.0, 1.0)).astype(vd_ref.dtype)

    ntile = IDX_HEADS * IDX_DIM // LANE
    roped = _rope_tiles([z[:, c + t * LANE:c + (t + 1) * LANE] for c in (_C_QI, _C_KI) for t in range(ntile)],
                        ci_ref[...], si_ref[...], rot_ref[2])
    qi_ref[...] = jnp.concatenate(roped[:ntile], axis=1).astype(qi_ref.dtype)
    ki_ref[...] = jnp.concatenate(roped[ntile:], axis=1).astype(ki_ref.dtype)


def _inproj_call(x, mod, pw, l, tabs, seq, tm):
    n, d = x.shape
    row = lambda w: pl.BlockSpec((tm, w), lambda i: (i, 0))
    out_widths = [(D_RNN, BF16), (D_RNN, BF16), (MLA_HEADS * LANE, BF16), (MLA_HEADS * LANE, BF16),
                  (MLA_HEADS * LANE, BF16), (DSA_HEADS * LANE, BF16), (LANE, BF16), (LANE, BF16),
                  (IDX_HEADS * IDX_DIM, BF16), (IDX_HEADS * IDX_DIM, BF16), (LANE, F32), (D_S5, BF16)]
    return pl.pallas_call(
        _inproj_kernel,
        grid=(n // tm,),
        in_specs=[
            row(d), _mod_spec(l, tm, seq, d), _layer((1, d), l, 1),
            _layer(pw["w_all"].shape[1:], l), _layer(pw["wuq"].shape[1:], l),
            _layer(pw["wukv"].shape[1:], l), _layer(pw["vec"].shape[1:], l),
            _resident(pw["rot"].shape),
        ] + [row(LANE)] * 6,
        out_specs=[row(w) for w, _ in out_widths],
        out_shape=[jax.ShapeDtypeStruct((n, w), dt) for w, dt in out_widths],
        compiler_params=_params(("arbitrary",)),
        name="in_proj",
    )(x, mod, pw["norm_g"], pw["w_all"], pw["wuq"], pw["wukv"], pw["vec"], pw["rot"], *tabs)


def _rglru_kernel(tc, x_ref, gate_ref, cw_ref, vec_ref, wa_ref, wx_ref, o_ref, xs_ref, h_ref):
    j = pl.program_id(1)

    @pl.when(j == 0)
    def _():
        xs_ref[0:8, :] = jnp.zeros((8, D_RNN), F32)
        h_ref[...] = jnp.zeros(h_ref.shape, F32)

    xs_ref[8:8 + tc, :] = x_ref[...].astype(F32)
    cw = cw_ref[...]
    vec = vec_ref[...]
    xc = vec[0:1]
    for k in range(CONV_WIDTH):
        xc = xc + cw[k:k + 1] * xs_ref[pl.ds(8 - (CONV_WIDTH - 1) + k, tc), :]
    xs_ref[0:8, :] = xs_ref[tc:tc + 8, :]

    xb = xc.astype(BF16)
    r = _sigmoid(_dot(xb, wa_ref[...]) + vec[1:2])
    ig = _sigmoid(_dot(xb, wx_ref[...]) + vec[2:3])
    nl = -vec[3:4]
    softplus = jnp.maximum(nl, 0.0) + jnp.log(1.0 + jnp.exp(-jnp.abs(nl)))
    log_a = (-LRU_C) * r * softplus
    a = jnp.exp(log_a)
    z = 2.0 * log_a
    series = -z * (1.0 + z * (0.5 + z * (1.0 / 6.0 + z * (1.0 / 24.0 + z * (1.0 / 120.0 + z * (1.0 / 720.0))))))
    nem1 = jnp.where(z > -0.25, series, 1.0 - jnp.exp(z))
    b = jnp.sqrt(nem1) * ig * xc

    sub = lax.broadcasted_iota(jnp.int32, (tc, D_RNN), 0) & (SUBLANES - 1)
    d = 1
    while d < SUBLANES:
        keep = sub >= d
        a_s = jnp.where(keep, pltpu.roll(a, d, 0), 1.0)
        b_s = jnp.where(keep, pltpu.roll(b, d, 0), 0.0)
        b = a * b_s + b
        a = a * a_s
        d *= 2
    carry = h_ref[...]
    groups = []
    for g in range(tc // SUBLANES):
        rows = slice(g * SUBLANES, (g + 1) * SUBLANES)
        hg = b[rows] + a[rows] * carry
        groups.append(hg)
        carry = hg[SUBLANES - 1:SUBLANES]
    h = jnp.concatenate(groups, axis=0)
    h_ref[...] = carry
    o_ref[...] = (h * _gelu_tanh(gate_ref[...].astype(F32))).astype(o_ref.dtype)


def _rglru_call(xr, gr, pw, l, bsz, seq, tc):
    n = xr.shape[0]
    nt = seq // tc
    row = pl.BlockSpec((tc, D_RNN), lambda b, j: (b * nt + j, 0))
    return pl.pallas_call(
        functools.partial(_rglru_kernel, tc),
        grid=(bsz, nt),
        in_specs=[row, row,
                  _layer((CONV_WIDTH, D_RNN), l), _layer((4, D_RNN), l),
                  _layer((D_RNN, D_RNN), l), _layer((D_RNN, D_RNN), l)],
        out_specs=row,
        out_shape=jax.ShapeDtypeStruct((n, D_RNN), BF16),
        scratch_shapes=[pltpu.VMEM((tc + 8, D_RNN), F32), pltpu.VMEM((1, D_RNN), F32)],
        compiler_params=_params(("arbitrary", "arbitrary")),
        name="rglru",
    )(xr, gr, pw["conv_w"], pw["rg_vec"], pw["rg_wa"], pw["rg_wx"])


def _flash_kernel(q_ref, k_ref, v_ref, o_ref, m_ref, acc_ref):
    i = pl.program_id(1)
    bq = q_ref.shape[0]
    bk = bq

    m_ref[...] = jnp.full(m_ref.shape, NEG_BIG, F32)
    acc_ref[...] = jnp.zeros(acc_ref.shape, F32)

    def step(j, masked):
        rows = pl.ds(pl.multiple_of(j * bk, bk), bk)
        if masked:
            causal = (lax.broadcasted_iota(jnp.int32, (bq, bk), 1)
                      <= lax.broadcasted_iota(jnp.int32, (bq, bk), 0))
        scores = [_dot_nt(q_ref[:, h * LANE:(h + 1) * LANE], k_ref[rows, h * LANE:(h + 1) * LANE])
                  for h in range(MLA_HEADS)]
        m_out, alphas, probs = [], [], []
        for h in range(MLA_HEADS):
            s = scores[h]
            if masked:
                s = jnp.where(causal, s, NEG_BIG)
            m_prev = m_ref[h]
            m_new = jnp.maximum(m_prev, jnp.max(s, axis=1, keepdims=True))
            probs.append(jnp.exp(s - jnp.concatenate([m_new] * (bk // LANE), axis=1)).astype(BF16))
            alphas.append(jnp.exp(m_prev - m_new))
            m_out.append(m_new)
        m_ref[...] = jnp.stack(m_out)
        acc_ref[...] = jnp.stack([alphas[h] * acc_ref[h] + _dot(probs[h], v_ref[rows, h * LANE:(h + 1) * LANE])
                                  for h in range(MLA_HEADS)])

    def below_diagonal(j, carry):
        step(j, False)
        return carry

    lax.fori_loop(0, i, below_diagonal, 0)
    step(i, True)
    lane = lax.broadcasted_iota(jnp.int32, (bq, LANE), 1)
    outs = []
    for pr in range(MLA_HEADS // 2):
        even = acc_ref[2 * pr]
        odd = acc_ref[2 * pr + 1]
        outs.append(jnp.where(lane < MLA_V, even / pltpu.roll(even, MLA_V, 1),
                              odd / pltpu.roll(odd, MLA_V, 1)))
    o_ref[...] = jnp.concatenate(outs, axis=1).astype(o_ref.dtype)


def _flash_call(q, k, v, bsz, seq, bq):
    n = q.shape[0]
    nq = seq // bq
    whole = pl.BlockSpec((seq, MLA_HEADS * LANE), lambda b, i: (b, 0))
    return pl.pallas_call(
        _flash_kernel,
        grid=(bsz, nq),
        in_specs=[pl.BlockSpec((bq, MLA_HEADS * LANE), lambda b, i: (b * nq + i, 0)), whole, whole],
        out_specs=pl.BlockSpec((bq, MLA_HEADS * MLA_V), lambda b, i: (b * nq + i, 0)),
        out_shape=jax.ShapeDtypeStruct((n, MLA_HEADS * MLA_V), BF16),
        scratch_shapes=[pltpu.VMEM((MLA_HEADS, bq, LANE), F32), pltpu.VMEM((MLA_HEADS, bq, LANE), F32)],
        compiler_params=_params(("arbitrary",) * 2),
        name="mla_flash",
    )(q, k, v)


_DSA_QB = 256
_DSA_HG = 2
_DSA_CHAINS = 8
_INT_MIN = -2 ** 31


def _dsa_body(ns, topk, i, q_ref, k_ref, v_ref, qi_ref, ki_ref, w_ref, tri_ref, o_ref):
    qb = _DSA_QB
    hg = _DSA_HG
    qi = qi_ref[...]
    ki = ki_ref[0:ns, :]
    w = w_ref[...]
    head_of_lane = jnp.right_shift(lax.broadcasted_iota(jnp.int32, qi.shape, 1), int(math.log2(IDX_DIM)))
    zero = jnp.zeros_like(qi)
    rels = [_dot_nt(jnp.concatenate([jnp.where(head_of_lane == h, qi, zero) for h in range(g, g + hg)], axis=0), ki)
            for g in range(0, IDX_HEADS, hg)]
    score = jnp.zeros((qb, ns), F32)
    for g, rel in enumerate(rels):
        for t in range(hg):
            score = score + w[:, g * hg + t:g * hg + t + 1] * jnp.maximum(rel[t * qb:(t + 1) * qb], 0.0)

    qpos = lax.broadcasted_iota(jnp.int32, (qb, ns), 0) + i * qb
    kpos = lax.broadcasted_iota(jnp.int32, (qb, ns), 1)
    bits = pltpu.bitcast(score + 0.0, jnp.int32)
    key = jnp.where(bits < 0, bits ^ jnp.int32(0x7FFFFFFF), bits)
    key = jnp.where(kpos <= qpos, key, jnp.int32(_INT_MIN))
    kk = jnp.minimum(lax.broadcasted_iota(jnp.int32, (qb, 1), 0) + (i * qb + 1), topk).astype(F32)

    rows = qb // _DSA_CHAINS
    keys = [key[r * rows:(r + 1) * rows] for r in range(_DSA_CHAINS)]
    kks = [kk[r * rows:(r + 1) * rows] for r in range(_DSA_CHAINS)]

    def body(it, thrs):
        bit = lax.shift_left(jnp.int32(1), jnp.int32(31) - it)
        out = []
        for kr, kkr, t in zip(keys, kks, thrs):
            c = t + bit
            cnt = jnp.sum(jnp.where(kr >= c, 1.0, 0.0), axis=1, keepdims=True)
            out.append(jnp.where(cnt >= kkr, c, t))
        return tuple(out)

    thrs = lax.fori_loop(0, 32, body, tuple(jnp.full((rows, 1), _INT_MIN, jnp.int32) for _ in keys), unroll=16)
    thr = jnp.concatenate(thrs, axis=0)

    nchunk = ns // LANE
    need = kk - jnp.sum(jnp.where(key > thr, 1.0, 0.0), axis=1, keepdims=True)
    eqs = [jnp.where(key[:, c * LANE:(c + 1) * LANE] == thr, 1.0, 0.0) for c in range(nchunk)]
    before_all = _dot(jnp.concatenate(eqs, axis=0).astype(BF16), tri_ref[...])
    run = jnp.zeros((qb, 1), F32)
    bias_chunks = []
    for c in range(nchunk):
        before = before_all[c * qb:(c + 1) * qb] + run
        take = jnp.where(key[:, c * LANE:(c + 1) * LANE] > thr, 1.0, jnp.where(before < need, eqs[c], 0.0))
        bias_chunks.append(jnp.where(take > 0.5, 0.0, NEG_BIG))
        run = run + jnp.sum(eqs[c], axis=1, keepdims=True)
    bias = jnp.concatenate(bias_chunks, axis=1)

    k = k_ref[0:ns, :]
    v = v_ref[0:ns, :]
    lane = lax.broadcasted_iota(jnp.int32, (qb, LANE), 1)
    scores = [_dot_nt(jnp.concatenate([q_ref[:, h * LANE:(h + 1) * LANE] for h in range(g, g + hg)], axis=0), k)
              for g in range(0, DSA_HEADS, hg)]
    out_tiles = []
    for s in scores:
        s = s.reshape(hg, qb, ns) + bias[None]
        p = jnp.exp(s - jnp.max(s, axis=2, keepdims=True))
        o = _dot(p.reshape(hg * qb, ns).astype(BF16), v)
        for t in range(0, hg, 2):
            even = o[t * qb:(t + 1) * qb]
            odd = o[(t + 1) * qb:(t + 2) * qb]
            out_tiles.append(jnp.where(lane < DSA_HEAD_DIM, even / pltpu.roll(even, DSA_HEAD_DIM, 1),
                                       pltpu.roll(odd, DSA_HEAD_DIM, 1) / odd))
    o_ref[...] = jnp.concatenate(out_tiles, axis=1).astype(o_ref.dtype)


def _dsa_kernel(seq, bucket, topk, q_ref, k_ref, v_ref, qi_ref, ki_ref, w_ref, tri_ref, o_ref):
    i = pl.program_id(1)
    for bk in range(seq // bucket):
        @pl.when((i * _DSA_QB) // bucket == bk)
        def _(bk=bk):
            _dsa_body((bk + 1) * bucket, topk, i, q_ref, k_ref, v_ref, qi_ref, ki_ref, w_ref,
                      tri_ref, o_ref)


def _dsa_call(qd, kd, vd, qi, ki, wi, tri, bsz, seq):
    n = qd.shape[0]
    qb = _DSA_QB
    nq = seq // qb
    bucket = min(DSA_KEY_STEP, seq)
    topk = min(TOPK_MAX, seq // 4)
    qrow = lambda w: pl.BlockSpec((qb, w), lambda b, i: (b * nq + i, 0))
    full = lambda w: pl.BlockSpec((seq, w), lambda b, i: (b, 0))
    return pl.pallas_call(
        functools.partial(_dsa_kernel, seq, bucket, topk),
        grid=(bsz, nq),
        in_specs=[qrow(DSA_HEADS * LANE), full(LANE), full(LANE),
                  qrow(IDX_HEADS * IDX_DIM), full(IDX_HEADS * IDX_DIM), qrow(LANE),
                  _resident((LANE, LANE))],
        out_specs=qrow(DSA_HEADS * DSA_HEAD_DIM),
        out_shape=jax.ShapeDtypeStruct((n, DSA_HEADS * DSA_HEAD_DIM), BF16),
        compiler_params=_params(("arbitrary", "arbitrary")),
        name="dsa",
    )(qd, kd, vd, qi, ki, wi, tri)


def _s5_kernel(rows, u_ref, wt_ref, ws_ref, wo_ref, lam_ref, d_ref, o_ref, uf_ref, yf_ref):
    half = S5_GB * S5_STATE
    rowi = lax.broadcasted_iota(jnp.int32, (rows, half), 0)
    sub = rowi & (SUBLANES - 1)
    ufs, ys, sts = [], [], []
    for gb in range(S5_NGB):
        uf_ref[gb] = u_ref[:, gb * LANE:(gb + 1) * LANE].astype(F32)
        uf = jnp.concatenate([uf_ref[gb, pl.ds(a, rows, stride=S5_CHUNK), :] for a in range(S5_CHUNK)], axis=1)
        u = uf.astype(BF16)
        ufs.append(uf)
        ys.append(_dot(u, wt_ref[gb]))
        sts.append(_dot(u, ws_ref[gb]))
    for gb in range(S5_NGB):
        uf, y, st = ufs[gb], ys[gb], sts[gb]
        xr = st[:, :half]
        xi = st[:, half:]
        lam = lam_ref[gb]
        lr = lam[0:1]
        li = lam[1:2]
        pw_r, pw_i = [lr], [li]
        for _ in range(SUBLANES - 1):
            pw_r, pw_i = pw_r + [pw_r[-1] * lr - pw_i[-1] * li], pw_i + [pw_r[-1] * li + pw_i[-1] * lr]
        pw_r = jnp.concatenate(pw_r, axis=0)
        pw_i = jnp.concatenate(pw_i, axis=0)
        d = 1
        while d < SUBLANES:
            keep = sub >= d
            sr = jnp.where(keep, pltpu.roll(xr, d, 0), 0.0)
            si = jnp.where(keep, pltpu.roll(xi, d, 0), 0.0)
            xr, xi = xr + lr * sr - li * si, xi + lr * si + li * sr
            lr, li = lr * lr - li * li, 2.0 * lr * li
            d *= 2
        cr = jnp.zeros((1, half), F32)
        ci = jnp.zeros((1, half), F32)
        gr, gi = [], []
        for g in range(rows // SUBLANES):
            rs = slice(g * SUBLANES, (g + 1) * SUBLANES)
            hr = xr[rs] + pw_r * cr - pw_i * ci
            hi = xi[rs] + pw_r * ci + pw_i * cr
            gr.append(hr)
            gi.append(hi)
            cr = hr[SUBLANES - 1:SUBLANES]
            ci = hi[SUBLANES - 1:SUBLANES]
        xr = jnp.concatenate(gr, axis=0)
        xi = jnp.concatenate(gi, axis=0)
        keep = rowi >= 1
        pr = jnp.where(keep, pltpu.roll(xr, 1, 0), 0.0)
        pi = jnp.where(keep, pltpu.roll(xi, 1, 0), 0.0)
        xp = jnp.concatenate([pr, pi], axis=1).astype(BF16)
        yg = _gelu_tanh(y + _dot(xp, wo_ref[gb]) + d_ref[gb] * uf)
        for a in range(S5_CHUNK):
            yf_ref[gb, pl.ds(a, rows, stride=S5_CHUNK), :] = yg[:, a * LANE:(a + 1) * LANE]
    o_ref[...] = jnp.concatenate([yf_ref[gb] for gb in range(S5_NGB)], axis=1).astype(o_ref.dtype)


def _s5_call(us, pw, l, bsz, seq):
    n = us.shape[0]
    blk = pl.BlockSpec((seq, D_S5), lambda b: (b, 0))
    wshape = pw["s5_wt"].shape[1:]
    return pl.pallas_call(
        functools.partial(_s5_kernel, seq // S5_CHUNK),
        grid=(bsz,),
        in_specs=[blk, _layer(wshape, l), _layer(wshape, l), _layer(wshape, l),
                  _layer(pw["s5_lam"].shape[1:], l), _layer(pw["s5_d"].shape[1:], l)],
        out_specs=blk,
        out_shape=jax.ShapeDtypeStruct((n, D_S5), BF16),
        scratch_shapes=[pltpu.VMEM((S5_NGB, seq, LANE), F32), pltpu.VMEM((S5_NGB, seq, LANE), F32)],
        compiler_params=_params(("arbitrary",)),
        name="s5",
    )(us, pw["s5_wt"], pw["s5_ws"], pw["s5_wo"], pw["s5_lam"], pw["s5_d"])


def _merge_kernel(x_ref, mod_ref, g_ref, ya_ref, yb_ref, yc_ref, yd_ref,
                  wg_ref, wb_ref, wglu_ref, bglu_ref, wout_ref, o_ref):
    x = x_ref[...]
    m = mod_ref[0]
    d = x.shape[1]
    u = _rms_mod(x, g_ref[...], m[3:4], m[4:5]).astype(BF16)
    yd = yd_ref[...]
    ydg = (yd.astype(F32) * _sigmoid(_dot(yd, wglu_ref[...]) + bglu_ref[...])).astype(BF16)
    ys = (ya_ref[...], yb_ref[...], yc_ref[...], ydg)
    merged = jnp.zeros(x.shape, F32)
    for nb in range(N_BRANCH):
        gate = _sigmoid(_dot(u, wg_ref[:, nb * d:(nb + 1) * d]))
        merged = merged + gate * _dot(ys[nb], wb_ref[nb])
    o_ref[...] = x + (1.0 + m[5:6]) * _dot(merged.astype(BF16), wout_ref[...])


def _merge_call(x, mod, ya, yb, yc, yd, pw, l, seq, tm):
    n, d = x.shape
    row = lambda w: pl.BlockSpec((tm, w), lambda i: (i, 0))
    return pl.pallas_call(
        _merge_kernel,
        grid=(n // tm,),
        in_specs=[row(d), _mod_spec(l, tm, seq, d), _layer((1, d), l, 1),
                  row(BRANCH_W), row(BRANCH_W), row(BRANCH_W), row(BRANCH_W),
                  _layer((d, N_BRANCH * d), l), _layer((N_BRANCH, BRANCH_W, d), l),
                  _layer((D_S5, D_S5), l), _layer((1, D_S5), l), _layer((d, d), l)],
        out_specs=row(d),
        out_shape=jax.ShapeDtypeStruct((n, d), F32),
        compiler_params=_params(("arbitrary",)),
        name="merge",
    )(x, mod, pw["norm_g"], ya, yb, yc, yd, pw["w_gate"], pw["w_branch"], pw["w_glu"], pw["b_glu"], pw["w_out"])


def _to_slots(w, nheads, hdim):
    lead = w.shape[:-1]
    wh = w.reshape(lead + (nheads, hdim))
    wh = jnp.pad(wh, [(0, 0)] * (len(lead) + 1) + [(0, LANE - hdim)])
    return wh.reshape(lead + (nheads * LANE,))


def _pad_lane(v, width=LANE):
    return jnp.pad(v, [(0, 0)] * (v.ndim - 1) + [(0, width - v.shape[-1])])


def _rope_table(positions, rot, period):
    j = jnp.arange(LANE) % period
    inv = jnp.where(j < rot, ROPE_THETA ** (-(2 * (j % (rot // 2))).astype(F32) / rot), 0.0)
    ang = positions.astype(F32).reshape(-1, 1) * inv[None, :]
    return jnp.cos(ang), jnp.sin(ang)


def _rot_matrix(rot, period):
    src = jnp.arange(LANE)[:, None]
    dst = jnp.arange(LANE)[None, :]
    j = dst % period
    half = rot // 2
    return (jnp.where((j < half) & (src == dst + half), -1.0, 0.0)
            + jnp.where((j >= half) & (j < rot) & (src == dst - half), 1.0, 0.0)).astype(BF16)


def _s5_weights(lam_re, lam_im, log_dt, b_re, b_im, c_re, c_im, dvec):
    hp = lax.Precision.HIGHEST
    g, p = lam_re.shape
    ch = S5_CHUNK
    dt = jnp.exp(log_dt)[:, None]
    mag = jnp.exp(lam_re * dt)
    ar, ai = mag * jnp.cos(lam_im * dt), mag * jnp.sin(lam_im * dt)
    den = lam_re * lam_re + lam_im * lam_im
    nr, ni = ar - 1.0, ai
    f_re = (nr * lam_re + ni * lam_im) / den
    f_im = (ni * lam_re - nr * lam_im) / den
    bb_re = f_re[..., None] * b_re - f_im[..., None] * b_im
    bb_im = f_re[..., None] * b_im + f_im[..., None] * b_re
    pr, pi = [jnp.ones_like(ar)], [jnp.zeros_like(ar)]
    for _ in range(ch):
        pr, pi = pr + [pr[-1] * ar - pi[-1] * ai], pi + [pr[-1] * ai + pi[-1] * ar]
    pr, pi = jnp.stack(pr), jnp.stack(pi)
    mr = pr[..., None] * bb_re - pi[..., None] * bb_im
    mi = pr[..., None] * bb_im + pi[..., None] * bb_re
    kern = (jnp.einsum('gjp,tgpi->tgji', c_re, mr, precision=hp)
            - jnp.einsum('gjp,tgpi->tgji', c_im, mi, precision=hp))

    def group_diag(x):
        r, c = x.shape[-2:]
        x = x.reshape(x.shape[:-3] + (S5_NGB, S5_GB * r, c))
        on_diag = jnp.arange(S5_GB * r)[:, None] // r == jnp.arange(S5_GB * c)[None, :] // c
        return jnp.tile(x, (1,) * (x.ndim - 1) + (S5_GB,)) * on_diag.astype(x.dtype)

    a_in = jnp.arange(ch)[:, None]
    a_out = jnp.arange(ch)[None, :]
    lag = jnp.clip(a_out - a_in, 0, ch)
    bd_k = group_diag(kern.transpose(0, 1, 3, 2)).astype(BF16)
    toe = bd_k[lag] * (a_out >= a_in)[..., None, None, None].astype(BF16)
    w_toe = toe.transpose(2, 0, 3, 1, 4).reshape(S5_NGB, ch * LANE, ch * LANE)
    rev = jnp.arange(ch - 1, -1, -1)
    st = jnp.concatenate([group_diag(mr[:ch].transpose(0, 1, 3, 2)), group_diag(mi[:ch].transpose(0, 1, 3, 2))],
                         axis=-1).astype(BF16)
    w_st = st[rev].transpose(1, 0, 2, 3).reshape(S5_NGB, ch * LANE, 2 * S5_GB * p)
    pr1, pi1 = pr[1:], pi[1:]
    co_re = c_re[None] * pr1[:, :, None, :] - c_im[None] * pi1[:, :, None, :]
    co_im = -(c_re[None] * pi1[:, :, None, :] + c_im[None] * pr1[:, :, None, :])
    co = jnp.stack([group_diag(co_re.transpose(0, 1, 3, 2)), group_diag(co_im.transpose(0, 1, 3, 2))],
                   axis=2).astype(BF16)
    w_out = co.reshape(ch, S5_NGB, 2 * S5_GB * p, LANE).transpose(1, 2, 0, 3).reshape(
        S5_NGB, 2 * S5_GB * p, ch * LANE)
    lam_c = jnp.stack([pr[ch].reshape(S5_NGB, S5_GB * p), pi[ch].reshape(S5_NGB, S5_GB * p)], axis=1)
    d_t = jnp.tile(dvec.reshape(S5_NGB, 1, LANE), (1, 1, ch))
    return w_toe, w_st, w_out, lam_c, d_t


def _block_diag(w):
    h, a, b = w.shape
    eye = jnp.eye(h, dtype=w.dtype)
    return jnp.einsum('hij,hg->higj', w, eye).reshape(h * a, h * b)


def _prep_params(p):
    w_in = p["w_in"]
    nl = w_in.shape[0]
    offs = [0]
    for s in IN_SPLITS:
        offs.append(offs[-1] + s)
    seg = [w_in[:, :, offs[k]:offs[k + 1]] for k in range(len(IN_SPLITS))]
    (w_xr, w_gr, w_ql, w_kvl, w_kpe, w_qd, w_kd, w_vd, w_qi, w_ki, w_wi, w_us, w_gate) = seg
    w_all = jnp.concatenate([w_xr, w_gr, w_ql, w_kvl, _pad_lane(w_kpe), w_qd, _pad_lane(w_kd), _pad_lane(w_vd),
                             w_qi, jnp.tile(w_ki, (1, 1, IDX_HEADS)), _pad_lane(w_wi), w_us],
                            axis=2).astype(BF16)
    assert w_all.shape[2] == _C_END

    wuq = _to_slots(p["mla_w_uq"], MLA_HEADS, MLA_QK).astype(BF16)
    wkv = p["mla_w_ukv"].reshape(nl, MLA_KV_LORA, MLA_HEADS, MLA_NOPE + MLA_V)
    kn = jnp.pad(wkv[..., :MLA_NOPE], ((0, 0), (0, 0), (0, 0), (MLA_ROPE, LANE - MLA_QK)))
    wv = wkv[..., MLA_NOPE:].reshape(nl, MLA_KV_LORA, MLA_HEADS // 2, 2, MLA_V)
    zv = jnp.zeros_like(wv[:, :, :, 0])
    wv = jnp.stack([jnp.concatenate([wv[:, :, :, 0], zv], axis=-1),
                    jnp.concatenate([zv, wv[:, :, :, 1]], axis=-1)], axis=3)
    wukv = jnp.concatenate([kn.reshape(nl, MLA_KV_LORA, MLA_HEADS * LANE),
                            wv.reshape(nl, MLA_KV_LORA, MLA_HEADS * LANE)], axis=2).astype(BF16)

    dq = p["dsa_qk_gain"][:, 0]
    vec = jnp.stack([_pad_lane(v, MLA_Q_LORA) for v in (
        p["mla_q_norm"], p["mla_kv_norm"], p["mla_qk_gain"][:, 0], p["mla_qk_gain"][:, 1],
        jnp.concatenate([dq, dq], axis=-1), p["dsa_qk_gain"][:, 1])], axis=1)

    s5_wt, s5_ws, s5_wo, s5_lam, s5_d = jax.vmap(_s5_weights)(
        p["s5_lambda_re"], p["s5_lambda_im"], p["s5_log_dt"], p["s5_b_re"], p["s5_b_im"],
        p["s5_c_re"], p["s5_c_im"], p["s5_d"])

    return dict(
        w_all=w_all, wuq=wuq, wukv=wukv, vec=vec,
        rot=jnp.stack([_rot_matrix(MLA_ROPE, LANE), _rot_matrix(DSA_ROT, DSA_HEAD_DIM),
                       _rot_matrix(IDX_ROT, IDX_DIM)]),
        conv_w=p["conv_w"],
        rg_vec=jnp.stack([p["conv_b"], p["rg_ba"], p["rg_bx"], p["rg_lambda"]], axis=1),
        rg_wa=jax.vmap(_block_diag)(p["rg_wa"]).astype(BF16),
        rg_wx=jax.vmap(_block_diag)(p["rg_wx"]).astype(BF16),
        s5_wt=s5_wt, s5_ws=s5_ws, s5_wo=s5_wo, s5_lam=s5_lam, s5_d=s5_d,
        w_gate=w_gate.astype(BF16),
        w_branch=p["w_branch"].astype(BF16),
        w_glu=p["s5_w_glu"].astype(BF16),
        b_glu=p["s5_b_glu"][:, None, :],
        w_out=p["w_out"].astype(BF16),
        ffn_w1=p["ffn_w1"].astype(BF16), ffn_w3=p["ffn_w3"].astype(BF16), ffn_w2=p["ffn_w2"].astype(BF16),
        norm_g=p["norm_g"][:, :, None, :],
    )


def _pick_tile(n, pref):
    t = pref
    while n % t:
        t //= 2
    return t


def kernel(x, c, positions, ada_w, ada_b, norm_g, ffn_w1, ffn_w3, ffn_w2, w_in,
           conv_w, conv_b, rg_wa, rg_ba, rg_wx, rg_bx, rg_lambda,
           mla_q_norm, mla_w_uq, mla_kv_norm, mla_w_ukv, mla_qk_gain, dsa_qk_gain,
           s5_lambda_re, s5_lambda_im, s5_log_dt, s5_b_re, s5_b_im, s5_c_re, s5_c_im,
           s5_d, s5_w_glu, s5_b_glu, w_branch, w_out):
    p = dict(norm_g=norm_g, ffn_w1=ffn_w1, ffn_w3=ffn_w3, ffn_w2=ffn_w2, w_in=w_in,
             conv_w=conv_w, conv_b=conv_b, rg_wa=rg_wa, rg_ba=rg_ba, rg_wx=rg_wx, rg_bx=rg_bx,
             rg_lambda=rg_lambda, mla_q_norm=mla_q_norm, mla_w_uq=mla_w_uq, mla_kv_norm=mla_kv_norm,
             mla_w_ukv=mla_w_ukv, mla_qk_gain=mla_qk_gain, dsa_qk_gain=dsa_qk_gain,
             s5_lambda_re=s5_lambda_re, s5_lambda_im=s5_lambda_im, s5_log_dt=s5_log_dt,
             s5_b_re=s5_b_re, s5_b_im=s5_b_im, s5_c_re=s5_c_re, s5_c_im=s5_c_im, s5_d=s5_d,
             s5_w_glu=s5_w_glu, s5_b_glu=s5_b_glu, w_branch=w_branch, w_out=w_out)
    bsz, seq, d = x.shape
    n = bsz * seq
    depth = ada_w.shape[0]
    assert seq % (S5_CHUNK * 8) == 0 and seq % _DSA_QB == 0

    tm = _pick_tile(seq, PROJ_ROWS)
    tw = _pick_tile(seq, WIDE_ROWS)
    tc = _pick_tile(seq, SCAN_ROWS)
    bq = _pick_tile(seq, ATTN_ROWS)

    mod = _ada_call(c, ada_w, ada_b)
    pw = _prep_params(p)
    tabs = (_rope_table(positions, MLA_ROPE, LANE) + _rope_table(positions, DSA_ROT, DSA_HEAD_DIM)
            + _rope_table(positions, IDX_ROT, IDX_DIM))
    tri = (jnp.arange(LANE)[:, None] < jnp.arange(LANE)[None, :]).astype(BF16)

    xf = x.reshape(n, d)
    for l in range(depth):
        xf = _ffn_call(xf, mod, pw, l, 0, seq, tw)
        (xr, gr, qm, km, vm, qd, kd, vd, qi, ki, wi, us) = _inproj_call(xf, mod, pw, l, tabs, seq, tm)
        ya = _rglru_call(xr, gr, pw, l, bsz, seq, tc)
        yb = _flash_call(qm, km, vm, bsz, seq, bq)
        yc = _dsa_call(qd, kd, vd, qi, ki, wi, tri, bsz, seq)
        yd = _s5_call(us, pw, l, bsz, seq)
        xf = _merge_call(xf, mod, ya, yb, yc, yd, pw, l, seq, tw)
        xf = _ffn_call(xf, mod, pw, l, 1, seq, tw)
    return xf.reshape(bsz, seq, d)
```
